```python
import math
import jax, jax.numpy as jnp
from jax import lax
import numpy as np

D_MODEL = 4096
BATCH = 8
SEQ = 4096
DEPTH = 2

N_MIXERS = 2
N_LAYERS_A = (DEPTH + 1) // 2
N_LAYERS_B = DEPTH // 2

A_DK = 128
A_HEADS = D_MODEL // A_DK
A_DV = D_MODEL // A_HEADS
A_KEY_WIDTH = A_HEADS * A_DK
A_VAL_WIDTH = A_HEADS * A_DV
A_IN_WIDTH = 2 * A_KEY_WIDTH + 2 * A_VAL_WIDTH

B_DK = 128
B_DV = 128
B_HEADS = D_MODEL // B_DV
B_KEY_WIDTH = B_HEADS * B_DK
B_VAL_WIDTH = B_HEADS * B_DV
B_CONV_WIDTH = 4
B_CONV_CH = 2 * B_KEY_WIDTH + B_VAL_WIDTH
B_IN_WIDTH = B_CONV_CH + B_VAL_WIDTH + 2 * B_HEADS

CHUNK = 64
DEEPNORM_ALPHA = (2.0 * DEPTH) ** 0.25
DEEPNORM_BETA = (8.0 * DEPTH) ** -0.25
LN_EPS = 1e-5
RMS_EPS = 1e-6
L2_EPS = 1e-6

kernel_name = "hybrid_hgrn2_gated_deltanet_deepnorm"


def layer_norm(x, g, b):
    xf = x.astype(jnp.float32)
    mu = jnp.mean(xf, axis=-1, keepdims=True)
    var = jnp.mean(jnp.square(xf - mu), axis=-1, keepdims=True)
    return ((xf - mu) * lax.rsqrt(var + LN_EPS) * g.astype(jnp.float32) + b.astype(jnp.float32)).astype(x.dtype)


def gated_rmsnorm(o, z, w):
    b, s, h, dv = o.shape
    on = o * lax.rsqrt(jnp.mean(o * o, axis=-1, keepdims=True) + RMS_EPS) * w.astype(jnp.float32)
    return on.reshape(b, s, h * dv) * jax.nn.silu(z.astype(jnp.float32))


def to_chunks(x):
    b, s, h, d = x.shape
    return x.reshape(b, s // CHUNK, CHUNK, h, d).transpose(1, 0, 3, 2, 4)


def from_chunks(y):
    n, b, h, c, d = y.shape
    return y.transpose(1, 0, 3, 2, 4).reshape(b, n * c, h, d)


def causal_depthwise_conv(x, w):
    k = w.shape[0]
    return lax.conv_general_dilated(x, w[:, None, :], window_strides=(1,), padding=[(k - 1, 0)],
                                    dimension_numbers=("NWC", "WIO", "NWC"),
                                    feature_group_count=x.shape[-1])


def hgrn2_chunk_scan(q, k, v, log_f):
    n, b, h, c, dk = q.shape
    dv = v.shape[-1]
    causal = jnp.tril(jnp.ones((c, c), dtype=bool))

    def step(state, xs):
        q_c, k_c, v_c, g_c = xs
        cum = jnp.cumsum(g_c, axis=-2)
        diff = cum[..., :, None, :] - cum[..., None, :, :]
        decay = jnp.exp(jnp.where(causal[:, :, None], diff, -jnp.inf))
        attn = jnp.einsum("bhtd,bhsd,bhtsd->bhts", q_c, k_c, decay)
        o = (jnp.einsum("bhts,bhsv->bhtv", attn, v_c)
             + jnp.einsum("bhtd,bhdv->bhtv", q_c * jnp.exp(cum), state))
        last = cum[..., -1:, :]
        state = (jnp.exp(last[..., 0, :])[..., None] * state
                 + jnp.einsum("bhsd,bhsv->bhdv", k_c * jnp.exp(last - cum), v_c))
        return state, o

    state0 = jnp.zeros((b, h, dk, dv), jnp.float32)
    _, o = lax.scan(step, state0, (q, k, v, log_f))
    return o


def hgrn2_mixer(x, w_in, lb, norm_w, w_out):
    b, s, _ = x.shape
    hproj = x @ w_in
    q, f, i, z = jnp.split(hproj, [A_KEY_WIDTH, 2 * A_KEY_WIDTH, 2 * A_KEY_WIDTH + A_VAL_WIDTH], axis=-1)
    q = jax.nn.silu(q.astype(jnp.float32))
    forget = lb + (1.0 - lb) * jax.nn.sigmoid(f.astype(jnp.float32))
    k = 1.0 - forget
    log_f = jnp.log(forget)
    heads_k = lambda t: to_chunks(t.reshape(b, s, A_HEADS, A_DK))
    o = hgrn2_chunk_scan(heads_k(q), heads_k(k),
                         to_chunks(i.astype(jnp.float32).reshape(b, s, A_HEADS, A_DV)),
                         heads_k(log_f))
    y = gated_rmsnorm(from_chunks(o), z, norm_w)
    return y.astype(x.dtype) @ w_out


def gated_delta_chunk(q, k, v, beta, g):
    c = q.shape[-2]
    dv = v.shape[-1]
    strict = jnp.tril(jnp.ones((c, c), dtype=bool), -1)
    incl = jnp.tril(jnp.ones((c, c), dtype=bool))
    cum = jnp.cumsum(g, axis=-1)
    diff = cum[..., :, None] - cum[..., None, :]
    kk = jnp.einsum("nbhtd,nbhsd->nbhts", k, k)
    lower = beta[..., :, None] * kk * jnp.exp(jnp.where(strict, diff, -jnp.inf))
    eye = jnp.eye(c, dtype=jnp.float32)
    rhs = jnp.concatenate([beta[..., None] * v, (beta * jnp.exp(cum))[..., None] * k], axis=-1)
    sol = lax.linalg.triangular_solve(eye + lower, rhs, left_side=True, lower=True, unit_diagonal=True)
    u0, w = sol[..., :dv], sol[..., dv:]
    qk = jnp.einsum("nbhtd,nbhsd->nbhts", q, k) * jnp.exp(jnp.where(incl, diff, -jnp.inf))
    q_decay = q * jnp.exp(cum)[..., None]
    k_decay = k * jnp.exp(cum[..., -1:] - cum)[..., None]
    last_decay = jnp.exp(cum[..., -1])

    def step(state, xs):
        u0_c, w_c, qk_c, qd_c, kd_c, ld_c = xs
        u = u0_c - jnp.einsum("bhcd,bhdv->bhcv", w_c, state)
        o = jnp.einsum("bhcd,bhdv->bhcv", qd_c, state) + jnp.einsum("bhts,bhsv->bhtv", qk_c, u)
        state = ld_c[..., None, None] * state + jnp.einsum("bhcd,bhcv->bhdv", kd_c, u)
        return state, o

    n, b, h, _, dk = q.shape
    state0 = jnp.zeros((b, h, dk, dv), jnp.float32)
    _, o = lax.scan(step, state0, (u0, w, qk, q_decay, k_decay, last_decay))
    return o


def l2_normalize(t):
    return t * lax.rsqrt(jnp.sum(t * t, axis=-1, keepdims=True) + L2_EPS)


def gated_deltanet_mixer(x, w_in, conv_w, a_log, dt_bias, norm_w, w_out):
    b, s, _ = x.shape
    hproj = x @ w_in
    qkv, z, beta_logit, a = jnp.split(
        hproj, [B_CONV_CH, B_CONV_CH + B_VAL_WIDTH, B_CONV_CH + B_VAL_WIDTH + B_HEADS], axis=-1)
    qkv = jax.nn.silu(causal_depthwise_conv(qkv, conv_w).astype(jnp.float32))
    q, k, v = jnp.split(qkv, [B_KEY_WIDTH, 2 * B_KEY_WIDTH], axis=-1)
    q = l2_normalize(q.reshape(b, s, B_HEADS, B_DK)) * (B_DK ** -0.5)
    k = l2_normalize(k.reshape(b, s, B_HEADS, B_DK))
    v = v.reshape(b, s, B_HEADS, B_DV)
    beta = jax.nn.sigmoid(beta_logit.astype(jnp.float32))
    g = -jnp.exp(a_log.astype(jnp.float32)) * jax.nn.softplus(
        a.astype(jnp.float32) + dt_bias.astype(jnp.float32))
    o = gated_delta_chunk(to_chunks(q), to_chunks(k), to_chunks(v),
                          to_chunks(beta[..., None])[..., 0], to_chunks(g[..., None])[..., 0])
    y = gated_rmsnorm(from_chunks(o), z, norm_w)
    return y.astype(x.dtype) @ w_out


def _fwd_setup_inputs(seed: int = 0) -> dict:
    key = jax.random.key(seed)
    ks = jax.random.split(key, 16)
    f32 = jnp.float32
    D = D_MODEL
    x = jax.random.normal(ks[0], (BATCH, SEQ, D), f32)
    a_w_in = jax.random.normal(ks[1], (N_LAYERS_A, D, A_IN_WIDTH), f32) * (D ** -0.5)
    a_lower_bounds = jax.random.normal(ks[2], (DEPTH + 1, A_KEY_WIDTH), f32) * 0.5
    a_norm_w = 1.0 + 0.02 * jax.random.normal(ks[3], (N_LAYERS_A, A_DV), f32)
    a_w_out = jax.random.normal(ks[4], (N_LAYERS_A, A_VAL_WIDTH, D), f32) * (A_VAL_WIDTH ** -0.5) * DEEPNORM_BETA
    b_w_in = jax.random.normal(ks[5], (N_LAYERS_B, D, B_IN_WIDTH), f32) * (D ** -0.5)
    b_conv_w = jax.random.normal(ks[6], (N_LAYERS_B, B_CONV_WIDTH, B_CONV_CH), f32) * (B_CONV_WIDTH ** -0.5)
    b_a_log = jnp.log(jax.random.uniform(ks[7], (N_LAYERS_B, B_HEADS), f32, 1.0, 16.0))
    dt = jnp.exp(jax.random.uniform(ks[8], (N_LAYERS_B, B_HEADS), f32, math.log(1e-3), math.log(1e-1)))
    b_dt_bias = dt + jnp.log(-jnp.expm1(-dt))
    b_norm_w = 1.0 + 0.02 * jax.random.normal(ks[9], (N_LAYERS_B, B_DV), f32)
    b_w_out = jax.random.normal(ks[10], (N_LAYERS_B, B_VAL_WIDTH, D), f32) * (B_VAL_WIDTH ** -0.5) * DEEPNORM_BETA
    ln_g = 1.0 + 0.02 * jax.random.normal(ks[11], (DEPTH, D), f32)
    ln_b = 0.02 * jax.random.normal(ks[12], (DEPTH, D), f32)
    return {"x": x, "a_w_in": a_w_in, "a_lower_bounds": a_lower_bounds, "a_norm_w": a_norm_w,
            "a_w_out": a_w_out, "b_w_in": b_w_in, "b_conv_w": b_conv_w, "b_a_log": b_a_log,
            "b_dt_bias": b_dt_bias, "b_norm_w": b_norm_w, "b_w_out": b_w_out,
            "ln_g": ln_g, "ln_b": ln_b}


def _fwd_reference(x, a_w_in, a_lower_bounds, a_norm_w, a_w_out, b_w_in, b_conv_w, b_a_log,
              b_dt_bias, b_norm_w, b_w_out, ln_g, ln_b):
    lb_all = jnp.cumsum(jax.nn.softmax(a_lower_bounds.astype(jnp.float32), axis=0), axis=0)
    for i in range(DEPTH):
        j = i // N_MIXERS
        if i % N_MIXERS == 0:
            y = hgrn2_mixer(x, a_w_in[j], lb_all[i], a_norm_w[j], a_w_out[j])
        else:
            y = gated_deltanet_mixer(x, b_w_in[j], b_conv_w[j], b_a_log[j], b_dt_bias[j],
                                     b_norm_w[j], b_w_out[j])
        x = layer_norm(DEEPNORM_ALPHA * x + y, ln_g[i], ln_b[i])
    return x


import jax as _jax
import jax.numpy as _jnp

TWIN_FORMAT = 'train_step'
FWD_PARAMS = ['x', 'a_w_in', 'a_lower_bounds', 'a_norm_w', 'a_w_out', 'b_w_in', 'b_conv_w', 'b_a_log', 'b_dt_bias', 'b_norm_w', 'b_w_out', 'ln_g', 'ln_b']
TWIN_WEIGHTS = ['a_w_in', 'a_lower_bounds', 'a_norm_w', 'a_w_out', 'b_w_in', 'b_conv_w', 'b_a_log', 'b_dt_bias', 'b_norm_w', 'b_w_out', 'ln_g', 'ln_b']
TWIN_DIFF_INPUT = 'x'
TWIN_INPUTS = ['x', 'a_w_in', 'a_lower_bounds', 'a_norm_w', 'a_w_out', 'b_w_in', 'b_conv_w', 'b_a_log', 'b_dt_bias', 'b_norm_w', 'b_w_out', 'ln_g', 'ln_b', 'loss_target', 'm_a_w_in', 'm_a_lower_bounds', 'm_a_norm_w', 'm_a_w_out', 'm_b_w_in', 'm_b_conv_w', 'm_b_a_log', 'm_b_dt_bias', 'm_b_norm_w', 'm_b_w_out', 'm_ln_g', 'm_ln_b', 'v_a_w_in', 'v_a_lower_bounds', 'v_a_norm_w', 'v_a_w_out', 'v_b_w_in', 'v_b_conv_w', 'v_b_a_log', 'v_b_dt_bias', 'v_b_norm_w', 'v_b_w_out', 'v_ln_g', 'v_ln_b']
TWIN_OUTPUTS = ['loss', 'grad_x', 'grad_a_w_in', 'grad_a_lower_bounds', 'grad_a_norm_w', 'grad_a_w_out', 'grad_b_w_in', 'grad_b_conv_w', 'grad_b_a_log', 'grad_b_dt_bias', 'grad_b_norm_w', 'grad_b_w_out', 'grad_ln_g', 'grad_ln_b', 'delta_a_w_in', 'delta_a_lower_bounds', 'delta_a_norm_w', 'delta_a_w_out', 'delta_b_w_in', 'delta_b_conv_w', 'delta_b_a_log', 'delta_b_dt_bias', 'delta_b_norm_w', 'delta_b_w_out', 'delta_ln_g', 'delta_ln_b', 'new_m_a_w_in', 'new_m_a_lower_bounds', 'new_m_a_norm_w', 'new_m_a_w_out', 'new_m_b_w_in', 'new_m_b_conv_w', 'new_m_b_a_log', 'new_m_b_dt_bias', 'new_m_b_norm_w', 'new_m_b_w_out', 'new_m_ln_g', 'new_m_ln_b', 'new_v_a_w_in', 'new_v_a_lower_bounds', 'new_v_a_norm_w', 'new_v_a_w_out', 'new_v_b_w_in', 'new_v_b_conv_w', 'new_v_b_a_log', 'new_v_b_dt_bias', 'new_v_b_norm_w', 'new_v_b_w_out', 'new_v_ln_g', 'new_v_ln_b']
TWIN_LEAF_KINDS = {'loss': 'loss', 'grad_x': 'grad_x', 'grad_a_w_in': 'grad_w', 'grad_a_lower_bounds': 'grad_w', 'grad_a_norm_w': 'grad_w', 'grad_a_w_out': 'grad_w', 'grad_b_w_in': 'grad_w', 'grad_b_conv_w': 'grad_w', 'grad_b_a_log': 'grad_w', 'grad_b_dt_bias': 'grad_w', 'grad_b_norm_w': 'grad_w', 'grad_b_w_out': 'grad_w', 'grad_ln_g': 'grad_w', 'grad_ln_b': 'grad_w', 'delta_a_w_in': 'delta_w', 'delta_a_lower_bounds': 'delta_w', 'delta_a_norm_w': 'delta_w', 'delta_a_w_out': 'delta_w', 'delta_b_w_in': 'delta_w', 'delta_b_conv_w': 'delta_w', 'delta_b_a_log': 'delta_w', 'delta_b_dt_bias': 'delta_w', 'delta_b_norm_w': 'delta_w', 'delta_b_w_out': 'delta_w', 'delta_ln_g': 'delta_w', 'delta_ln_b': 'delta_w', 'new_m_a_w_in': 'new_m', 'new_m_a_lower_bounds': 'new_m', 'new_m_a_norm_w': 'new_m', 'new_m_a_w_out': 'new_m', 'new_m_b_w_in': 'new_m', 'new_m_b_conv_w': 'new_m', 'new_m_b_a_log': 'new_m', 'new_m_b_dt_bias': 'new_m', 'new_m_b_norm_w': 'new_m', 'new_m_b_w_out': 'new_m', 'new_m_ln_g': 'new_m', 'new_m_ln_b': 'new_m', 'new_v_a_w_in': 'new_v', 'new_v_a_lower_bounds': 'new_v', 'new_v_a_norm_w': 'new_v', 'new_v_a_w_out': 'new_v', 'new_v_b_w_in': 'new_v', 'new_v_b_conv_w': 'new_v', 'new_v_b_a_log': 'new_v', 'new_v_b_dt_bias': 'new_v', 'new_v_b_norm_w': 'new_v', 'new_v_b_w_out': 'new_v', 'new_v_ln_g': 'new_v', 'new_v_ln_b': 'new_v'}


def _forward(args):
    return _fwd_reference(*[args[k] for k in FWD_PARAMS])


def _output_shape():
    out = _jax.eval_shape(lambda: _forward(_fwd_setup_inputs(0)))
    return out.shape, out.dtype

N_MICROBATCH = 1
ADAM_LR = 0.001
ADAM_B1 = 0.9
ADAM_B2 = 0.999
ADAM_EPS = 1e-08
ADAM_WD = 0.01
ADAM_STEP = 10
PER_EXAMPLE_BATCH_AXIS = {'x': 0, 'loss_target': 0}
SHARED_INPUTS = []
_WEIGHT_DTYPES = {'a_w_in': _jnp.float32, 'a_lower_bounds': _jnp.float32, 'a_norm_w': _jnp.float32, 'a_w_out': _jnp.float32, 'b_w_in': _jnp.float32, 'b_conv_w': _jnp.float32, 'b_a_log': _jnp.float32, 'b_dt_bias': _jnp.float32, 'b_norm_w': _jnp.float32, 'b_w_out': _jnp.float32, 'ln_g': _jnp.float32, 'ln_b': _jnp.float32}
MOMENT_SCALE = {'a_w_in': 6.867744e-03, 'a_lower_bounds': 4.266213e-04, 'a_norm_w': 7.400672e-02, 'a_w_out': 1.894989e-02, 'b_w_in': 7.467798e-03, 'b_conv_w': 7.029525e-03, 'b_a_log': 4.294821e-02, 'b_dt_bias': 4.276117e-02, 'b_norm_w': 5.313196e-02, 'b_w_out': 1.962824e-02, 'ln_g': 5.650631e+00, 'ln_b': 2.581402e-01}


def _to_microbatches(a, axis):
    t = _jnp.moveaxis(a, axis, 0)
    t = t.reshape((N_MICROBATCH, t.shape[0] // N_MICROBATCH) + t.shape[1:])
    return _jnp.moveaxis(t, 1, axis + 1)


def setup_inputs(seed: int = 0) -> dict:
    inp = _fwd_setup_inputs(seed)
    key = _jax.random.fold_in(_jax.random.key(seed), 7919)
    shape, _ = _output_shape()
    out = dict(inp)
    out["loss_target"] = _jax.random.normal(_jax.random.fold_in(key, 0), shape, _jnp.float32)
    for i, name in enumerate(TWIN_WEIGHTS):
        w = inp[name].astype(_jnp.float32)
        if MOMENT_SCALE is None:
            s = _jnp.sqrt(_jnp.mean(_jnp.square(w)) + 1e-30)
        else:
            s = MOMENT_SCALE[name]
        km, kv = _jax.random.split(_jax.random.fold_in(key, i + 1))
        out[name] = w
        out["m_" + name] = s * _jax.random.normal(km, w.shape, _jnp.float32)
        out["v_" + name] = (s * s) * _jax.random.uniform(kv, w.shape, _jnp.float32, 0.5, 1.5)
    if N_MICROBATCH > 1:
        for name, axis in PER_EXAMPLE_BATCH_AXIS.items():
            out[name] = _to_microbatches(out[name], axis)
    return {'x': out['x'], 'a_w_in': out['a_w_in'], 'a_lower_bounds': out['a_lower_bounds'], 'a_norm_w': out['a_norm_w'], 'a_w_out': out['a_w_out'], 'b_w_in': out['b_w_in'], 'b_conv_w': out['b_conv_w'], 'b_a_log': out['b_a_log'], 'b_dt_bias': out['b_dt_bias'], 'b_norm_w': out['b_norm_w'], 'b_w_out': out['b_w_out'], 'ln_g': out['ln_g'], 'ln_b': out['ln_b'], 'loss_target': out['loss_target'], 'm_a_w_in': out['m_a_w_in'], 'm_a_lower_bounds': out['m_a_lower_bounds'], 'm_a_norm_w': out['m_a_norm_w'], 'm_a_w_out': out['m_a_w_out'], 'm_b_w_in': out['m_b_w_in'], 'm_b_conv_w': out['m_b_conv_w'], 'm_b_a_log': out['m_b_a_log'], 'm_b_dt_bias': out['m_b_dt_bias'], 'm_b_norm_w': out['m_b_norm_w'], 'm_b_w_out': out['m_b_w_out'], 'm_ln_g': out['m_ln_g'], 'm_ln_b': out['m_ln_b'], 'v_a_w_in': out['v_a_w_in'], 'v_a_lower_bounds': out['v_a_lower_bounds'], 'v_a_norm_w': out['v_a_norm_w'], 'v_a_w_out': out['v_a_w_out'], 'v_b_w_in': out['v_b_w_in'], 'v_b_conv_w': out['v_b_conv_w'], 'v_b_a_log': out['v_b_a_log'], 'v_b_dt_bias': out['v_b_dt_bias'], 'v_b_norm_w': out['v_b_norm_w'], 'v_b_w_out': out['v_b_w_out'], 'v_ln_g': out['v_ln_g'], 'v_ln_b': out['v_ln_b']}


def _loss(weights, diff, rest, loss_target):
    with _jax.named_scope("forward"):
        args = {**rest, TWIN_DIFF_INPUT: diff, **{k: w.astype(_WEIGHT_DTYPES[k]) for k, w in weights.items()}}
        y = _forward(args)
    with _jax.named_scope("loss_head"):
        err = _jnp.square(y.astype(_jnp.float32) - loss_target)
        return 0.5 * _jnp.sum(_jnp.mean(err, axis=-1)) if err.ndim else 0.5 * err


def _adamw(w, g, m, v):
    m = ADAM_B1 * m + (1.0 - ADAM_B1) * g
    v = ADAM_B2 * v + (1.0 - ADAM_B2) * _jnp.square(g)
    m_hat = m / (1.0 - ADAM_B1 ** ADAM_STEP)
    v_hat = v / (1.0 - ADAM_B2 ** ADAM_STEP)
    delta = -ADAM_LR * (m_hat / (_jnp.sqrt(v_hat) + ADAM_EPS) + ADAM_WD * w)
    return delta, m, v


def reference(x, a_w_in, a_lower_bounds, a_norm_w, a_w_out, b_w_in, b_conv_w, b_a_log, b_dt_bias, b_norm_w, b_w_out, ln_g, ln_b, loss_target, m_a_w_in, m_a_lower_bounds, m_a_norm_w, m_a_w_out, m_b_w_in, m_b_conv_w, m_b_a_log, m_b_dt_bias, m_b_norm_w, m_b_w_out, m_ln_g, m_ln_b, v_a_w_in, v_a_lower_bounds, v_a_norm_w, v_a_w_out, v_b_w_in, v_b_conv_w, v_b_a_log, v_b_dt_bias, v_b_norm_w, v_b_w_out, v_ln_g, v_ln_b):
    given = dict(x=x, a_w_in=a_w_in, a_lower_bounds=a_lower_bounds, a_norm_w=a_norm_w, a_w_out=a_w_out, b_w_in=b_w_in, b_conv_w=b_conv_w, b_a_log=b_a_log, b_dt_bias=b_dt_bias, b_norm_w=b_norm_w, b_w_out=b_w_out, ln_g=ln_g, ln_b=ln_b, loss_target=loss_target, m_a_w_in=m_a_w_in, m_a_lower_bounds=m_a_lower_bounds, m_a_norm_w=m_a_norm_w, m_a_w_out=m_a_w_out, m_b_w_in=m_b_w_in, m_b_conv_w=m_b_conv_w, m_b_a_log=m_b_a_log, m_b_dt_bias=m_b_dt_bias, m_b_norm_w=m_b_norm_w, m_b_w_out=m_b_w_out, m_ln_g=m_ln_g, m_ln_b=m_ln_b, v_a_w_in=v_a_w_in, v_a_lower_bounds=v_a_lower_bounds, v_a_norm_w=v_a_norm_w, v_a_w_out=v_a_w_out, v_b_w_in=v_b_w_in, v_b_conv_w=v_b_conv_w, v_b_a_log=v_b_a_log, v_b_dt_bias=v_b_dt_bias, v_b_norm_w=v_b_norm_w, v_b_w_out=v_b_w_out, v_ln_g=v_ln_g, v_ln_b=v_ln_b)
    weights = {n: given[n] for n in TWIN_WEIGHTS}
    shared = {n: given[n] for n in SHARED_INPUTS}
    per_example = {n: given[n] for n in ['x']}
    grad_fn = _jax.value_and_grad(_loss, argnums=(0, 1))

    def one_microbatch(ex, loss_target):
        ex = dict(ex)
        diff = ex.pop(TWIN_DIFF_INPUT)
        return grad_fn(weights, diff, {**shared, **ex}, loss_target)

    if N_MICROBATCH == 1:
        loss, (grad_w, grad_x) = one_microbatch(per_example, given["loss_target"])
    else:
        def body(carry, xs):
            loss_sum, grad_sum = carry
            l_k, (gw_k, gx_k) = one_microbatch(xs[0], xs[1])
            with _jax.named_scope("update"):
                return (loss_sum + l_k, _jax.tree.map(_jnp.add, grad_sum, gw_k)), gx_k

        init = (_jnp.zeros((), _jnp.float32), _jax.tree.map(_jnp.zeros_like, weights))
        (loss, grad_w), grad_x = _jax.lax.scan(body, init, (per_example, given["loss_target"]))
    with _jax.named_scope("update"):
        delta_w, new_m, new_v = {}, {}, {}
        for n in TWIN_WEIGHTS:
            delta_w[n], new_m[n], new_v[n] = _adamw(weights[n], grad_w[n], given["m_" + n], given["v_" + n])
    return (loss, grad_x, *[grad_w[n] for n in TWIN_WEIGHTS], *[delta_w[n] for n in TWIN_WEIGHTS],
            *[new_m[n] for n in TWIN_WEIGHTS], *[new_v[n] for n in TWIN_WEIGHTS])
```

```python
import functools
import math

import jax
import jax.numpy as jnp
from jax import lax
from jax.experimental import pallas as pl
from jax.experimental.pallas import tpu as pltpu

f32 = jnp.float32
bf16 = jnp.bfloat16
HIGHEST = lax.Precision.HIGHEST

HEAD_DIM = 128
CHUNK = 64
SUB = 16
DEPTH = 2
DEEPNORM_ALPHA = (2.0 * DEPTH) ** 0.25
LN_EPS = 1e-5
RMS_EPS = 1e-6
L2_EPS = 1e-6
ADAM_LR, ADAM_B1, ADAM_B2, ADAM_EPS, ADAM_WD, ADAM_STEP = 0.001, 0.9, 0.999, 1e-08, 0.01, 10
CONV_WIDTH = 4
EXP_CLAMP = 40.0
VMEM_LIMIT = 56 << 20
MXU_DTYPE = bf16
SUBLANES = 8
LANES = 128
N_CHIPS = 4
N_DEV = 8

NN = (((1,), (0,)), ((), ()))
NT = (((1,), (1,)), ((), ()))
TN = (((0,), (0,)), ((), ()))
MESH = pl.DeviceIdType.MESH
ANY = pl.BlockSpec(memory_space=pl.ANY)
VMEM = pl.BlockSpec(memory_space=pltpu.VMEM)


def _bdot(a, b, dn):
    return lax.dot_general(a.astype(MXU_DTYPE), b.astype(MXU_DTYPE), dn, preferred_element_type=f32)


def _hdot(a, b, dn):
    return lax.dot_general(a, b, dn, precision=HIGHEST, preferred_element_type=f32)


def _iota(shape, dim):
    return lax.broadcasted_iota(jnp.int32, shape, dim)


def _same_sub(a, b):
    shift = int(math.log2(SUB))
    return jnp.right_shift(a, shift) == jnp.right_shift(b, shift)


def _silu(x):
    return x * jax.nn.sigmoid(x)


def _gated_rmsnorm(o, hz, nw):
    return o * lax.rsqrt(jnp.mean(o * o, axis=-1, keepdims=True) + RMS_EPS) * nw * _silu(hz)


def _hgrn2_chunk(hq, hf, hi, hz, lb, nw, st):
    c = CHUNK
    q = _silu(hq)
    forget = lb + (1.0 - lb) * jax.nn.sigmoid(hf)
    k = 1.0 - forget
    g = jnp.log(forget)
    tri = (_iota((c, c), 1) <= _iota((c, c), 0)).astype(f32)
    cum = _hdot(tri, g, NN)
    total = jnp.sum(g, axis=0, keepdims=True)
    nb = c // SUB
    blk = [cum[SUB * j:SUB * (j + 1)] for j in range(nb)]
    r16 = _iota((SUB, HEAD_DIM), 0)

    def row(j, i):
        return jnp.sum(jnp.where(r16 == i, blk[j], 0.0), axis=0, keepdims=True)

    c15, c31, c47 = row(0, SUB - 1), row(1, SUB - 1), row(2, SUB - 1)
    z = jnp.zeros((SUB, HEAD_DIM), f32)
    e = jnp.exp
    q32 = jnp.concatenate([z, z, e(blk[2] - c31), e(blk[3] - c31)], axis=0)
    k32 = jnp.concatenate([e(c31 - blk[0]), e(c31 - blk[1]), z, z], axis=0)
    q16a = jnp.concatenate([z, e(blk[1] - c15), z, z], axis=0)
    k16a = jnp.concatenate([e(c15 - blk[0]), z, z, z], axis=0)
    q16b = jnp.concatenate([z, z, z, e(blk[3] - c47)], axis=0)
    k16b = jnp.concatenate([z, z, e(c47 - blk[2]), z], axis=0)
    dmid = jnp.concatenate([blk[j] - row(j, SUB // 2 - 1) for j in range(nb)], axis=0)
    qd = e(jnp.minimum(dmid, EXP_CLAMP))
    kd = e(jnp.minimum(-dmid, EXP_CLAMP))
    q2 = jnp.concatenate([q * q32, q * q16a, q * q16b], axis=1)
    k2 = jnp.concatenate([k * k32, k * k16a, k * k16b], axis=1)
    rr, cc = _iota((c, c), 0), _iota((c, c), 1)
    diag = _same_sub(rr, cc) & (cc <= rr)
    att = _bdot(q2, k2, NT) + jnp.where(diag, _bdot(q * qd, k * kd, NT), 0.0)
    o = _bdot(att, hi, NN) + _bdot(q * e(cum), st, NT)
    st_new = st * e(total) + _bdot(hi, k * e(total - cum), TN)
    return _gated_rmsnorm(o, hz, nw), st_new


def _unit_lower_inverse(lower):
    c = CHUNK
    rr, cc = _iota((c, c), 0), _iota((c, c), 1)
    eye = (rr == cc).astype(f32)
    same = _same_sub(rr, cc)
    ld = jnp.where(same, lower, 0.0)
    off = lower - ld
    x = eye - ld
    p = ld
    for _ in range(int(math.log2(SUB)) - 1):
        p = _hdot(p, p, NN)
        x = x + _hdot(x, p, NN)
    n = _hdot(x, off, NN)
    y = eye - n
    p = n
    for _ in range(int(math.log2(c // SUB)) - 1):
        p = _hdot(p, p, NN)
        y = y + _hdot(y, p, NN)
    return _hdot(y, x, NN)


def _gdn_chunk(cq, ck, cv, hz, bl, al, alog, dtb, nw, st):
    c = CHUNK
    sq, sk, v = _silu(cq), _silu(ck), _silu(cv)
    q = sq * lax.rsqrt(jnp.sum(sq * sq, axis=-1, keepdims=True) + L2_EPS) * (HEAD_DIM ** -0.5)
    k = sk * lax.rsqrt(jnp.sum(sk * sk, axis=-1, keepdims=True) + L2_EPS)
    beta = jax.nn.sigmoid(bl)
    xs = al + dtb
    pos = xs > 0.0
    g = -jnp.exp(alog) * (jnp.where(pos, xs, 0.0) + jnp.log(1.0 + jnp.exp(jnp.where(pos, -xs, xs))))
    rr, cc = _iota((c, c), 0), _iota((c, c), 1)
    tri = (cc <= rr).astype(f32)
    gc = jnp.broadcast_to(g, (c, c))
    cum_c = _hdot(tri, gc, NN)
    cum_r = _hdot(gc, (rr <= cc).astype(f32), TN)
    g128 = jnp.broadcast_to(g, (c, HEAD_DIM))
    cum = _hdot(tri, g128, NN)
    total = jnp.sum(g128, axis=0, keepdims=True)
    diff = cum_c - cum_r
    ninf = -jnp.inf
    lower = beta * _bdot(k, k, NT) * jnp.exp(jnp.where(cc < rr, diff, ninf))
    rhs = jnp.concatenate([beta * v, (beta * jnp.exp(cum)) * k], axis=1)
    sol = _hdot(_unit_lower_inverse(lower), rhs, NN)
    u0, w = sol[:, :HEAD_DIM], sol[:, HEAD_DIM:]
    qk = _bdot(q, k, NT) * jnp.exp(jnp.where(cc <= rr, diff, ninf))
    u = u0 - _bdot(w, st, NN)
    o = _bdot(q * jnp.exp(cum), st, NN) + _bdot(qk, u, NN)
    st_new = st * jnp.exp(total) + _bdot(k * jnp.exp(total - cum), u, TN)
    return _gated_rmsnorm(o, hz, nw), st_new


def _rows(c):
    return pl.ds(pl.multiple_of(c * CHUNK, CHUNK), CHUNK)


def _scan_dims(s, d):
    h, n = d // HEAD_DIM, s // CHUNK
    ncb = min(n, 8)
    return h, n, ncb, n // ncb, ncb * CHUNK


def _scan_specs(t, nblk, ncb, reverse):
    blk = (lambda j: nblk - 1 - j) if reverse else (lambda j: j)
    grp = lambda g: pl.BlockSpec((g, t, HEAD_DIM), lambda h, j: (0, blk(j), h))
    one = lambda g: pl.BlockSpec((None, t, HEAD_DIM), lambda h, j: (g, blk(j), h))
    col = lambda: pl.BlockSpec((t, HEAD_DIM), lambda h, j: (blk(j), h))
    vec = lambda: pl.BlockSpec((1, HEAD_DIM), lambda h, j: (0, h))
    shared = lambda: pl.BlockSpec((1, HEAD_DIM), lambda h, j: (0, 0))
    state = lambda: pl.BlockSpec((1, ncb, HEAD_DIM, HEAD_DIM), lambda h, j: (h, blk(j), 0, 0))
    tok = lambda: pl.BlockSpec((1, t, 1), lambda h, j: (h, blk(j), 0))
    par = lambda: pl.BlockSpec((1, 1, 1), lambda h, j: (h, 0, 0))
    return dict(grp=grp, one=one, col=col, vec=vec, shared=shared, state=state, tok=tok, par=par)


def _scan_params():
    return pltpu.CompilerParams(dimension_semantics=("arbitrary", "arbitrary"), vmem_limit_bytes=VMEM_LIMIT)


def _state_scratch():
    return [pltpu.VMEM((HEAD_DIM, HEAD_DIM), f32)]


def _hgrn2_fwd(h4, lb, nw):
    _, s, d = h4.shape
    h, n, ncb, nblk, t = _scan_dims(s, d)
    sp = _scan_specs(t, nblk, ncb, False)

    def body(h_ref, lb_ref, nw_ref, y_ref, st_ref, st_scr):
        @pl.when(pl.program_id(1) == 0)
        def _():
            st_scr[...] = jnp.zeros_like(st_scr)

        def step(c, carry):
            r = _rows(c)
            st = st_scr[...]
            st_ref[0, c] = st
            y, st_new = _hgrn2_chunk(h_ref[0, r, :], h_ref[1, r, :], h_ref[2, r, :], h_ref[3, r, :],
                                     lb_ref[...], nw_ref[...], st)
            y_ref[r, :] = y.astype(y_ref.dtype)
            st_scr[...] = st_new
            return carry

        lax.fori_loop(0, ncb, step, 0)

    return pl.pallas_call(
        body, grid=(h, nblk), name="hgrn2_fwd",
        in_specs=[sp["grp"](4), sp["vec"](), sp["shared"]()],
        out_specs=[sp["col"](), sp["state"]()],
        out_shape=[jax.ShapeDtypeStruct((s, d), bf16), jax.ShapeDtypeStruct((h, n, HEAD_DIM, HEAD_DIM), f32)],
        scratch_shapes=_state_scratch(), compiler_params=_scan_params())(h4, lb, nw)


def _hgrn2_bwd(h4, lb, nw, states, dy):
    _, s, d = h4.shape
    h, n, ncb, nblk, t = _scan_dims(s, d)
    sp = _scan_specs(t, nblk, ncb, True)

    def body(h_ref, lb_ref, nw_ref, st_ref, dy_ref, dh_ref, dlb_ref, dnw_ref, dst_scr):
        first_block = pl.program_id(1) == 0

        @pl.when(first_block)
        def _():
            dst_scr[...] = jnp.zeros_like(dst_scr)
            dlb_ref[...] = jnp.zeros_like(dlb_ref)

        @pl.when(first_block & (pl.program_id(0) == 0))
        def _():
            dnw_ref[...] = jnp.zeros_like(dnw_ref)

        def step(i, carry):
            c = ncb - 1 - i
            r = _rows(c)
            args = (h_ref[0, r, :], h_ref[1, r, :], h_ref[2, r, :], h_ref[3, r, :], lb_ref[...], nw_ref[...], st_ref[0, c])
            _, vjp = jax.vjp(_hgrn2_chunk, *args)
            dq, df, di, dz, dlb, dnw, dst = vjp((dy_ref[r, :].astype(f32), dst_scr[...]))
            for g, val in enumerate((dq, df, di, dz)):
                dh_ref[g, r, :] = val.astype(dh_ref.dtype)
            dlb_ref[...] += dlb
            dnw_ref[...] += dnw
            dst_scr[...] = dst
            return carry

        lax.fori_loop(0, ncb, step, 0)

    return pl.pallas_call(
        body, grid=(h, nblk), name="hgrn2_bwd",
        in_specs=[sp["grp"](4), sp["vec"](), sp["shared"](), sp["state"](), sp["col"]()],
        out_specs=[sp["grp"](4), sp["vec"](), sp["shared"]()],
        out_shape=[jax.ShapeDtypeStruct((4, s, d), bf16), jax.ShapeDtypeStruct((1, d), f32),
                   jax.ShapeDtypeStruct((1, HEAD_DIM), f32)],
        scratch_shapes=_state_scratch(), compiler_params=_scan_params())(h4, lb, nw, states, dy)


def _gdn_fwd(c3, h4, bl, al, alog, dtb, nw):
    _, s, d = c3.shape
    h, n, ncb, nblk, t = _scan_dims(s, d)
    sp = _scan_specs(t, nblk, ncb, False)

    def body(c_ref, hz_ref, bl_ref, al_ref, alog_ref, dtb_ref, nw_ref, y_ref, st_ref, st_scr):
        @pl.when(pl.program_id(1) == 0)
        def _():
            st_scr[...] = jnp.zeros_like(st_scr)

        def step(c, carry):
            r = _rows(c)
            st = st_scr[...]
            st_ref[0, c] = st
            y, st_new = _gdn_chunk(c_ref[0, r, :], c_ref[1, r, :], c_ref[2, r, :], hz_ref[r, :], bl_ref[0, r, :],
                                   al_ref[0, r, :], alog_ref[0], dtb_ref[0], nw_ref[...], st)
            y_ref[r, :] = y.astype(y_ref.dtype)
            st_scr[...] = st_new
            return carry

        lax.fori_loop(0, ncb, step, 0)

    return pl.pallas_call(
        body, grid=(h, nblk), name="gdn_fwd",
        in_specs=[sp["grp"](3), sp["one"](3), sp["tok"](), sp["tok"](), sp["par"](), sp["par"](), sp["shared"]()],
        out_specs=[sp["col"](), sp["state"]()],
        out_shape=[jax.ShapeDtypeStruct((s, d), bf16), jax.ShapeDtypeStruct((h, n, HEAD_DIM, HEAD_DIM), f32)],
        scratch_shapes=_state_scratch(), compiler_params=_scan_params())(c3, h4, bl, al, alog, dtb, nw)


def _gdn_bwd(c3, h4, bl, al, alog, dtb, nw, states, dy):
    _, s, d = c3.shape
    h, n, ncb, nblk, t = _scan_dims(s, d)
    sp = _scan_specs(t, nblk, ncb, True)

    def body(c_ref, hz_ref, bl_ref, al_ref, alog_ref, dtb_ref, nw_ref, st_ref, dy_ref,
             dc_ref, dz_ref, dbl_ref, dal_ref, dalog_ref, ddtb_ref, dnw_ref, dst_scr):
        first_block = pl.program_id(1) == 0

        @pl.when(first_block)
        def _():
            dst_scr[...] = jnp.zeros_like(dst_scr)
            dalog_ref[...] = jnp.zeros_like(dalog_ref)
            ddtb_ref[...] = jnp.zeros_like(ddtb_ref)

        @pl.when(first_block & (pl.program_id(0) == 0))
        def _():
            dnw_ref[...] = jnp.zeros_like(dnw_ref)

        def step(i, carry):
            c = ncb - 1 - i
            r = _rows(c)
            args = (c_ref[0, r, :], c_ref[1, r, :], c_ref[2, r, :], hz_ref[r, :], bl_ref[0, r, :], al_ref[0, r, :],
                    alog_ref[0], dtb_ref[0], nw_ref[...], st_ref[0, c])
            _, vjp = jax.vjp(_gdn_chunk, *args)
            dq, dk, dv, dz, dbl, dal, dalog, ddtb, dnw, dst = vjp((dy_ref[r, :].astype(f32), dst_scr[...]))
            for g, val in enumerate((dq, dk, dv)):
                dc_ref[g, r, :] = val
            dz_ref[r, :] = dz.astype(dz_ref.dtype)
            dbl_ref[0, r, :] = dbl
            dal_ref[0, r, :] = dal
            dalog_ref[0] += dalog
            ddtb_ref[0] += ddtb
            dnw_ref[...] += dnw
            dst_scr[...] = dst
            return carry

        lax.fori_loop(0, ncb, step, 0)

    tokens = jax.ShapeDtypeStruct((h, s, 1), f32)
    heads = jax.ShapeDtypeStruct((h, 1, 1), f32)
    return pl.pallas_call(
        body, grid=(h, nblk), name="gdn_bwd",
        in_specs=[sp["grp"](3), sp["one"](3), sp["tok"](), sp["tok"](), sp["par"](), sp["par"](), sp["shared"](),
                  sp["state"](), sp["col"]()],
        out_specs=[sp["grp"](3), sp["col"](), sp["tok"](), sp["tok"](), sp["par"](), sp["par"](), sp["shared"]()],
        out_shape=[jax.ShapeDtypeStruct((3, s, d), f32), jax.ShapeDtypeStruct((s, d), bf16), tokens, tokens, heads, heads,
                   jax.ShapeDtypeStruct((1, HEAD_DIM), f32)],
        scratch_shapes=_state_scratch(), compiler_params=_scan_params())(c3, h4, bl, al, alog, dtb, nw, states, dy)


def _tile(n, pref):
    return n if n <= pref else pref


def _mm(a, b, mode, *, name, out_dtype=f32, add=None, add_scale=1.0, b_g0=0, groups=None):
    squeeze = a.ndim == 2 and b.ndim == 2
    a3 = a if a.ndim == 3 else a[None]
    b3 = b if b.ndim == 3 else b[None]
    if mode == "nt":
        g = groups or a3.shape[0]
        (_, m, r), (_, n, _) = a3.shape, b3.shape
    elif mode == "nn":
        g = groups or b3.shape[0]
        (_, m, r), (_, _, n) = a3.shape, b3.shape
    else:
        g = groups or b3.shape[0]
        (_, r, m), (_, _, n) = a3.shape, b3.shape
    tm, tn, tr = _tile(m, 1024), _tile(n, 1024), _tile(r, 512)
    nr = r // tr
    a_grouped = a3.shape[0] > 1
    dn = {"nn": NN, "nt": NT, "tn": TN}[mode]

    if mode == "nt":
        grid = (m // tm, n // tn, g * nr)
        last = g * nr - 1
        kax = 2
        a_spec = pl.BlockSpec((None, tm, tr), lambda i, j, k: (k // nr, i, k % nr))
        b_spec = pl.BlockSpec((None, tn, tr), lambda i, j, k: (b_g0 + k // nr, j, k % nr))
        o_spec = pl.BlockSpec((tm, tn), lambda i, j, k: (i, j))
        add_spec = pl.BlockSpec((tm, tn), lambda i, j, k: (i, j))
        out_shape = jax.ShapeDtypeStruct((m, n), out_dtype)
        sem = ("parallel", "parallel", "arbitrary")
    else:
        grid = (g, m // tm, n // tn, nr)
        last = nr - 1
        kax = 3
        ag = (lambda q: q) if a_grouped else (lambda q: 0)
        if mode == "nn":
            a_spec = pl.BlockSpec((None, tm, tr), lambda q, i, j, k: (ag(q), i, k))
        else:
            a_spec = pl.BlockSpec((None, tr, tm), lambda q, i, j, k: (ag(q), k, i))
        b_spec = pl.BlockSpec((None, tr, tn), lambda q, i, j, k: (b_g0 + q, k, j))
        o_spec = pl.BlockSpec((None, tm, tn), lambda q, i, j, k: (q, i, j))
        add_spec = None
        out_shape = jax.ShapeDtypeStruct((g, m, n), out_dtype)
        sem = ("parallel", "parallel", "parallel", "arbitrary")
        assert add is None

    def body(*refs):
        if add is None:
            a_ref, b_ref, o_ref, acc = refs
        else:
            a_ref, b_ref, add_ref, o_ref, acc = refs
        k = pl.program_id(kax)

        @pl.when(k == 0)
        def _():
            acc[...] = jnp.zeros_like(acc)

        acc[...] += _bdot(a_ref[...], b_ref[...], dn)

        @pl.when(k == last)
        def _():
            res = acc[...]
            if add is not None:
                res = res + add_scale * add_ref[...]
            o_ref[...] = res.astype(o_ref.dtype)

    operands = (a3, b3) if add is None else (a3, b3, add)
    in_specs = [a_spec, b_spec] if add is None else [a_spec, b_spec, add_spec]
    out = pl.pallas_call(
        body, grid=grid, name=name, in_specs=in_specs, out_specs=o_spec, out_shape=out_shape,
        scratch_shapes=[pltpu.VMEM((tm, tn), f32)],
        compiler_params=pltpu.CompilerParams(dimension_semantics=sem, vmem_limit_bytes=VMEM_LIMIT))(*operands)
    return out[0] if (squeeze and mode != "nt") else out


def _row_block(rows, cols, target_bytes=1 << 20):
    if rows <= SUBLANES:
        return rows
    tr = SUBLANES
    while tr * 2 <= rows and tr * 2 * cols * 4 <= target_bytes and rows % (tr * 2) == 0:
        tr *= 2
    return tr


def _ew_params(n_axes=1):
    return pltpu.CompilerParams(dimension_semantics=("parallel",) * n_axes, vmem_limit_bytes=VMEM_LIMIT)


def _cast(x, dtype, name):
    r, c = x.shape
    tr = _row_block(r, c, 2 << 20)
    tr = max(tr, 16) if r >= 16 else tr
    spec = pl.BlockSpec((tr, c), lambda i: (i, 0))

    def body(x_ref, o_ref):
        o_ref[...] = x_ref[...].astype(dtype)

    return pl.pallas_call(body, grid=(r // tr,), name=name, in_specs=[spec], out_specs=spec,
                          out_shape=jax.ShapeDtypeStruct((r, c), dtype), compiler_params=_ew_params())(x)


def _ln_stats(r):
    mu = jnp.mean(r, axis=-1, keepdims=True)
    var = jnp.mean(jnp.square(r - mu), axis=-1, keepdims=True)
    return mu, lax.rsqrt(var + LN_EPS)


def _ln_fwd(x, yo, g, b):
    s, d = x.shape
    tr = _row_block(s, d, 2 << 20)
    tr = max(tr, 16)
    big = pl.BlockSpec((tr, d), lambda i: (i, 0))
    vec = pl.BlockSpec((1, d), lambda i: (0, 0))

    def body(x_ref, yo_ref, g_ref, b_ref, o_ref, ob_ref):
        r = DEEPNORM_ALPHA * x_ref[...] + yo_ref[...]
        mu, rstd = _ln_stats(r)
        y = (r - mu) * rstd * g_ref[...] + b_ref[...]
        o_ref[...] = y
        ob_ref[...] = y.astype(bf16)

    return pl.pallas_call(
        body, grid=(s // tr,), name="ln_fwd", in_specs=[big, big, vec, vec], out_specs=[big, big],
        out_shape=[jax.ShapeDtypeStruct((s, d), f32), jax.ShapeDtypeStruct((s, d), bf16)],
        compiler_params=_ew_params())(x, yo, g, b)


def _ln_bwd(x, yo, g, *, dy=None, b=None, target=None, name):
    s, d = x.shape
    with_loss = target is not None
    tr = max(_row_block(s, d, 2 << 20), 16)
    big = pl.BlockSpec((tr, d), lambda i: (i, 0))
    vec = pl.BlockSpec((1, d), lambda i: (0, 0))
    lane = pl.BlockSpec((1, LANES), lambda i: (0, 0))

    def body(*refs):
        if with_loss:
            x_ref, yo_ref, g_ref, b_ref, t_ref, dr_ref, drb_ref, dg_ref, db_ref, loss_ref = refs
        else:
            x_ref, yo_ref, g_ref, dy_ref, dr_ref, drb_ref, dg_ref, db_ref = refs

        @pl.when(pl.program_id(0) == 0)
        def _():
            dg_ref[...] = jnp.zeros_like(dg_ref)
            db_ref[...] = jnp.zeros_like(db_ref)
            if with_loss:
                loss_ref[...] = jnp.zeros_like(loss_ref)

        r = DEEPNORM_ALPHA * x_ref[...] + yo_ref[...]
        mu, rstd = _ln_stats(r)
        xhat = (r - mu) * rstd
        if with_loss:
            err = xhat * g_ref[...] + b_ref[...] - t_ref[...]
            loss_ref[...] += jnp.sum(err * err) * (0.5 / d)
            dyv = err * (1.0 / d)
        else:
            dyv = dy_ref[...]
        dxhat = dyv * g_ref[...]
        m1 = jnp.mean(dxhat, axis=-1, keepdims=True)
        m2 = jnp.mean(dxhat * xhat, axis=-1, keepdims=True)
        dr = rstd * (dxhat - m1 - xhat * m2)
        dr_ref[...] = dr
        drb_ref[...] = dr.astype(bf16)
        dg_ref[...] += jnp.sum(dyv * xhat, axis=0, keepdims=True)
        db_ref[...] += jnp.sum(dyv, axis=0, keepdims=True)

    operands = (x, yo, g, b, target) if with_loss else (x, yo, g, dy)
    in_specs = [big, big, vec, vec, big] if with_loss else [big, big, vec, big]
    out_specs = [big, big, vec, vec] + ([lane] if with_loss else [])
    out_shape = [jax.ShapeDtypeStruct((s, d), f32), jax.ShapeDtypeStruct((s, d), bf16),
                 jax.ShapeDtypeStruct((1, d), f32), jax.ShapeDtypeStruct((1, d), f32)]
    if with_loss:
        out_shape.append(jax.ShapeDtypeStruct((1, LANES), f32))
    return pl.pallas_call(
        body, grid=(s // tr,), name=name, in_specs=in_specs, out_specs=out_specs, out_shape=out_shape,
        compiler_params=pltpu.CompilerParams(dimension_semantics=("arbitrary",), vmem_limit_bytes=VMEM_LIMIT))(*operands)


def _lower_bound(a_lb):
    _, d = a_lb.shape

    def body(a_ref, o_ref):
        a = a_ref[...]
        ex = jnp.exp(a - jnp.max(a, axis=0, keepdims=True))
        p = ex / jnp.sum(ex, axis=0, keepdims=True)
        o_ref[...] = jnp.sum(jnp.where(_iota(p.shape, 0) == 0, p, 0.0), axis=0, keepdims=True)

    return pl.pallas_call(body, name="lower_bound", in_specs=[VMEM], out_specs=VMEM,
                          out_shape=jax.ShapeDtypeStruct((1, d), f32))(a_lb)


def _shift_rows(ext, k, rows):
    return pltpu.roll(ext, k, axis=0)[SUBLANES:SUBLANES + rows]


def _conv_tiles(s, d):
    return _tile(s, 512), _tile(d, 512)


def _conv_fwd(h4, w):
    _, s, d = h4.shape
    tr, tc = _conv_tiles(s, d)
    nj = d // tc
    hb = tr // SUBLANES
    cur = pl.BlockSpec((None, tr, tc), lambda g, j, i: (g, i, j))
    prev = pl.BlockSpec((None, SUBLANES, tc), lambda g, j, i: (g, jnp.maximum(i * hb - 1, 0), j))
    wsp = pl.BlockSpec((CONV_WIDTH, tc), lambda g, j, i: (0, g * nj + j))

    def body(x_ref, p_ref, w_ref, o_ref):
        x = x_ref[...]
        halo = jnp.where(pl.program_id(2) == 0, 0.0, p_ref[...])
        ext = jnp.concatenate([halo, x], axis=0)
        acc = w_ref[CONV_WIDTH - 1:CONV_WIDTH, :] * x
        for k in range(1, CONV_WIDTH):
            acc = acc + w_ref[CONV_WIDTH - 1 - k:CONV_WIDTH - k, :] * _shift_rows(ext, k, tr)
        o_ref[...] = acc

    return pl.pallas_call(
        body, grid=(3, nj, s // tr), name="conv_fwd", in_specs=[cur, prev, wsp], out_specs=cur,
        out_shape=jax.ShapeDtypeStruct((3, s, d), f32), compiler_params=_ew_params(3))(h4, h4, w)


def _conv_bwd(dy3, h4, w):
    _, s, d = dy3.shape
    tr, tc = _conv_tiles(s, d)
    nj = d // tc
    hb = tr // SUBLANES
    ni = s // tr
    cur = pl.BlockSpec((None, tr, tc), lambda g, j, i: (g, i, j))
    prev = pl.BlockSpec((None, SUBLANES, tc), lambda g, j, i: (g, jnp.maximum(i * hb - 1, 0), j))
    nxt = pl.BlockSpec((None, SUBLANES, tc), lambda g, j, i: (g, jnp.minimum((i + 1) * hb, s // SUBLANES - 1), j))
    wsp = pl.BlockSpec((CONV_WIDTH, tc), lambda g, j, i: (0, g * nj + j))

    def body(dy_ref, dn_ref, x_ref, p_ref, w_ref, dx_ref, dw_ref):
        i = pl.program_id(2)

        @pl.when(i == 0)
        def _():
            dw_ref[...] = jnp.zeros_like(dw_ref)

        dy = dy_ref[...]
        x = x_ref[...]
        tail = jnp.where(i == ni - 1, 0.0, dn_ref[...])
        dext = jnp.concatenate([dy, tail], axis=0)
        halo = jnp.where(i == 0, 0.0, p_ref[...])
        xext = jnp.concatenate([halo, x], axis=0)
        acc = w_ref[CONV_WIDTH - 1:CONV_WIDTH, :] * dy
        dw_ref[CONV_WIDTH - 1:CONV_WIDTH, :] += jnp.sum(dy * x, axis=0, keepdims=True)
        for k in range(1, CONV_WIDTH):
            ahead = pltpu.roll(dext, tr + SUBLANES - k, axis=0)[:tr]
            acc = acc + w_ref[CONV_WIDTH - 1 - k:CONV_WIDTH - k, :] * ahead
            dw_ref[CONV_WIDTH - 1 - k:CONV_WIDTH - k, :] += jnp.sum(dy * _shift_rows(xext, k, tr), axis=0, keepdims=True)
        dx_ref[...] = acc.astype(dx_ref.dtype)

    return pl.pallas_call(
        body, grid=(3, nj, ni), name="conv_bwd", in_specs=[cur, nxt, cur, prev, wsp], out_specs=[cur, wsp],
        out_shape=[jax.ShapeDtypeStruct((3, s, d), bf16), jax.ShapeDtypeStruct((CONV_WIDTH, 3 * d), f32)],
        compiler_params=pltpu.CompilerParams(dimension_semantics=("parallel", "parallel", "arbitrary"),
                                             vmem_limit_bytes=VMEM_LIMIT))(dy3, dy3, h4, h4, w)


def _chip_sum(halves, other, sel, name):
    _, r2, c = other.shape
    tr = _row_block(r2, c)
    own = pl.BlockSpec((None, tr, c), lambda p, i, s_ref: (2 * p + s_ref[0], i, 0))
    blk = pl.BlockSpec((None, tr, c), lambda p, i, s_ref: (p, i, 0))

    def body(s_ref, a_ref, b_ref, o_ref):
        o_ref[...] = a_ref[...] + b_ref[...]

    return pl.pallas_call(
        body, name=name,
        grid_spec=pltpu.PrefetchScalarGridSpec(num_scalar_prefetch=1, grid=(N_CHIPS, r2 // tr), in_specs=[own, blk],
                                               out_specs=blk),
        out_shape=jax.ShapeDtypeStruct(other.shape, f32), compiler_params=_ew_params(2))(sel, halves, other)


def _final_sum(chip_sums, others, sel, name):
    _, r2, c = others.shape
    tr = _row_block(r2, c)
    own = pl.BlockSpec((None, tr, c), lambda i, s_ref: (s_ref[1], i, 0))
    slot = lambda j: pl.BlockSpec((None, tr, c), lambda i, s_ref: (j, i, 0))

    def body(s_ref, a_ref, b0_ref, b1_ref, b2_ref, o_ref):
        o_ref[...] = ((a_ref[...] + b0_ref[...]) + b1_ref[...]) + b2_ref[...]

    return pl.pallas_call(
        body, name=name,
        grid_spec=pltpu.PrefetchScalarGridSpec(num_scalar_prefetch=1, grid=(r2 // tr,),
                                               in_specs=[own, slot(0), slot(1), slot(2)],
                                               out_specs=pl.BlockSpec((tr, c), lambda i, s_ref: (i, 0))),
        out_shape=jax.ShapeDtypeStruct((r2, c), f32), compiler_params=_ew_params())(sel, chip_sums, others, others, others)


def _adamw(w, g, m, v, name):
    r, c = w.shape
    tr = _row_block(r, c)
    spec = pl.BlockSpec((tr, c), lambda i: (i, 0))

    def body(w_ref, g_ref, m_ref, v_ref, d_ref, nm_ref, nv_ref):
        gv = g_ref[...]
        nm = ADAM_B1 * m_ref[...] + (1.0 - ADAM_B1) * gv
        nv = ADAM_B2 * v_ref[...] + (1.0 - ADAM_B2) * jnp.square(gv)
        m_hat = nm / (1.0 - ADAM_B1 ** ADAM_STEP)
        v_hat = nv / (1.0 - ADAM_B2 ** ADAM_STEP)
        d_ref[...] = -ADAM_LR * (m_hat / (jnp.sqrt(v_hat) + ADAM_EPS) + ADAM_WD * w_ref[...])
        nm_ref[...] = nm
        nv_ref[...] = nv

    out = jax.ShapeDtypeStruct((r, c), f32)
    return pl.pallas_call(body, grid=(r // tr,), name=name, in_specs=[spec] * 4, out_specs=[spec] * 3,
                          out_shape=[out, out, out], compiler_params=_ew_params())(w, g, m, v)


def _place():
    return lax.axis_index("x"), lax.axis_index("y"), lax.axis_index("c")


def _chip_peers(x, y):
    out = []
    for fx, fy in ((1, 0), (0, 1), (1, 1)):
        px, py = x ^ fx, y ^ fy
        out.append((px, py, 2 * px + py))
    return out


def _all_gather(arrays, split):
    na = len(arrays)

    def body(*refs):
        ins, outs = refs[:na], refs[na:2 * na]
        send_sems, recv_sems, local_sems = refs[2 * na:]
        x, y, c = _place()
        me = 2 * x + y
        sibling = (x, y, 1 - c)
        peers = _chip_peers(x, y)

        def half(ref, a, which):
            rows = arrays[a].shape[0] // 2
            return ref.at[pl.ds(which * rows, rows), :]

        local = [pltpu.make_async_copy(ins[a], outs[a].at[me], local_sems.at[a]) for a in range(na)]
        for cp in local:
            cp.start()
        sends, fwds = [], []
        for a in range(na):
            for j, (px, py, _) in enumerate(peers):
                src = half(ins[a], a, c) if split[a] else ins[a]
                dst = half(outs[a].at[me], a, c) if split[a] else outs[a].at[me]
                cp = pltpu.make_async_remote_copy(src, dst, send_sems.at[a, j], recv_sems.at[a, j],
                                                  device_id=(px, py, c), device_id_type=MESH)
                cp.start()
                sends.append(cp)
        for a in range(na):
            for j, (px, py, pid) in enumerate(peers):
                dst = half(outs[a].at[pid], a, c) if split[a] else outs[a].at[pid]
                pltpu.make_async_remote_copy(dst, dst, send_sems.at[a, j], recv_sems.at[a, j],
                                             device_id=(px, py, c), device_id_type=MESH).wait_recv()
                if split[a]:
                    cp = pltpu.make_async_remote_copy(dst, dst, send_sems.at[a, 3 + j], recv_sems.at[a, 3 + j],
                                                      device_id=sibling, device_id_type=MESH)
                    cp.start()
                    fwds.append(cp)
        for a in range(na):
            if split[a]:
                for j, (_, _, pid) in enumerate(peers):
                    dst = half(outs[a].at[pid], a, 1 - c)
                    pltpu.make_async_remote_copy(dst, dst, send_sems.at[a, 3 + j], recv_sems.at[a, 3 + j],
                                                 device_id=sibling, device_id_type=MESH).wait_recv()
        for cp in sends + fwds:
            cp.wait_send()
        for cp in local:
            cp.wait()

    return pl.pallas_call(
        body, name="all_gather_weights", in_specs=[ANY] * na, out_specs=[ANY] * na,
        out_shape=[jax.ShapeDtypeStruct((N_CHIPS,) + a.shape, a.dtype) for a in arrays],
        scratch_shapes=[pltpu.SemaphoreType.DMA((na, 6)), pltpu.SemaphoreType.DMA((na, 6)), pltpu.SemaphoreType.DMA((na,))],
        )(*arrays)


def _swap_halves(arrays):
    na = len(arrays)

    def body(*refs):
        ins, outs = refs[:na], refs[na:2 * na]
        send_sems, recv_sems = refs[2 * na:]
        x, y, c = _place()
        copies = []
        for a in range(na):
            rows = arrays[a].shape[1] // 2
            src = ins[a].at[:, pl.ds((1 - c) * rows, rows), :]
            cp = pltpu.make_async_remote_copy(src, outs[a], send_sems.at[a], recv_sems.at[a],
                                              device_id=(x, y, 1 - c), device_id_type=MESH)
            cp.start()
            copies.append(cp)
        for cp in copies:
            cp.wait()

    return pl.pallas_call(
        body, name="grad_swap_halves", in_specs=[ANY] * na, out_specs=[ANY] * na,
        out_shape=[jax.ShapeDtypeStruct((a.shape[0], a.shape[1] // 2, a.shape[2]), a.dtype) for a in arrays],
        scratch_shapes=[pltpu.SemaphoreType.DMA((na,)), pltpu.SemaphoreType.DMA((na,))],
        )(*arrays)


def _scatter_blocks(arrays):
    na = len(arrays)

    def body(*refs):
        ins, outs = refs[:na], refs[na:2 * na]
        send_sems, recv_sems = refs[2 * na:]
        x, y, c = _place()
        copies = []
        for a in range(na):
            for j, (px, py, pid) in enumerate(_chip_peers(x, y)):
                cp = pltpu.make_async_remote_copy(ins[a].at[pid], outs[a].at[j], send_sems.at[a, j], recv_sems.at[a, j],
                                                  device_id=(px, py, c), device_id_type=MESH)
                cp.start()
                copies.append(cp)
        for cp in copies:
            cp.wait()

    return pl.pallas_call(
        body, name="grad_scatter_blocks", in_specs=[ANY] * na, out_specs=[ANY] * na,
        out_shape=[jax.ShapeDtypeStruct((3,) + a.shape[1:], a.dtype) for a in arrays],
        scratch_shapes=[pltpu.SemaphoreType.DMA((na, 3)), pltpu.SemaphoreType.DMA((na, 3))],
        )(*arrays)


def _join_halves(arrays):
    na = len(arrays)

    def body(*refs):
        ins, outs = refs[:na], refs[na:2 * na]
        send_sems, recv_sems, local_sems = refs[2 * na:]
        x, y, c = _place()
        copies = []
        for a in range(na):
            local = pltpu.make_async_copy(ins[a], outs[a].at[c], local_sems.at[a])
            local.start()
            cp = pltpu.make_async_remote_copy(ins[a], outs[a].at[c], send_sems.at[a], recv_sems.at[a],
                                              device_id=(x, y, 1 - c), device_id_type=MESH)
            cp.start()
            copies.append((local, cp, a))
        for local, cp, a in copies:
            cp.wait_send()
            pltpu.make_async_remote_copy(ins[a], outs[a].at[1 - c], send_sems.at[a], recv_sems.at[a],
                                         device_id=(x, y, 1 - c), device_id_type=MESH).wait_recv()
            local.wait()

    return pl.pallas_call(
        body, name="grad_join_halves", in_specs=[ANY] * na, out_specs=[ANY] * na,
        out_shape=[jax.ShapeDtypeStruct((2,) + a.shape, a.dtype) for a in arrays],
        scratch_shapes=[pltpu.SemaphoreType.DMA((na,)), pltpu.SemaphoreType.DMA((na,)), pltpu.SemaphoreType.DMA((na,))],
        )(*arrays)


SMALL_ROWS = 24


def _all_reduce_small(a_lb, dlb, pieces):
    d = a_lb.shape[1]
    npc = len(pieces)

    def body(*refs):
        alb_ref, dlb_ref = refs[:2]
        piece_refs = refs[2:2 + npc]
        out_ref, pack, slots, send_sems, recv_sems = refs[2 + npc:]
        x, y, c = _place()
        me = 4 * x + 2 * y + c
        pack[...] = jnp.zeros_like(pack)
        a = alb_ref[...]
        ex = jnp.exp(a - jnp.max(a, axis=0, keepdims=True))
        p = ex / jnp.sum(ex, axis=0, keepdims=True)
        first = (_iota(p.shape, 0) == 0).astype(f32)
        p0 = jnp.sum(p * first, axis=0, keepdims=True)
        pack[0:3, :] = p * dlb_ref[...] * (first - p0)
        for ref, (arr, r0) in zip(piece_refs, pieces):
            pack[r0:r0 + arr.shape[0], 0:arr.shape[1]] = ref[...]
        slots[me] = pack[...]
        copies = []
        for k in range(1, N_DEV):
            peer = (x ^ (k >> 2), y ^ ((k >> 1) & 1), c ^ (k & 1))
            cp = pltpu.make_async_remote_copy(pack, slots.at[me], send_sems.at[k - 1], recv_sems.at[k - 1],
                                              device_id=peer, device_id_type=MESH)
            cp.start()
            copies.append(cp)
        for k in range(1, N_DEV):
            src = (x ^ (k >> 2)) * 4 + (y ^ ((k >> 1) & 1)) * 2 + (c ^ (k & 1))
            pltpu.make_async_remote_copy(pack, slots.at[src], send_sems.at[k - 1], recv_sems.at[k - 1],
                                         device_id=(x, y, c), device_id_type=MESH).wait_recv()
        for cp in copies:
            cp.wait_send()
        acc = slots[0]
        for i in range(1, N_DEV):
            acc = acc + slots[i]
        out_ref[...] = acc

    return pl.pallas_call(
        body, name="all_reduce_small", in_specs=[VMEM] * (2 + npc), out_specs=VMEM,
        out_shape=jax.ShapeDtypeStruct((SMALL_ROWS, d), f32),
        scratch_shapes=[pltpu.VMEM((SMALL_ROWS, d), f32), pltpu.VMEM((N_DEV, SMALL_ROWS, d), f32),
                        pltpu.SemaphoreType.DMA((N_DEV - 1,)), pltpu.SemaphoreType.DMA((N_DEV - 1,))],
        compiler_params=pltpu.CompilerParams(vmem_limit_bytes=VMEM_LIMIT))(
            a_lb, dlb, *[p[0] for p in pieces])


def _reduce_scatter(grads):
    x, y, c = _place()
    sel = jnp.stack([c, 2 * x + y]).astype(jnp.int32)
    from_sibling = _swap_halves(grads)
    chip_sums = []
    for a, (g, t) in enumerate(zip(grads, from_sibling)):
        halves = g.reshape(2 * g.shape[0], g.shape[1] // 2, g.shape[2])
        chip_sums.append(_chip_sum(halves, t, sel, f"grad_chip_sum_{a}"))
    from_chips = _scatter_blocks(chip_sums)
    finals = [_final_sum(h, t, sel, f"grad_final_sum_{a}") for a, (h, t) in enumerate(zip(chip_sums, from_chips))]
    joined = _join_halves(finals)
    return [j.reshape(2 * j.shape[1], j.shape[2]) for j in joined]


def kernel(x, a_w_in, a_lower_bounds, a_norm_w, a_w_out, b_w_in, b_conv_w, b_a_log, b_dt_bias, b_norm_w, b_w_out, ln_g, ln_b, loss_target, m_a_w_in, m_a_lower_bounds, m_a_norm_w, m_a_w_out, m_b_w_in, m_b_conv_w, m_b_a_log, m_b_dt_bias, m_b_norm_w, m_b_w_out, m_ln_g, m_ln_b, v_a_w_in, v_a_lower_bounds, v_a_norm_w, v_a_w_out, v_b_w_in, v_b_conv_w, v_b_a_log, v_b_dt_bias, v_b_norm_w, v_b_w_out, v_ln_g, v_ln_b):
    _, s, d = x.shape
    h = d // HEAD_DIM
    wb = b_w_in.shape[2]
    x2, target = x[0], loss_target[0]
    chip = 2 * lax.axis_index("x") + lax.axis_index("y")

    shards = [_cast(a_w_in[0], bf16, "cast_a_w_in"), _cast(a_w_out[0], bf16, "cast_a_w_out"),
              _cast(b_w_in[0], bf16, "cast_b_w_in"), _cast(b_w_out[0], bf16, "cast_b_w_out")]
    wa4, wa_out_g, wb_g, wb_out_g, conv_g = _all_gather(shards + [b_conv_w[0]], [True] * 4 + [False])
    wa_out = wa_out_g.reshape(d, d)
    wb_out = wb_out_g.reshape(d, d)
    wb2d = wb_g.transpose(1, 0, 2).reshape(d, N_CHIPS * wb)
    wb4 = wb2d[:, :4 * d].reshape(d, 4, d).transpose(1, 0, 2)
    wba = jnp.pad(wb2d[:, 4 * d:], ((0, 0), (0, LANES - 2 * h)))
    conv_w = conv_g.transpose(1, 0, 2).reshape(CONV_WIDTH, 3 * d)

    x_bf = _cast(x2, bf16, "cast_x")
    lb = _lower_bound(a_lower_bounds)
    h4a = _mm(x_bf, wa4, "nn", name="a_in_fwd")
    ya, st_a = _hgrn2_fwd(h4a, lb, a_norm_w)
    yo_a = _mm(ya, wa_out, "nn", name="a_out_fwd")
    x1, x1_bf = _ln_fwd(x2, yo_a, ln_g[0:1], ln_b[0:1])
    h4b = _mm(x1_bf, wb4, "nn", name="b_in_fwd")
    ba = _mm(x1_bf, wba, "nn", name="b_gate_fwd")
    c3 = _conv_fwd(h4b, conv_w)
    bl = ba[:, :h].T.reshape(h, s, 1)
    al = ba[:, h:2 * h].T.reshape(h, s, 1)
    alog, dtb = b_a_log.reshape(h, 1, 1), b_dt_bias.reshape(h, 1, 1)
    yb, st_b = _gdn_fwd(c3, h4b, bl, al, alog, dtb, b_norm_w)
    yo_b = _mm(yb, wb_out, "nn", name="b_out_fwd")

    dr2, dr2_bf, dg2, db2, loss = _ln_bwd(x1, yo_b, ln_g[1:2], b=ln_b[1:2], target=target, name="ln2_loss_bwd")
    g_wb_out = _mm(yb, dr2_bf, "tn", name="b_out_dw")
    dyb = _mm(dr2_bf, wb_out, "nt", name="b_out_dx")
    dc3, dhz, dbl, dal, dalog, ddtb, dnw_b = _gdn_bwd(c3, h4b, bl, al, alog, dtb, b_norm_w, st_b, dyb)
    dh3, dconv = _conv_bwd(dc3, h4b, conv_w)
    dba = jnp.pad(jnp.concatenate([dbl.reshape(h, s).T, dal.reshape(h, s).T], axis=1), ((0, 0), (0, LANES - 2 * h)))
    g_wb3 = _mm(x1_bf, dh3, "tn", name="b_in_dw_qkv")
    g_wbz = _mm(x1_bf, dhz, "tn", name="b_in_dw_z")
    g_wba = _mm(x1_bf, dba, "tn", name="b_in_dw_gate")
    dx1 = _mm(dh3, wb4, "nt", name="b_in_dx_qkv", groups=3, add=dr2, add_scale=DEEPNORM_ALPHA)
    dx1 = _mm(dhz, wb4, "nt", name="b_in_dx_z", b_g0=3, groups=1, add=dx1)
    dx1 = _mm(dba, wba, "nt", name="b_in_dx_gate", add=dx1)
    dr1, dr1_bf, dg1, db1 = _ln_bwd(x2, yo_a, ln_g[0:1], dy=dx1, name="ln1_bwd")
    g_wa_out = _mm(ya, dr1_bf, "tn", name="a_out_dw")
    dya = _mm(dr1_bf, wa_out, "nt", name="a_out_dx")
    dh4, dlb, dnw_a = _hgrn2_bwd(h4a, lb, a_norm_w, st_a, dya)
    g_wa4 = _mm(x_bf, dh4, "tn", name="a_in_dw")
    grad_x = _mm(dh4, wa4, "nt", name="a_in_dx", add=dr1, add_scale=DEEPNORM_ALPHA)

    g_wb2d = jnp.concatenate([g_wb3[0], g_wb3[1], g_wb3[2], g_wbz, g_wba[:, :2 * h]], axis=1)
    g_wb4 = g_wb2d.reshape(d, N_CHIPS, wb).transpose(1, 0, 2)
    big = _reduce_scatter([g_wa4, g_wa_out.reshape(N_CHIPS, d // N_CHIPS, d), g_wb4, g_wb_out.reshape(N_CHIPS, d // N_CHIPS, d)])
    small = _all_reduce_small(a_lower_bounds, dlb, [
        (dg1, 3), (dg2, 4), (db1, 5), (db2, 6), (dnw_a, 7), (dnw_b, 8), (dalog.reshape(1, h), 9), (ddtb.reshape(1, h), 10),
        (loss, 11), (dconv.reshape(3 * CONV_WIDTH, d), 12)])
    loss_out = small[11, 0]
    cw = 3 * d // N_CHIPS
    g_conv = lax.dynamic_slice_in_dim(small[12:24].reshape(CONV_WIDTH, 3 * d), chip * cw, cw, axis=1)
    grads = {
        "a_w_in": big[0], "a_lower_bounds": small[0:3], "a_norm_w": small[7:8, :HEAD_DIM], "a_w_out": big[1],
        "b_w_in": big[2], "b_conv_w": g_conv, "b_a_log": small[9:10, :h], "b_dt_bias": small[10:11, :h],
        "b_norm_w": small[8:9, :HEAD_DIM], "b_w_out": big[3],
        "ln_g": jnp.concatenate([small[3:4], small[4:5]], axis=0), "ln_b": jnp.concatenate([small[5:6], small[6:7]], axis=0),
    }

    weights = dict(a_w_in=a_w_in, a_lower_bounds=a_lower_bounds, a_norm_w=a_norm_w, a_w_out=a_w_out, b_w_in=b_w_in,
                   b_conv_w=b_conv_w, b_a_log=b_a_log, b_dt_bias=b_dt_bias, b_norm_w=b_norm_w, b_w_out=b_w_out,
                   ln_g=ln_g, ln_b=ln_b)
    m_in = dict(a_w_in=m_a_w_in, a_lower_bounds=m_a_lower_bounds, a_norm_w=m_a_norm_w, a_w_out=m_a_w_out, b_w_in=m_b_w_in,
                b_conv_w=m_b_conv_w, b_a_log=m_b_a_log, b_dt_bias=m_b_dt_bias, b_norm_w=m_b_norm_w, b_w_out=m_b_w_out,
                ln_g=m_ln_g, ln_b=m_ln_b)
    v_in = dict(a_w_in=v_a_w_in, a_lower_bounds=v_a_lower_bounds, a_norm_w=v_a_norm_w, a_w_out=v_a_w_out, b_w_in=v_b_w_in,
                b_conv_w=v_b_conv_w, b_a_log=v_b_a_log, b_dt_bias=v_b_dt_bias, b_norm_w=v_b_norm_w, b_w_out=v_b_w_out,
                ln_g=v_ln_g, ln_b=v_ln_b)
    out_g, out_d, out_m, out_v = [], [], [], []
    for name, w in weights.items():
        shape2 = w.shape[-2:]
        g = grads[name].reshape(shape2)
        delta, nm, nv = _adamw(w.reshape(shape2), g, m_in[name].reshape(shape2), v_in[name].reshape(shape2), "adamw_" + name)
        out_g.append(g.reshape(w.shape))
        out_d.append(delta.reshape(w.shape))
        out_m.append(nm.reshape(w.shape))
        out_v.append(nv.reshape(w.shape))
    return (loss_out, grad_x.reshape(x.shape), *out_g, *out_d, *out_m, *out_v)
```

```python
import functools
import math

import jax
import jax.numpy as jnp
from jax import lax
from jax.experimental import pallas as pl
from jax.experimental.pallas import tpu as pltpu

f32 = jnp.float32
bf16 = jnp.bfloat16
HIGHEST = lax.Precision.HIGHEST

HEAD_DIM = 128
CHUNK = 64
SUB = 16
DEPTH = 2
DEEPNORM_ALPHA = (2.0 * DEPTH) ** 0.25
LN_EPS = 1e-5
RMS_EPS = 1e-6
L2_EPS = 1e-6
ADAM_LR, ADAM_B1, ADAM_B2, ADAM_EPS, ADAM_WD, ADAM_STEP = 0.001, 0.9, 0.999, 1e-08, 0.01, 10
CONV_WIDTH = 4
EXP_CLAMP = 40.0
VMEM_LIMIT = 56 << 20
MXU_DTYPE = bf16
SUBLANES = 8
LANES = 128
N_CHIPS = 4
N_DEV = 8

NN = (((1,), (0,)), ((), ()))
NT = (((1,), (1,)), ((), ()))
TN = (((0,), (0,)), ((), ()))
MESH = pl.DeviceIdType.MESH
ANY = pl.BlockSpec(memory_space=pl.ANY)
VMEM = pl.BlockSpec(memory_space=pltpu.VMEM)


def _bdot(a, b, dn):
    return lax.dot_general(a.astype(MXU_DTYPE), b.astype(MXU_DTYPE), dn, preferred_element_type=f32)


def _hdot(a, b, dn):
    return lax.dot_general(a, b, dn, precision=lax.Precision.HIGH, preferred_element_type=f32)


def _iota(shape, dim):
    return lax.broadcasted_iota(jnp.int32, shape, dim)


def _same_sub(a, b):
    shift = int(math.log2(SUB))
    return jnp.right_shift(a, shift) == jnp.right_shift(b, shift)


def _silu(x):
    return x * jax.nn.sigmoid(x)


def _gated_rmsnorm(o, hz, nw):
    return o * lax.rsqrt(jnp.mean(o * o, axis=-1, keepdims=True) + RMS_EPS) * nw * _silu(hz)


def _hgrn2_chunk(hq, hf, hi, hz, lb, nw, st):
    c = CHUNK
    q = _silu(hq)
    forget = lb + (1.0 - lb) * jax.nn.sigmoid(hf)
    k = 1.0 - forget
    g = jnp.log(forget)
    tri = (_iota((c, c), 1) <= _iota((c, c), 0)).astype(f32)
    cum = _hdot(tri, g, NN)
    total = jnp.sum(g, axis=0, keepdims=True)
    nb = c // SUB
    blk = [cum[SUB * j:SUB * (j + 1)] for j in range(nb)]
    r16 = _iota((SUB, HEAD_DIM), 0)

    def row(j, i):
        return jnp.sum(jnp.where(r16 == i, blk[j], 0.0), axis=0, keepdims=True)

    c15, c31, c47 = row(0, SUB - 1), row(1, SUB - 1), row(2, SUB - 1)
    z = jnp.zeros((SUB, HEAD_DIM), f32)
    e = jnp.exp
    q32 = jnp.concatenate([z, z, e(blk[2] - c31), e(blk[3] - c31)], axis=0)
    k32 = jnp.concatenate([e(c31 - blk[0]), e(c31 - blk[1]), z, z], axis=0)
    q16a = jnp.concatenate([z, e(blk[1] - c15), z, z], axis=0)
    k16a = jnp.concatenate([e(c15 - blk[0]), z, z, z], axis=0)
    q16b = jnp.concatenate([z, z, z, e(blk[3] - c47)], axis=0)
    k16b = jnp.concatenate([z, z, e(c47 - blk[2]), z], axis=0)
    dmid = jnp.concatenate([blk[j] - row(j, SUB // 2 - 1) for j in range(nb)], axis=0)
    qd = e(jnp.minimum(dmid, EXP_CLAMP))
    kd = e(jnp.minimum(-dmid, EXP_CLAMP))
    q2 = jnp.concatenate([q * q32, q * q16a, q * q16b], axis=1)
    k2 = jnp.concatenate([k * k32, k * k16a, k * k16b], axis=1)
    rr, cc = _iota((c, c), 0), _iota((c, c), 1)
    diag = _same_sub(rr, cc) & (cc <= rr)
    att = _bdot(q2, k2, NT) + jnp.where(diag, _bdot(q * qd, k * kd, NT), 0.0)
    o = _bdot(att, hi, NN) + _bdot(q * e(cum), st, NT)
    st_new = st * e(total) + _bdot(hi, k * e(total - cum), TN)
    return _gated_rmsnorm(o, hz, nw), st_new


def _unit_lower_inverse(lower):
    c = CHUNK
    rr, cc = _iota((c, c), 0), _iota((c, c), 1)
    eye = (rr == cc).astype(f32)
    same = _same_sub(rr, cc)
    ld = jnp.where(same, lower, 0.0)
    off = lower - ld
    x = eye - ld
    p = ld
    for _ in range(int(math.log2(SUB)) - 1):
        p = _hdot(p, p, NN)
        x = x + _hdot(x, p, NN)
    n = _hdot(x, off, NN)
    y = eye - n
    p = n
    for _ in range(int(math.log2(c // SUB)) - 1):
        p = _hdot(p, p, NN)
        y = y + _hdot(y, p, NN)
    return _hdot(y, x, NN)


def _gdn_chunk(cq, ck, cv, hz, bl, al, alog, dtb, nw, st):
    c = CHUNK
    sq, sk, v = _silu(cq), _silu(ck), _silu(cv)
    q = sq * lax.rsqrt(jnp.sum(sq * sq, axis=-1, keepdims=True) + L2_EPS) * (HEAD_DIM ** -0.5)
    k = sk * lax.rsqrt(jnp.sum(sk * sk, axis=-1, keepdims=True) + L2_EPS)
    beta = jax.nn.sigmoid(bl)
    xs = al + dtb
    pos = xs > 0.0
    g = -jnp.exp(alog) * (jnp.where(pos, xs, 0.0) + jnp.log(1.0 + jnp.exp(jnp.where(pos, -xs, xs))))
    rr, cc = _iota((c, c), 0), _iota((c, c), 1)
    tri = (cc <= rr).astype(f32)
    gc = jnp.broadcast_to(g, (c, c))
    cum_c = _hdot(tri, gc, NN)
    cum_r = _hdot(gc, (rr <= cc).astype(f32), TN)
    g128 = jnp.broadcast_to(g, (c, HEAD_DIM))
    cum = _hdot(tri, g128, NN)
    total = jnp.sum(g128, axis=0, keepdims=True)
    diff = cum_c - cum_r
    ninf = -jnp.inf
    lower = beta * _bdot(k, k, NT) * jnp.exp(jnp.where(cc < rr, diff, ninf))
    rhs = jnp.concatenate([beta * v, (beta * jnp.exp(cum)) * k], axis=1)
    sol = _hdot(_unit_lower_inverse(lower), rhs, NN)
    u0, w = sol[:, :HEAD_DIM], sol[:, HEAD_DIM:]
    qk = _bdot(q, k, NT) * jnp.exp(jnp.where(cc <= rr, diff, ninf))
    u = u0 - _bdot(w, st, NN)
    o = _bdot(q * jnp.exp(cum), st, NN) + _bdot(qk, u, NN)
    st_new = st * jnp.exp(total) + _bdot(k * jnp.exp(total - cum), u, TN)
    return _gated_rmsnorm(o, hz, nw), st_new


def _rows(c):
    return pl.ds(pl.multiple_of(c * CHUNK, CHUNK), CHUNK)


HEADS_PER_STEP = 4


def _scan_dims(s, d):
    h, n = d // HEAD_DIM, s // CHUNK
    ncb = min(n, 8)
    hp = min(h, HEADS_PER_STEP)
    return h, n, ncb, n // ncb, ncb * CHUNK, hp


def _scan_specs(t, nblk, ncb, hp, reverse):
    blk = (lambda j: nblk - 1 - j) if reverse else (lambda j: j)
    w = hp * HEAD_DIM
    grp = lambda g: pl.BlockSpec((g, t, w), lambda h, j: (0, blk(j), h))
    one = lambda g: pl.BlockSpec((None, t, w), lambda h, j: (g, blk(j), h))
    col = lambda: pl.BlockSpec((t, w), lambda h, j: (blk(j), h))
    vec = lambda: pl.BlockSpec((1, w), lambda h, j: (0, h))
    shared = lambda: pl.BlockSpec((1, HEAD_DIM), lambda h, j: (0, 0))
    state = lambda: pl.BlockSpec((hp, ncb, HEAD_DIM, HEAD_DIM), lambda h, j: (h, blk(j), 0, 0))
    tok = lambda: pl.BlockSpec((hp, t, 1), lambda h, j: (h, blk(j), 0))
    par = lambda: pl.BlockSpec((hp, 1, 1), lambda h, j: (h, 0, 0))
    return dict(grp=grp, one=one, col=col, vec=vec, shared=shared, state=state, tok=tok, par=par)


def _scan_params():
    return pltpu.CompilerParams(dimension_semantics=("arbitrary", "arbitrary"), vmem_limit_bytes=VMEM_LIMIT)


def _state_scratch(hp):
    return [pltpu.VMEM((hp, HEAD_DIM, HEAD_DIM), f32)]


def _lanes(i):
    return pl.ds(i * HEAD_DIM, HEAD_DIM)


def _stack_heads(per_head):
    return tuple(jnp.stack(vals) for vals in zip(*per_head))


def _hgrn2_fwd(h4, lb, nw):
    _, s, d = h4.shape
    h, n, ncb, nblk, t, hp = _scan_dims(s, d)
    sp = _scan_specs(t, nblk, ncb, hp, False)

    def body(h_ref, lb_ref, nw_ref, y_ref, st_ref, st_scr):
        @pl.when(pl.program_id(1) == 0)
        def _():
            st_scr[...] = jnp.zeros_like(st_scr)

        def step(c, carry):
            r = _rows(c)
            args = [(h_ref[0, r, _lanes(i)], h_ref[1, r, _lanes(i)], h_ref[2, r, _lanes(i)], h_ref[3, r, _lanes(i)],
                     lb_ref[:, _lanes(i)], nw_ref[...], st_scr[i]) for i in range(hp)]
            y, st_new = jax.vmap(_hgrn2_chunk)(*_stack_heads(args))
            for i, a in enumerate(args):
                st_ref[i, c] = a[-1]
                y_ref[r, _lanes(i)] = y[i].astype(y_ref.dtype)
                st_scr[i] = st_new[i]
            return carry

        lax.fori_loop(0, ncb, step, 0)

    return pl.pallas_call(
        body, grid=(h // hp, nblk), name="hgrn2_fwd",
        in_specs=[sp["grp"](4), sp["vec"](), sp["shared"]()],
        out_specs=[sp["col"](), sp["state"]()],
        out_shape=[jax.ShapeDtypeStruct((s, d), bf16), jax.ShapeDtypeStruct((h, n, HEAD_DIM, HEAD_DIM), f32)],
        scratch_shapes=_state_scratch(hp), compiler_params=_scan_params())(h4, lb, nw)


def _hgrn2_bwd(h4, lb, nw, states, dy):
    _, s, d = h4.shape
    h, n, ncb, nblk, t, hp = _scan_dims(s, d)
    sp = _scan_specs(t, nblk, ncb, hp, True)

    def body(h_ref, lb_ref, nw_ref, st_ref, dy_ref, dh_ref, dlb_ref, dnw_ref, dst_scr):
        first_block = pl.program_id(1) == 0

        @pl.when(first_block)
        def _():
            dst_scr[...] = jnp.zeros_like(dst_scr)
            dlb_ref[...] = jnp.zeros_like(dlb_ref)

        @pl.when(first_block & (pl.program_id(0) == 0))
        def _():
            dnw_ref[...] = jnp.zeros_like(dnw_ref)

        def step(k, carry):
            c = ncb - 1 - k
            r = _rows(c)
            args = [(h_ref[0, r, _lanes(i)], h_ref[1, r, _lanes(i)], h_ref[2, r, _lanes(i)], h_ref[3, r, _lanes(i)],
                     lb_ref[:, _lanes(i)], nw_ref[...], st_ref[i, c]) for i in range(hp)]
            cots = [(dy_ref[r, _lanes(i)].astype(f32), dst_scr[i]) for i in range(hp)]
            dq, df, di, dz, dlb, dnw, dst = jax.vjp(jax.vmap(_hgrn2_chunk), *_stack_heads(args))[1](_stack_heads(cots))
            dnw_sum = dnw_ref[...]
            for i in range(hp):
                for g, val in enumerate((dq, df, di, dz)):
                    dh_ref[g, r, _lanes(i)] = val[i].astype(dh_ref.dtype)
                dlb_ref[:, _lanes(i)] += dlb[i]
                dnw_sum = dnw_sum + dnw[i]
                dst_scr[i] = dst[i]
            dnw_ref[...] = dnw_sum
            return carry

        lax.fori_loop(0, ncb, step, 0)

    return pl.pallas_call(
        body, grid=(h // hp, nblk), name="hgrn2_bwd",
        in_specs=[sp["grp"](4), sp["vec"](), sp["shared"](), sp["state"](), sp["col"]()],
        out_specs=[sp["grp"](4), sp["vec"](), sp["shared"]()],
        out_shape=[jax.ShapeDtypeStruct((4, s, d), bf16), jax.ShapeDtypeStruct((1, d), f32),
                   jax.ShapeDtypeStruct((1, HEAD_DIM), f32)],
        scratch_shapes=_state_scratch(hp), compiler_params=_scan_params())(h4, lb, nw, states, dy)


def _gdn_fwd(c3, h4, bl, al, alog, dtb, nw):
    _, s, d = c3.shape
    h, n, ncb, nblk, t, hp = _scan_dims(s, d)
    sp = _scan_specs(t, nblk, ncb, hp, False)

    def body(c_ref, hz_ref, bl_ref, al_ref, alog_ref, dtb_ref, nw_ref, y_ref, st_ref, st_scr):
        @pl.when(pl.program_id(1) == 0)
        def _():
            st_scr[...] = jnp.zeros_like(st_scr)

        def step(c, carry):
            r = _rows(c)
            args = [(c_ref[0, r, _lanes(i)], c_ref[1, r, _lanes(i)], c_ref[2, r, _lanes(i)], hz_ref[r, _lanes(i)],
                     bl_ref[i, r, :], al_ref[i, r, :], alog_ref[i], dtb_ref[i], nw_ref[...], st_scr[i]) for i in range(hp)]
            y, st_new = jax.vmap(_gdn_chunk)(*_stack_heads(args))
            for i, a in enumerate(args):
                st_ref[i, c] = a[-1]
                y_ref[r, _lanes(i)] = y[i].astype(y_ref.dtype)
                st_scr[i] = st_new[i]
            return carry

        lax.fori_loop(0, ncb, step, 0)

    return pl.pallas_call(
        body, grid=(h // hp, nblk), name="gdn_fwd",
        in_specs=[sp["grp"](3), sp["one"](3), sp["tok"](), sp["tok"](), sp["par"](), sp["par"](), sp["shared"]()],
        out_specs=[sp["col"](), sp["state"]()],
        out_shape=[jax.ShapeDtypeStruct((s, d), bf16), jax.ShapeDtypeStruct((h, n, HEAD_DIM, HEAD_DIM), f32)],
        scratch_shapes=_state_scratch(hp), compiler_params=_scan_params())(c3, h4, bl, al, alog, dtb, nw)


def _gdn_bwd(c3, h4, bl, al, alog, dtb, nw, states, dy):
    _, s, d = c3.shape
    h, n, ncb, nblk, t, hp = _scan_dims(s, d)
    sp = _scan_specs(t, nblk, ncb, hp, True)

    def body(c_ref, hz_ref, bl_ref, al_ref, alog_ref, dtb_ref, nw_ref, st_ref, dy_ref,
             dc_ref, dz_ref, dbl_ref, dal_ref, dalog_ref, ddtb_ref, dnw_ref, dst_scr):
        first_block = pl.program_id(1) == 0

        @pl.when(first_block)
        def _():
            dst_scr[...] = jnp.zeros_like(dst_scr)
            dalog_ref[...] = jnp.zeros_like(dalog_ref)
            ddtb_ref[...] = jnp.zeros_like(ddtb_ref)

        @pl.when(first_block & (pl.program_id(0) == 0))
        def _():
            dnw_ref[...] = jnp.zeros_like(dnw_ref)

        def step(k, carry):
            c = ncb - 1 - k
            r = _rows(c)
            args = [(c_ref[0, r, _lanes(i)], c_ref[1, r, _lanes(i)], c_ref[2, r, _lanes(i)], hz_ref[r, _lanes(i)],
                     bl_ref[i, r, :], al_ref[i, r, :], alog_ref[i], dtb_ref[i], nw_ref[...], st_ref[i, c]) for i in range(hp)]
            cots = [(dy_ref[r, _lanes(i)].astype(f32), dst_scr[i]) for i in range(hp)]
            dq, dk, dv, dz, dbl, dal, dalog, ddtb, dnw, dst = jax.vjp(
                jax.vmap(_gdn_chunk), *_stack_heads(args))[1](_stack_heads(cots))
            dnw_sum = dnw_ref[...]
            for i in range(hp):
                for g, val in enumerate((dq, dk, dv)):
                    dc_ref[g, r, _lanes(i)] = val[i]
                dz_ref[r, _lanes(i)] = dz[i].astype(dz_ref.dtype)
                dbl_ref[i, r, :] = dbl[i]
                dal_ref[i, r, :] = dal[i]
                dalog_ref[i] += dalog[i]
                ddtb_ref[i] += ddtb[i]
                dnw_sum = dnw_sum + dnw[i]
                dst_scr[i] = dst[i]
            dnw_ref[...] = dnw_sum
            return carry

        lax.fori_loop(0, ncb, step, 0)

    tokens = jax.ShapeDtypeStruct((h, s, 1), f32)
    heads = jax.ShapeDtypeStruct((h, 1, 1), f32)
    return pl.pallas_call(
        body, grid=(h // hp, nblk), name="gdn_bwd",
        in_specs=[sp["grp"](3), sp["one"](3), sp["tok"](), sp["tok"](), sp["par"](), sp["par"](), sp["shared"](),
                  sp["state"](), sp["col"]()],
        out_specs=[sp["grp"](3), sp["col"](), sp["tok"](), sp["tok"](), sp["par"](), sp["par"](), sp["shared"]()],
        out_shape=[jax.ShapeDtypeStruct((3, s, d), f32), jax.ShapeDtypeStruct((s, d), bf16), tokens, tokens, heads, heads,
                   jax.ShapeDtypeStruct((1, HEAD_DIM), f32)],
        scratch_shapes=_state_scratch(hp), compiler_params=_scan_params())(c3, h4, bl, al, alog, dtb, nw, states, dy)


def _tile(n, pref):
    return n if n <= pref else pref


def _mm(a, b, mode, *, name, out_dtype=f32, add=None, add_scale=1.0, b_g0=0, groups=None):
    squeeze = a.ndim == 2 and b.ndim == 2
    a3 = a if a.ndim == 3 else a[None]
    b3 = b if b.ndim == 3 else b[None]
    if mode == "nt":
        g = groups or a3.shape[0]
        (_, m, r), (_, n, _) = a3.shape, b3.shape
    elif mode == "nn":
        g = groups or b3.shape[0]
        (_, m, r), (_, _, n) = a3.shape, b3.shape
    else:
        g = groups or b3.shape[0]
        (_, r, m), (_, _, n) = a3.shape, b3.shape
    tm, tn, tr = _tile(m, 1024), _tile(n, 1024), _tile(r, 512)
    nr = r // tr
    a_grouped = a3.shape[0] > 1
    dn = {"nn": NN, "nt": NT, "tn": TN}[mode]

    if mode == "nt":
        grid = (m // tm, n // tn, g * nr)
        last = g * nr - 1
        kax = 2
        a_spec = pl.BlockSpec((None, tm, tr), lambda i, j, k: (k // nr, i, k % nr))
        b_spec = pl.BlockSpec((None, tn, tr), lambda i, j, k: (b_g0 + k // nr, j, k % nr))
        o_spec = pl.BlockSpec((tm, tn), lambda i, j, k: (i, j))
        add_spec = pl.BlockSpec((tm, tn), lambda i, j, k: (i, j))
        out_shape = jax.ShapeDtypeStruct((m, n), out_dtype)
        sem = ("parallel", "parallel", "arbitrary")
    else:
        grid = (g, m // tm, n // tn, nr)
        last = nr - 1
        kax = 3
        ag = (lambda q: q) if a_grouped else (lambda q: 0)
        if mode == "nn":
            a_spec = pl.BlockSpec((None, tm, tr), lambda q, i, j, k: (ag(q), i, k))
        else:
            a_spec = pl.BlockSpec((None, tr, tm), lambda q, i, j, k: (ag(q), k, i))
        b_spec = pl.BlockSpec((None, tr, tn), lambda q, i, j, k: (b_g0 + q, k, j))
        o_spec = pl.BlockSpec((None, tm, tn), lambda q, i, j, k: (q, i, j))
        add_spec = None
        out_shape = jax.ShapeDtypeStruct((g, m, n), out_dtype)
        sem = ("parallel", "parallel", "parallel", "arbitrary")
        assert add is None

    def body(*refs):
        if add is None:
            a_ref, b_ref, o_ref, acc = refs
        else:
            a_ref, b_ref, add_ref, o_ref, acc = refs
        k = pl.program_id(kax)

        @pl.when(k == 0)
        def _():
            acc[...] = jnp.zeros_like(acc)

        acc[...] += _bdot(a_ref[...], b_ref[...], dn)

        @pl.when(k == last)
        def _():
            res = acc[...]
            if add is not None:
                res = res + add_scale * add_ref[...]
            o_ref[...] = res.astype(o_ref.dtype)

    operands = (a3, b3) if add is None else (a3, b3, add)
    in_specs = [a_spec, b_spec] if add is None else [a_spec, b_spec, add_spec]
    out = pl.pallas_call(
        body, grid=grid, name=name, in_specs=in_specs, out_specs=o_spec, out_shape=out_shape,
        scratch_shapes=[pltpu.VMEM((tm, tn), f32)],
        compiler_params=pltpu.CompilerParams(dimension_semantics=sem, vmem_limit_bytes=VMEM_LIMIT))(*operands)
    return out[0] if (squeeze and mode != "nt") else out


def _row_block(rows, cols, target_bytes=1 << 20):
    if rows <= SUBLANES:
        return rows
    tr = SUBLANES
    while tr * 2 <= rows and tr * 2 * cols * 4 <= target_bytes and rows % (tr * 2) == 0:
        tr *= 2
    return tr


def _ew_params(n_axes=1):
    return pltpu.CompilerParams(dimension_semantics=("parallel",) * n_axes, vmem_limit_bytes=VMEM_LIMIT)


def _cast(x, dtype, name):
    r, c = x.shape
    tr = _row_block(r, c, 2 << 20)
    tr = max(tr, 16) if r >= 16 else tr
    spec = pl.BlockSpec((tr, c), lambda i: (i, 0))

    def body(x_ref, o_ref):
        o_ref[...] = x_ref[...].astype(dtype)

    return pl.pallas_call(body, grid=(r // tr,), name=name, in_specs=[spec], out_specs=spec,
                          out_shape=jax.ShapeDtypeStruct((r, c), dtype), compiler_params=_ew_params())(x)


def _ln_stats(r):
    mu = jnp.mean(r, axis=-1, keepdims=True)
    var = jnp.mean(jnp.square(r - mu), axis=-1, keepdims=True)
    return mu, lax.rsqrt(var + LN_EPS)


def _ln_fwd(x, yo, g, b):
    s, d = x.shape
    tr = _row_block(s, d, 2 << 20)
    tr = max(tr, 16)
    big = pl.BlockSpec((tr, d), lambda i: (i, 0))
    vec = pl.BlockSpec((1, d), lambda i: (0, 0))

    def body(x_ref, yo_ref, g_ref, b_ref, o_ref, ob_ref):
        r = DEEPNORM_ALPHA * x_ref[...] + yo_ref[...]
        mu, rstd = _ln_stats(r)
        y = (r - mu) * rstd * g_ref[...] + b_ref[...]
        o_ref[...] = y
        ob_ref[...] = y.astype(bf16)

    return pl.pallas_call(
        body, grid=(s // tr,), name="ln_fwd", in_specs=[big, big, vec, vec], out_specs=[big, big],
        out_shape=[jax.ShapeDtypeStruct((s, d), f32), jax.ShapeDtypeStruct((s, d), bf16)],
        compiler_params=_ew_params())(x, yo, g, b)


def _ln_bwd(x, yo, g, *, dy=None, b=None, target=None, name):
    s, d = x.shape
    with_loss = target is not None
    tr = max(_row_block(s, d, 2 << 20), 16)
    big = pl.BlockSpec((tr, d), lambda i: (i, 0))
    vec = pl.BlockSpec((1, d), lambda i: (0, 0))
    lane = pl.BlockSpec((1, LANES), lambda i: (0, 0))

    def body(*refs):
        if with_loss:
            x_ref, yo_ref, g_ref, b_ref, t_ref, dr_ref, drb_ref, dg_ref, db_ref, loss_ref = refs
        else:
            x_ref, yo_ref, g_ref, dy_ref, dr_ref, drb_ref, dg_ref, db_ref = refs

        @pl.when(pl.program_id(0) == 0)
        def _():
            dg_ref[...] = jnp.zeros_like(dg_ref)
            db_ref[...] = jnp.zeros_like(db_ref)
            if with_loss:
                loss_ref[...] = jnp.zeros_like(loss_ref)

        r = DEEPNORM_ALPHA * x_ref[...] + yo_ref[...]
        mu, rstd = _ln_stats(r)
        xhat = (r - mu) * rstd
        if with_loss:
            err = xhat * g_ref[...] + b_ref[...] - t_ref[...]
            loss_ref[...] += jnp.sum(err * err) * (0.5 / d)
            dyv = err * (1.0 / d)
        else:
            dyv = dy_ref[...]
        dxhat = dyv * g_ref[...]
        m1 = jnp.mean(dxhat, axis=-1, keepdims=True)
        m2 = jnp.mean(dxhat * xhat, axis=-1, keepdims=True)
        dr = rstd * (dxhat - m1 - xhat * m2)
        dr_ref[...] = dr
        drb_ref[...] = dr.astype(bf16)
        dg_ref[...] += jnp.sum(dyv * xhat, axis=0, keepdims=True)
        db_ref[...] += jnp.sum(dyv, axis=0, keepdims=True)

    operands = (x, yo, g, b, target) if with_loss else (x, yo, g, dy)
    in_specs = [big, big, vec, vec, big] if with_loss else [big, big, vec, big]
    out_specs = [big, big, vec, vec] + ([lane] if with_loss else [])
    out_shape = [jax.ShapeDtypeStruct((s, d), f32), jax.ShapeDtypeStruct((s, d), bf16),
                 jax.ShapeDtypeStruct((1, d), f32), jax.ShapeDtypeStruct((1, d), f32)]
    if with_loss:
        out_shape.append(jax.ShapeDtypeStruct((1, LANES), f32))
    return pl.pallas_call(
        body, grid=(s // tr,), name=name, in_specs=in_specs, out_specs=out_specs, out_shape=out_shape,
        compiler_params=pltpu.CompilerParams(dimension_semantics=("arbitrary",), vmem_limit_bytes=VMEM_LIMIT))(*operands)


def _lower_bound(a_lb):
    _, d = a_lb.shape

    def body(a_ref, o_ref):
        a = a_ref[...]
        ex = jnp.exp(a - jnp.max(a, axis=0, keepdims=True))
        p = ex / jnp.sum(ex, axis=0, keepdims=True)
        o_ref[...] = jnp.sum(jnp.where(_iota(p.shape, 0) == 0, p, 0.0), axis=0, keepdims=True)

    return pl.pallas_call(body, name="lower_bound", in_specs=[VMEM], out_specs=VMEM,
                          out_shape=jax.ShapeDtypeStruct((1, d), f32))(a_lb)


def _shift_rows(ext, k, rows):
    return pltpu.roll(ext, k, axis=0)[SUBLANES:SUBLANES + rows]


def _conv_tiles(s, d):
    return _tile(s, 512), _tile(d, 512)


def _conv_fwd(h4, w):
    _, s, d = h4.shape
    tr, tc = _conv_tiles(s, d)
    nj = d // tc
    hb = tr // SUBLANES
    cur = pl.BlockSpec((None, tr, tc), lambda g, j, i: (g, i, j))
    prev = pl.BlockSpec((None, SUBLANES, tc), lambda g, j, i: (g, jnp.maximum(i * hb - 1, 0), j))
    wsp = pl.BlockSpec((CONV_WIDTH, tc), lambda g, j, i: (0, g * nj + j))

    def body(x_ref, p_ref, w_ref, o_ref):
        x = x_ref[...]
        halo = jnp.where(pl.program_id(2) == 0, 0.0, p_ref[...])
        ext = jnp.concatenate([halo, x], axis=0)
        acc = w_ref[CONV_WIDTH - 1:CONV_WIDTH, :] * x
        for k in range(1, CONV_WIDTH):
            acc = acc + w_ref[CONV_WIDTH - 1 - k:CONV_WIDTH - k, :] * _shift_rows(ext, k, tr)
        o_ref[...] = acc

    return pl.pallas_call(
        body, grid=(3, nj, s // tr), name="conv_fwd", in_specs=[cur, prev, wsp], out_specs=cur,
        out_shape=jax.ShapeDtypeStruct((3, s, d), f32), compiler_params=_ew_params(3))(h4, h4, w)


def _conv_bwd(dy3, h4, w):
    _, s, d = dy3.shape
    tr, tc = _conv_tiles(s, d)
    nj = d // tc
    hb = tr // SUBLANES
    ni = s // tr
    cur = pl.BlockSpec((None, tr, tc), lambda g, j, i: (g, i, j))
    prev = pl.BlockSpec((None, SUBLANES, tc), lambda g, j, i: (g, jnp.maximum(i * hb - 1, 0), j))
    nxt = pl.BlockSpec((None, SUBLANES, tc), lambda g, j, i: (g, jnp.minimum((i + 1) * hb, s // SUBLANES - 1), j))
    wsp = pl.BlockSpec((CONV_WIDTH, tc), lambda g, j, i: (0, g * nj + j))

    def body(dy_ref, dn_ref, x_ref, p_ref, w_ref, dx_ref, dw_ref):
        i = pl.program_id(2)

        @pl.when(i == 0)
        def _():
            dw_ref[...] = jnp.zeros_like(dw_ref)

        dy = dy_ref[...]
        x = x_ref[...]
        tail = jnp.where(i == ni - 1, 0.0, dn_ref[...])
        dext = jnp.concatenate([dy, tail], axis=0)
        halo = jnp.where(i == 0, 0.0, p_ref[...])
        xext = jnp.concatenate([halo, x], axis=0)
        acc = w_ref[CONV_WIDTH - 1:CONV_WIDTH, :] * dy
        dw_ref[CONV_WIDTH - 1:CONV_WIDTH, :] += jnp.sum(dy * x, axis=0, keepdims=True)
        for k in range(1, CONV_WIDTH):
            ahead = pltpu.roll(dext, tr + SUBLANES - k, axis=0)[:tr]
            acc = acc + w_ref[CONV_WIDTH - 1 - k:CONV_WIDTH - k, :] * ahead
            dw_ref[CONV_WIDTH - 1 - k:CONV_WIDTH - k, :] += jnp.sum(dy * _shift_rows(xext, k, tr), axis=0, keepdims=True)
        dx_ref[...] = acc.astype(dx_ref.dtype)

    return pl.pallas_call(
        body, grid=(3, nj, ni), name="conv_bwd", in_specs=[cur, nxt, cur, prev, wsp], out_specs=[cur, wsp],
        out_shape=[jax.ShapeDtypeStruct((3, s, d), bf16), jax.ShapeDtypeStruct((CONV_WIDTH, 3 * d), f32)],
        compiler_params=pltpu.CompilerParams(dimension_semantics=("parallel", "parallel", "arbitrary"),
                                             vmem_limit_bytes=VMEM_LIMIT))(dy3, dy3, h4, h4, w)


def _chip_sum(halves, other, sel, name):
    _, r2, c = other.shape
    tr = _row_block(r2, c)
    own = pl.BlockSpec((None, tr, c), lambda p, i, s_ref: (2 * p + s_ref[0], i, 0))
    blk = pl.BlockSpec((None, tr, c), lambda p, i, s_ref: (p, i, 0))

    def body(s_ref, a_ref, b_ref, o_ref):
        o_ref[...] = a_ref[...] + b_ref[...]

    return pl.pallas_call(
        body, name=name,
        grid_spec=pltpu.PrefetchScalarGridSpec(num_scalar_prefetch=1, grid=(N_CHIPS, r2 // tr), in_specs=[own, blk],
                                               out_specs=blk),
        out_shape=jax.ShapeDtypeStruct(other.shape, f32), compiler_params=_ew_params(2))(sel, halves, other)


def _final_sum(chip_sums, others, sel, name):
    _, r2, c = others.shape
    tr = _row_block(r2, c)
    own = pl.BlockSpec((None, tr, c), lambda i, s_ref: (s_ref[1], i, 0))
    slot = lambda j: pl.BlockSpec((None, tr, c), lambda i, s_ref: (j, i, 0))

    def body(s_ref, a_ref, b0_ref, b1_ref, b2_ref, o_ref):
        o_ref[...] = ((a_ref[...] + b0_ref[...]) + b1_ref[...]) + b2_ref[...]

    return pl.pallas_call(
        body, name=name,
        grid_spec=pltpu.PrefetchScalarGridSpec(num_scalar_prefetch=1, grid=(r2 // tr,),
                                               in_specs=[own, slot(0), slot(1), slot(2)],
                                               out_specs=pl.BlockSpec((tr, c), lambda i, s_ref: (i, 0))),
        out_shape=jax.ShapeDtypeStruct((r2, c), f32), compiler_params=_ew_params())(sel, chip_sums, others, others, others)


def _adamw(w, g, m, v, name):
    r, c = w.shape
    tr = _row_block(r, c)
    spec = pl.BlockSpec((tr, c), lambda i: (i, 0))

    def body(w_ref, g_ref, m_ref, v_ref, d_ref, nm_ref, nv_ref):
        gv = g_ref[...]
        nm = ADAM_B1 * m_ref[...] + (1.0 - ADAM_B1) * gv
        nv = ADAM_B2 * v_ref[...] + (1.0 - ADAM_B2) * jnp.square(gv)
        m_hat = nm / (1.0 - ADAM_B1 ** ADAM_STEP)
        v_hat = nv / (1.0 - ADAM_B2 ** ADAM_STEP)
        d_ref[...] = -ADAM_LR * (m_hat / (jnp.sqrt(v_hat) + ADAM_EPS) + ADAM_WD * w_ref[...])
        nm_ref[...] = nm
        nv_ref[...] = nv

    out = jax.ShapeDtypeStruct((r, c), f32)
    return pl.pallas_call(body, grid=(r // tr,), name=name, in_specs=[spec] * 4, out_specs=[spec] * 3,
                          out_shape=[out, out, out], compiler_params=_ew_params())(w, g, m, v)


def _place():
    return lax.axis_index("x"), lax.axis_index("y"), lax.axis_index("c")


def _chip_peers(x, y):
    out = []
    for fx, fy in ((1, 0), (0, 1), (1, 1)):
        px, py = x ^ fx, y ^ fy
        out.append((px, py, 2 * px + py))
    return out


def _all_gather(arrays, split):
    na = len(arrays)

    def body(*refs):
        ins, outs = refs[:na], refs[na:2 * na]
        send_sems, recv_sems, local_sems = refs[2 * na:]
        x, y, c = _place()
        me = 2 * x + y
        sibling = (x, y, 1 - c)
        peers = _chip_peers(x, y)

        def half(ref, a, which):
            rows = arrays[a].shape[0] // 2
            return ref.at[pl.ds(which * rows, rows), :]

        local = [pltpu.make_async_copy(ins[a], outs[a].at[me], local_sems.at[a]) for a in range(na)]
        for cp in local:
            cp.start()
        sends, fwds = [], []
        for a in range(na):
            for j, (px, py, _) in enumerate(peers):
                src = half(ins[a], a, c) if split[a] else ins[a]
                dst = half(outs[a].at[me], a, c) if split[a] else outs[a].at[me]
                cp = pltpu.make_async_remote_copy(src, dst, send_sems.at[a, j], recv_sems.at[a, j],
                                                  device_id=(px, py, c), device_id_type=MESH)
                cp.start()
                sends.append(cp)
        for a in range(na):
            for j, (px, py, pid) in enumerate(peers):
                dst = half(outs[a].at[pid], a, c) if split[a] else outs[a].at[pid]
                pltpu.make_async_remote_copy(dst, dst, send_sems.at[a, j], recv_sems.at[a, j],
                                             device_id=(px, py, c), device_id_type=MESH).wait_recv()
                if split[a]:
                    cp = pltpu.make_async_remote_copy(dst, dst, send_sems.at[a, 3 + j], recv_sems.at[a, 3 + j],
                                                      device_id=sibling, device_id_type=MESH)
                    cp.start()
                    fwds.append(cp)
        for a in range(na):
            if split[a]:
                for j, (_, _, pid) in enumerate(peers):
                    dst = half(outs[a].at[pid], a, 1 - c)
                    pltpu.make_async_remote_copy(dst, dst, send_sems.at[a, 3 + j], recv_sems.at[a, 3 + j],
                                                 device_id=sibling, device_id_type=MESH).wait_recv()
        for cp in sends + fwds:
            cp.wait_send()
        for cp in local:
            cp.wait()

    return pl.pallas_call(
        body, name="all_gather_weights", in_specs=[ANY] * na, out_specs=[ANY] * na,
        out_shape=[jax.ShapeDtypeStruct((N_CHIPS,) + a.shape, a.dtype) for a in arrays],
        scratch_shapes=[pltpu.SemaphoreType.DMA((na, 6)), pltpu.SemaphoreType.DMA((na, 6)), pltpu.SemaphoreType.DMA((na,))],
        )(*arrays)


def _swap_halves(arrays):
    na = len(arrays)

    def body(*refs):
        ins, outs = refs[:na], refs[na:2 * na]
        send_sems, recv_sems = refs[2 * na:]
        x, y, c = _place()
        copies = []
        for a in range(na):
            rows = arrays[a].shape[1] // 2
            src = ins[a].at[:, pl.ds((1 - c) * rows, rows), :]
            cp = pltpu.make_async_remote_copy(src, outs[a], send_sems.at[a], recv_sems.at[a],
                                              device_id=(x, y, 1 - c), device_id_type=MESH)
            cp.start()
            copies.append(cp)
        for cp in copies:
            cp.wait()

    return pl.pallas_call(
        body, name="grad_swap_halves", in_specs=[ANY] * na, out_specs=[ANY] * na,
        out_shape=[jax.ShapeDtypeStruct((a.shape[0], a.shape[1] // 2, a.shape[2]), a.dtype) for a in arrays],
        scratch_shapes=[pltpu.SemaphoreType.DMA((na,)), pltpu.SemaphoreType.DMA((na,))],
        )(*arrays)


def _scatter_blocks(arrays):
    na = len(arrays)

    def body(*refs):
        ins, outs = refs[:na], refs[na:2 * na]
        send_sems, recv_sems = refs[2 * na:]
        x, y, c = _place()
        copies = []
        for a in range(na):
            for j, (px, py, pid) in enumerate(_chip_peers(x, y)):
                cp = pltpu.make_async_remote_copy(ins[a].at[pid], outs[a].at[j], send_sems.at[a, j], recv_sems.at[a, j],
                                                  device_id=(px, py, c), device_id_type=MESH)
                cp.start()
                copies.append(cp)
        for cp in copies:
            cp.wait()

    return pl.pallas_call(
        body, name="grad_scatter_blocks", in_specs=[ANY] * na, out_specs=[ANY] * na,
        out_shape=[jax.ShapeDtypeStruct((3,) + a.shape[1:], a.dtype) for a in arrays],
        scratch_shapes=[pltpu.SemaphoreType.DMA((na, 3)), pltpu.SemaphoreType.DMA((na, 3))],
        )(*arrays)


def _join_halves(arrays):
    na = len(arrays)

    def body(*refs):
        ins, outs = refs[:na], refs[na:2 * na]
        send_sems, recv_sems, local_sems = refs[2 * na:]
        x, y, c = _place()
        copies = []
        for a in range(na):
            local = pltpu.make_async_copy(ins[a], outs[a].at[c], local_sems.at[a])
            local.start()
            cp = pltpu.make_async_remote_copy(ins[a], outs[a].at[c], send_sems.at[a], recv_sems.at[a],
                                              device_id=(x, y, 1 - c), device_id_type=MESH)
            cp.start()
            copies.append((local, cp, a))
        for local, cp, a in copies:
            cp.wait_send()
            pltpu.make_async_remote_copy(ins[a], outs[a].at[1 - c], send_sems.at[a], recv_sems.at[a],
                                         device_id=(x, y, 1 - c), device_id_type=MESH).wait_recv()
            local.wait()

    return pl.pallas_call(
        body, name="grad_join_halves", in_specs=[ANY] * na, out_specs=[ANY] * na,
        out_shape=[jax.ShapeDtypeStruct((2,) + a.shape, a.dtype) for a in arrays],
        scratch_shapes=[pltpu.SemaphoreType.DMA((na,)), pltpu.SemaphoreType.DMA((na,)), pltpu.SemaphoreType.DMA((na,))],
        )(*arrays)


SMALL_ROWS = 24


def _all_reduce_small(a_lb, dlb, pieces):
    d = a_lb.shape[1]
    npc = len(pieces)

    def body(*refs):
        alb_ref, dlb_ref = refs[:2]
        piece_refs = refs[2:2 + npc]
        out_ref, pack, slots, send_sems, recv_sems = refs[2 + npc:]
        x, y, c = _place()
        me = 4 * x + 2 * y + c
        pack[...] = jnp.zeros_like(pack)
        a = alb_ref[...]
        ex = jnp.exp(a - jnp.max(a, axis=0, keepdims=True))
        p = ex / jnp.sum(ex, axis=0, keepdims=True)
        first = (_iota(p.shape, 0) == 0).astype(f32)
        p0 = jnp.sum(p * first, axis=0, keepdims=True)
        pack[0:3, :] = p * dlb_ref[...] * (first - p0)
        for ref, (arr, r0) in zip(piece_refs, pieces):
            pack[r0:r0 + arr.shape[0], 0:arr.shape[1]] = ref[...]
        slots[me] = pack[...]
        copies = []
        for k in range(1, N_DEV):
            peer = (x ^ (k >> 2), y ^ ((k >> 1) & 1), c ^ (k & 1))
            cp = pltpu.make_async_remote_copy(pack, slots.at[me], send_sems.at[k - 1], recv_sems.at[k - 1],
                                              device_id=peer, device_id_type=MESH)
            cp.start()
            copies.append(cp)
        for k in range(1, N_DEV):
            src = (x ^ (k >> 2)) * 4 + (y ^ ((k >> 1) & 1)) * 2 + (c ^ (k & 1))
            pltpu.make_async_remote_copy(pack, slots.at[src], send_sems.at[k - 1], recv_sems.at[k - 1],
                                         device_id=(x, y, c), device_id_type=MESH).wait_recv()
        for cp in copies:
            cp.wait_send()
        acc = slots[0]
        for i in range(1, N_DEV):
            acc = acc + slots[i]
        out_ref[...] = acc

    return pl.pallas_call(
        body, name="all_reduce_small", in_specs=[VMEM] * (2 + npc), out_specs=VMEM,
        out_shape=jax.ShapeDtypeStruct((SMALL_ROWS, d), f32),
        scratch_shapes=[pltpu.VMEM((SMALL_ROWS, d), f32), pltpu.VMEM((N_DEV, SMALL_ROWS, d), f32),
                        pltpu.SemaphoreType.DMA((N_DEV - 1,)), pltpu.SemaphoreType.DMA((N_DEV - 1,))],
        compiler_params=pltpu.CompilerParams(vmem_limit_bytes=VMEM_LIMIT))(
            a_lb, dlb, *[p[0] for p in pieces])


def _reduce_scatter(grads):
    x, y, c = _place()
    sel = jnp.stack([c, 2 * x + y]).astype(jnp.int32)
    from_sibling = _swap_halves(grads)
    chip_sums = []
    for a, (g, t) in enumerate(zip(grads, from_sibling)):
        halves = g.reshape(2 * g.shape[0], g.shape[1] // 2, g.shape[2])
        chip_sums.append(_chip_sum(halves, t, sel, f"grad_chip_sum_{a}"))
    from_chips = _scatter_blocks(chip_sums)
    finals = [_final_sum(h, t, sel, f"grad_final_sum_{a}") for a, (h, t) in enumerate(zip(chip_sums, from_chips))]
    joined = _join_halves(finals)
    return [j.reshape(2 * j.shape[1], j.shape[2]) for j in joined]


def kernel(x, a_w_in, a_lower_bounds, a_norm_w, a_w_out, b_w_in, b_conv_w, b_a_log, b_dt_bias, b_norm_w, b_w_out, ln_g, ln_b, loss_target, m_a_w_in, m_a_lower_bounds, m_a_norm_w, m_a_w_out, m_b_w_in, m_b_conv_w, m_b_a_log, m_b_dt_bias, m_b_norm_w, m_b_w_out, m_ln_g, m_ln_b, v_a_w_in, v_a_lower_bounds, v_a_norm_w, v_a_w_out, v_b_w_in, v_b_conv_w, v_b_a_log, v_b_dt_bias, v_b_norm_w, v_b_w_out, v_ln_g, v_ln_b):
    _, s, d = x.shape
    h = d // HEAD_DIM
    wb = b_w_in.shape[2]
    x2, target = x[0], loss_target[0]
    chip = 2 * lax.axis_index("x") + lax.axis_index("y")

    shards = [_cast(a_w_in[0], bf16, "cast_a_w_in"), _cast(a_w_out[0], bf16, "cast_a_w_out"),
              _cast(b_w_in[0], bf16, "cast_b_w_in"), _cast(b_w_out[0], bf16, "cast_b_w_out")]
    wa4, wa_out_g, wb_g, wb_out_g, conv_g = _all_gather(shards + [b_conv_w[0]], [True] * 4 + [False])
    wa_out = wa_out_g.reshape(d, d)
    wb_out = wb_out_g.reshape(d, d)
    wb2d = wb_g.transpose(1, 0, 2).reshape(d, N_CHIPS * wb)
    wb4 = wb2d[:, :4 * d].reshape(d, 4, d).transpose(1, 0, 2)
    wba = jnp.pad(wb2d[:, 4 * d:], ((0, 0), (0, LANES - 2 * h)))
    conv_w = conv_g.transpose(1, 0, 2).reshape(CONV_WIDTH, 3 * d)

    x_bf = _cast(x2, bf16, "cast_x")
    lb = _lower_bound(a_lower_bounds)
    h4a = _mm(x_bf, wa4, "nn", name="a_in_fwd")
    ya, st_a = _hgrn2_fwd(h4a, lb, a_norm_w)
    yo_a = _mm(ya, wa_out, "nn", name="a_out_fwd")
    x1, x1_bf = _ln_fwd(x2, yo_a, ln_g[0:1], ln_b[0:1])
    h4b = _mm(x1_bf, wb4, "nn", name="b_in_fwd")
    ba = _mm(x1_bf, wba, "nn", name="b_gate_fwd")
    c3 = _conv_fwd(h4b, conv_w)
    bl = ba[:, :h].T.reshape(h, s, 1)
    al = ba[:, h:2 * h].T.reshape(h, s, 1)
    alog, dtb = b_a_log.reshape(h, 1, 1), b_dt_bias.reshape(h, 1, 1)
    yb, st_b = _gdn_fwd(c3, h4b, bl, al, alog, dtb, b_norm_w)
    yo_b = _mm(yb, wb_out, "nn", name="b_out_fwd")

    dr2, dr2_bf, dg2, db2, loss = _ln_bwd(x1, yo_b, ln_g[1:2], b=ln_b[1:2], target=target, name="ln2_loss_bwd")
    g_wb_out = _mm(yb, dr2_bf, "tn", name="b_out_dw")
    dyb = _mm(dr2_bf, wb_out, "nt", name="b_out_dx")
    dc3, dhz, dbl, dal, dalog, ddtb, dnw_b = _gdn_bwd(c3, h4b, bl, al, alog, dtb, b_norm_w, st_b, dyb)
    dh3, dconv = _conv_bwd(dc3, h4b, conv_w)
    dba = jnp.pad(jnp.concatenate([dbl.reshape(h, s).T, dal.reshape(h, s).T], axis=1), ((0, 0), (0, LANES - 2 * h)))
    g_wb3 = _mm(x1_bf, dh3, "tn", name="b_in_dw_qkv")
    g_wbz = _mm(x1_bf, dhz, "tn", name="b_in_dw_z")
    g_wba = _mm(x1_bf, dba, "tn", name="b_in_dw_gate")
    dx1 = _mm(dh3, wb4, "nt", name="b_in_dx_qkv", groups=3, add=dr2, add_scale=DEEPNORM_ALPHA)
    dx1 = _mm(dhz, wb4, "nt", name="b_in_dx_z", b_g0=3, groups=1, add=dx1)
    dx1 = _mm(dba, wba, "nt", name="b_in_dx_gate", add=dx1)
    dr1, dr1_bf, dg1, db1 = _ln_bwd(x2, yo_a, ln_g[0:1], dy=dx1, name="ln1_bwd")
    g_wa_out = _mm(ya, dr1_bf, "tn", name="a_out_dw")
    dya = _mm(dr1_bf, wa_out, "nt", name="a_out_dx")
    dh4, dlb, dnw_a = _hgrn2_bwd(h4a, lb, a_norm_w, st_a, dya)
    g_wa4 = _mm(x_bf, dh4, "tn", name="a_in_dw")
    grad_x = _mm(dh4, wa4, "nt", name="a_in_dx", add=dr1, add_scale=DEEPNORM_ALPHA)

    g_wb2d = jnp.concatenate([g_wb3[0], g_wb3[1], g_wb3[2], g_wbz, g_wba[:, :2 * h]], axis=1)
    g_wb4 = g_wb2d.reshape(d, N_CHIPS, wb).transpose(1, 0, 2)
    big = _reduce_scatter([g_wa4, g_wa_out.reshape(N_CHIPS, d // N_CHIPS, d), g_wb4, g_wb_out.reshape(N_CHIPS, d // N_CHIPS, d)])
    small = _all_reduce_small(a_lower_bounds, dlb, [
        (dg1, 3), (dg2, 4), (db1, 5), (db2, 6), (dnw_a, 7), (dnw_b, 8), (dalog.reshape(1, h), 9), (ddtb.reshape(1, h), 10),
        (loss, 11), (dconv.reshape(3 * CONV_WIDTH, d), 12)])
    loss_out = small[11, 0]
    cw = 3 * d // N_CHIPS
    g_conv = lax.dynamic_slice_in_dim(small[12:24].reshape(CONV_WIDTH, 3 * d), chip * cw, cw, axis=1)
    grads = {
        "a_w_in": big[0], "a_lower_bounds": small[0:3], "a_norm_w": small[7:8, :HEAD_DIM], "a_w_out": big[1],
        "b_w_in": big[2], "b_conv_w": g_conv, "b_a_log": small[9:10, :h], "b_dt_bias": small[10:11, :h],
        "b_norm_w": small[8:9, :HEAD_DIM], "b_w_out": big[3],
        "ln_g": jnp.concatenate([small[3:4], small[4:5]], axis=0), "ln_b": jnp.concatenate([small[5:6], small[6:7]], axis=0),
    }

    weights = dict(a_w_in=a_w_in, a_lower_bounds=a_lower_bounds, a_norm_w=a_norm_w, a_w_out=a_w_out, b_w_in=b_w_in,
                   b_conv_w=b_conv_w, b_a_log=b_a_log, b_dt_bias=b_dt_bias, b_norm_w=b_norm_w, b_w_out=b_w_out,
                   ln_g=ln_g, ln_b=ln_b)
    m_in = dict(a_w_in=m_a_w_in, a_lower_bounds=m_a_lower_bounds, a_norm_w=m_a_norm_w, a_w_out=m_a_w_out, b_w_in=m_b_w_in,
                b_conv_w=m_b_conv_w, b_a_log=m_b_a_log, b_dt_bias=m_b_dt_bias, b_norm_w=m_b_norm_w, b_w_out=m_b_w_out,
                ln_g=m_ln_g, ln_b=m_ln_b)
    v_in = dict(a_w_in=v_a_w_in, a_lower_bounds=v_a_lower_bounds, a_norm_w=v_a_norm_w, a_w_out=v_a_w_out, b_w_in=v_b_w_in,
                b_conv_w=v_b_conv_w, b_a_log=v_b_a_log, b_dt_bias=v_b_dt_bias, b_norm_w=v_b_norm_w, b_w_out=v_b_w_out,
                ln_g=v_ln_g, ln_b=v_ln_b)
    out_g, out_d, out_m, out_v = [], [], [], []
    for name, w in weights.items():
        shape2 = w.shape[-2:]
        g = grads[name].reshape(shape2)
        delta, nm, nv = _adamw(w.reshape(shape2), g, m_in[name].reshape(shape2), v_in[name].reshape(shape2), "adamw_" + name)
        out_g.append(g.reshape(w.shape))
        out_d.append(delta.reshape(w.shape))
        out_m.append(nm.reshape(w.shape))
        out_v.append(nv.reshape(w.shape))
    return (loss_out, grad_x.reshape(x.shape), *out_g, *out_d, *out_m, *out_v)
```

```python
import functools
import math

import jax
import jax.numpy as jnp
from jax import lax
from jax.experimental import pallas as pl
from jax.experimental.pallas import tpu as pltpu

f32 = jnp.float32
bf16 = jnp.bfloat16
HIGHEST = lax.Precision.HIGHEST

HEAD_DIM = 128
CHUNK = 64
SUB = 16
DEPTH = 2
DEEPNORM_ALPHA = (2.0 * DEPTH) ** 0.25
LN_EPS = 1e-5
RMS_EPS = 1e-6
L2_EPS = 1e-6
ADAM_LR, ADAM_B1, ADAM_B2, ADAM_EPS, ADAM_WD, ADAM_STEP = 0.001, 0.9, 0.999, 1e-08, 0.01, 10
CONV_WIDTH = 4
EXP_CLAMP = 40.0
VMEM_LIMIT = 56 << 20
MXU_DTYPE = bf16
SUBLANES = 8
LANES = 128
N_CHIPS = 4
N_DEV = 8

NN = (((1,), (0,)), ((), ()))
NT = (((1,), (1,)), ((), ()))
TN = (((0,), (0,)), ((), ()))
MESH = pl.DeviceIdType.MESH
ANY = pl.BlockSpec(memory_space=pl.ANY)
VMEM = pl.BlockSpec(memory_space=pltpu.VMEM)


def _bdot(a, b, dn):
    return lax.dot_general(a.astype(MXU_DTYPE), b.astype(MXU_DTYPE), dn, preferred_element_type=f32)


def _hdot(a, b, dn):
    return lax.dot_general(a, b, dn, precision=lax.Precision.HIGH, preferred_element_type=f32)


def _iota(shape, dim):
    return lax.broadcasted_iota(jnp.int32, shape, dim)


def _same_sub(a, b):
    shift = int(math.log2(SUB))
    return jnp.right_shift(a, shift) == jnp.right_shift(b, shift)


def _silu(x):
    return x * jax.nn.sigmoid(x)


def _gated_rmsnorm(o, hz, nw):
    return o * lax.rsqrt(jnp.mean(o * o, axis=-1, keepdims=True) + RMS_EPS) * nw * _silu(hz)


def _hgrn2_chunk(hq, hf, hi, hz, lb, nw, st):
    c = CHUNK
    q = _silu(hq)
    forget = lb + (1.0 - lb) * jax.nn.sigmoid(hf)
    k = 1.0 - forget
    g = jnp.log(forget)
    tri = (_iota((c, c), 1) <= _iota((c, c), 0)).astype(f32)
    cum = _hdot(tri, g, NN)
    total = jnp.sum(g, axis=0, keepdims=True)
    nb = c // SUB
    blk = [cum[SUB * j:SUB * (j + 1)] for j in range(nb)]
    r16 = _iota((SUB, HEAD_DIM), 0)

    def row(j, i):
        return jnp.sum(jnp.where(r16 == i, blk[j], 0.0), axis=0, keepdims=True)

    c15, c31, c47 = row(0, SUB - 1), row(1, SUB - 1), row(2, SUB - 1)
    z = jnp.zeros((SUB, HEAD_DIM), f32)
    e = jnp.exp
    q32 = jnp.concatenate([z, z, e(blk[2] - c31), e(blk[3] - c31)], axis=0)
    k32 = jnp.concatenate([e(c31 - blk[0]), e(c31 - blk[1]), z, z], axis=0)
    q16a = jnp.concatenate([z, e(blk[1] - c15), z, z], axis=0)
    k16a = jnp.concatenate([e(c15 - blk[0]), z, z, z], axis=0)
    q16b = jnp.concatenate([z, z, z, e(blk[3] - c47)], axis=0)
    k16b = jnp.concatenate([z, z, e(c47 - blk[2]), z], axis=0)
    dmid = jnp.concatenate([blk[j] - row(j, SUB // 2 - 1) for j in range(nb)], axis=0)
    qd = e(jnp.minimum(dmid, EXP_CLAMP))
    kd = e(jnp.minimum(-dmid, EXP_CLAMP))
    q2 = jnp.concatenate([q * q32, q * q16a, q * q16b], axis=1)
    k2 = jnp.concatenate([k * k32, k * k16a, k * k16b], axis=1)
    rr, cc = _iota((c, c), 0), _iota((c, c), 1)
    diag = _same_sub(rr, cc) & (cc <= rr)
    att = _bdot(q2, k2, NT) + jnp.where(diag, _bdot(q * qd, k * kd, NT), 0.0)
    o = _bdot(att, hi, NN) + _bdot(q * e(cum), st, NT)
    st_new = st * e(total) + _bdot(hi, k * e(total - cum), TN)
    return _gated_rmsnorm(o, hz, nw), st_new


def _unit_lower_inverse(lower):
    c = CHUNK
    rr, cc = _iota((c, c), 0), _iota((c, c), 1)
    eye = (rr == cc).astype(f32)
    same = _same_sub(rr, cc)
    ld = jnp.where(same, lower, 0.0)
    off = lower - ld
    x = eye - ld
    p = ld
    for _ in range(int(math.log2(SUB)) - 1):
        p = _hdot(p, p, NN)
        x = x + _hdot(x, p, NN)
    n = _hdot(x, off, NN)
    y = eye - n
    p = n
    for _ in range(int(math.log2(c // SUB)) - 1):
        p = _hdot(p, p, NN)
        y = y + _hdot(y, p, NN)
    return _hdot(y, x, NN)


def _gdn_chunk(cq, ck, cv, hz, bl, al, alog, dtb, nw, st):
    c = CHUNK
    sq, sk, v = _silu(cq), _silu(ck), _silu(cv)
    q = sq * lax.rsqrt(jnp.sum(sq * sq, axis=-1, keepdims=True) + L2_EPS) * (HEAD_DIM ** -0.5)
    k = sk * lax.rsqrt(jnp.sum(sk * sk, axis=-1, keepdims=True) + L2_EPS)
    beta = jax.nn.sigmoid(bl)
    xs = al + dtb
    pos = xs > 0.0
    g = -jnp.exp(alog) * (jnp.where(pos, xs, 0.0) + jnp.log(1.0 + jnp.exp(jnp.where(pos, -xs, xs))))
    rr, cc = _iota((c, c), 0), _iota((c, c), 1)
    tri = (cc <= rr).astype(f32)
    gc = jnp.broadcast_to(g, (c, c))
    cum_c = _hdot(tri, gc, NN)
    cum_r = _hdot(gc, (rr <= cc).astype(f32), TN)
    g128 = jnp.broadcast_to(g, (c, HEAD_DIM))
    cum = _hdot(tri, g128, NN)
    total = jnp.sum(g128, axis=0, keepdims=True)
    diff = cum_c - cum_r
    ninf = -jnp.inf
    lower = beta * _bdot(k, k, NT) * jnp.exp(jnp.where(cc < rr, diff, ninf))
    rhs = jnp.concatenate([beta * v, (beta * jnp.exp(cum)) * k], axis=1)
    sol = _hdot(_unit_lower_inverse(lower), rhs, NN)
    u0, w = sol[:, :HEAD_DIM], sol[:, HEAD_DIM:]
    qk = _bdot(q, k, NT) * jnp.exp(jnp.where(cc <= rr, diff, ninf))
    u = u0 - _bdot(w, st, NN)
    o = _bdot(q * jnp.exp(cum), st, NN) + _bdot(qk, u, NN)
    st_new = st * jnp.exp(total) + _bdot(k * jnp.exp(total - cum), u, TN)
    return _gated_rmsnorm(o, hz, nw), st_new


def _rows(c):
    return pl.ds(pl.multiple_of(c * CHUNK, CHUNK), CHUNK)


HEADS_PER_STEP = 8
CHUNKS_PER_STEP = 4


def _scan_dims(s, d):
    h, n = d // HEAD_DIM, s // CHUNK
    ncb = min(n, CHUNKS_PER_STEP)
    hp = min(h, HEADS_PER_STEP)
    return h, n, ncb, n // ncb, ncb * CHUNK, hp


def _scan_specs(t, nblk, ncb, hp, reverse):
    blk = (lambda j: nblk - 1 - j) if reverse else (lambda j: j)
    w = hp * HEAD_DIM
    grp = lambda g: pl.BlockSpec((g, t, w), lambda h, j: (0, blk(j), h))
    one = lambda g: pl.BlockSpec((None, t, w), lambda h, j: (g, blk(j), h))
    col = lambda: pl.BlockSpec((t, w), lambda h, j: (blk(j), h))
    vec = lambda: pl.BlockSpec((1, w), lambda h, j: (0, h))
    shared = lambda: pl.BlockSpec((1, HEAD_DIM), lambda h, j: (0, 0))
    state = lambda: pl.BlockSpec((hp, ncb, HEAD_DIM, HEAD_DIM), lambda h, j: (h, blk(j), 0, 0))
    tok = lambda: pl.BlockSpec((hp, t, 1), lambda h, j: (h, blk(j), 0))
    par = lambda: pl.BlockSpec((hp, 1, 1), lambda h, j: (h, 0, 0))
    return dict(grp=grp, one=one, col=col, vec=vec, shared=shared, state=state, tok=tok, par=par)


def _scan_params():
    return pltpu.CompilerParams(dimension_semantics=("arbitrary", "arbitrary"), vmem_limit_bytes=VMEM_LIMIT)


def _state_scratch(hp):
    return [pltpu.VMEM((hp, HEAD_DIM, HEAD_DIM), f32)]


def _lanes(i):
    return pl.ds(i * HEAD_DIM, HEAD_DIM)


def _stack_heads(per_head):
    return tuple(jnp.stack(vals) for vals in zip(*per_head))


def _hgrn2_fwd(h4, lb, nw):
    _, s, d = h4.shape
    h, n, ncb, nblk, t, hp = _scan_dims(s, d)
    sp = _scan_specs(t, nblk, ncb, hp, False)

    def body(h_ref, lb_ref, nw_ref, y_ref, st_ref, st_scr):
        @pl.when(pl.program_id(1) == 0)
        def _():
            st_scr[...] = jnp.zeros_like(st_scr)

        def step(c, carry):
            r = _rows(c)
            args = [(h_ref[0, r, _lanes(i)], h_ref[1, r, _lanes(i)], h_ref[2, r, _lanes(i)], h_ref[3, r, _lanes(i)],
                     lb_ref[:, _lanes(i)], nw_ref[...], st_scr[i]) for i in range(hp)]
            y, st_new = jax.vmap(_hgrn2_chunk)(*_stack_heads(args))
            for i, a in enumerate(args):
                st_ref[i, c] = a[-1]
                y_ref[r, _lanes(i)] = y[i].astype(y_ref.dtype)
                st_scr[i] = st_new[i]
            return carry

        lax.fori_loop(0, ncb, step, 0)

    return pl.pallas_call(
        body, grid=(h // hp, nblk), name="hgrn2_fwd",
        in_specs=[sp["grp"](4), sp["vec"](), sp["shared"]()],
        out_specs=[sp["col"](), sp["state"]()],
        out_shape=[jax.ShapeDtypeStruct((s, d), bf16), jax.ShapeDtypeStruct((h, n, HEAD_DIM, HEAD_DIM), f32)],
        scratch_shapes=_state_scratch(hp), compiler_params=_scan_params())(h4, lb, nw)


def _hgrn2_bwd(h4, lb, nw, states, dy):
    _, s, d = h4.shape
    h, n, ncb, nblk, t, hp = _scan_dims(s, d)
    sp = _scan_specs(t, nblk, ncb, hp, True)

    def body(h_ref, lb_ref, nw_ref, st_ref, dy_ref, dh_ref, dlb_ref, dnw_ref, dst_scr):
        first_block = pl.program_id(1) == 0

        @pl.when(first_block)
        def _():
            dst_scr[...] = jnp.zeros_like(dst_scr)
            dlb_ref[...] = jnp.zeros_like(dlb_ref)

        @pl.when(first_block & (pl.program_id(0) == 0))
        def _():
            dnw_ref[...] = jnp.zeros_like(dnw_ref)

        def step(k, carry):
            c = ncb - 1 - k
            r = _rows(c)
            args = [(h_ref[0, r, _lanes(i)], h_ref[1, r, _lanes(i)], h_ref[2, r, _lanes(i)], h_ref[3, r, _lanes(i)],
                     lb_ref[:, _lanes(i)], nw_ref[...], st_ref[i, c]) for i in range(hp)]
            cots = [(dy_ref[r, _lanes(i)].astype(f32), dst_scr[i]) for i in range(hp)]
            dq, df, di, dz, dlb, dnw, dst = jax.vjp(jax.vmap(_hgrn2_chunk), *_stack_heads(args))[1](_stack_heads(cots))
            dnw_sum = dnw_ref[...]
            for i in range(hp):
                for g, val in enumerate((dq, df, di, dz)):
                    dh_ref[g, r, _lanes(i)] = val[i].astype(dh_ref.dtype)
                dlb_ref[:, _lanes(i)] += dlb[i]
                dnw_sum = dnw_sum + dnw[i]
                dst_scr[i] = dst[i]
            dnw_ref[...] = dnw_sum
            return carry

        lax.fori_loop(0, ncb, step, 0)

    return pl.pallas_call(
        body, grid=(h // hp, nblk), name="hgrn2_bwd",
        in_specs=[sp["grp"](4), sp["vec"](), sp["shared"](), sp["state"](), sp["col"]()],
        out_specs=[sp["grp"](4), sp["vec"](), sp["shared"]()],
        out_shape=[jax.ShapeDtypeStruct((4, s, d), bf16), jax.ShapeDtypeStruct((1, d), f32),
                   jax.ShapeDtypeStruct((1, HEAD_DIM), f32)],
        scratch_shapes=_state_scratch(hp), compiler_params=_scan_params())(h4, lb, nw, states, dy)


def _gdn_fwd(c3, h4, bl, al, alog, dtb, nw):
    _, s, d = c3.shape
    h, n, ncb, nblk, t, hp = _scan_dims(s, d)
    sp = _scan_specs(t, nblk, ncb, hp, False)

    def body(c_ref, hz_ref, bl_ref, al_ref, alog_ref, dtb_ref, nw_ref, y_ref, st_ref, st_scr):
        @pl.when(pl.program_id(1) == 0)
        def _():
            st_scr[...] = jnp.zeros_like(st_scr)

        def step(c, carry):
            r = _rows(c)
            args = [(c_ref[0, r, _lanes(i)], c_ref[1, r, _lanes(i)], c_ref[2, r, _lanes(i)], hz_ref[r, _lanes(i)],
                     bl_ref[i, r, :], al_ref[i, r, :], alog_ref[i], dtb_ref[i], nw_ref[...], st_scr[i]) for i in range(hp)]
            y, st_new = jax.vmap(_gdn_chunk)(*_stack_heads(args))
            for i, a in enumerate(args):
                st_ref[i, c] = a[-1]
                y_ref[r, _lanes(i)] = y[i].astype(y_ref.dtype)
                st_scr[i] = st_new[i]
            return carry

        lax.fori_loop(0, ncb, step, 0)

    return pl.pallas_call(
        body, grid=(h // hp, nblk), name="gdn_fwd",
        in_specs=[sp["grp"](3), sp["one"](3), sp["tok"](), sp["tok"](), sp["par"](), sp["par"](), sp["shared"]()],
        out_specs=[sp["col"](), sp["state"]()],
        out_shape=[jax.ShapeDtypeStruct((s, d), bf16), jax.ShapeDtypeStruct((h, n, HEAD_DIM, HEAD_DIM), f32)],
        scratch_shapes=_state_scratch(hp), compiler_params=_scan_params())(c3, h4, bl, al, alog, dtb, nw)


def _gdn_bwd(c3, h4, bl, al, alog, dtb, nw, states, dy):
    _, s, d = c3.shape
    h, n, ncb, nblk, t, hp = _scan_dims(s, d)
    sp = _scan_specs(t, nblk, ncb, hp, True)

    def body(c_ref, hz_ref, bl_ref, al_ref, alog_ref, dtb_ref, nw_ref, st_ref, dy_ref,
             dc_ref, dz_ref, dbl_ref, dal_ref, dalog_ref, ddtb_ref, dnw_ref, dst_scr):
        first_block = pl.program_id(1) == 0

        @pl.when(first_block)
        def _():
            dst_scr[...] = jnp.zeros_like(dst_scr)
            dalog_ref[...] = jnp.zeros_like(dalog_ref)
            ddtb_ref[...] = jnp.zeros_like(ddtb_ref)

        @pl.when(first_block & (pl.program_id(0) == 0))
        def _():
            dnw_ref[...] = jnp.zeros_like(dnw_ref)

        def step(k, carry):
            c = ncb - 1 - k
            r = _rows(c)
            args = [(c_ref[0, r, _lanes(i)], c_ref[1, r, _lanes(i)], c_ref[2, r, _lanes(i)], hz_ref[r, _lanes(i)],
                     bl_ref[i, r, :], al_ref[i, r, :], alog_ref[i], dtb_ref[i], nw_ref[...], st_ref[i, c]) for i in range(hp)]
            cots = [(dy_ref[r, _lanes(i)].astype(f32), dst_scr[i]) for i in range(hp)]
            dq, dk, dv, dz, dbl, dal, dalog, ddtb, dnw, dst = jax.vjp(
                jax.vmap(_gdn_chunk), *_stack_heads(args))[1](_stack_heads(cots))
            dnw_sum = dnw_ref[...]
            for i in range(hp):
                for g, val in enumerate((dq, dk, dv)):
                    dc_ref[g, r, _lanes(i)] = val[i]
                dz_ref[r, _lanes(i)] = dz[i].astype(dz_ref.dtype)
                dbl_ref[i, r, :] = dbl[i]
                dal_ref[i, r, :] = dal[i]
                dalog_ref[i] += dalog[i]
                ddtb_ref[i] += ddtb[i]
                dnw_sum = dnw_sum + dnw[i]
                dst_scr[i] = dst[i]
            dnw_ref[...] = dnw_sum
            return carry

        lax.fori_loop(0, ncb, step, 0)

    tokens = jax.ShapeDtypeStruct((h, s, 1), f32)
    heads = jax.ShapeDtypeStruct((h, 1, 1), f32)
    return pl.pallas_call(
        body, grid=(h // hp, nblk), name="gdn_bwd",
        in_specs=[sp["grp"](3), sp["one"](3), sp["tok"](), sp["tok"](), sp["par"](), sp["par"](), sp["shared"](),
                  sp["state"](), sp["col"]()],
        out_specs=[sp["grp"](3), sp["col"](), sp["tok"](), sp["tok"](), sp["par"](), sp["par"](), sp["shared"]()],
        out_shape=[jax.ShapeDtypeStruct((3, s, d), f32), jax.ShapeDtypeStruct((s, d), bf16), tokens, tokens, heads, heads,
                   jax.ShapeDtypeStruct((1, HEAD_DIM), f32)],
        scratch_shapes=_state_scratch(hp), compiler_params=_scan_params())(c3, h4, bl, al, alog, dtb, nw, states, dy)


def _tile(n, pref):
    return n if n <= pref else pref


def _mm(a, b, mode, *, name, out_dtype=f32, add=None, add_scale=1.0, b_g0=0, groups=None):
    squeeze = a.ndim == 2 and b.ndim == 2
    a3 = a if a.ndim == 3 else a[None]
    b3 = b if b.ndim == 3 else b[None]
    if mode == "nt":
        g = groups or a3.shape[0]
        (_, m, r), (_, n, _) = a3.shape, b3.shape
    elif mode == "nn":
        g = groups or b3.shape[0]
        (_, m, r), (_, _, n) = a3.shape, b3.shape
    else:
        g = groups or b3.shape[0]
        (_, r, m), (_, _, n) = a3.shape, b3.shape
    tm, tn, tr = _tile(m, 1024), _tile(n, 1024), _tile(r, 2048)
    nr = r // tr
    a_grouped = a3.shape[0] > 1
    dn = {"nn": NN, "nt": NT, "tn": TN}[mode]

    if mode == "nt":
        grid = (m // tm, n // tn, g * nr)
        last = g * nr - 1
        kax = 2
        a_spec = pl.BlockSpec((None, tm, tr), lambda i, j, k: (k // nr, i, k % nr))
        b_spec = pl.BlockSpec((None, tn, tr), lambda i, j, k: (b_g0 + k // nr, j, k % nr))
        o_spec = pl.BlockSpec((tm, tn), lambda i, j, k: (i, j))
        add_spec = pl.BlockSpec((tm, tn), lambda i, j, k: (i, j))
        out_shape = jax.ShapeDtypeStruct((m, n), out_dtype)
        sem = ("parallel", "parallel", "arbitrary")
    else:
        grid = (g, m // tm, n // tn, nr)
        last = nr - 1
        kax = 3
        ag = (lambda q: q) if a_grouped else (lambda q: 0)
        if mode == "nn":
            a_spec = pl.BlockSpec((None, tm, tr), lambda q, i, j, k: (ag(q), i, k))
        else:
            a_spec = pl.BlockSpec((None, tr, tm), lambda q, i, j, k: (ag(q), k, i))
        b_spec = pl.BlockSpec((None, tr, tn), lambda q, i, j, k: (b_g0 + q, k, j))
        o_spec = pl.BlockSpec((None, tm, tn), lambda q, i, j, k: (q, i, j))
        add_spec = None
        out_shape = jax.ShapeDtypeStruct((g, m, n), out_dtype)
        sem = ("parallel", "parallel", "parallel", "arbitrary")
        assert add is None

    def body(*refs):
        if add is None:
            a_ref, b_ref, o_ref, acc = refs
        else:
            a_ref, b_ref, add_ref, o_ref, acc = refs
        k = pl.program_id(kax)

        @pl.when(k == 0)
        def _():
            acc[...] = jnp.zeros_like(acc)

        acc[...] += _bdot(a_ref[...], b_ref[...], dn)

        @pl.when(k == last)
        def _():
            res = acc[...]
            if add is not None:
                res = res + add_scale * add_ref[...]
            o_ref[...] = res.astype(o_ref.dtype)

    operands = (a3, b3) if add is None else (a3, b3, add)
    in_specs = [a_spec, b_spec] if add is None else [a_spec, b_spec, add_spec]
    out = pl.pallas_call(
        body, grid=grid, name=name, in_specs=in_specs, out_specs=o_spec, out_shape=out_shape,
        scratch_shapes=[pltpu.VMEM((tm, tn), f32)],
        compiler_params=pltpu.CompilerParams(dimension_semantics=sem, vmem_limit_bytes=VMEM_LIMIT))(*operands)
    return out[0] if (squeeze and mode != "nt") else out


def _row_block(rows, cols, target_bytes=1 << 20):
    if rows <= SUBLANES:
        return rows
    tr = SUBLANES
    while tr * 2 <= rows and tr * 2 * cols * 4 <= target_bytes and rows % (tr * 2) == 0:
        tr *= 2
    return tr


def _ew_params(n_axes=1):
    return pltpu.CompilerParams(dimension_semantics=("parallel",) * n_axes, vmem_limit_bytes=VMEM_LIMIT)


def _cast(x, dtype, name):
    r, c = x.shape
    tr = _row_block(r, c, 2 << 20)
    tr = max(tr, 16) if r >= 16 else tr
    spec = pl.BlockSpec((tr, c), lambda i: (i, 0))

    def body(x_ref, o_ref):
        o_ref[...] = x_ref[...].astype(dtype)

    return pl.pallas_call(body, grid=(r // tr,), name=name, in_specs=[spec], out_specs=spec,
                          out_shape=jax.ShapeDtypeStruct((r, c), dtype), compiler_params=_ew_params())(x)


def _ln_stats(r):
    mu = jnp.mean(r, axis=-1, keepdims=True)
    var = jnp.mean(jnp.square(r - mu), axis=-1, keepdims=True)
    return mu, lax.rsqrt(var + LN_EPS)


def _ln_fwd(x, yo, g, b):
    s, d = x.shape
    tr = _row_block(s, d, 2 << 20)
    tr = max(tr, 16)
    big = pl.BlockSpec((tr, d), lambda i: (i, 0))
    vec = pl.BlockSpec((1, d), lambda i: (0, 0))

    def body(x_ref, yo_ref, g_ref, b_ref, o_ref, ob_ref):
        r = DEEPNORM_ALPHA * x_ref[...] + yo_ref[...]
        mu, rstd = _ln_stats(r)
        y = (r - mu) * rstd * g_ref[...] + b_ref[...]
        o_ref[...] = y
        ob_ref[...] = y.astype(bf16)

    return pl.pallas_call(
        body, grid=(s // tr,), name="ln_fwd", in_specs=[big, big, vec, vec], out_specs=[big, big],
        out_shape=[jax.ShapeDtypeStruct((s, d), f32), jax.ShapeDtypeStruct((s, d), bf16)],
        compiler_params=_ew_params())(x, yo, g, b)


def _ln_bwd(x, yo, g, *, dy=None, b=None, target=None, name):
    s, d = x.shape
    with_loss = target is not None
    tr = max(_row_block(s, d, 2 << 20), 16)
    big = pl.BlockSpec((tr, d), lambda i: (i, 0))
    vec = pl.BlockSpec((1, d), lambda i: (0, 0))
    lane = pl.BlockSpec((1, LANES), lambda i: (0, 0))

    def body(*refs):
        if with_loss:
            x_ref, yo_ref, g_ref, b_ref, t_ref, dr_ref, drb_ref, dg_ref, db_ref, loss_ref = refs
        else:
            x_ref, yo_ref, g_ref, dy_ref, dr_ref, drb_ref, dg_ref, db_ref = refs

        @pl.when(pl.program_id(0) == 0)
        def _():
            dg_ref[...] = jnp.zeros_like(dg_ref)
            db_ref[...] = jnp.zeros_like(db_ref)
            if with_loss:
                loss_ref[...] = jnp.zeros_like(loss_ref)

        r = DEEPNORM_ALPHA * x_ref[...] + yo_ref[...]
        mu, rstd = _ln_stats(r)
        xhat = (r - mu) * rstd
        if with_loss:
            err = xhat * g_ref[...] + b_ref[...] - t_ref[...]
            loss_ref[...] += jnp.sum(err * err) * (0.5 / d)
            dyv = err * (1.0 / d)
        else:
            dyv = dy_ref[...]
        dxhat = dyv * g_ref[...]
        m1 = jnp.mean(dxhat, axis=-1, keepdims=True)
        m2 = jnp.mean(dxhat * xhat, axis=-1, keepdims=True)
        dr = rstd * (dxhat - m1 - xhat * m2)
        dr_ref[...] = dr
        drb_ref[...] = dr.astype(bf16)
        dg_ref[...] += jnp.sum(dyv * xhat, axis=0, keepdims=True)
        db_ref[...] += jnp.sum(dyv, axis=0, keepdims=True)

    operands = (x, yo, g, b, target) if with_loss else (x, yo, g, dy)
    in_specs = [big, big, vec, vec, big] if with_loss else [big, big, vec, big]
    out_specs = [big, big, vec, vec] + ([lane] if with_loss else [])
    out_shape = [jax.ShapeDtypeStruct((s, d), f32), jax.ShapeDtypeStruct((s, d), bf16),
                 jax.ShapeDtypeStruct((1, d), f32), jax.ShapeDtypeStruct((1, d), f32)]
    if with_loss:
        out_shape.append(jax.ShapeDtypeStruct((1, LANES), f32))
    return pl.pallas_call(
        body, grid=(s // tr,), name=name, in_specs=in_specs, out_specs=out_specs, out_shape=out_shape,
        compiler_params=pltpu.CompilerParams(dimension_semantics=("arbitrary",), vmem_limit_bytes=VMEM_LIMIT))(*operands)


def _lower_bound(a_lb):
    _, d = a_lb.shape

    def body(a_ref, o_ref):
        a = a_ref[...]
        ex = jnp.exp(a - jnp.max(a, axis=0, keepdims=True))
        p = ex / jnp.sum(ex, axis=0, keepdims=True)
        o_ref[...] = jnp.sum(jnp.where(_iota(p.shape, 0) == 0, p, 0.0), axis=0, keepdims=True)

    return pl.pallas_call(body, name="lower_bound", in_specs=[VMEM], out_specs=VMEM,
                          out_shape=jax.ShapeDtypeStruct((1, d), f32))(a_lb)


def _shift_rows(ext, k, rows):
    return pltpu.roll(ext, k, axis=0)[SUBLANES:SUBLANES + rows]


def _conv_tiles(s, d):
    return _tile(s, 512), _tile(d, 512)


def _conv_fwd(h4, w):
    _, s, d = h4.shape
    tr, tc = _conv_tiles(s, d)
    nj = d // tc
    hb = tr // SUBLANES
    cur = pl.BlockSpec((None, tr, tc), lambda g, j, i: (g, i, j))
    prev = pl.BlockSpec((None, SUBLANES, tc), lambda g, j, i: (g, jnp.maximum(i * hb - 1, 0), j))
    wsp = pl.BlockSpec((CONV_WIDTH, tc), lambda g, j, i: (0, g * nj + j))

    def body(x_ref, p_ref, w_ref, o_ref):
        x = x_ref[...]
        halo = jnp.where(pl.program_id(2) == 0, 0.0, p_ref[...])
        ext = jnp.concatenate([halo, x], axis=0)
        acc = w_ref[CONV_WIDTH - 1:CONV_WIDTH, :] * x
        for k in range(1, CONV_WIDTH):
            acc = acc + w_ref[CONV_WIDTH - 1 - k:CONV_WIDTH - k, :] * _shift_rows(ext, k, tr)
        o_ref[...] = acc

    return pl.pallas_call(
        body, grid=(3, nj, s // tr), name="conv_fwd", in_specs=[cur, prev, wsp], out_specs=cur,
        out_shape=jax.ShapeDtypeStruct((3, s, d), f32), compiler_params=_ew_params(3))(h4, h4, w)


def _conv_bwd(dy3, h4, w):
    _, s, d = dy3.shape
    tr, tc = _conv_tiles(s, d)
    nj = d // tc
    hb = tr // SUBLANES
    ni = s // tr
    cur = pl.BlockSpec((None, tr, tc), lambda g, j, i: (g, i, j))
    prev = pl.BlockSpec((None, SUBLANES, tc), lambda g, j, i: (g, jnp.maximum(i * hb - 1, 0), j))
    nxt = pl.BlockSpec((None, SUBLANES, tc), lambda g, j, i: (g, jnp.minimum((i + 1) * hb, s // SUBLANES - 1), j))
    wsp = pl.BlockSpec((CONV_WIDTH, tc), lambda g, j, i: (0, g * nj + j))

    def body(dy_ref, dn_ref, x_ref, p_ref, w_ref, dx_ref, dw_ref):
        i = pl.program_id(2)

        @pl.when(i == 0)
        def _():
            dw_ref[...] = jnp.zeros_like(dw_ref)

        dy = dy_ref[...]
        x = x_ref[...]
        tail = jnp.where(i == ni - 1, 0.0, dn_ref[...])
        dext = jnp.concatenate([dy, tail], axis=0)
        halo = jnp.where(i == 0, 0.0, p_ref[...])
        xext = jnp.concatenate([halo, x], axis=0)
        acc = w_ref[CONV_WIDTH - 1:CONV_WIDTH, :] * dy
        dw_ref[CONV_WIDTH - 1:CONV_WIDTH, :] += jnp.sum(dy * x, axis=0, keepdims=True)
        for k in range(1, CONV_WIDTH):
            ahead = pltpu.roll(dext, tr + SUBLANES - k, axis=0)[:tr]
            acc = acc + w_ref[CONV_WIDTH - 1 - k:CONV_WIDTH - k, :] * ahead
            dw_ref[CONV_WIDTH - 1 - k:CONV_WIDTH - k, :] += jnp.sum(dy * _shift_rows(xext, k, tr), axis=0, keepdims=True)
        dx_ref[...] = acc.astype(dx_ref.dtype)

    return pl.pallas_call(
        body, grid=(3, nj, ni), name="conv_bwd", in_specs=[cur, nxt, cur, prev, wsp], out_specs=[cur, wsp],
        out_shape=[jax.ShapeDtypeStruct((3, s, d), bf16), jax.ShapeDtypeStruct((CONV_WIDTH, 3 * d), f32)],
        compiler_params=pltpu.CompilerParams(dimension_semantics=("parallel", "parallel", "arbitrary"),
                                             vmem_limit_bytes=VMEM_LIMIT))(dy3, dy3, h4, h4, w)


def _my_core():
    return lax.axis_index("c")


def _my_chip():
    return 2 * lax.axis_index("x") + lax.axis_index("y")


def _chip_sum(halves, other, name):
    _, r2, c = other.shape
    tr = max(_row_block(r2, c), 16)
    own = pl.BlockSpec((None, tr, c), lambda p, i: (2 * p + _my_core(), i, 0))
    blk = pl.BlockSpec((None, tr, c), lambda p, i: (p, i, 0))

    def body(a_ref, b_ref, o_ref):
        o_ref[...] = (a_ref[...] + b_ref[...]).astype(o_ref.dtype)

    return pl.pallas_call(
        body, name=name, grid=(N_CHIPS, r2 // tr), in_specs=[own, blk], out_specs=blk,
        out_shape=jax.ShapeDtypeStruct(other.shape, bf16), compiler_params=_ew_params(2))(halves, other)


def _final_sum(halves, other, remote, name):
    _, r2, c = remote.shape
    tr = max(_row_block(r2, c), 16)
    own = pl.BlockSpec((None, tr, c), lambda i: (2 * _my_chip() + _my_core(), i, 0))
    sib = pl.BlockSpec((None, tr, c), lambda i: (_my_chip(), i, 0))
    slot = lambda j: pl.BlockSpec((None, tr, c), lambda i: (j, i, 0))

    def body(a_ref, b_ref, r0_ref, r1_ref, r2_ref, o_ref):
        acc = a_ref[...] + b_ref[...]
        for ref in (r0_ref, r1_ref, r2_ref):
            acc = acc + ref[...].astype(f32)
        o_ref[...] = acc

    return pl.pallas_call(
        body, name=name, grid=(r2 // tr,), in_specs=[own, sib, slot(0), slot(1), slot(2)],
        out_specs=pl.BlockSpec((tr, c), lambda i: (i, 0)),
        out_shape=jax.ShapeDtypeStruct((r2, c), f32), compiler_params=_ew_params())(halves, other, remote, remote, remote)


def _adamw(w, g, m, v, name):
    r, c = w.shape
    tr = _row_block(r, c)
    spec = pl.BlockSpec((tr, c), lambda i: (i, 0))

    def body(w_ref, g_ref, m_ref, v_ref, d_ref, nm_ref, nv_ref):
        gv = g_ref[...]
        nm = ADAM_B1 * m_ref[...] + (1.0 - ADAM_B1) * gv
        nv = ADAM_B2 * v_ref[...] + (1.0 - ADAM_B2) * jnp.square(gv)
        m_hat = nm / (1.0 - ADAM_B1 ** ADAM_STEP)
        v_hat = nv / (1.0 - ADAM_B2 ** ADAM_STEP)
        d_ref[...] = -ADAM_LR * (m_hat / (jnp.sqrt(v_hat) + ADAM_EPS) + ADAM_WD * w_ref[...])
        nm_ref[...] = nm
        nv_ref[...] = nv

    out = jax.ShapeDtypeStruct((r, c), f32)
    return pl.pallas_call(body, grid=(r // tr,), name=name, in_specs=[spec] * 4, out_specs=[spec] * 3,
                          out_shape=[out, out, out], compiler_params=_ew_params())(w, g, m, v)


def _place():
    return lax.axis_index("x"), lax.axis_index("y"), lax.axis_index("c")


def _chip_peers(x, y):
    out = []
    for fx, fy in ((1, 0), (0, 1), (1, 1)):
        px, py = x ^ fx, y ^ fy
        out.append((px, py, 2 * px + py))
    return out


def _all_gather(arrays, split):
    na = len(arrays)

    def body(*refs):
        ins, outs = refs[:na], refs[na:2 * na]
        send_sems, recv_sems, local_sems = refs[2 * na:]
        x, y, c = _place()
        me = 2 * x + y
        sibling = (x, y, 1 - c)
        peers = _chip_peers(x, y)

        def half(ref, a, which):
            rows = arrays[a].shape[0] // 2
            return ref.at[pl.ds(which * rows, rows), :]

        local = [pltpu.make_async_copy(ins[a], outs[a].at[me], local_sems.at[a]) for a in range(na)]
        for cp in local:
            cp.start()
        sends, fwds = [], []
        for a in range(na):
            for j, (px, py, _) in enumerate(peers):
                src = half(ins[a], a, c) if split[a] else ins[a]
                dst = half(outs[a].at[me], a, c) if split[a] else outs[a].at[me]
                cp = pltpu.make_async_remote_copy(src, dst, send_sems.at[a, j], recv_sems.at[a, j],
                                                  device_id=(px, py, c), device_id_type=MESH)
                cp.start()
                sends.append(cp)
        for a in range(na):
            for j, (px, py, pid) in enumerate(peers):
                dst = half(outs[a].at[pid], a, c) if split[a] else outs[a].at[pid]
                pltpu.make_async_remote_copy(dst, dst, send_sems.at[a, j], recv_sems.at[a, j],
                                             device_id=(px, py, c), device_id_type=MESH).wait_recv()
                if split[a]:
                    cp = pltpu.make_async_remote_copy(dst, dst, send_sems.at[a, 3 + j], recv_sems.at[a, 3 + j],
                                                      device_id=sibling, device_id_type=MESH)
                    cp.start()
                    fwds.append(cp)
        for a in range(na):
            if split[a]:
                for j, (_, _, pid) in enumerate(peers):
                    dst = half(outs[a].at[pid], a, 1 - c)
                    pltpu.make_async_remote_copy(dst, dst, send_sems.at[a, 3 + j], recv_sems.at[a, 3 + j],
                                                 device_id=sibling, device_id_type=MESH).wait_recv()
        for cp in sends + fwds:
            cp.wait_send()
        for cp in local:
            cp.wait()

    return pl.pallas_call(
        body, name="all_gather_weights", in_specs=[ANY] * na, out_specs=[ANY] * na,
        out_shape=[jax.ShapeDtypeStruct((N_CHIPS,) + a.shape, a.dtype) for a in arrays],
        scratch_shapes=[pltpu.SemaphoreType.DMA((na, 6)), pltpu.SemaphoreType.DMA((na, 6)), pltpu.SemaphoreType.DMA((na,))],
        )(*arrays)


def _swap_halves(arrays):
    na = len(arrays)

    def body(*refs):
        ins, outs = refs[:na], refs[na:2 * na]
        send_sems, recv_sems = refs[2 * na:]
        x, y, c = _place()
        copies = []
        for a in range(na):
            rows = arrays[a].shape[1] // 2
            src = ins[a].at[:, pl.ds((1 - c) * rows, rows), :]
            cp = pltpu.make_async_remote_copy(src, outs[a], send_sems.at[a], recv_sems.at[a],
                                              device_id=(x, y, 1 - c), device_id_type=MESH)
            cp.start()
            copies.append(cp)
        for cp in copies:
            cp.wait()

    return pl.pallas_call(
        body, name="grad_swap_halves", in_specs=[ANY] * na, out_specs=[ANY] * na,
        out_shape=[jax.ShapeDtypeStruct((a.shape[0], a.shape[1] // 2, a.shape[2]), a.dtype) for a in arrays],
        scratch_shapes=[pltpu.SemaphoreType.DMA((na,)), pltpu.SemaphoreType.DMA((na,))],
        )(*arrays)


def _scatter_blocks(arrays):
    na = len(arrays)

    def body(*refs):
        ins, outs = refs[:na], refs[na:2 * na]
        send_sems, recv_sems = refs[2 * na:]
        x, y, c = _place()
        copies = []
        for a in range(na):
            for j, (px, py, pid) in enumerate(_chip_peers(x, y)):
                cp = pltpu.make_async_remote_copy(ins[a].at[pid], outs[a].at[j], send_sems.at[a, j], recv_sems.at[a, j],
                                                  device_id=(px, py, c), device_id_type=MESH)
                cp.start()
                copies.append(cp)
        for cp in copies:
            cp.wait()

    return pl.pallas_call(
        body, name="grad_scatter_blocks", in_specs=[ANY] * na, out_specs=[ANY] * na,
        out_shape=[jax.ShapeDtypeStruct((3,) + a.shape[1:], a.dtype) for a in arrays],
        scratch_shapes=[pltpu.SemaphoreType.DMA((na, 3)), pltpu.SemaphoreType.DMA((na, 3))],
        )(*arrays)


def _join_halves(arrays):
    na = len(arrays)

    def body(*refs):
        ins, outs = refs[:na], refs[na:2 * na]
        send_sems, recv_sems, local_sems = refs[2 * na:]
        x, y, c = _place()
        copies = []
        for a in range(na):
            local = pltpu.make_async_copy(ins[a], outs[a].at[c], local_sems.at[a])
            local.start()
            cp = pltpu.make_async_remote_copy(ins[a], outs[a].at[c], send_sems.at[a], recv_sems.at[a],
                                              device_id=(x, y, 1 - c), device_id_type=MESH)
            cp.start()
            copies.append((local, cp, a))
        for local, cp, a in copies:
            cp.wait_send()
            pltpu.make_async_remote_copy(ins[a], outs[a].at[1 - c], send_sems.at[a], recv_sems.at[a],
                                         device_id=(x, y, 1 - c), device_id_type=MESH).wait_recv()
            local.wait()

    return pl.pallas_call(
        body, name="grad_join_halves", in_specs=[ANY] * na, out_specs=[ANY] * na,
        out_shape=[jax.ShapeDtypeStruct((2,) + a.shape, a.dtype) for a in arrays],
        scratch_shapes=[pltpu.SemaphoreType.DMA((na,)), pltpu.SemaphoreType.DMA((na,)), pltpu.SemaphoreType.DMA((na,))],
        )(*arrays)


SMALL_ROWS = 24


def _all_reduce_small(a_lb, dlb, pieces):
    d = a_lb.shape[1]
    npc = len(pieces)

    def body(*refs):
        alb_ref, dlb_ref = refs[:2]
        piece_refs = refs[2:2 + npc]
        out_ref, pack, slots, send_sems, recv_sems = refs[2 + npc:]
        x, y, c = _place()
        me = 4 * x + 2 * y + c
        pack[...] = jnp.zeros_like(pack)
        a = alb_ref[...]
        ex = jnp.exp(a - jnp.max(a, axis=0, keepdims=True))
        p = ex / jnp.sum(ex, axis=0, keepdims=True)
        first = (_iota(p.shape, 0) == 0).astype(f32)
        p0 = jnp.sum(p * first, axis=0, keepdims=True)
        pack[0:3, :] = p * dlb_ref[...] * (first - p0)
        for ref, (arr, r0) in zip(piece_refs, pieces):
            pack[r0:r0 + arr.shape[0], 0:arr.shape[1]] = ref[...]
        slots[me] = pack[...]
        copies = []
        for k in range(1, N_DEV):
            peer = (x ^ (k >> 2), y ^ ((k >> 1) & 1), c ^ (k & 1))
            cp = pltpu.make_async_remote_copy(pack, slots.at[me], send_sems.at[k - 1], recv_sems.at[k - 1],
                                              device_id=peer, device_id_type=MESH)
            cp.start()
            copies.append(cp)
        for k in range(1, N_DEV):
            src = (x ^ (k >> 2)) * 4 + (y ^ ((k >> 1) & 1)) * 2 + (c ^ (k & 1))
            pltpu.make_async_remote_copy(pack, slots.at[src], send_sems.at[k - 1], recv_sems.at[k - 1],
                                         device_id=(x, y, c), device_id_type=MESH).wait_recv()
        for cp in copies:
            cp.wait_send()
        acc = slots[0]
        for i in range(1, N_DEV):
            acc = acc + slots[i]
        out_ref[...] = acc

    return pl.pallas_call(
        body, name="all_reduce_small", in_specs=[VMEM] * (2 + npc), out_specs=VMEM,
        out_shape=jax.ShapeDtypeStruct((SMALL_ROWS, d), f32),
        scratch_shapes=[pltpu.VMEM((SMALL_ROWS, d), f32), pltpu.VMEM((N_DEV, SMALL_ROWS, d), f32),
                        pltpu.SemaphoreType.DMA((N_DEV - 1,)), pltpu.SemaphoreType.DMA((N_DEV - 1,))],
        compiler_params=pltpu.CompilerParams(vmem_limit_bytes=VMEM_LIMIT))(
            a_lb, dlb, *[p[0] for p in pieces])


def _reduce_scatter(grads):
    from_sibling = _swap_halves(grads)
    halves = [g.reshape(2 * g.shape[0], g.shape[1] // 2, g.shape[2]) for g in grads]
    chip_sums = [_chip_sum(h, t, f"grad_chip_sum_{a}") for a, (h, t) in enumerate(zip(halves, from_sibling))]
    from_chips = _scatter_blocks(chip_sums)
    finals = [_final_sum(h, t, r, f"grad_final_sum_{a}")
              for a, (h, t, r) in enumerate(zip(halves, from_sibling, from_chips))]
    joined = _join_halves(finals)
    return [j.reshape(2 * j.shape[1], j.shape[2]) for j in joined]


def kernel(x, a_w_in, a_lower_bounds, a_norm_w, a_w_out, b_w_in, b_conv_w, b_a_log, b_dt_bias, b_norm_w, b_w_out, ln_g, ln_b, loss_target, m_a_w_in, m_a_lower_bounds, m_a_norm_w, m_a_w_out, m_b_w_in, m_b_conv_w, m_b_a_log, m_b_dt_bias, m_b_norm_w, m_b_w_out, m_ln_g, m_ln_b, v_a_w_in, v_a_lower_bounds, v_a_norm_w, v_a_w_out, v_b_w_in, v_b_conv_w, v_b_a_log, v_b_dt_bias, v_b_norm_w, v_b_w_out, v_ln_g, v_ln_b):
    _, s, d = x.shape
    h = d // HEAD_DIM
    wb = b_w_in.shape[2]
    x2, target = x[0], loss_target[0]
    chip = 2 * lax.axis_index("x") + lax.axis_index("y")

    shards = [_cast(a_w_in[0], bf16, "cast_a_w_in"), _cast(a_w_out[0], bf16, "cast_a_w_out"),
              _cast(b_w_in[0], bf16, "cast_b_w_in"), _cast(b_w_out[0], bf16, "cast_b_w_out")]
    wa4, wa_out_g, wb_g, wb_out_g, conv_g = _all_gather(shards + [b_conv_w[0]], [True] * 4 + [False])
    wa_out = wa_out_g.reshape(d, d)
    wb_out = wb_out_g.reshape(d, d)
    wb2d = wb_g.transpose(1, 0, 2).reshape(d, N_CHIPS * wb)
    wb4 = wb2d[:, :4 * d].reshape(d, 4, d).transpose(1, 0, 2)
    wba = jnp.pad(wb2d[:, 4 * d:], ((0, 0), (0, LANES - 2 * h)))
    conv_w = conv_g.transpose(1, 0, 2).reshape(CONV_WIDTH, 3 * d)

    x_bf = _cast(x2, bf16, "cast_x")
    lb = _lower_bound(a_lower_bounds)
    h4a = _mm(x_bf, wa4, "nn", name="a_in_fwd")
    ya, st_a = _hgrn2_fwd(h4a, lb, a_norm_w)
    yo_a = _mm(ya, wa_out, "nn", name="a_out_fwd")
    x1, x1_bf = _ln_fwd(x2, yo_a, ln_g[0:1], ln_b[0:1])
    h4b = _mm(x1_bf, wb4, "nn", name="b_in_fwd")
    ba = _mm(x1_bf, wba, "nn", name="b_gate_fwd")
    c3 = _conv_fwd(h4b, conv_w)
    bl = ba[:, :h].T.reshape(h, s, 1)
    al = ba[:, h:2 * h].T.reshape(h, s, 1)
    alog, dtb = b_a_log.reshape(h, 1, 1), b_dt_bias.reshape(h, 1, 1)
    yb, st_b = _gdn_fwd(c3, h4b, bl, al, alog, dtb, b_norm_w)
    yo_b = _mm(yb, wb_out, "nn", name="b_out_fwd")

    dr2, dr2_bf, dg2, db2, loss = _ln_bwd(x1, yo_b, ln_g[1:2], b=ln_b[1:2], target=target, name="ln2_loss_bwd")
    g_wb_out = _mm(yb, dr2_bf, "tn", name="b_out_dw")
    dyb = _mm(dr2_bf, wb_out, "nt", name="b_out_dx")
    dc3, dhz, dbl, dal, dalog, ddtb, dnw_b = _gdn_bwd(c3, h4b, bl, al, alog, dtb, b_norm_w, st_b, dyb)
    dh3, dconv = _conv_bwd(dc3, h4b, conv_w)
    dba = jnp.pad(jnp.concatenate([dbl.reshape(h, s).T, dal.reshape(h, s).T], axis=1), ((0, 0), (0, LANES - 2 * h)))
    g_wb3 = _mm(x1_bf, dh3, "tn", name="b_in_dw_qkv")
    g_wbz = _mm(x1_bf, dhz, "tn", name="b_in_dw_z")
    g_wba = _mm(x1_bf, dba, "tn", name="b_in_dw_gate")
    dx1 = _mm(dh3, wb4, "nt", name="b_in_dx_qkv", groups=3, add=dr2, add_scale=DEEPNORM_ALPHA)
    dx1 = _mm(dhz, wb4, "nt", name="b_in_dx_z", b_g0=3, groups=1, add=dx1)
    dx1 = _mm(dba, wba, "nt", name="b_in_dx_gate", add=dx1)
    dr1, dr1_bf, dg1, db1 = _ln_bwd(x2, yo_a, ln_g[0:1], dy=dx1, name="ln1_bwd")
    g_wa_out = _mm(ya, dr1_bf, "tn", name="a_out_dw")
    dya = _mm(dr1_bf, wa_out, "nt", name="a_out_dx")
    dh4, dlb, dnw_a = _hgrn2_bwd(h4a, lb, a_norm_w, st_a, dya)
    g_wa4 = _mm(x_bf, dh4, "tn", name="a_in_dw")
    grad_x = _mm(dh4, wa4, "nt", name="a_in_dx", add=dr1, add_scale=DEEPNORM_ALPHA)

    g_wb2d = jnp.concatenate([g_wb3[0], g_wb3[1], g_wb3[2], g_wbz, g_wba[:, :2 * h]], axis=1)
    g_wb4 = g_wb2d.reshape(d, N_CHIPS, wb).transpose(1, 0, 2)
    big = _reduce_scatter([g_wa4, g_wa_out.reshape(N_CHIPS, d // N_CHIPS, d), g_wb4, g_wb_out.reshape(N_CHIPS, d // N_CHIPS, d)])
    small = _all_reduce_small(a_lower_bounds, dlb, [
        (dg1, 3), (dg2, 4), (db1, 5), (db2, 6), (dnw_a, 7), (dnw_b, 8), (dalog.reshape(1, h), 9), (ddtb.reshape(1, h), 10),
        (loss, 11), (dconv.reshape(3 * CONV_WIDTH, d), 12)])
    loss_out = small[11, 0]
    cw = 3 * d // N_CHIPS
    g_conv = lax.dynamic_slice_in_dim(small[12:24].reshape(CONV_WIDTH, 3 * d), chip * cw, cw, axis=1)
    grads = {
        "a_w_in": big[0], "a_lower_bounds": small[0:3], "a_norm_w": small[7:8, :HEAD_DIM], "a_w_out": big[1],
        "b_w_in": big[2], "b_conv_w": g_conv, "b_a_log": small[9:10, :h], "b_dt_bias": small[10:11, :h],
        "b_norm_w": small[8:9, :HEAD_DIM], "b_w_out": big[3],
        "ln_g": jnp.concatenate([small[3:4], small[4:5]], axis=0), "ln_b": jnp.concatenate([small[5:6], small[6:7]], axis=0),
    }

    weights = dict(a_w_in=a_w_in, a_lower_bounds=a_lower_bounds, a_norm_w=a_norm_w, a_w_out=a_w_out, b_w_in=b_w_in,
                   b_conv_w=b_conv_w, b_a_log=b_a_log, b_dt_bias=b_dt_bias, b_norm_w=b_norm_w, b_w_out=b_w_out,
                   ln_g=ln_g, ln_b=ln_b)
    m_in = dict(a_w_in=m_a_w_in, a_lower_bounds=m_a_lower_bounds, a_norm_w=m_a_norm_w, a_w_out=m_a_w_out, b_w_in=m_b_w_in,
                b_conv_w=m_b_conv_w, b_a_log=m_b_a_log, b_dt_bias=m_b_dt_bias, b_norm_w=m_b_norm_w, b_w_out=m_b_w_out,
                ln_g=m_ln_g, ln_b=m_ln_b)
    v_in = dict(a_w_in=v_a_w_in, a_lower_bounds=v_a_lower_bounds, a_norm_w=v_a_norm_w, a_w_out=v_a_w_out, b_w_in=v_b_w_in,
                b_conv_w=v_b_conv_w, b_a_log=v_b_a_log, b_dt_bias=v_b_dt_bias, b_norm_w=v_b_norm_w, b_w_out=v_b_w_out,
                ln_g=v_ln_g, ln_b=v_ln_b)
    out_g, out_d, out_m, out_v = [], [], [], []
    for name, w in weights.items():
        shape2 = w.shape[-2:]
        g = grads[name].reshape(shape2)
        delta, nm, nv = _adamw(w.reshape(shape2), g, m_in[name].reshape(shape2), v_in[name].reshape(shape2), "adamw_" + name)
        out_g.append(g.reshape(w.shape))
        out_d.append(delta.reshape(w.shape))
        out_m.append(nm.reshape(w.shape))
        out_v.append(nv.reshape(w.shape))
    return (loss_out, grad_x.reshape(x.shape), *out_g, *out_d, *out_m, *out_v)
```

```python
import functools
import math

import jax
import jax.numpy as jnp
from jax import lax
from jax.experimental import pallas as pl
from jax.experimental.pallas import tpu as pltpu

f32 = jnp.float32
bf16 = jnp.bfloat16
HIGHEST = lax.Precision.HIGHEST

HEAD_DIM = 128
CHUNK = 64
SUB = 16
DEPTH = 2
DEEPNORM_ALPHA = (2.0 * DEPTH) ** 0.25
LN_EPS = 1e-5
RMS_EPS = 1e-6
L2_EPS = 1e-6
ADAM_LR, ADAM_B1, ADAM_B2, ADAM_EPS, ADAM_WD, ADAM_STEP = 0.001, 0.9, 0.999, 1e-08, 0.01, 10
CONV_WIDTH = 4
EXP_CLAMP = 40.0
VMEM_LIMIT = 56 << 20
MXU_DTYPE = bf16
SUBLANES = 8
LANES = 128
N_CHIPS = 4
N_DEV = 8

NN = (((1,), (0,)), ((), ()))
NT = (((1,), (1,)), ((), ()))
TN = (((0,), (0,)), ((), ()))
MESH = pl.DeviceIdType.MESH
ANY = pl.BlockSpec(memory_space=pl.ANY)
VMEM = pl.BlockSpec(memory_space=pltpu.VMEM)


def _bdot(a, b, dn):
    return lax.dot_general(a.astype(MXU_DTYPE), b.astype(MXU_DTYPE), dn, preferred_element_type=f32)


def _hdot(a, b, dn):
    return lax.dot_general(a, b, dn, precision=lax.Precision.HIGH, preferred_element_type=f32)


def _iota(shape, dim):
    return lax.broadcasted_iota(jnp.int32, shape, dim)


def _same_sub(a, b):
    shift = int(math.log2(SUB))
    return jnp.right_shift(a, shift) == jnp.right_shift(b, shift)


def _silu(x):
    return x * jax.nn.sigmoid(x)


def _gated_rmsnorm(o, hz, nw):
    return o * lax.rsqrt(jnp.mean(o * o, axis=-1, keepdims=True) + RMS_EPS) * nw * _silu(hz)


def _hgrn2_chunk(hq, hf, hi, hz, lb, nw, st):
    c = CHUNK
    q = _silu(hq)
    forget = lb + (1.0 - lb) * jax.nn.sigmoid(hf)
    k = 1.0 - forget
    g = jnp.log(forget)
    tri = (_iota((c, c), 1) <= _iota((c, c), 0)).astype(f32)
    cum = _hdot(tri, g, NN)
    total = jnp.sum(g, axis=0, keepdims=True)
    nb = c // SUB
    blk = [cum[SUB * j:SUB * (j + 1)] for j in range(nb)]
    r16 = _iota((SUB, HEAD_DIM), 0)

    def row(j, i):
        return jnp.sum(jnp.where(r16 == i, blk[j], 0.0), axis=0, keepdims=True)

    c15, c31, c47 = row(0, SUB - 1), row(1, SUB - 1), row(2, SUB - 1)
    z = jnp.zeros((SUB, HEAD_DIM), f32)
    e = jnp.exp
    q32 = jnp.concatenate([z, z, e(blk[2] - c31), e(blk[3] - c31)], axis=0)
    k32 = jnp.concatenate([e(c31 - blk[0]), e(c31 - blk[1]), z, z], axis=0)
    q16a = jnp.concatenate([z, e(blk[1] - c15), z, z], axis=0)
    k16a = jnp.concatenate([e(c15 - blk[0]), z, z, z], axis=0)
    q16b = jnp.concatenate([z, z, z, e(blk[3] - c47)], axis=0)
    k16b = jnp.concatenate([z, z, e(c47 - blk[2]), z], axis=0)
    dmid = jnp.concatenate([blk[j] - row(j, SUB // 2 - 1) for j in range(nb)], axis=0)
    qd = e(jnp.minimum(dmid, EXP_CLAMP))
    kd = e(jnp.minimum(-dmid, EXP_CLAMP))
    q2 = jnp.concatenate([q * q32, q * q16a, q * q16b], axis=1)
    k2 = jnp.concatenate([k * k32, k * k16a, k * k16b], axis=1)
    rr, cc = _iota((c, c), 0), _iota((c, c), 1)
    diag = _same_sub(rr, cc) & (cc <= rr)
    att = _bdot(q2, k2, NT) + jnp.where(diag, _bdot(q * qd, k * kd, NT), 0.0)
    o = _bdot(att, hi, NN) + _bdot(q * e(cum), st, NT)
    st_new = st * e(total) + _bdot(hi, k * e(total - cum), TN)
    return _gated_rmsnorm(o, hz, nw), st_new


def _unit_lower_inverse(lower):
    c = CHUNK
    rr, cc = _iota((c, c), 0), _iota((c, c), 1)
    eye = (rr == cc).astype(f32)
    same = _same_sub(rr, cc)
    ld = jnp.where(same, lower, 0.0)
    off = lower - ld
    x = eye - ld
    p = ld
    for _ in range(int(math.log2(SUB)) - 1):
        p = _hdot(p, p, NN)
        x = x + _hdot(x, p, NN)
    n = _hdot(x, off, NN)
    y = eye - n
    p = n
    for _ in range(int(math.log2(c // SUB)) - 1):
        p = _hdot(p, p, NN)
        y = y + _hdot(y, p, NN)
    return _hdot(y, x, NN)


def _gdn_chunk(cq, ck, cv, hz, bl, al, alog, dtb, nw, st):
    c = CHUNK
    sq, sk, v = _silu(cq), _silu(ck), _silu(cv)
    q = sq * lax.rsqrt(jnp.sum(sq * sq, axis=-1, keepdims=True) + L2_EPS) * (HEAD_DIM ** -0.5)
    k = sk * lax.rsqrt(jnp.sum(sk * sk, axis=-1, keepdims=True) + L2_EPS)
    beta = jax.nn.sigmoid(bl)
    xs = al + dtb
    pos = xs > 0.0
    g = -jnp.exp(alog) * (jnp.where(pos, xs, 0.0) + jnp.log(1.0 + jnp.exp(jnp.where(pos, -xs, xs))))
    rr, cc = _iota((c, c), 0), _iota((c, c), 1)
    tri = (cc <= rr).astype(f32)
    gc = jnp.broadcast_to(g, (c, c))
    cum_c = _hdot(tri, gc, NN)
    cum_r = _hdot(gc, (rr <= cc).astype(f32), TN)
    g128 = jnp.broadcast_to(g, (c, HEAD_DIM))
    cum = _hdot(tri, g128, NN)
    total = jnp.sum(g128, axis=0, keepdims=True)
    diff = cum_c - cum_r
    ninf = -jnp.inf
    lower = beta * _bdot(k, k, NT) * jnp.exp(jnp.where(cc < rr, diff, ninf))
    rhs = jnp.concatenate([beta * v, (beta * jnp.exp(cum)) * k], axis=1)
    sol = _hdot(_unit_lower_inverse(lower), rhs, NN)
    u0, w = sol[:, :HEAD_DIM], sol[:, HEAD_DIM:]
    qk = _bdot(q, k, NT) * jnp.exp(jnp.where(cc <= rr, diff, ninf))
    u = u0 - _bdot(w, st, NN)
    o = _bdot(q * jnp.exp(cum), st, NN) + _bdot(qk, u, NN)
    st_new = st * jnp.exp(total) + _bdot(k * jnp.exp(total - cum), u, TN)
    return _gated_rmsnorm(o, hz, nw), st_new


def _rows(c):
    return pl.ds(pl.multiple_of(c * CHUNK, CHUNK), CHUNK)


HEADS_PER_STEP = 8
CHUNKS_PER_STEP = 4


def _scan_dims(s, d):
    h, n = d // HEAD_DIM, s // CHUNK
    ncb = min(n, CHUNKS_PER_STEP)
    hp = min(h, HEADS_PER_STEP)
    return h, n, ncb, n // ncb, ncb * CHUNK, hp


def _scan_specs(t, nblk, ncb, hp, reverse):
    blk = (lambda j: nblk - 1 - j) if reverse else (lambda j: j)
    w = hp * HEAD_DIM
    grp = lambda g: pl.BlockSpec((g, t, w), lambda h, j: (0, blk(j), h))
    one = lambda g: pl.BlockSpec((None, t, w), lambda h, j: (g, blk(j), h))
    col = lambda: pl.BlockSpec((t, w), lambda h, j: (blk(j), h))
    vec = lambda: pl.BlockSpec((1, w), lambda h, j: (0, h))
    shared = lambda: pl.BlockSpec((1, HEAD_DIM), lambda h, j: (0, 0))
    state = lambda: pl.BlockSpec((hp, ncb, HEAD_DIM, HEAD_DIM), lambda h, j: (h, blk(j), 0, 0))
    tok = lambda: pl.BlockSpec((hp, t, 1), lambda h, j: (h, blk(j), 0))
    par = lambda: pl.BlockSpec((hp, 1, 1), lambda h, j: (h, 0, 0))
    return dict(grp=grp, one=one, col=col, vec=vec, shared=shared, state=state, tok=tok, par=par)


def _scan_params():
    return pltpu.CompilerParams(dimension_semantics=("arbitrary", "arbitrary"), vmem_limit_bytes=VMEM_LIMIT)


def _state_scratch(hp):
    return [pltpu.VMEM((hp, HEAD_DIM, HEAD_DIM), f32)]


def _lanes(i):
    return pl.ds(i * HEAD_DIM, HEAD_DIM)


def _stack_heads(per_head):
    return tuple(jnp.stack(vals) for vals in zip(*per_head))


def _hgrn2_fwd(h4, lb, nw):
    _, s, d = h4.shape
    h, n, ncb, nblk, t, hp = _scan_dims(s, d)
    sp = _scan_specs(t, nblk, ncb, hp, False)

    def body(h_ref, lb_ref, nw_ref, y_ref, st_ref, st_scr):
        @pl.when(pl.program_id(1) == 0)
        def _():
            st_scr[...] = jnp.zeros_like(st_scr)

        def step(c, carry):
            r = _rows(c)
            args = [(h_ref[0, r, _lanes(i)], h_ref[1, r, _lanes(i)], h_ref[2, r, _lanes(i)], h_ref[3, r, _lanes(i)],
                     lb_ref[:, _lanes(i)], nw_ref[...], st_scr[i]) for i in range(hp)]
            y, st_new = jax.vmap(_hgrn2_chunk)(*_stack_heads(args))
            for i, a in enumerate(args):
                st_ref[i, c] = a[-1]
                y_ref[r, _lanes(i)] = y[i].astype(y_ref.dtype)
                st_scr[i] = st_new[i]
            return carry

        lax.fori_loop(0, ncb, step, 0)

    return pl.pallas_call(
        body, grid=(h // hp, nblk), name="hgrn2_fwd",
        in_specs=[sp["grp"](4), sp["vec"](), sp["shared"]()],
        out_specs=[sp["col"](), sp["state"]()],
        out_shape=[jax.ShapeDtypeStruct((s, d), bf16), jax.ShapeDtypeStruct((h, n, HEAD_DIM, HEAD_DIM), f32)],
        scratch_shapes=_state_scratch(hp), compiler_params=_scan_params())(h4, lb, nw)


def _hgrn2_bwd(h4, lb, nw, states, dy):
    _, s, d = h4.shape
    h, n, ncb, nblk, t, hp = _scan_dims(s, d)
    sp = _scan_specs(t, nblk, ncb, hp, True)

    def body(h_ref, lb_ref, nw_ref, st_ref, dy_ref, dh_ref, dlb_ref, dnw_ref, dst_scr):
        first_block = pl.program_id(1) == 0

        @pl.when(first_block)
        def _():
            dst_scr[...] = jnp.zeros_like(dst_scr)
            dlb_ref[...] = jnp.zeros_like(dlb_ref)

        @pl.when(first_block & (pl.program_id(0) == 0))
        def _():
            dnw_ref[...] = jnp.zeros_like(dnw_ref)

        def step(k, carry):
            c = ncb - 1 - k
            r = _rows(c)
            args = [(h_ref[0, r, _lanes(i)], h_ref[1, r, _lanes(i)], h_ref[2, r, _lanes(i)], h_ref[3, r, _lanes(i)],
                     lb_ref[:, _lanes(i)], nw_ref[...], st_ref[i, c]) for i in range(hp)]
            cots = [(dy_ref[r, _lanes(i)].astype(f32), dst_scr[i]) for i in range(hp)]
            dq, df, di, dz, dlb, dnw, dst = jax.vjp(jax.vmap(_hgrn2_chunk), *_stack_heads(args))[1](_stack_heads(cots))
            dnw_sum = dnw_ref[...]
            for i in range(hp):
                for g, val in enumerate((dq, df, di, dz)):
                    dh_ref[g, r, _lanes(i)] = val[i].astype(dh_ref.dtype)
                dlb_ref[:, _lanes(i)] += dlb[i]
                dnw_sum = dnw_sum + dnw[i]
                dst_scr[i] = dst[i]
            dnw_ref[...] = dnw_sum
            return carry

        lax.fori_loop(0, ncb, step, 0)

    return pl.pallas_call(
        body, grid=(h // hp, nblk), name="hgrn2_bwd",
        in_specs=[sp["grp"](4), sp["vec"](), sp["shared"](), sp["state"](), sp["col"]()],
        out_specs=[sp["grp"](4), sp["vec"](), sp["shared"]()],
        out_shape=[jax.ShapeDtypeStruct((4, s, d), bf16), jax.ShapeDtypeStruct((1, d), f32),
                   jax.ShapeDtypeStruct((1, HEAD_DIM), f32)],
        scratch_shapes=_state_scratch(hp), compiler_params=_scan_params())(h4, lb, nw, states, dy)


def _gdn_fwd(c3, h4, bl, al, alog, dtb, nw):
    _, s, d = c3.shape
    h, n, ncb, nblk, t, hp = _scan_dims(s, d)
    sp = _scan_specs(t, nblk, ncb, hp, False)

    def body(c_ref, hz_ref, bl_ref, al_ref, alog_ref, dtb_ref, nw_ref, y_ref, st_ref, st_scr):
        @pl.when(pl.program_id(1) == 0)
        def _():
            st_scr[...] = jnp.zeros_like(st_scr)

        def step(c, carry):
            r = _rows(c)
            args = [(c_ref[0, r, _lanes(i)], c_ref[1, r, _lanes(i)], c_ref[2, r, _lanes(i)], hz_ref[r, _lanes(i)],
                     bl_ref[i, r, :], al_ref[i, r, :], alog_ref[i], dtb_ref[i], nw_ref[...], st_scr[i]) for i in range(hp)]
            y, st_new = jax.vmap(_gdn_chunk)(*_stack_heads(args))
            for i, a in enumerate(args):
                st_ref[i, c] = a[-1]
                y_ref[r, _lanes(i)] = y[i].astype(y_ref.dtype)
                st_scr[i] = st_new[i]
            return carry

        lax.fori_loop(0, ncb, step, 0)

    return pl.pallas_call(
        body, grid=(h // hp, nblk), name="gdn_fwd",
        in_specs=[sp["grp"](3), sp["one"](3), sp["tok"](), sp["tok"](), sp["par"](), sp["par"](), sp["shared"]()],
        out_specs=[sp["col"](), sp["state"]()],
        out_shape=[jax.ShapeDtypeStruct((s, d), bf16), jax.ShapeDtypeStruct((h, n, HEAD_DIM, HEAD_DIM), f32)],
        scratch_shapes=_state_scratch(hp), compiler_params=_scan_params())(c3, h4, bl, al, alog, dtb, nw)


def _gdn_bwd(c3, h4, bl, al, alog, dtb, nw, states, dy):
    _, s, d = c3.shape
    h, n, ncb, nblk, t, hp = _scan_dims(s, d)
    sp = _scan_specs(t, nblk, ncb, hp, True)

    def body(c_ref, hz_ref, bl_ref, al_ref, alog_ref, dtb_ref, nw_ref, st_ref, dy_ref,
             dc_ref, dz_ref, dbl_ref, dal_ref, dalog_ref, ddtb_ref, dnw_ref, dst_scr):
        first_block = pl.program_id(1) == 0

        @pl.when(first_block)
        def _():
            dst_scr[...] = jnp.zeros_like(dst_scr)
            dalog_ref[...] = jnp.zeros_like(dalog_ref)
            ddtb_ref[...] = jnp.zeros_like(ddtb_ref)

        @pl.when(first_block & (pl.program_id(0) == 0))
        def _():
            dnw_ref[...] = jnp.zeros_like(dnw_ref)

        def step(k, carry):
            c = ncb - 1 - k
            r = _rows(c)
            args = [(c_ref[0, r, _lanes(i)], c_ref[1, r, _lanes(i)], c_ref[2, r, _lanes(i)], hz_ref[r, _lanes(i)],
                     bl_ref[i, r, :], al_ref[i, r, :], alog_ref[i], dtb_ref[i], nw_ref[...], st_ref[i, c]) for i in range(hp)]
            cots = [(dy_ref[r, _lanes(i)].astype(f32), dst_scr[i]) for i in range(hp)]
            dq, dk, dv, dz, dbl, dal, dalog, ddtb, dnw, dst = jax.vjp(
                jax.vmap(_gdn_chunk), *_stack_heads(args))[1](_stack_heads(cots))
            dnw_sum = dnw_ref[...]
            for i in range(hp):
                for g, val in enumerate((dq, dk, dv)):
                    dc_ref[g, r, _lanes(i)] = val[i]
                dz_ref[r, _lanes(i)] = dz[i].astype(dz_ref.dtype)
                dbl_ref[i, r, :] = dbl[i]
                dal_ref[i, r, :] = dal[i]
                dalog_ref[i] += dalog[i]
                ddtb_ref[i] += ddtb[i]
                dnw_sum = dnw_sum + dnw[i]
                dst_scr[i] = dst[i]
            dnw_ref[...] = dnw_sum
            return carry

        lax.fori_loop(0, ncb, step, 0)

    tokens = jax.ShapeDtypeStruct((h, s, 1), f32)
    heads = jax.ShapeDtypeStruct((h, 1, 1), f32)
    return pl.pallas_call(
        body, grid=(h // hp, nblk), name="gdn_bwd",
        in_specs=[sp["grp"](3), sp["one"](3), sp["tok"](), sp["tok"](), sp["par"](), sp["par"](), sp["shared"](),
                  sp["state"](), sp["col"]()],
        out_specs=[sp["grp"](3), sp["col"](), sp["tok"](), sp["tok"](), sp["par"](), sp["par"](), sp["shared"]()],
        out_shape=[jax.ShapeDtypeStruct((3, s, d), f32), jax.ShapeDtypeStruct((s, d), bf16), tokens, tokens, heads, heads,
                   jax.ShapeDtypeStruct((1, HEAD_DIM), f32)],
        scratch_shapes=_state_scratch(hp), compiler_params=_scan_params())(c3, h4, bl, al, alog, dtb, nw, states, dy)


def _tile(n, pref):
    return n if n <= pref else pref


def _mm(a, b, mode, *, name, out_dtype=f32, add=None, add_scale=1.0, b_g0=0, groups=None):
    squeeze = a.ndim == 2 and b.ndim == 2
    a3 = a if a.ndim == 3 else a[None]
    b3 = b if b.ndim == 3 else b[None]
    if mode == "nt":
        g = groups or a3.shape[0]
        (_, m, r), (_, n, _) = a3.shape, b3.shape
    elif mode == "nn":
        g = groups or b3.shape[0]
        (_, m, r), (_, _, n) = a3.shape, b3.shape
    else:
        g = groups or b3.shape[0]
        (_, r, m), (_, _, n) = a3.shape, b3.shape
    tm, tn, tr = _tile(m, 1024), _tile(n, 1024), _tile(r, 2048)
    nr = r // tr
    a_grouped = a3.shape[0] > 1
    dn = {"nn": NN, "nt": NT, "tn": TN}[mode]

    if mode == "nt":
        grid = (m // tm, n // tn, g * nr)
        last = g * nr - 1
        kax = 2
        a_spec = pl.BlockSpec((None, tm, tr), lambda i, j, k: (k // nr, i, k % nr))
        b_spec = pl.BlockSpec((None, tn, tr), lambda i, j, k: (b_g0 + k // nr, j, k % nr))
        o_spec = pl.BlockSpec((tm, tn), lambda i, j, k: (i, j))
        add_spec = pl.BlockSpec((tm, tn), lambda i, j, k: (i, j))
        out_shape = jax.ShapeDtypeStruct((m, n), out_dtype)
        sem = ("parallel", "parallel", "arbitrary")
    else:
        grid = (g, m // tm, n // tn, nr)
        last = nr - 1
        kax = 3
        ag = (lambda q: q) if a_grouped else (lambda q: 0)
        if mode == "nn":
            a_spec = pl.BlockSpec((None, tm, tr), lambda q, i, j, k: (ag(q), i, k))
        else:
            a_spec = pl.BlockSpec((None, tr, tm), lambda q, i, j, k: (ag(q), k, i))
        b_spec = pl.BlockSpec((None, tr, tn), lambda q, i, j, k: (b_g0 + q, k, j))
        o_spec = pl.BlockSpec((None, tm, tn), lambda q, i, j, k: (q, i, j))
        add_spec = None
        out_shape = jax.ShapeDtypeStruct((g, m, n), out_dtype)
        sem = ("parallel", "parallel", "parallel", "arbitrary")
        assert add is None

    def body(*refs):
        if add is None:
            a_ref, b_ref, o_ref, acc = refs
        else:
            a_ref, b_ref, add_ref, o_ref, acc = refs
        k = pl.program_id(kax)

        @pl.when(k == 0)
        def _():
            acc[...] = jnp.zeros_like(acc)

        acc[...] += _bdot(a_ref[...], b_ref[...], dn)

        @pl.when(k == last)
        def _():
            res = acc[...]
            if add is not None:
                res = res + add_scale * add_ref[...]
            o_ref[...] = res.astype(o_ref.dtype)

    operands = (a3, b3) if add is None else (a3, b3, add)
    in_specs = [a_spec, b_spec] if add is None else [a_spec, b_spec, add_spec]
    out = pl.pallas_call(
        body, grid=grid, name=name, in_specs=in_specs, out_specs=o_spec, out_shape=out_shape,
        scratch_shapes=[pltpu.VMEM((tm, tn), f32)],
        compiler_params=pltpu.CompilerParams(dimension_semantics=sem, vmem_limit_bytes=VMEM_LIMIT))(*operands)
    return out[0] if (squeeze and mode != "nt") else out


def _row_block(rows, cols, target_bytes=1 << 20):
    if rows <= SUBLANES:
        return rows
    tr = SUBLANES
    while tr * 2 <= rows and tr * 2 * cols * 4 <= target_bytes and rows % (tr * 2) == 0:
        tr *= 2
    return tr


def _ew_params(n_axes=1):
    return pltpu.CompilerParams(dimension_semantics=("parallel",) * n_axes, vmem_limit_bytes=VMEM_LIMIT)


def _cast(x, dtype, name, into_chip_slot=False):
    r, c = x.shape
    tr = _row_block(r, c, 2 << 20)
    tr = max(tr, 16) if r >= 16 else tr
    spec = pl.BlockSpec((tr, c), lambda i: (i, 0))
    if into_chip_slot:
        out_spec = pl.BlockSpec((None, tr, c), lambda i: (_my_chip(), i, 0))
        out_shape = jax.ShapeDtypeStruct((N_CHIPS, r, c), dtype)
    else:
        out_spec, out_shape = spec, jax.ShapeDtypeStruct((r, c), dtype)

    def body(x_ref, o_ref):
        o_ref[...] = x_ref[...].astype(dtype)

    return pl.pallas_call(body, grid=(r // tr,), name=name, in_specs=[spec], out_specs=out_spec,
                          out_shape=out_shape, compiler_params=_ew_params())(x)


def _ln_stats(r):
    mu = jnp.mean(r, axis=-1, keepdims=True)
    var = jnp.mean(jnp.square(r - mu), axis=-1, keepdims=True)
    return mu, lax.rsqrt(var + LN_EPS)


def _ln_fwd(x, yo, g, b):
    s, d = x.shape
    tr = _row_block(s, d, 2 << 20)
    tr = max(tr, 16)
    big = pl.BlockSpec((tr, d), lambda i: (i, 0))
    vec = pl.BlockSpec((1, d), lambda i: (0, 0))

    def body(x_ref, yo_ref, g_ref, b_ref, o_ref, ob_ref):
        r = DEEPNORM_ALPHA * x_ref[...] + yo_ref[...]
        mu, rstd = _ln_stats(r)
        y = (r - mu) * rstd * g_ref[...] + b_ref[...]
        o_ref[...] = y
        ob_ref[...] = y.astype(bf16)

    return pl.pallas_call(
        body, grid=(s // tr,), name="ln_fwd", in_specs=[big, big, vec, vec], out_specs=[big, big],
        out_shape=[jax.ShapeDtypeStruct((s, d), f32), jax.ShapeDtypeStruct((s, d), bf16)],
        compiler_params=_ew_params())(x, yo, g, b)


def _ln_bwd(x, yo, g, *, dy=None, b=None, target=None, name):
    s, d = x.shape
    with_loss = target is not None
    tr = max(_row_block(s, d, 2 << 20), 16)
    big = pl.BlockSpec((tr, d), lambda i: (i, 0))
    vec = pl.BlockSpec((1, d), lambda i: (0, 0))
    lane = pl.BlockSpec((1, LANES), lambda i: (0, 0))

    def body(*refs):
        if with_loss:
            x_ref, yo_ref, g_ref, b_ref, t_ref, dr_ref, drb_ref, dg_ref, db_ref, loss_ref = refs
        else:
            x_ref, yo_ref, g_ref, dy_ref, dr_ref, drb_ref, dg_ref, db_ref = refs

        @pl.when(pl.program_id(0) == 0)
        def _():
            dg_ref[...] = jnp.zeros_like(dg_ref)
            db_ref[...] = jnp.zeros_like(db_ref)
            if with_loss:
                loss_ref[...] = jnp.zeros_like(loss_ref)

        r = DEEPNORM_ALPHA * x_ref[...] + yo_ref[...]
        mu, rstd = _ln_stats(r)
        xhat = (r - mu) * rstd
        if with_loss:
            err = xhat * g_ref[...] + b_ref[...] - t_ref[...]
            loss_ref[...] += jnp.sum(err * err) * (0.5 / d)
            dyv = err * (1.0 / d)
        else:
            dyv = dy_ref[...]
        dxhat = dyv * g_ref[...]
        m1 = jnp.mean(dxhat, axis=-1, keepdims=True)
        m2 = jnp.mean(dxhat * xhat, axis=-1, keepdims=True)
        dr = rstd * (dxhat - m1 - xhat * m2)
        dr_ref[...] = dr
        drb_ref[...] = dr.astype(bf16)
        dg_ref[...] += jnp.sum(dyv * xhat, axis=0, keepdims=True)
        db_ref[...] += jnp.sum(dyv, axis=0, keepdims=True)

    operands = (x, yo, g, b, target) if with_loss else (x, yo, g, dy)
    in_specs = [big, big, vec, vec, big] if with_loss else [big, big, vec, big]
    out_specs = [big, big, vec, vec] + ([lane] if with_loss else [])
    out_shape = [jax.ShapeDtypeStruct((s, d), f32), jax.ShapeDtypeStruct((s, d), bf16),
                 jax.ShapeDtypeStruct((1, d), f32), jax.ShapeDtypeStruct((1, d), f32)]
    if with_loss:
        out_shape.append(jax.ShapeDtypeStruct((1, LANES), f32))
    return pl.pallas_call(
        body, grid=(s // tr,), name=name, in_specs=in_specs, out_specs=out_specs, out_shape=out_shape,
        compiler_params=pltpu.CompilerParams(dimension_semantics=("arbitrary",), vmem_limit_bytes=VMEM_LIMIT))(*operands)


def _lower_bound(a_lb):
    _, d = a_lb.shape

    def body(a_ref, o_ref):
        a = a_ref[...]
        ex = jnp.exp(a - jnp.max(a, axis=0, keepdims=True))
        p = ex / jnp.sum(ex, axis=0, keepdims=True)
        o_ref[...] = jnp.sum(jnp.where(_iota(p.shape, 0) == 0, p, 0.0), axis=0, keepdims=True)

    return pl.pallas_call(body, name="lower_bound", in_specs=[VMEM], out_specs=VMEM,
                          out_shape=jax.ShapeDtypeStruct((1, d), f32))(a_lb)


def _shift_rows(ext, k, rows):
    return pltpu.roll(ext, k, axis=0)[SUBLANES:SUBLANES + rows]


def _conv_tiles(s, d):
    return _tile(s, 512), _tile(d, 512)


def _conv_fwd(h4, w):
    _, s, d = h4.shape
    tr, tc = _conv_tiles(s, d)
    nj = d // tc
    hb = tr // SUBLANES
    cur = pl.BlockSpec((None, tr, tc), lambda g, j, i: (g, i, j))
    prev = pl.BlockSpec((None, SUBLANES, tc), lambda g, j, i: (g, jnp.maximum(i * hb - 1, 0), j))
    wsp = pl.BlockSpec((CONV_WIDTH, tc), lambda g, j, i: (0, g * nj + j))

    def body(x_ref, p_ref, w_ref, o_ref):
        x = x_ref[...]
        halo = jnp.where(pl.program_id(2) == 0, 0.0, p_ref[...])
        ext = jnp.concatenate([halo, x], axis=0)
        acc = w_ref[CONV_WIDTH - 1:CONV_WIDTH, :] * x
        for k in range(1, CONV_WIDTH):
            acc = acc + w_ref[CONV_WIDTH - 1 - k:CONV_WIDTH - k, :] * _shift_rows(ext, k, tr)
        o_ref[...] = acc

    return pl.pallas_call(
        body, grid=(3, nj, s // tr), name="conv_fwd", in_specs=[cur, prev, wsp], out_specs=cur,
        out_shape=jax.ShapeDtypeStruct((3, s, d), f32), compiler_params=_ew_params(3))(h4, h4, w)


def _conv_bwd(dy3, h4, w):
    _, s, d = dy3.shape
    tr, tc = _conv_tiles(s, d)
    nj = d // tc
    hb = tr // SUBLANES
    ni = s // tr
    cur = pl.BlockSpec((None, tr, tc), lambda g, j, i: (g, i, j))
    prev = pl.BlockSpec((None, SUBLANES, tc), lambda g, j, i: (g, jnp.maximum(i * hb - 1, 0), j))
    nxt = pl.BlockSpec((None, SUBLANES, tc), lambda g, j, i: (g, jnp.minimum((i + 1) * hb, s // SUBLANES - 1), j))
    wsp = pl.BlockSpec((CONV_WIDTH, tc), lambda g, j, i: (0, g * nj + j))

    def body(dy_ref, dn_ref, x_ref, p_ref, w_ref, dx_ref, dw_ref):
        i = pl.program_id(2)

        @pl.when(i == 0)
        def _():
            dw_ref[...] = jnp.zeros_like(dw_ref)

        dy = dy_ref[...]
        x = x_ref[...]
        tail = jnp.where(i == ni - 1, 0.0, dn_ref[...])
        dext = jnp.concatenate([dy, tail], axis=0)
        halo = jnp.where(i == 0, 0.0, p_ref[...])
        xext = jnp.concatenate([halo, x], axis=0)
        acc = w_ref[CONV_WIDTH - 1:CONV_WIDTH, :] * dy
        dw_ref[CONV_WIDTH - 1:CONV_WIDTH, :] += jnp.sum(dy * x, axis=0, keepdims=True)
        for k in range(1, CONV_WIDTH):
            ahead = pltpu.roll(dext, tr + SUBLANES - k, axis=0)[:tr]
            acc = acc + w_ref[CONV_WIDTH - 1 - k:CONV_WIDTH - k, :] * ahead
            dw_ref[CONV_WIDTH - 1 - k:CONV_WIDTH - k, :] += jnp.sum(dy * _shift_rows(xext, k, tr), axis=0, keepdims=True)
        dx_ref[...] = acc.astype(dx_ref.dtype)

    return pl.pallas_call(
        body, grid=(3, nj, ni), name="conv_bwd", in_specs=[cur, nxt, cur, prev, wsp], out_specs=[cur, wsp],
        out_shape=[jax.ShapeDtypeStruct((3, s, d), bf16), jax.ShapeDtypeStruct((CONV_WIDTH, 3 * d), f32)],
        compiler_params=pltpu.CompilerParams(dimension_semantics=("parallel", "parallel", "arbitrary"),
                                             vmem_limit_bytes=VMEM_LIMIT))(dy3, dy3, h4, h4, w)


def _my_core():
    return lax.axis_index("c")


def _my_chip():
    return 2 * lax.axis_index("x") + lax.axis_index("y")


def _chip_sum(halves, other, name):
    _, r2, c = other.shape
    tr = max(_row_block(r2, c), 16)
    own = pl.BlockSpec((None, tr, c), lambda p, i: (2 * p + _my_core(), i, 0))
    blk = pl.BlockSpec((None, tr, c), lambda p, i: (p, i, 0))

    def body(a_ref, b_ref, o_ref):
        o_ref[...] = (a_ref[...] + b_ref[...]).astype(o_ref.dtype)

    return pl.pallas_call(
        body, name=name, grid=(N_CHIPS, r2 // tr), in_specs=[own, blk], out_specs=blk,
        out_shape=jax.ShapeDtypeStruct(other.shape, bf16), compiler_params=_ew_params(2))(halves, other)


def _final_sum(halves, other, remote, name):
    _, r2, c = remote.shape
    tr = max(_row_block(r2, c), 16)
    own = pl.BlockSpec((None, tr, c), lambda i: (2 * _my_chip() + _my_core(), i, 0))
    sib = pl.BlockSpec((None, tr, c), lambda i: (_my_chip(), i, 0))
    slot = lambda j: pl.BlockSpec((None, tr, c), lambda i: (j, i, 0))

    def body(a_ref, b_ref, r0_ref, r1_ref, r2_ref, o_ref):
        acc = a_ref[...] + b_ref[...]
        for ref in (r0_ref, r1_ref, r2_ref):
            acc = acc + ref[...].astype(f32)
        o_ref[...] = acc

    return pl.pallas_call(
        body, name=name, grid=(r2 // tr,), in_specs=[own, sib, slot(0), slot(1), slot(2)],
        out_specs=pl.BlockSpec((None, tr, c), lambda i: (_my_core(), i, 0)),
        out_shape=jax.ShapeDtypeStruct((2, r2, c), f32), compiler_params=_ew_params())(halves, other, remote, remote, remote)


def _adamw(w, g, m, v, name):
    r, c = w.shape
    tr = _row_block(r, c)
    spec = pl.BlockSpec((tr, c), lambda i: (i, 0))

    def body(w_ref, g_ref, m_ref, v_ref, d_ref, nm_ref, nv_ref):
        gv = g_ref[...]
        nm = ADAM_B1 * m_ref[...] + (1.0 - ADAM_B1) * gv
        nv = ADAM_B2 * v_ref[...] + (1.0 - ADAM_B2) * jnp.square(gv)
        m_hat = nm / (1.0 - ADAM_B1 ** ADAM_STEP)
        v_hat = nv / (1.0 - ADAM_B2 ** ADAM_STEP)
        d_ref[...] = -ADAM_LR * (m_hat / (jnp.sqrt(v_hat) + ADAM_EPS) + ADAM_WD * w_ref[...])
        nm_ref[...] = nm
        nv_ref[...] = nv

    out = jax.ShapeDtypeStruct((r, c), f32)
    return pl.pallas_call(body, grid=(r // tr,), name=name, in_specs=[spec] * 4, out_specs=[spec] * 3,
                          out_shape=[out, out, out], compiler_params=_ew_params())(w, g, m, v)


def _place():
    return lax.axis_index("x"), lax.axis_index("y"), lax.axis_index("c")


def _chip_peers(x, y):
    out = []
    for fx, fy in ((1, 0), (0, 1), (1, 1)):
        px, py = x ^ fx, y ^ fy
        out.append((px, py, 2 * px + py))
    return out


def _all_gather(arrays, split):
    na = len(arrays)

    def body(*refs):
        outs = refs[na:2 * na]
        send_sems, recv_sems = refs[2 * na:]
        x, y, c = _place()
        me = 2 * x + y
        sibling = (x, y, 1 - c)
        peers = _chip_peers(x, y)

        def half(ref, a, which):
            rows = arrays[a].shape[1] // 2
            return ref.at[pl.ds(which * rows, rows), :]

        sends, fwds = [], []
        for a in range(na):
            for j, (px, py, _) in enumerate(peers):
                mine = half(outs[a].at[me], a, c) if split[a] else outs[a].at[me]
                cp = pltpu.make_async_remote_copy(mine, mine, send_sems.at[a, j], recv_sems.at[a, j],
                                                  device_id=(px, py, c), device_id_type=MESH)
                cp.start()
                sends.append(cp)
        for a in range(na):
            for j, (px, py, pid) in enumerate(peers):
                dst = half(outs[a].at[pid], a, c) if split[a] else outs[a].at[pid]
                pltpu.make_async_remote_copy(dst, dst, send_sems.at[a, j], recv_sems.at[a, j],
                                             device_id=(px, py, c), device_id_type=MESH).wait_recv()
                if split[a]:
                    cp = pltpu.make_async_remote_copy(dst, dst, send_sems.at[a, 3 + j], recv_sems.at[a, 3 + j],
                                                      device_id=sibling, device_id_type=MESH)
                    cp.start()
                    fwds.append(cp)
        for a in range(na):
            if split[a]:
                for j, (_, _, pid) in enumerate(peers):
                    dst = half(outs[a].at[pid], a, 1 - c)
                    pltpu.make_async_remote_copy(dst, dst, send_sems.at[a, 3 + j], recv_sems.at[a, 3 + j],
                                                 device_id=sibling, device_id_type=MESH).wait_recv()
        for cp in sends + fwds:
            cp.wait_send()

    return pl.pallas_call(
        body, name="all_gather_weights", in_specs=[ANY] * na, out_specs=[ANY] * na,
        out_shape=[jax.ShapeDtypeStruct(a.shape, a.dtype) for a in arrays],
        input_output_aliases={a: a for a in range(na)},
        scratch_shapes=[pltpu.SemaphoreType.DMA((na, 6)), pltpu.SemaphoreType.DMA((na, 6))],
        )(*arrays)


def _swap_halves(arrays):
    na = len(arrays)

    def body(*refs):
        ins, outs = refs[:na], refs[na:2 * na]
        send_sems, recv_sems = refs[2 * na:]
        x, y, c = _place()
        copies = []
        for a in range(na):
            rows = arrays[a].shape[1] // 2
            src = ins[a].at[:, pl.ds((1 - c) * rows, rows), :]
            cp = pltpu.make_async_remote_copy(src, outs[a], send_sems.at[a], recv_sems.at[a],
                                              device_id=(x, y, 1 - c), device_id_type=MESH)
            cp.start()
            copies.append(cp)
        for cp in copies:
            cp.wait()

    return pl.pallas_call(
        body, name="grad_swap_halves", in_specs=[ANY] * na, out_specs=[ANY] * na,
        out_shape=[jax.ShapeDtypeStruct((a.shape[0], a.shape[1] // 2, a.shape[2]), a.dtype) for a in arrays],
        scratch_shapes=[pltpu.SemaphoreType.DMA((na,)), pltpu.SemaphoreType.DMA((na,))],
        )(*arrays)


def _scatter_blocks(arrays):
    na = len(arrays)

    def body(*refs):
        ins, outs = refs[:na], refs[na:2 * na]
        send_sems, recv_sems = refs[2 * na:]
        x, y, c = _place()
        copies = []
        for a in range(na):
            for j, (px, py, pid) in enumerate(_chip_peers(x, y)):
                cp = pltpu.make_async_remote_copy(ins[a].at[pid], outs[a].at[j], send_sems.at[a, j], recv_sems.at[a, j],
                                                  device_id=(px, py, c), device_id_type=MESH)
                cp.start()
                copies.append(cp)
        for cp in copies:
            cp.wait()

    return pl.pallas_call(
        body, name="grad_scatter_blocks", in_specs=[ANY] * na, out_specs=[ANY] * na,
        out_shape=[jax.ShapeDtypeStruct((3,) + a.shape[1:], a.dtype) for a in arrays],
        scratch_shapes=[pltpu.SemaphoreType.DMA((na, 3)), pltpu.SemaphoreType.DMA((na, 3))],
        )(*arrays)


def _join_halves(arrays):
    na = len(arrays)

    def body(*refs):
        outs = refs[na:2 * na]
        send_sems, recv_sems = refs[2 * na:]
        x, y, c = _place()
        copies = []
        for a in range(na):
            cp = pltpu.make_async_remote_copy(outs[a].at[c], outs[a].at[c], send_sems.at[a], recv_sems.at[a],
                                              device_id=(x, y, 1 - c), device_id_type=MESH)
            cp.start()
            copies.append(cp)
        for a, cp in enumerate(copies):
            cp.wait_send()
            pltpu.make_async_remote_copy(outs[a].at[1 - c], outs[a].at[1 - c], send_sems.at[a], recv_sems.at[a],
                                         device_id=(x, y, 1 - c), device_id_type=MESH).wait_recv()

    return pl.pallas_call(
        body, name="grad_join_halves", in_specs=[ANY] * na, out_specs=[ANY] * na,
        out_shape=[jax.ShapeDtypeStruct(a.shape, a.dtype) for a in arrays],
        input_output_aliases={a: a for a in range(na)},
        scratch_shapes=[pltpu.SemaphoreType.DMA((na,)), pltpu.SemaphoreType.DMA((na,))],
        )(*arrays)


SMALL_ROWS = 24


def _all_reduce_small(a_lb, dlb, pieces):
    d = a_lb.shape[1]
    npc = len(pieces)

    def body(*refs):
        alb_ref, dlb_ref = refs[:2]
        piece_refs = refs[2:2 + npc]
        out_ref, pack, slots, send_sems, recv_sems = refs[2 + npc:]
        x, y, c = _place()
        me = 4 * x + 2 * y + c
        pack[...] = jnp.zeros_like(pack)
        a = alb_ref[...]
        ex = jnp.exp(a - jnp.max(a, axis=0, keepdims=True))
        p = ex / jnp.sum(ex, axis=0, keepdims=True)
        first = (_iota(p.shape, 0) == 0).astype(f32)
        p0 = jnp.sum(p * first, axis=0, keepdims=True)
        pack[0:3, :] = p * dlb_ref[...] * (first - p0)
        for ref, (arr, r0) in zip(piece_refs, pieces):
            pack[r0:r0 + arr.shape[0], 0:arr.shape[1]] = ref[...]
        slots[me] = pack[...]
        copies = []
        for k in range(1, N_DEV):
            peer = (x ^ (k >> 2), y ^ ((k >> 1) & 1), c ^ (k & 1))
            cp = pltpu.make_async_remote_copy(pack, slots.at[me], send_sems.at[k - 1], recv_sems.at[k - 1],
                                              device_id=peer, device_id_type=MESH)
            cp.start()
            copies.append(cp)
        for k in range(1, N_DEV):
            src = (x ^ (k >> 2)) * 4 + (y ^ ((k >> 1) & 1)) * 2 + (c ^ (k & 1))
            pltpu.make_async_remote_copy(pack, slots.at[src], send_sems.at[k - 1], recv_sems.at[k - 1],
                                         device_id=(x, y, c), device_id_type=MESH).wait_recv()
        for cp in copies:
            cp.wait_send()
        acc = slots[0]
        for i in range(1, N_DEV):
            acc = acc + slots[i]
        out_ref[...] = acc

    return pl.pallas_call(
        body, name="all_reduce_small", in_specs=[VMEM] * (2 + npc), out_specs=VMEM,
        out_shape=jax.ShapeDtypeStruct((SMALL_ROWS, d), f32),
        scratch_shapes=[pltpu.VMEM((SMALL_ROWS, d), f32), pltpu.VMEM((N_DEV, SMALL_ROWS, d), f32),
                        pltpu.SemaphoreType.DMA((N_DEV - 1,)), pltpu.SemaphoreType.DMA((N_DEV - 1,))],
        compiler_params=pltpu.CompilerParams(vmem_limit_bytes=VMEM_LIMIT))(
            a_lb, dlb, *[p[0] for p in pieces])


def _reduce_scatter(grads):
    from_sibling = _swap_halves(grads)
    halves = [g.reshape(2 * g.shape[0], g.shape[1] // 2, g.shape[2]) for g in grads]
    chip_sums = [_chip_sum(h, t, f"grad_chip_sum_{a}") for a, (h, t) in enumerate(zip(halves, from_sibling))]
    from_chips = _scatter_blocks(chip_sums)
    finals = [_final_sum(h, t, r, f"grad_final_sum_{a}")
              for a, (h, t, r) in enumerate(zip(halves, from_sibling, from_chips))]
    joined = _join_halves(finals)
    return [j.reshape(2 * j.shape[1], j.shape[2]) for j in joined]


def kernel(x, a_w_in, a_lower_bounds, a_norm_w, a_w_out, b_w_in, b_conv_w, b_a_log, b_dt_bias, b_norm_w, b_w_out, ln_g, ln_b, loss_target, m_a_w_in, m_a_lower_bounds, m_a_norm_w, m_a_w_out, m_b_w_in, m_b_conv_w, m_b_a_log, m_b_dt_bias, m_b_norm_w, m_b_w_out, m_ln_g, m_ln_b, v_a_w_in, v_a_lower_bounds, v_a_norm_w, v_a_w_out, v_b_w_in, v_b_conv_w, v_b_a_log, v_b_dt_bias, v_b_norm_w, v_b_w_out, v_ln_g, v_ln_b):
    _, s, d = x.shape
    h = d // HEAD_DIM
    wb = b_w_in.shape[2]
    x2, target = x[0], loss_target[0]
    chip = 2 * lax.axis_index("x") + lax.axis_index("y")

    shards = [_cast(a_w_in[0], bf16, "cast_a_w_in", True), _cast(a_w_out[0], bf16, "cast_a_w_out", True),
              _cast(b_w_in[0], bf16, "cast_b_w_in", True), _cast(b_w_out[0], bf16, "cast_b_w_out", True),
              _cast(b_conv_w[0], f32, "copy_b_conv_w", True)]
    wa4, wa_out_g, wb_g, wb_out_g, conv_g = _all_gather(shards, [True] * 4 + [False])
    wa_out = wa_out_g.reshape(d, d)
    wb_out = wb_out_g.reshape(d, d)
    wb2d = wb_g.transpose(1, 0, 2).reshape(d, N_CHIPS * wb)
    wb4 = wb2d[:, :4 * d].reshape(d, 4, d).transpose(1, 0, 2)
    wba = jnp.pad(wb2d[:, 4 * d:], ((0, 0), (0, LANES - 2 * h)))
    conv_w = conv_g.transpose(1, 0, 2).reshape(CONV_WIDTH, 3 * d)

    x_bf = _cast(x2, bf16, "cast_x")
    lb = _lower_bound(a_lower_bounds)
    h4a = _mm(x_bf, wa4, "nn", name="a_in_fwd")
    ya, st_a = _hgrn2_fwd(h4a, lb, a_norm_w)
    yo_a = _mm(ya, wa_out, "nn", name="a_out_fwd")
    x1, x1_bf = _ln_fwd(x2, yo_a, ln_g[0:1], ln_b[0:1])
    h4b = _mm(x1_bf, wb4, "nn", name="b_in_fwd")
    ba = _mm(x1_bf, wba, "nn", name="b_gate_fwd")
    c3 = _conv_fwd(h4b, conv_w)
    bl = ba[:, :h].T.reshape(h, s, 1)
    al = ba[:, h:2 * h].T.reshape(h, s, 1)
    alog, dtb = b_a_log.reshape(h, 1, 1), b_dt_bias.reshape(h, 1, 1)
    yb, st_b = _gdn_fwd(c3, h4b, bl, al, alog, dtb, b_norm_w)
    yo_b = _mm(yb, wb_out, "nn", name="b_out_fwd")

    dr2, dr2_bf, dg2, db2, loss = _ln_bwd(x1, yo_b, ln_g[1:2], b=ln_b[1:2], target=target, name="ln2_loss_bwd")
    g_wb_out = _mm(yb, dr2_bf, "tn", name="b_out_dw")
    dyb = _mm(dr2_bf, wb_out, "nt", name="b_out_dx")
    dc3, dhz, dbl, dal, dalog, ddtb, dnw_b = _gdn_bwd(c3, h4b, bl, al, alog, dtb, b_norm_w, st_b, dyb)
    dh3, dconv = _conv_bwd(dc3, h4b, conv_w)
    dba = jnp.pad(jnp.concatenate([dbl.reshape(h, s).T, dal.reshape(h, s).T], axis=1), ((0, 0), (0, LANES - 2 * h)))
    g_wb3 = _mm(x1_bf, dh3, "tn", name="b_in_dw_qkv")
    g_wbz = _mm(x1_bf, dhz, "tn", name="b_in_dw_z")
    g_wba = _mm(x1_bf, dba, "tn", name="b_in_dw_gate")
    dx1 = _mm(dh3, wb4, "nt", name="b_in_dx_qkv", groups=3, add=dr2, add_scale=DEEPNORM_ALPHA)
    dx1 = _mm(dhz, wb4, "nt", name="b_in_dx_z", b_g0=3, groups=1, add=dx1)
    dx1 = _mm(dba, wba, "nt", name="b_in_dx_gate", add=dx1)
    dr1, dr1_bf, dg1, db1 = _ln_bwd(x2, yo_a, ln_g[0:1], dy=dx1, name="ln1_bwd")
    g_wa_out = _mm(ya, dr1_bf, "tn", name="a_out_dw")
    dya = _mm(dr1_bf, wa_out, "nt", name="a_out_dx")
    dh4, dlb, dnw_a = _hgrn2_bwd(h4a, lb, a_norm_w, st_a, dya)
    g_wa4 = _mm(x_bf, dh4, "tn", name="a_in_dw")
    grad_x = _mm(dh4, wa4, "nt", name="a_in_dx", add=dr1, add_scale=DEEPNORM_ALPHA)

    g_wb2d = jnp.concatenate([g_wb3[0], g_wb3[1], g_wb3[2], g_wbz, g_wba[:, :2 * h]], axis=1)
    g_wb4 = g_wb2d.reshape(d, N_CHIPS, wb).transpose(1, 0, 2)
    big = _reduce_scatter([g_wa4, g_wa_out.reshape(N_CHIPS, d // N_CHIPS, d), g_wb4, g_wb_out.reshape(N_CHIPS, d // N_CHIPS, d)])
    small = _all_reduce_small(a_lower_bounds, dlb, [
        (dg1, 3), (dg2, 4), (db1, 5), (db2, 6), (dnw_a, 7), (dnw_b, 8), (dalog.reshape(1, h), 9), (ddtb.reshape(1, h), 10),
        (loss, 11), (dconv.reshape(3 * CONV_WIDTH, d), 12)])
    loss_out = small[11, 0]
    cw = 3 * d // N_CHIPS
    g_conv = lax.dynamic_slice_in_dim(small[12:24].reshape(CONV_WIDTH, 3 * d), chip * cw, cw, axis=1)
    grads = {
        "a_w_in": big[0], "a_lower_bounds": small[0:3], "a_norm_w": small[7:8, :HEAD_DIM], "a_w_out": big[1],
        "b_w_in": big[2], "b_conv_w": g_conv, "b_a_log": small[9:10, :h], "b_dt_bias": small[10:11, :h],
        "b_norm_w": small[8:9, :HEAD_DIM], "b_w_out": big[3],
        "ln_g": jnp.concatenate([small[3:4], small[4:5]], axis=0), "ln_b": jnp.concatenate([small[5:6], small[6:7]], axis=0),
    }

    weights = dict(a_w_in=a_w_in, a_lower_bounds=a_lower_bounds, a_norm_w=a_norm_w, a_w_out=a_w_out, b_w_in=b_w_in,
                   b_conv_w=b_conv_w, b_a_log=b_a_log, b_dt_bias=b_dt_bias, b_norm_w=b_norm_w, b_w_out=b_w_out,
                   ln_g=ln_g, ln_b=ln_b)
    m_in = dict(a_w_in=m_a_w_in, a_lower_bounds=m_a_lower_bounds, a_norm_w=m_a_norm_w, a_w_out=m_a_w_out, b_w_in=m_b_w_in,
                b_conv_w=m_b_conv_w, b_a_log=m_b_a_log, b_dt_bias=m_b_dt_bias, b_norm_w=m_b_norm_w, b_w_out=m_b_w_out,
                ln_g=m_ln_g, ln_b=m_ln_b)
    v_in = dict(a_w_in=v_a_w_in, a_lower_bounds=v_a_lower_bounds, a_norm_w=v_a_norm_w, a_w_out=v_a_w_out, b_w_in=v_b_w_in,
                b_conv_w=v_b_conv_w, b_a_log=v_b_a_log, b_dt_bias=v_b_dt_bias, b_norm_w=v_b_norm_w, b_w_out=v_b_w_out,
                ln_g=v_ln_g, ln_b=v_ln_b)
    out_g, out_d, out_m, out_v = [], [], [], []
    for name, w in weights.items():
        shape2 = w.shape[-2:]
        g = grads[name].reshape(shape2)
        delta, nm, nv = _adamw(w.reshape(shape2), g, m_in[name].reshape(shape2), v_in[name].reshape(shape2), "adamw_" + name)
        out_g.append(g.reshape(w.shape))
        out_d.append(delta.reshape(w.shape))
        out_m.append(nm.reshape(w.shape))
        out_v.append(nv.reshape(w.shape))
    return (loss_out, grad_x.reshape(x.shape), *out_g, *out_d, *out_m, *out_v)
```

```python
import functools
import math
from typing import Callable, NamedTuple

import jax
import jax.numpy as jnp
from jax import lax
from jax.experimental import pallas as pl
from jax.experimental.pallas import tpu as pltpu

f32 = jnp.float32
bf16 = jnp.bfloat16
HIGHEST = lax.Precision.HIGHEST

HEAD_DIM = 128
CHUNK = 64
SUB = 16
DEPTH = 2
DEEPNORM_ALPHA = (2.0 * DEPTH) ** 0.25
LN_EPS = 1e-5
RMS_EPS = 1e-6
L2_EPS = 1e-6
ADAM_LR, ADAM_B1, ADAM_B2, ADAM_EPS, ADAM_WD, ADAM_STEP = 0.001, 0.9, 0.999, 1e-08, 0.01, 10
CONV_WIDTH = 4
EXP_CLAMP = 40.0
VMEM_LIMIT = 56 << 20
MXU_DTYPE = bf16
SUBLANES = 8
LANES = 128
N_CHIPS = 4
N_DEV = 8

NN = (((1,), (0,)), ((), ()))
NT = (((1,), (1,)), ((), ()))
TN = (((0,), (0,)), ((), ()))
MESH = pl.DeviceIdType.MESH
ANY = pl.BlockSpec(memory_space=pl.ANY)
VMEM = pl.BlockSpec(memory_space=pltpu.VMEM)


def _bdot(a, b, dn):
    return lax.dot_general(a.astype(MXU_DTYPE), b.astype(MXU_DTYPE), dn, preferred_element_type=f32)


def _hdot(a, b, dn):
    return lax.dot_general(a, b, dn, precision=lax.Precision.HIGH, preferred_element_type=f32)


def _iota(shape, dim):
    return lax.broadcasted_iota(jnp.int32, shape, dim)


def _same_sub(a, b):
    shift = int(math.log2(SUB))
    return jnp.right_shift(a, shift) == jnp.right_shift(b, shift)


def _silu(x):
    return x * jax.nn.sigmoid(x)


def _gated_rmsnorm(o, hz, nw):
    return o * lax.rsqrt(jnp.mean(o * o, axis=-1, keepdims=True) + RMS_EPS) * nw * _silu(hz)


def _hgrn2_chunk(hq, hf, hi, hz, lb, nw, st):
    c = CHUNK
    q = _silu(hq)
    forget = lb + (1.0 - lb) * jax.nn.sigmoid(hf)
    k = 1.0 - forget
    g = jnp.log(forget)
    tri = (_iota((c, c), 1) <= _iota((c, c), 0)).astype(f32)
    cum = _hdot(tri, g, NN)
    total = jnp.sum(g, axis=0, keepdims=True)
    nb = c // SUB
    blk = [cum[SUB * j:SUB * (j + 1)] for j in range(nb)]
    r16 = _iota((SUB, HEAD_DIM), 0)

    def row(j, i):
        return jnp.sum(jnp.where(r16 == i, blk[j], 0.0), axis=0, keepdims=True)

    c15, c31, c47 = row(0, SUB - 1), row(1, SUB - 1), row(2, SUB - 1)
    z = jnp.zeros((SUB, HEAD_DIM), f32)
    e = jnp.exp
    q32 = jnp.concatenate([z, z, e(blk[2] - c31), e(blk[3] - c31)], axis=0)
    k32 = jnp.concatenate([e(c31 - blk[0]), e(c31 - blk[1]), z, z], axis=0)
    q16a = jnp.concatenate([z, e(blk[1] - c15), z, z], axis=0)
    k16a = jnp.concatenate([e(c15 - blk[0]), z, z, z], axis=0)
    q16b = jnp.concatenate([z, z, z, e(blk[3] - c47)], axis=0)
    k16b = jnp.concatenate([z, z, e(c47 - blk[2]), z], axis=0)
    dmid = jnp.concatenate([blk[j] - row(j, SUB // 2 - 1) for j in range(nb)], axis=0)
    qd = e(jnp.minimum(dmid, EXP_CLAMP))
    kd = e(jnp.minimum(-dmid, EXP_CLAMP))
    q2 = jnp.concatenate([q * q32, q * q16a, q * q16b], axis=1)
    k2 = jnp.concatenate([k * k32, k * k16a, k * k16b], axis=1)
    rr, cc = _iota((c, c), 0), _iota((c, c), 1)
    diag = _same_sub(rr, cc) & (cc <= rr)
    att = _bdot(q2, k2, NT) + jnp.where(diag, _bdot(q * qd, k * kd, NT), 0.0)
    o = _bdot(att, hi, NN) + _bdot(q * e(cum), st, NT)
    st_new = st * e(total) + _bdot(hi, k * e(total - cum), TN)
    return _gated_rmsnorm(o, hz, nw), st_new


def _unit_lower_inverse(lower):
    c = CHUNK
    rr, cc = _iota((c, c), 0), _iota((c, c), 1)
    eye = (rr == cc).astype(f32)
    same = _same_sub(rr, cc)
    ld = jnp.where(same, lower, 0.0)
    off = lower - ld
    x = eye - ld
    p = ld
    for _ in range(int(math.log2(SUB)) - 1):
        p = _hdot(p, p, NN)
        x = x + _hdot(x, p, NN)
    n = _hdot(x, off, NN)
    y = eye - n
    p = n
    for _ in range(int(math.log2(c // SUB)) - 1):
        p = _hdot(p, p, NN)
        y = y + _hdot(y, p, NN)
    return _hdot(y, x, NN)


@jax.custom_vjp
def _known_inverse(lower, inv):
    return inv


def _known_inverse_fwd(lower, inv):
    return inv, inv


def _known_inverse_bwd(inv, dinv):
    return -_hdot(_hdot(inv, dinv, TN), inv, NT), jnp.zeros_like(inv)


_known_inverse.defvjp(_known_inverse_fwd, _known_inverse_bwd)


def _gdn_chunk(cq, ck, cv, hz, bl, al, alog, dtb, nw, st, inv=None, with_inverse=False):
    c = CHUNK
    sq, sk, v = _silu(cq), _silu(ck), _silu(cv)
    q = sq * lax.rsqrt(jnp.sum(sq * sq, axis=-1, keepdims=True) + L2_EPS) * (HEAD_DIM ** -0.5)
    k = sk * lax.rsqrt(jnp.sum(sk * sk, axis=-1, keepdims=True) + L2_EPS)
    beta = jax.nn.sigmoid(bl)
    xs = al + dtb
    pos = xs > 0.0
    g = -jnp.exp(alog) * (jnp.where(pos, xs, 0.0) + jnp.log(1.0 + jnp.exp(jnp.where(pos, -xs, xs))))
    rr, cc = _iota((c, c), 0), _iota((c, c), 1)
    tri = (cc <= rr).astype(f32)
    gc = jnp.broadcast_to(g, (c, c))
    cum_c = _hdot(tri, gc, NN)
    cum_r = _hdot(gc, (rr <= cc).astype(f32), TN)
    g128 = jnp.broadcast_to(g, (c, HEAD_DIM))
    cum = _hdot(tri, g128, NN)
    total = jnp.sum(g128, axis=0, keepdims=True)
    diff = cum_c - cum_r
    ninf = -jnp.inf
    lower = beta * _bdot(k, k, NT) * jnp.exp(jnp.where(cc < rr, diff, ninf))
    rhs = jnp.concatenate([beta * v, (beta * jnp.exp(cum)) * k], axis=1)
    minv = _unit_lower_inverse(lower) if inv is None else _known_inverse(lower, inv)
    sol = _hdot(minv, rhs, NN)
    u0, w = sol[:, :HEAD_DIM], sol[:, HEAD_DIM:]
    qk = _bdot(q, k, NT) * jnp.exp(jnp.where(cc <= rr, diff, ninf))
    u = u0 - _bdot(w, st, NN)
    o = _bdot(q * jnp.exp(cum), st, NN) + _bdot(qk, u, NN)
    st_new = st * jnp.exp(total) + _bdot(k * jnp.exp(total - cum), u, TN)
    y = _gated_rmsnorm(o, hz, nw)
    return (y, st_new, minv) if with_inverse else (y, st_new)


def _rows(c):
    return pl.ds(pl.multiple_of(c * CHUNK, CHUNK), CHUNK)


HEADS_PER_STEP = 8
CHUNKS_PER_STEP = 4


def _scan_dims(s, d):
    h, n = d // HEAD_DIM, s // CHUNK
    ncb = min(n, CHUNKS_PER_STEP)
    hp = min(h, HEADS_PER_STEP)
    return h, n, ncb, n // ncb, ncb * CHUNK, hp


def _scan_specs(t, nblk, ncb, hp, reverse):
    blk = (lambda j: nblk - 1 - j) if reverse else (lambda j: j)
    w = hp * HEAD_DIM
    grp = lambda g: pl.BlockSpec((g, t, w), lambda h, j: (0, blk(j), h))
    one = lambda g: pl.BlockSpec((None, t, w), lambda h, j: (g, blk(j), h))
    col = lambda: pl.BlockSpec((t, w), lambda h, j: (blk(j), h))
    vec = lambda: pl.BlockSpec((1, w), lambda h, j: (0, h))
    shared = lambda: pl.BlockSpec((1, HEAD_DIM), lambda h, j: (0, 0))
    state = lambda: pl.BlockSpec((hp, ncb, HEAD_DIM, HEAD_DIM), lambda h, j: (h, blk(j), 0, 0))
    inverse = lambda: pl.BlockSpec((hp, ncb, CHUNK, CHUNK), lambda h, j: (h, blk(j), 0, 0))
    tok = lambda: pl.BlockSpec((hp, t, 1), lambda h, j: (h, blk(j), 0))
    par = lambda: pl.BlockSpec((hp, 1, 1), lambda h, j: (h, 0, 0))
    return dict(grp=grp, one=one, col=col, vec=vec, shared=shared, state=state, inverse=inverse, tok=tok, par=par)


def _scan_params():
    return pltpu.CompilerParams(dimension_semantics=("arbitrary", "arbitrary"), vmem_limit_bytes=VMEM_LIMIT)


def _state_scratch(hp):
    return [pltpu.VMEM((hp, HEAD_DIM, HEAD_DIM), f32)]


def _lanes(i):
    return pl.ds(i * HEAD_DIM, HEAD_DIM)


def _stack_heads(per_head):
    return tuple(jnp.stack(vals) for vals in zip(*per_head))


def _hgrn2_fwd(h4, lb, nw):
    _, s, d = h4.shape
    h, n, ncb, nblk, t, hp = _scan_dims(s, d)
    sp = _scan_specs(t, nblk, ncb, hp, False)

    def body(h_ref, lb_ref, nw_ref, y_ref, st_ref, st_scr):
        @pl.when(pl.program_id(1) == 0)
        def _():
            st_scr[...] = jnp.zeros_like(st_scr)

        def step(c, carry):
            r = _rows(c)
            args = [(h_ref[0, r, _lanes(i)], h_ref[1, r, _lanes(i)], h_ref[2, r, _lanes(i)], h_ref[3, r, _lanes(i)],
                     lb_ref[:, _lanes(i)], nw_ref[...], st_scr[i]) for i in range(hp)]
            y, st_new = jax.vmap(_hgrn2_chunk)(*_stack_heads(args))
            for i, a in enumerate(args):
                st_ref[i, c] = a[-1]
                y_ref[r, _lanes(i)] = y[i].astype(y_ref.dtype)
                st_scr[i] = st_new[i]
            return carry

        lax.fori_loop(0, ncb, step, 0)

    return pl.pallas_call(
        body, grid=(h // hp, nblk), name="hgrn2_fwd",
        in_specs=[sp["grp"](4), sp["vec"](), sp["shared"]()],
        out_specs=[sp["col"](), sp["state"]()],
        out_shape=[jax.ShapeDtypeStruct((s, d), bf16), jax.ShapeDtypeStruct((h, n, HEAD_DIM, HEAD_DIM), f32)],
        scratch_shapes=_state_scratch(hp), compiler_params=_scan_params())(h4, lb, nw)


def _hgrn2_bwd(h4, lb, nw, states, dy):
    _, s, d = h4.shape
    h, n, ncb, nblk, t, hp = _scan_dims(s, d)
    sp = _scan_specs(t, nblk, ncb, hp, True)

    def body(h_ref, lb_ref, nw_ref, st_ref, dy_ref, dh_ref, dlb_ref, dnw_ref, dst_scr):
        first_block = pl.program_id(1) == 0

        @pl.when(first_block)
        def _():
            dst_scr[...] = jnp.zeros_like(dst_scr)
            dlb_ref[...] = jnp.zeros_like(dlb_ref)

        @pl.when(first_block & (pl.program_id(0) == 0))
        def _():
            dnw_ref[...] = jnp.zeros_like(dnw_ref)

        def step(k, carry):
            c = ncb - 1 - k
            r = _rows(c)
            args = [(h_ref[0, r, _lanes(i)], h_ref[1, r, _lanes(i)], h_ref[2, r, _lanes(i)], h_ref[3, r, _lanes(i)],
                     lb_ref[:, _lanes(i)], nw_ref[...], st_ref[i, c]) for i in range(hp)]
            cots = [(dy_ref[r, _lanes(i)].astype(f32), dst_scr[i]) for i in range(hp)]
            dq, df, di, dz, dlb, dnw, dst = jax.vjp(jax.vmap(_hgrn2_chunk), *_stack_heads(args))[1](_stack_heads(cots))
            dnw_sum = dnw_ref[...]
            for i in range(hp):
                for g, val in enumerate((dq, df, di, dz)):
                    dh_ref[g, r, _lanes(i)] = val[i].astype(dh_ref.dtype)
                dlb_ref[:, _lanes(i)] += dlb[i]
                dnw_sum = dnw_sum + dnw[i]
                dst_scr[i] = dst[i]
            dnw_ref[...] = dnw_sum
            return carry

        lax.fori_loop(0, ncb, step, 0)

    return pl.pallas_call(
        body, grid=(h // hp, nblk), name="hgrn2_bwd",
        in_specs=[sp["grp"](4), sp["vec"](), sp["shared"](), sp["state"](), sp["col"]()],
        out_specs=[sp["grp"](4), sp["vec"](), sp["shared"]()],
        out_shape=[jax.ShapeDtypeStruct((4, s, d), bf16), jax.ShapeDtypeStruct((1, d), f32),
                   jax.ShapeDtypeStruct((1, HEAD_DIM), f32)],
        scratch_shapes=_state_scratch(hp), compiler_params=_scan_params())(h4, lb, nw, states, dy)


def _gdn_fwd(c3, h4, bl, al, alog, dtb, nw):
    _, s, d = c3.shape
    h, n, ncb, nblk, t, hp = _scan_dims(s, d)
    sp = _scan_specs(t, nblk, ncb, hp, False)

    def body(c_ref, hz_ref, bl_ref, al_ref, alog_ref, dtb_ref, nw_ref, y_ref, st_ref, inv_ref, st_scr):
        @pl.when(pl.program_id(1) == 0)
        def _():
            st_scr[...] = jnp.zeros_like(st_scr)

        def step(c, carry):
            r = _rows(c)
            args = [(c_ref[0, r, _lanes(i)], c_ref[1, r, _lanes(i)], c_ref[2, r, _lanes(i)], hz_ref[r, _lanes(i)],
                     bl_ref[i, r, :], al_ref[i, r, :], alog_ref[i], dtb_ref[i], nw_ref[...], st_scr[i]) for i in range(hp)]
            y, st_new, inv = jax.vmap(functools.partial(_gdn_chunk, with_inverse=True))(*_stack_heads(args))
            for i, a in enumerate(args):
                st_ref[i, c] = a[-1]
                inv_ref[i, c] = inv[i]
                y_ref[r, _lanes(i)] = y[i].astype(y_ref.dtype)
                st_scr[i] = st_new[i]
            return carry

        lax.fori_loop(0, ncb, step, 0)

    return pl.pallas_call(
        body, grid=(h // hp, nblk), name="gdn_fwd",
        in_specs=[sp["grp"](3), sp["one"](3), sp["tok"](), sp["tok"](), sp["par"](), sp["par"](), sp["shared"]()],
        out_specs=[sp["col"](), sp["state"](), sp["inverse"]()],
        out_shape=[jax.ShapeDtypeStruct((s, d), bf16), jax.ShapeDtypeStruct((h, n, HEAD_DIM, HEAD_DIM), f32),
                   jax.ShapeDtypeStruct((h, n, CHUNK, CHUNK), f32)],
        scratch_shapes=_state_scratch(hp), compiler_params=_scan_params())(c3, h4, bl, al, alog, dtb, nw)


def _gdn_bwd(c3, h4, bl, al, alog, dtb, nw, states, inverses, dy):
    _, s, d = c3.shape
    h, n, ncb, nblk, t, hp = _scan_dims(s, d)
    sp = _scan_specs(t, nblk, ncb, hp, True)

    def chunk(*a):
        return _gdn_chunk(*a[:-1], inv=a[-1])

    def body(c_ref, hz_ref, bl_ref, al_ref, alog_ref, dtb_ref, nw_ref, st_ref, inv_ref, dy_ref,
             dc_ref, dz_ref, dbl_ref, dal_ref, dalog_ref, ddtb_ref, dnw_ref, dst_scr):
        first_block = pl.program_id(1) == 0

        @pl.when(first_block)
        def _():
            dst_scr[...] = jnp.zeros_like(dst_scr)
            dalog_ref[...] = jnp.zeros_like(dalog_ref)
            ddtb_ref[...] = jnp.zeros_like(ddtb_ref)

        @pl.when(first_block & (pl.program_id(0) == 0))
        def _():
            dnw_ref[...] = jnp.zeros_like(dnw_ref)

        def step(k, carry):
            c = ncb - 1 - k
            r = _rows(c)
            args = [(c_ref[0, r, _lanes(i)], c_ref[1, r, _lanes(i)], c_ref[2, r, _lanes(i)], hz_ref[r, _lanes(i)],
                     bl_ref[i, r, :], al_ref[i, r, :], alog_ref[i], dtb_ref[i], nw_ref[...], st_ref[i, c], inv_ref[i, c])
                    for i in range(hp)]
            cots = [(dy_ref[r, _lanes(i)].astype(f32), dst_scr[i]) for i in range(hp)]
            dq, dk, dv, dz, dbl, dal, dalog, ddtb, dnw, dst, _ = jax.vjp(
                jax.vmap(chunk), *_stack_heads(args))[1](_stack_heads(cots))
            dnw_sum = dnw_ref[...]
            for i in range(hp):
                for g, val in enumerate((dq, dk, dv)):
                    dc_ref[g, r, _lanes(i)] = val[i]
                dz_ref[r, _lanes(i)] = dz[i].astype(dz_ref.dtype)
                dbl_ref[i, r, :] = dbl[i]
                dal_ref[i, r, :] = dal[i]
                dalog_ref[i] += dalog[i]
                ddtb_ref[i] += ddtb[i]
                dnw_sum = dnw_sum + dnw[i]
                dst_scr[i] = dst[i]
            dnw_ref[...] = dnw_sum
            return carry

        lax.fori_loop(0, ncb, step, 0)

    tokens = jax.ShapeDtypeStruct((h, s, 1), f32)
    heads = jax.ShapeDtypeStruct((h, 1, 1), f32)
    return pl.pallas_call(
        body, grid=(h // hp, nblk), name="gdn_bwd",
        in_specs=[sp["grp"](3), sp["one"](3), sp["tok"](), sp["tok"](), sp["par"](), sp["par"](), sp["shared"](),
                  sp["state"](), sp["inverse"](), sp["col"]()],
        out_specs=[sp["grp"](3), sp["col"](), sp["tok"](), sp["tok"](), sp["par"](), sp["par"](), sp["shared"]()],
        out_shape=[jax.ShapeDtypeStruct((3, s, d), f32), jax.ShapeDtypeStruct((s, d), bf16), tokens, tokens, heads, heads,
                   jax.ShapeDtypeStruct((1, HEAD_DIM), f32)],
        scratch_shapes=_state_scratch(hp), compiler_params=_scan_params())(
            c3, h4, bl, al, alog, dtb, nw, states, inverses, dy)


def _tile(n, pref):
    return n if n <= pref else pref


def _mm(a, b, mode, *, name, out_dtype=f32, add=None, add_scale=1.0, b_g0=0, groups=None, carry=None):
    squeeze = a.ndim == 2 and b.ndim == 2
    a3 = a if a.ndim == 3 else a[None]
    b3 = b if b.ndim == 3 else b[None]
    if mode == "nt":
        g = groups or a3.shape[0]
        (_, m, r), (_, n, _) = a3.shape, b3.shape
    elif mode == "nn":
        g = groups or b3.shape[0]
        (_, m, r), (_, _, n) = a3.shape, b3.shape
    else:
        g = groups or b3.shape[0]
        (_, r, m), (_, _, n) = a3.shape, b3.shape
    tm, tn, tr = _tile(m, 1024), _tile(n, 1024), _tile(r, 2048)
    nr = r // tr
    a_grouped = a3.shape[0] > 1
    dn = {"nn": NN, "nt": NT, "tn": TN}[mode]

    if mode == "nt":
        grid = (m // tm, n // tn, g * nr)
        last = g * nr - 1
        kax = 2
        a_spec = pl.BlockSpec((None, tm, tr), lambda i, j, k: (k // nr, i, k % nr))
        b_spec = pl.BlockSpec((None, tn, tr), lambda i, j, k: (b_g0 + k // nr, j, k % nr))
        o_spec = pl.BlockSpec((tm, tn), lambda i, j, k: (i, j))
        add_spec = pl.BlockSpec((tm, tn), lambda i, j, k: (i, j))
        out_shape = jax.ShapeDtypeStruct((m, n), out_dtype)
        sem = ("parallel", "parallel", "arbitrary")
    else:
        grid = (g, m // tm, n // tn, nr)
        last = nr - 1
        kax = 3
        ag = (lambda q: q) if a_grouped else (lambda q: 0)
        if mode == "nn":
            a_spec = pl.BlockSpec((None, tm, tr), lambda q, i, j, k: (ag(q), i, k))
        else:
            a_spec = pl.BlockSpec((None, tr, tm), lambda q, i, j, k: (ag(q), k, i))
        b_spec = pl.BlockSpec((None, tr, tn), lambda q, i, j, k: (b_g0 + q, k, j))
        o_spec = pl.BlockSpec((None, tm, tn), lambda q, i, j, k: (q, i, j))
        add_spec = None
        out_shape = jax.ShapeDtypeStruct((g, m, n), out_dtype)
        sem = ("parallel", "parallel", "parallel", "arbitrary")
        assert add is None

    n_main = 2 if add is None else 3
    ni = len(carry.arrays) if carry else 0
    no = len(carry.out_shapes) if carry else 0

    def body(*refs):
        a_ref, b_ref = refs[:2]
        add_ref = None if add is None else refs[2]
        ins = refs[n_main:n_main + ni]
        o_ref = refs[n_main + ni]
        outs = refs[n_main + ni + 1:n_main + ni + 1 + no]
        acc = refs[n_main + ni + 1 + no]
        sems = refs[n_main + ni + 2 + no:]
        k = pl.program_id(kax)
        ids = [pl.program_id(ax) for ax in range(len(grid))]
        if carry:
            @pl.when(functools.reduce(jnp.logical_and, [i == 0 for i in ids]))
            def _():
                carry.start(ins, outs, sems)

        @pl.when(k == 0)
        def _():
            acc[...] = jnp.zeros_like(acc)

        acc[...] += _bdot(a_ref[...], b_ref[...], dn)

        @pl.when(k == last)
        def _():
            res = acc[...]
            if add is not None:
                res = res + add_scale * add_ref[...]
            o_ref[...] = res.astype(o_ref.dtype)

        if carry:
            @pl.when(functools.reduce(jnp.logical_and, [i == e - 1 for i, e in zip(ids, grid)]))
            def _():
                carry.finish(ins, outs, sems)

    operands = [a3, b3] + ([] if add is None else [add])
    in_specs = [a_spec, b_spec] + ([] if add is None else [add_spec])
    if not carry:
        out = pl.pallas_call(
            body, grid=grid, name=name, in_specs=in_specs, out_specs=o_spec, out_shape=out_shape,
            scratch_shapes=[pltpu.VMEM((tm, tn), f32)],
            compiler_params=pltpu.CompilerParams(dimension_semantics=sem, vmem_limit_bytes=VMEM_LIMIT))(*operands)
        return out[0] if (squeeze and mode != "nt") else out
    out, *moved = pl.pallas_call(
        body, grid=grid, name=name, in_specs=in_specs + [ANY] * ni, out_specs=[o_spec] + [ANY] * no,
        out_shape=[out_shape] + list(carry.out_shapes),
        input_output_aliases={n_main + i: 1 + o for i, o in carry.aliases.items()},
        scratch_shapes=[pltpu.VMEM((tm, tn), f32)] + list(carry.sems),
        compiler_params=pltpu.CompilerParams(dimension_semantics=("arbitrary",) * len(grid),
                                             vmem_limit_bytes=VMEM_LIMIT))(*operands, *carry.arrays)
    return (out[0] if (squeeze and mode != "nt") else out), moved


def _row_block(rows, cols, target_bytes=1 << 20):
    if rows <= SUBLANES:
        return rows
    tr = SUBLANES
    while tr * 2 <= rows and tr * 2 * cols * 4 <= target_bytes and rows % (tr * 2) == 0:
        tr *= 2
    return tr


def _ew_params(n_axes=1):
    return pltpu.CompilerParams(dimension_semantics=("parallel",) * n_axes, vmem_limit_bytes=VMEM_LIMIT)


def _cast(x, dtype, name, into_chip_slot=False):
    r, c = x.shape
    tr = _row_block(r, c, 2 << 20)
    tr = max(tr, 16) if r >= 16 else tr
    spec = pl.BlockSpec((tr, c), lambda i: (i, 0))
    if into_chip_slot:
        out_spec = pl.BlockSpec((None, tr, c), lambda i: (_my_chip(), i, 0))
        out_shape = jax.ShapeDtypeStruct((N_CHIPS, r, c), dtype)
    else:
        out_spec, out_shape = spec, jax.ShapeDtypeStruct((r, c), dtype)

    def body(x_ref, o_ref):
        o_ref[...] = x_ref[...].astype(dtype)

    return pl.pallas_call(body, grid=(r // tr,), name=name, in_specs=[spec], out_specs=out_spec,
                          out_shape=out_shape, compiler_params=_ew_params())(x)


def _ln_stats(r):
    mu = jnp.mean(r, axis=-1, keepdims=True)
    var = jnp.mean(jnp.square(r - mu), axis=-1, keepdims=True)
    return mu, lax.rsqrt(var + LN_EPS)


def _ln_fwd(x, yo, g, b):
    s, d = x.shape
    tr = _row_block(s, d, 2 << 20)
    tr = max(tr, 16)
    big = pl.BlockSpec((tr, d), lambda i: (i, 0))
    vec = pl.BlockSpec((1, d), lambda i: (0, 0))

    def body(x_ref, yo_ref, g_ref, b_ref, o_ref, ob_ref):
        r = DEEPNORM_ALPHA * x_ref[...] + yo_ref[...]
        mu, rstd = _ln_stats(r)
        y = (r - mu) * rstd * g_ref[...] + b_ref[...]
        o_ref[...] = y
        ob_ref[...] = y.astype(bf16)

    return pl.pallas_call(
        body, grid=(s // tr,), name="ln_fwd", in_specs=[big, big, vec, vec], out_specs=[big, big],
        out_shape=[jax.ShapeDtypeStruct((s, d), f32), jax.ShapeDtypeStruct((s, d), bf16)],
        compiler_params=_ew_params())(x, yo, g, b)


def _ln_bwd(x, yo, g, *, dy=None, b=None, target=None, name):
    s, d = x.shape
    with_loss = target is not None
    tr = max(_row_block(s, d, 2 << 20), 16)
    big = pl.BlockSpec((tr, d), lambda i: (i, 0))
    vec = pl.BlockSpec((1, d), lambda i: (0, 0))
    lane = pl.BlockSpec((1, LANES), lambda i: (0, 0))

    def body(*refs):
        if with_loss:
            x_ref, yo_ref, g_ref, b_ref, t_ref, dr_ref, drb_ref, dg_ref, db_ref, loss_ref = refs
        else:
            x_ref, yo_ref, g_ref, dy_ref, dr_ref, drb_ref, dg_ref, db_ref = refs

        @pl.when(pl.program_id(0) == 0)
        def _():
            dg_ref[...] = jnp.zeros_like(dg_ref)
            db_ref[...] = jnp.zeros_like(db_ref)
            if with_loss:
                loss_ref[...] = jnp.zeros_like(loss_ref)

        r = DEEPNORM_ALPHA * x_ref[...] + yo_ref[...]
        mu, rstd = _ln_stats(r)
        xhat = (r - mu) * rstd
        if with_loss:
            err = xhat * g_ref[...] + b_ref[...] - t_ref[...]
            loss_ref[...] += jnp.sum(err * err) * (0.5 / d)
            dyv = err * (1.0 / d)
        else:
            dyv = dy_ref[...]
        dxhat = dyv * g_ref[...]
        m1 = jnp.mean(dxhat, axis=-1, keepdims=True)
        m2 = jnp.mean(dxhat * xhat, axis=-1, keepdims=True)
        dr = rstd * (dxhat - m1 - xhat * m2)
        dr_ref[...] = dr
        drb_ref[...] = dr.astype(bf16)
        dg_ref[...] += jnp.sum(dyv * xhat, axis=0, keepdims=True)
        db_ref[...] += jnp.sum(dyv, axis=0, keepdims=True)

    operands = (x, yo, g, b, target) if with_loss else (x, yo, g, dy)
    in_specs = [big, big, vec, vec, big] if with_loss else [big, big, vec, big]
    out_specs = [big, big, vec, vec] + ([lane] if with_loss else [])
    out_shape = [jax.ShapeDtypeStruct((s, d), f32), jax.ShapeDtypeStruct((s, d), bf16),
                 jax.ShapeDtypeStruct((1, d), f32), jax.ShapeDtypeStruct((1, d), f32)]
    if with_loss:
        out_shape.append(jax.ShapeDtypeStruct((1, LANES), f32))
    return pl.pallas_call(
        body, grid=(s // tr,), name=name, in_specs=in_specs, out_specs=out_specs, out_shape=out_shape,
        compiler_params=pltpu.CompilerParams(dimension_semantics=("arbitrary",), vmem_limit_bytes=VMEM_LIMIT))(*operands)


def _lower_bound(a_lb):
    _, d = a_lb.shape

    def body(a_ref, o_ref):
        a = a_ref[...]
        ex = jnp.exp(a - jnp.max(a, axis=0, keepdims=True))
        p = ex / jnp.sum(ex, axis=0, keepdims=True)
        o_ref[...] = jnp.sum(jnp.where(_iota(p.shape, 0) == 0, p, 0.0), axis=0, keepdims=True)

    return pl.pallas_call(body, name="lower_bound", in_specs=[VMEM], out_specs=VMEM,
                          out_shape=jax.ShapeDtypeStruct((1, d), f32))(a_lb)


def _shift_rows(ext, k, rows):
    return pltpu.roll(ext, k, axis=0)[SUBLANES:SUBLANES + rows]


def _conv_tiles(s, d):
    return _tile(s, 512), _tile(d, 512)


def _conv_fwd(h4, w):
    _, s, d = h4.shape
    tr, tc = _conv_tiles(s, d)
    nj = d // tc
    hb = tr // SUBLANES
    cur = pl.BlockSpec((None, tr, tc), lambda g, j, i: (g, i, j))
    prev = pl.BlockSpec((None, SUBLANES, tc), lambda g, j, i: (g, jnp.maximum(i * hb - 1, 0), j))
    wsp = pl.BlockSpec((CONV_WIDTH, tc), lambda g, j, i: (0, g * nj + j))

    def body(x_ref, p_ref, w_ref, o_ref):
        x = x_ref[...]
        halo = jnp.where(pl.program_id(2) == 0, 0.0, p_ref[...])
        ext = jnp.concatenate([halo, x], axis=0)
        acc = w_ref[CONV_WIDTH - 1:CONV_WIDTH, :] * x
        for k in range(1, CONV_WIDTH):
            acc = acc + w_ref[CONV_WIDTH - 1 - k:CONV_WIDTH - k, :] * _shift_rows(ext, k, tr)
        o_ref[...] = acc

    return pl.pallas_call(
        body, grid=(3, nj, s // tr), name="conv_fwd", in_specs=[cur, prev, wsp], out_specs=cur,
        out_shape=jax.ShapeDtypeStruct((3, s, d), f32), compiler_params=_ew_params(3))(h4, h4, w)


def _conv_bwd(dy3, h4, w):
    _, s, d = dy3.shape
    tr, tc = _conv_tiles(s, d)
    nj = d // tc
    hb = tr // SUBLANES
    ni = s // tr
    cur = pl.BlockSpec((None, tr, tc), lambda g, j, i: (g, i, j))
    prev = pl.BlockSpec((None, SUBLANES, tc), lambda g, j, i: (g, jnp.maximum(i * hb - 1, 0), j))
    nxt = pl.BlockSpec((None, SUBLANES, tc), lambda g, j, i: (g, jnp.minimum((i + 1) * hb, s // SUBLANES - 1), j))
    wsp = pl.BlockSpec((CONV_WIDTH, tc), lambda g, j, i: (0, g * nj + j))

    def body(dy_ref, dn_ref, x_ref, p_ref, w_ref, dx_ref, dw_ref):
        i = pl.program_id(2)

        @pl.when(i == 0)
        def _():
            dw_ref[...] = jnp.zeros_like(dw_ref)

        dy = dy_ref[...]
        x = x_ref[...]
        tail = jnp.where(i == ni - 1, 0.0, dn_ref[...])
        dext = jnp.concatenate([dy, tail], axis=0)
        halo = jnp.where(i == 0, 0.0, p_ref[...])
        xext = jnp.concatenate([halo, x], axis=0)
        acc = w_ref[CONV_WIDTH - 1:CONV_WIDTH, :] * dy
        dw_ref[CONV_WIDTH - 1:CONV_WIDTH, :] += jnp.sum(dy * x, axis=0, keepdims=True)
        for k in range(1, CONV_WIDTH):
            ahead = pltpu.roll(dext, tr + SUBLANES - k, axis=0)[:tr]
            acc = acc + w_ref[CONV_WIDTH - 1 - k:CONV_WIDTH - k, :] * ahead
            dw_ref[CONV_WIDTH - 1 - k:CONV_WIDTH - k, :] += jnp.sum(dy * _shift_rows(xext, k, tr), axis=0, keepdims=True)
        dx_ref[...] = acc.astype(dx_ref.dtype)

    return pl.pallas_call(
        body, grid=(3, nj, ni), name="conv_bwd", in_specs=[cur, nxt, cur, prev, wsp], out_specs=[cur, wsp],
        out_shape=[jax.ShapeDtypeStruct((3, s, d), bf16), jax.ShapeDtypeStruct((CONV_WIDTH, 3 * d), f32)],
        compiler_params=pltpu.CompilerParams(dimension_semantics=("parallel", "parallel", "arbitrary"),
                                             vmem_limit_bytes=VMEM_LIMIT))(dy3, dy3, h4, h4, w)


def _my_core():
    return lax.axis_index("c")


def _my_chip():
    return 2 * lax.axis_index("x") + lax.axis_index("y")


def _chip_sum(halves, other, name):
    _, r2, c = other.shape
    tr = max(_row_block(r2, c), 16)
    own = pl.BlockSpec((None, tr, c), lambda p, i: (2 * p + _my_core(), i, 0))
    blk = pl.BlockSpec((None, tr, c), lambda p, i: (p, i, 0))

    def body(a_ref, b_ref, o_ref):
        o_ref[...] = (a_ref[...] + b_ref[...]).astype(o_ref.dtype)

    return pl.pallas_call(
        body, name=name, grid=(N_CHIPS, r2 // tr), in_specs=[own, blk], out_specs=blk,
        out_shape=jax.ShapeDtypeStruct(other.shape, bf16), compiler_params=_ew_params(2))(halves, other)


def _final_sum(halves, other, remote, name):
    _, r2, c = remote.shape
    tr = max(_row_block(r2, c), 16)
    own = pl.BlockSpec((None, tr, c), lambda i: (2 * _my_chip() + _my_core(), i, 0))
    sib = pl.BlockSpec((None, tr, c), lambda i: (_my_chip(), i, 0))
    slot = lambda j: pl.BlockSpec((None, tr, c), lambda i: (j, i, 0))

    def body(a_ref, b_ref, r0_ref, r1_ref, r2_ref, o_ref):
        acc = a_ref[...] + b_ref[...]
        for ref in (r0_ref, r1_ref, r2_ref):
            acc = acc + ref[...].astype(f32)
        o_ref[...] = acc

    return pl.pallas_call(
        body, name=name, grid=(r2 // tr,), in_specs=[own, sib, slot(0), slot(1), slot(2)],
        out_specs=pl.BlockSpec((None, tr, c), lambda i: (_my_core(), i, 0)),
        out_shape=jax.ShapeDtypeStruct((2, r2, c), f32), compiler_params=_ew_params())(halves, other, remote, remote, remote)


def _adamw(w, g, m, v, name):
    r, c = w.shape
    tr = _row_block(r, c)
    spec = pl.BlockSpec((tr, c), lambda i: (i, 0))

    def body(w_ref, g_ref, m_ref, v_ref, d_ref, nm_ref, nv_ref):
        gv = g_ref[...]
        nm = ADAM_B1 * m_ref[...] + (1.0 - ADAM_B1) * gv
        nv = ADAM_B2 * v_ref[...] + (1.0 - ADAM_B2) * jnp.square(gv)
        m_hat = nm / (1.0 - ADAM_B1 ** ADAM_STEP)
        v_hat = nv / (1.0 - ADAM_B2 ** ADAM_STEP)
        d_ref[...] = -ADAM_LR * (m_hat / (jnp.sqrt(v_hat) + ADAM_EPS) + ADAM_WD * w_ref[...])
        nm_ref[...] = nm
        nv_ref[...] = nv

    out = jax.ShapeDtypeStruct((r, c), f32)
    return pl.pallas_call(body, grid=(r // tr,), name=name, in_specs=[spec] * 4, out_specs=[spec] * 3,
                          out_shape=[out, out, out], compiler_params=_ew_params())(w, g, m, v)


class _Comm(NamedTuple):
    arrays: list
    aliases: dict
    out_shapes: list
    sems: list
    start: Callable
    finish: Callable


def _run_comm(comm, name):
    ni, no = len(comm.arrays), len(comm.out_shapes)

    def body(*refs):
        ins, outs, sems = refs[:ni], refs[ni:ni + no], refs[ni + no:]
        comm.start(ins, outs, sems)
        comm.finish(ins, outs, sems)

    return pl.pallas_call(body, name=name, in_specs=[ANY] * ni, out_specs=[ANY] * no, out_shape=comm.out_shapes,
                          input_output_aliases=comm.aliases, scratch_shapes=comm.sems)(*comm.arrays)


def _place():
    return lax.axis_index("x"), lax.axis_index("y"), lax.axis_index("c")


def _chip_peers(x, y):
    out = []
    for fx, fy in ((1, 0), (0, 1), (1, 1)):
        px, py = x ^ fx, y ^ fy
        out.append((px, py, 2 * px + py))
    return out


def _all_gather(arrays, split):
    na = len(arrays)

    def part(ref, a, block, which):
        if not split[a]:
            return ref.at[block]
        rows = arrays[a].shape[1] // 2
        return ref.at[block, pl.ds(which * rows, rows), :]

    def ici_sends(outs, sems):
        send_sems, recv_sems = sems
        x, y, c = _place()
        me = 2 * x + y
        out = []
        for a in range(na):
            for j, (px, py, _) in enumerate(_chip_peers(x, y)):
                mine = part(outs[a], a, me, c)
                out.append(pltpu.make_async_remote_copy(mine, mine, send_sems.at[a, j], recv_sems.at[a, j],
                                                        device_id=(px, py, c), device_id_type=MESH))
        return out

    def forwards(outs, sems, which_core):
        send_sems, recv_sems = sems
        x, y, c = _place()
        out = []
        for a in range(na):
            if split[a]:
                for j, (_, _, pid) in enumerate(_chip_peers(x, y)):
                    got = part(outs[a], a, pid, which_core)
                    out.append(pltpu.make_async_remote_copy(got, got, send_sems.at[a, 3 + j], recv_sems.at[a, 3 + j],
                                                            device_id=(x, y, 1 - c), device_id_type=MESH))
        return out

    def arrivals(outs, sems):
        send_sems, recv_sems = sems
        x, y, c = _place()
        out = []
        for a in range(na):
            for j, (px, py, pid) in enumerate(_chip_peers(x, y)):
                got = part(outs[a], a, pid, c)
                out.append(pltpu.make_async_remote_copy(got, got, send_sems.at[a, j], recv_sems.at[a, j],
                                                        device_id=(px, py, c), device_id_type=MESH))
        return out

    def finish(ins, outs, sems):
        c = lax.axis_index("c")
        for cp in arrivals(outs, sems):
            cp.wait_recv()
        mine = forwards(outs, sems, c)
        for cp in mine:
            cp.start()
        for cp in forwards(outs, sems, 1 - c):
            cp.wait_recv()
        for cp in ici_sends(outs, sems) + mine:
            cp.wait_send()

    return _Comm(
        arrays=list(arrays), aliases={a: a for a in range(na)},
        out_shapes=[jax.ShapeDtypeStruct(a.shape, a.dtype) for a in arrays],
        sems=[pltpu.SemaphoreType.DMA((na, 6)), pltpu.SemaphoreType.DMA((na, 6))],
        start=lambda ins, outs, sems: [cp.start() for cp in ici_sends(outs, sems)], finish=finish)


def _swap_halves(arrays):
    na = len(arrays)

    def copies(ins, outs, sems):
        send_sems, recv_sems = sems
        x, y, c = _place()
        out = []
        for a in range(na):
            rows = arrays[a].shape[1] // 2
            src = ins[a].at[:, pl.ds((1 - c) * rows, rows), :]
            out.append(pltpu.make_async_remote_copy(src, outs[a], send_sems.at[a], recv_sems.at[a],
                                                    device_id=(x, y, 1 - c), device_id_type=MESH))
        return out

    return _Comm(
        arrays=list(arrays), aliases={},
        out_shapes=[jax.ShapeDtypeStruct((a.shape[0], a.shape[1] // 2, a.shape[2]), a.dtype) for a in arrays],
        sems=[pltpu.SemaphoreType.DMA((na,)), pltpu.SemaphoreType.DMA((na,))],
        start=lambda ins, outs, sems: [cp.start() for cp in copies(ins, outs, sems)],
        finish=lambda ins, outs, sems: [cp.wait() for cp in copies(ins, outs, sems)])


def _scatter_blocks(arrays):
    na = len(arrays)

    def copies(ins, outs, sems):
        send_sems, recv_sems = sems
        x, y, c = _place()
        out = []
        for a in range(na):
            for j, (px, py, pid) in enumerate(_chip_peers(x, y)):
                out.append(pltpu.make_async_remote_copy(ins[a].at[pid], outs[a].at[j], send_sems.at[a, j],
                                                        recv_sems.at[a, j], device_id=(px, py, c), device_id_type=MESH))
        return out

    return _Comm(
        arrays=list(arrays), aliases={},
        out_shapes=[jax.ShapeDtypeStruct((3,) + a.shape[1:], a.dtype) for a in arrays],
        sems=[pltpu.SemaphoreType.DMA((na, 3)), pltpu.SemaphoreType.DMA((na, 3))],
        start=lambda ins, outs, sems: [cp.start() for cp in copies(ins, outs, sems)],
        finish=lambda ins, outs, sems: [cp.wait() for cp in copies(ins, outs, sems)])


def _join_halves(arrays):
    na = len(arrays)

    def copies(outs, sems, slot):
        send_sems, recv_sems = sems
        x, y, c = _place()
        which = c if slot == "mine" else 1 - c
        return [pltpu.make_async_remote_copy(outs[a].at[which], outs[a].at[which], send_sems.at[a], recv_sems.at[a],
                                             device_id=(x, y, 1 - c), device_id_type=MESH) for a in range(na)]

    def finish(ins, outs, sems):
        for cp in copies(outs, sems, "mine"):
            cp.wait_send()
        for cp in copies(outs, sems, "sibling's"):
            cp.wait_recv()

    return _Comm(
        arrays=list(arrays), aliases={a: a for a in range(na)},
        out_shapes=[jax.ShapeDtypeStruct(a.shape, a.dtype) for a in arrays],
        sems=[pltpu.SemaphoreType.DMA((na,)), pltpu.SemaphoreType.DMA((na,))],
        start=lambda ins, outs, sems: [cp.start() for cp in copies(outs, sems, "mine")], finish=finish)


SMALL_ROWS = 24


def _all_reduce_small(a_lb, dlb, pieces):
    d = a_lb.shape[1]
    npc = len(pieces)

    def body(*refs):
        alb_ref, dlb_ref = refs[:2]
        piece_refs = refs[2:2 + npc]
        out_ref, pack, slots, send_sems, recv_sems = refs[2 + npc:]
        x, y, c = _place()
        me = 4 * x + 2 * y + c
        pack[...] = jnp.zeros_like(pack)
        a = alb_ref[...]
        ex = jnp.exp(a - jnp.max(a, axis=0, keepdims=True))
        p = ex / jnp.sum(ex, axis=0, keepdims=True)
        first = (_iota(p.shape, 0) == 0).astype(f32)
        p0 = jnp.sum(p * first, axis=0, keepdims=True)
        pack[0:3, :] = p * dlb_ref[...] * (first - p0)
        for ref, (arr, r0) in zip(piece_refs, pieces):
            pack[r0:r0 + arr.shape[0], 0:arr.shape[1]] = ref[...]
        slots[me] = pack[...]
        copies = []
        for k in range(1, N_DEV):
            peer = (x ^ (k >> 2), y ^ ((k >> 1) & 1), c ^ (k & 1))
            cp = pltpu.make_async_remote_copy(pack, slots.at[me], send_sems.at[k - 1], recv_sems.at[k - 1],
                                              device_id=peer, device_id_type=MESH)
            cp.start()
            copies.append(cp)
        for k in range(1, N_DEV):
            src = (x ^ (k >> 2)) * 4 + (y ^ ((k >> 1) & 1)) * 2 + (c ^ (k & 1))
            pltpu.make_async_remote_copy(pack, slots.at[src], send_sems.at[k - 1], recv_sems.at[k - 1],
                                         device_id=(x, y, c), device_id_type=MESH).wait_recv()
        for cp in copies:
            cp.wait_send()
        acc = slots[0]
        for i in range(1, N_DEV):
            acc = acc + slots[i]
        out_ref[...] = acc

    return pl.pallas_call(
        body, name="all_reduce_small", in_specs=[VMEM] * (2 + npc), out_specs=VMEM,
        out_shape=jax.ShapeDtypeStruct((SMALL_ROWS, d), f32),
        scratch_shapes=[pltpu.VMEM((SMALL_ROWS, d), f32), pltpu.VMEM((N_DEV, SMALL_ROWS, d), f32),
                        pltpu.SemaphoreType.DMA((N_DEV - 1,)), pltpu.SemaphoreType.DMA((N_DEV - 1,))],
        compiler_params=pltpu.CompilerParams(vmem_limit_bytes=VMEM_LIMIT))(
            a_lb, dlb, *[p[0] for p in pieces])


def _as_halves(g):
    return g.reshape(2 * g.shape[0], g.shape[1] // 2, g.shape[2])


def _chip_sums(grads, from_sibling, tag):
    return [_chip_sum(_as_halves(g), t, f"grad_chip_sum_{tag}{a}") for a, (g, t) in enumerate(zip(grads, from_sibling))]


def _final_sums(grads, from_sibling, from_chips, tag):
    return [_final_sum(_as_halves(g), t, r, f"grad_final_sum_{tag}{a}")
            for a, (g, t, r) in enumerate(zip(grads, from_sibling, from_chips))]


def kernel(x, a_w_in, a_lower_bounds, a_norm_w, a_w_out, b_w_in, b_conv_w, b_a_log, b_dt_bias, b_norm_w, b_w_out, ln_g, ln_b, loss_target, m_a_w_in, m_a_lower_bounds, m_a_norm_w, m_a_w_out, m_b_w_in, m_b_conv_w, m_b_a_log, m_b_dt_bias, m_b_norm_w, m_b_w_out, m_ln_g, m_ln_b, v_a_w_in, v_a_lower_bounds, v_a_norm_w, v_a_w_out, v_b_w_in, v_b_conv_w, v_b_a_log, v_b_dt_bias, v_b_norm_w, v_b_w_out, v_ln_g, v_ln_b):
    _, s, d = x.shape
    h = d // HEAD_DIM
    wb = b_w_in.shape[2]
    x2, target = x[0], loss_target[0]
    chip = 2 * lax.axis_index("x") + lax.axis_index("y")

    shards = [_cast(a_w_in[0], bf16, "cast_a_w_in", True), _cast(a_w_out[0], bf16, "cast_a_w_out", True),
              _cast(b_w_in[0], bf16, "cast_b_w_in", True), _cast(b_w_out[0], bf16, "cast_b_w_out", True),
              _cast(b_conv_w[0], f32, "copy_b_conv_w", True)]
    sh_a_in, sh_a_out, sh_b_in, sh_b_out, sh_conv = shards
    wa4, wa_out_g, conv_g = _run_comm(_all_gather([sh_a_in, sh_a_out, sh_conv], [True, True, False]), "all_gather_layer_a")
    wa_out = wa_out_g.reshape(d, d)
    conv_w = conv_g.transpose(1, 0, 2).reshape(CONV_WIDTH, 3 * d)

    x_bf = _cast(x2, bf16, "cast_x")
    lb = _lower_bound(a_lower_bounds)
    h4a, (wb_g, wb_out_g) = _mm(x_bf, wa4, "nn", name="a_in_fwd", carry=_all_gather([sh_b_in, sh_b_out], [True, True]))
    wb_out = wb_out_g.reshape(d, d)
    wb2d = wb_g.transpose(1, 0, 2).reshape(d, N_CHIPS * wb)
    wb4 = wb2d[:, :4 * d].reshape(d, 4, d).transpose(1, 0, 2)
    wba = jnp.pad(wb2d[:, 4 * d:], ((0, 0), (0, LANES - 2 * h)))
    ya, st_a = _hgrn2_fwd(h4a, lb, a_norm_w)
    yo_a = _mm(ya, wa_out, "nn", name="a_out_fwd")
    x1, x1_bf = _ln_fwd(x2, yo_a, ln_g[0:1], ln_b[0:1])
    h4b = _mm(x1_bf, wb4, "nn", name="b_in_fwd")
    ba = _mm(x1_bf, wba, "nn", name="b_gate_fwd")
    c3 = _conv_fwd(h4b, conv_w)
    bl = ba[:, :h].T.reshape(h, s, 1)
    al = ba[:, h:2 * h].T.reshape(h, s, 1)
    alog, dtb = b_a_log.reshape(h, 1, 1), b_dt_bias.reshape(h, 1, 1)
    yb, st_b, inv_b = _gdn_fwd(c3, h4b, bl, al, alog, dtb, b_norm_w)
    yo_b = _mm(yb, wb_out, "nn", name="b_out_fwd")

    dr2, dr2_bf, dg2, db2, loss = _ln_bwd(x1, yo_b, ln_g[1:2], b=ln_b[1:2], target=target, name="ln2_loss_bwd")
    g_wb_out = _mm(yb, dr2_bf, "tn", name="b_out_dw")
    dyb = _mm(dr2_bf, wb_out, "nt", name="b_out_dx")
    dc3, dhz, dbl, dal, dalog, ddtb, dnw_b = _gdn_bwd(c3, h4b, bl, al, alog, dtb, b_norm_w, st_b, inv_b, dyb)
    dh3, dconv = _conv_bwd(dc3, h4b, conv_w)
    dba = jnp.pad(jnp.concatenate([dbl.reshape(h, s).T, dal.reshape(h, s).T], axis=1), ((0, 0), (0, LANES - 2 * h)))
    g_wb3 = _mm(x1_bf, dh3, "tn", name="b_in_dw_qkv")
    g_wbz = _mm(x1_bf, dhz, "tn", name="b_in_dw_z")
    g_wba = _mm(x1_bf, dba, "tn", name="b_in_dw_gate")
    dx1 = _mm(dh3, wb4, "nt", name="b_in_dx_qkv", groups=3, add=dr2, add_scale=DEEPNORM_ALPHA)
    dx1 = _mm(dhz, wb4, "nt", name="b_in_dx_z", b_g0=3, groups=1, add=dx1)
    dx1 = _mm(dba, wba, "nt", name="b_in_dx_gate", add=dx1)
    dr1, dr1_bf, dg1, db1 = _ln_bwd(x2, yo_a, ln_g[0:1], dy=dx1, name="ln1_bwd")
    g_wb2d = jnp.concatenate([g_wb3[0], g_wb3[1], g_wb3[2], g_wbz, g_wba[:, :2 * h]], axis=1)
    grads_b = [g_wb2d.reshape(d, N_CHIPS, wb).transpose(1, 0, 2), g_wb_out.reshape(N_CHIPS, d // N_CHIPS, d)]
    g_wa_out, sib_b = _mm(ya, dr1_bf, "tn", name="a_out_dw", carry=_swap_halves(grads_b))
    chip_b = _chip_sums(grads_b, sib_b, "b")
    dya = _mm(dr1_bf, wa_out, "nt", name="a_out_dx")
    dh4, dlb, dnw_a = _hgrn2_bwd(h4a, lb, a_norm_w, st_a, dya)
    g_wa4, far_b = _mm(x_bf, dh4, "tn", name="a_in_dw", carry=_scatter_blocks(chip_b))
    grad_x = _mm(dh4, wa4, "nt", name="a_in_dx", add=dr1, add_scale=DEEPNORM_ALPHA)

    grads_a = [g_wa4, g_wa_out.reshape(N_CHIPS, d // N_CHIPS, d)]
    sib_a = _run_comm(_swap_halves(grads_a), "grad_swap_halves_a")
    far_a = _run_comm(_scatter_blocks(_chip_sums(grads_a, sib_a, "a")), "grad_scatter_blocks_a")
    finals = _final_sums(grads_a, sib_a, far_a, "a") + _final_sums(grads_b, sib_b, far_b, "b")
    big = [j.reshape(2 * j.shape[1], j.shape[2]) for j in _run_comm(_join_halves(finals), "grad_join_halves")]
    small = _all_reduce_small(a_lower_bounds, dlb, [
        (dg1, 3), (dg2, 4), (db1, 5), (db2, 6), (dnw_a, 7), (dnw_b, 8), (dalog.reshape(1, h), 9), (ddtb.reshape(1, h), 10),
        (loss, 11), (dconv.reshape(3 * CONV_WIDTH, d), 12)])
    loss_out = small[11, 0]
    cw = 3 * d // N_CHIPS
    g_conv = lax.dynamic_slice_in_dim(small[12:24].reshape(CONV_WIDTH, 3 * d), chip * cw, cw, axis=1)
    grads = {
        "a_w_in": big[0], "a_lower_bounds": small[0:3], "a_norm_w": small[7:8, :HEAD_DIM], "a_w_out": big[1],
        "b_w_in": big[2], "b_conv_w": g_conv, "b_a_log": small[9:10, :h], "b_dt_bias": small[10:11, :h],
        "b_norm_w": small[8:9, :HEAD_DIM], "b_w_out": big[3],
        "ln_g": jnp.concatenate([small[3:4], small[4:5]], axis=0), "ln_b": jnp.concatenate([small[5:6], small[6:7]], axis=0),
    }

    weights = dict(a_w_in=a_w_in, a_lower_bounds=a_lower_bounds, a_norm_w=a_norm_w, a_w_out=a_w_out, b_w_in=b_w_in,
                   b_conv_w=b_conv_w, b_a_log=b_a_log, b_dt_bias=b_dt_bias, b_norm_w=b_norm_w, b_w_out=b_w_out,
                   ln_g=ln_g, ln_b=ln_b)
    m_in = dict(a_w_in=m_a_w_in, a_lower_bounds=m_a_lower_bounds, a_norm_w=m_a_norm_w, a_w_out=m_a_w_out, b_w_in=m_b_w_in,
                b_conv_w=m_b_conv_w, b_a_log=m_b_a_log, b_dt_bias=m_b_dt_bias, b_norm_w=m_b_norm_w, b_w_out=m_b_w_out,
                ln_g=m_ln_g, ln_b=m_ln_b)
    v_in = dict(a_w_in=v_a_w_in, a_lower_bounds=v_a_lower_bounds, a_norm_w=v_a_norm_w, a_w_out=v_a_w_out, b_w_in=v_b_w_in,
                b_conv_w=v_b_conv_w, b_a_log=v_b_a_log, b_dt_bias=v_b_dt_bias, b_norm_w=v_b_norm_w, b_w_out=v_b_w_out,
                ln_g=v_ln_g, ln_b=v_ln_b)
    out_g, out_d, out_m, out_v = [], [], [], []
    for name, w in weights.items():
        shape2 = w.shape[-2:]
        g = grads[name].reshape(shape2)
        delta, nm, nv = _adamw(w.reshape(shape2), g, m_in[name].reshape(shape2), v_in[name].reshape(shape2), "adamw_" + name)
        out_g.append(g.reshape(w.shape))
        out_d.append(delta.reshape(w.shape))
        out_m.append(nm.reshape(w.shape))
        out_v.append(nv.reshape(w.shape))
    return (loss_out, grad_x.reshape(x.shape), *out_g, *out_d, *out_m, *out_v)
```

```python
import functools
import math
from typing import Callable, NamedTuple

import jax
import jax.numpy as jnp
from jax import lax
from jax.experimental import pallas as pl
from jax.experimental.pallas import tpu as pltpu

f32 = jnp.float32
bf16 = jnp.bfloat16
HIGHEST = lax.Precision.HIGHEST

HEAD_DIM = 128
CHUNK = 64
SUB = 16
DEPTH = 2
DEEPNORM_ALPHA = (2.0 * DEPTH) ** 0.25
LN_EPS = 1e-5
RMS_EPS = 1e-6
L2_EPS = 1e-6
ADAM_LR, ADAM_B1, ADAM_B2, ADAM_EPS, ADAM_WD, ADAM_STEP = 0.001, 0.9, 0.999, 1e-08, 0.01, 10
CONV_WIDTH = 4
EXP_CLAMP = 40.0
VMEM_LIMIT = 56 << 20
MXU_DTYPE = bf16
SUBLANES = 8
LANES = 128
N_CHIPS = 4
N_DEV = 8

NN = (((1,), (0,)), ((), ()))
NT = (((1,), (1,)), ((), ()))
TN = (((0,), (0,)), ((), ()))
MESH = pl.DeviceIdType.MESH
ANY = pl.BlockSpec(memory_space=pl.ANY)
VMEM = pl.BlockSpec(memory_space=pltpu.VMEM)


def _bdot(a, b, dn):
    return lax.dot_general(a.astype(MXU_DTYPE), b.astype(MXU_DTYPE), dn, preferred_element_type=f32)


def _hdot(a, b, dn):
    return lax.dot_general(a, b, dn, precision=lax.Precision.HIGH, preferred_element_type=f32)


def _iota(shape, dim):
    return lax.broadcasted_iota(jnp.int32, shape, dim)


def _same_sub(a, b):
    shift = int(math.log2(SUB))
    return jnp.right_shift(a, shift) == jnp.right_shift(b, shift)


def _silu(x):
    return x * jax.nn.sigmoid(x)


def _gated_rmsnorm(o, hz, nw):
    return o * lax.rsqrt(jnp.mean(o * o, axis=-1, keepdims=True) + RMS_EPS) * nw * _silu(hz)


def _hgrn2_chunk(hq, hf, hi, hz, lb, nw, st):
    c = CHUNK
    q = _silu(hq)
    forget = lb + (1.0 - lb) * jax.nn.sigmoid(hf)
    k = 1.0 - forget
    g = jnp.log(forget)
    tri = (_iota((c, c), 1) <= _iota((c, c), 0)).astype(f32)
    cum = _hdot(tri, g, NN)
    total = jnp.sum(g, axis=0, keepdims=True)
    nb = c // SUB
    blk = [cum[SUB * j:SUB * (j + 1)] for j in range(nb)]
    r16 = _iota((SUB, HEAD_DIM), 0)

    def row(j, i):
        return jnp.sum(jnp.where(r16 == i, blk[j], 0.0), axis=0, keepdims=True)

    c15, c31, c47 = row(0, SUB - 1), row(1, SUB - 1), row(2, SUB - 1)
    z = jnp.zeros((SUB, HEAD_DIM), f32)
    e = jnp.exp
    q32 = jnp.concatenate([z, z, e(blk[2] - c31), e(blk[3] - c31)], axis=0)
    k32 = jnp.concatenate([e(c31 - blk[0]), e(c31 - blk[1]), z, z], axis=0)
    q16a = jnp.concatenate([z, e(blk[1] - c15), z, z], axis=0)
    k16a = jnp.concatenate([e(c15 - blk[0]), z, z, z], axis=0)
    q16b = jnp.concatenate([z, z, z, e(blk[3] - c47)], axis=0)
    k16b = jnp.concatenate([z, z, e(c47 - blk[2]), z], axis=0)
    dmid = jnp.concatenate([blk[j] - row(j, SUB // 2 - 1) for j in range(nb)], axis=0)
    qd = e(jnp.minimum(dmid, EXP_CLAMP))
    kd = e(jnp.minimum(-dmid, EXP_CLAMP))
    q2 = jnp.concatenate([q * q32, q * q16a, q * q16b], axis=1)
    k2 = jnp.concatenate([k * k32, k * k16a, k * k16b], axis=1)
    rr, cc = _iota((c, c), 0), _iota((c, c), 1)
    diag = _same_sub(rr, cc) & (cc <= rr)
    att = _bdot(q2, k2, NT) + jnp.where(diag, _bdot(q * qd, k * kd, NT), 0.0)
    o = _bdot(att, hi, NN) + _bdot(q * e(cum), st, NT)
    st_new = st * e(total) + _bdot(hi, k * e(total - cum), TN)
    return _gated_rmsnorm(o, hz, nw), st_new


def _unit_lower_inverse(lower):
    c = CHUNK
    rr, cc = _iota((c, c), 0), _iota((c, c), 1)
    eye = (rr == cc).astype(f32)
    same = _same_sub(rr, cc)
    ld = jnp.where(same, lower, 0.0)
    off = lower - ld
    x = eye - ld
    p = ld
    for _ in range(int(math.log2(SUB)) - 1):
        p = _hdot(p, p, NN)
        x = x + _hdot(x, p, NN)
    n = _hdot(x, off, NN)
    y = eye - n
    p = n
    for _ in range(int(math.log2(c // SUB)) - 1):
        p = _hdot(p, p, NN)
        y = y + _hdot(y, p, NN)
    return _hdot(y, x, NN)


@jax.custom_vjp
def _known_inverse(lower, inv):
    return inv


def _known_inverse_fwd(lower, inv):
    return inv, inv


def _known_inverse_bwd(inv, dinv):
    return -_hdot(_hdot(inv, dinv, TN), inv, NT), jnp.zeros_like(inv)


_known_inverse.defvjp(_known_inverse_fwd, _known_inverse_bwd)


def _gdn_chunk(cq, ck, cv, hz, bl, al, alog, dtb, nw, st, inv=None, with_inverse=False):
    c = CHUNK
    sq, sk, v = _silu(cq), _silu(ck), _silu(cv)
    q = sq * lax.rsqrt(jnp.sum(sq * sq, axis=-1, keepdims=True) + L2_EPS) * (HEAD_DIM ** -0.5)
    k = sk * lax.rsqrt(jnp.sum(sk * sk, axis=-1, keepdims=True) + L2_EPS)
    beta = jax.nn.sigmoid(bl)
    xs = al + dtb
    pos = xs > 0.0
    g = -jnp.exp(alog) * (jnp.where(pos, xs, 0.0) + jnp.log(1.0 + jnp.exp(jnp.where(pos, -xs, xs))))
    rr, cc = _iota((c, c), 0), _iota((c, c), 1)
    tri = (cc <= rr).astype(f32)
    gc = jnp.broadcast_to(g, (c, c))
    cum_c = _hdot(tri, gc, NN)
    cum_r = _hdot(gc, (rr <= cc).astype(f32), TN)
    g128 = jnp.broadcast_to(g, (c, HEAD_DIM))
    cum = _hdot(tri, g128, NN)
    total = jnp.sum(g128, axis=0, keepdims=True)
    diff = cum_c - cum_r
    ninf = -jnp.inf
    lower = beta * _bdot(k, k, NT) * jnp.exp(jnp.where(cc < rr, diff, ninf))
    rhs = jnp.concatenate([beta * v, (beta * jnp.exp(cum)) * k], axis=1)
    minv = _unit_lower_inverse(lower) if inv is None else _known_inverse(lower, inv)
    sol = _hdot(minv, rhs, NN)
    u0, w = sol[:, :HEAD_DIM], sol[:, HEAD_DIM:]
    qk = _bdot(q, k, NT) * jnp.exp(jnp.where(cc <= rr, diff, ninf))
    u = u0 - _bdot(w, st, NN)
    o = _bdot(q * jnp.exp(cum), st, NN) + _bdot(qk, u, NN)
    st_new = st * jnp.exp(total) + _bdot(k * jnp.exp(total - cum), u, TN)
    y = _gated_rmsnorm(o, hz, nw)
    return (y, st_new, minv) if with_inverse else (y, st_new)


def _rows(c):
    return pl.ds(pl.multiple_of(c * CHUNK, CHUNK), CHUNK)


HEADS_PER_STEP = 8
CHUNKS_PER_STEP = 4


def _scan_dims(s, d):
    h, n = d // HEAD_DIM, s // CHUNK
    ncb = min(n, CHUNKS_PER_STEP)
    hp = min(h, HEADS_PER_STEP)
    return h, n, ncb, n // ncb, ncb * CHUNK, hp


def _scan_specs(t, nblk, ncb, hp, reverse):
    blk = (lambda j: nblk - 1 - j) if reverse else (lambda j: j)
    w = hp * HEAD_DIM
    grp = lambda g: pl.BlockSpec((g, t, w), lambda h, j: (0, blk(j), h))
    one = lambda g: pl.BlockSpec((None, t, w), lambda h, j: (g, blk(j), h))
    col = lambda: pl.BlockSpec((t, w), lambda h, j: (blk(j), h))
    vec = lambda: pl.BlockSpec((1, w), lambda h, j: (0, h))
    shared = lambda: pl.BlockSpec((1, HEAD_DIM), lambda h, j: (0, 0))
    state = lambda: pl.BlockSpec((hp, ncb, HEAD_DIM, HEAD_DIM), lambda h, j: (h, blk(j), 0, 0))
    inverse = lambda: pl.BlockSpec((hp, ncb, CHUNK, CHUNK), lambda h, j: (h, blk(j), 0, 0))
    tok = lambda: pl.BlockSpec((hp, t, 1), lambda h, j: (h, blk(j), 0))
    par = lambda: pl.BlockSpec((hp, 1, 1), lambda h, j: (h, 0, 0))
    return dict(grp=grp, one=one, col=col, vec=vec, shared=shared, state=state, inverse=inverse, tok=tok, par=par)


def _scan_params():
    return pltpu.CompilerParams(dimension_semantics=("arbitrary", "arbitrary"), vmem_limit_bytes=VMEM_LIMIT)


def _state_scratch(hp):
    return [pltpu.VMEM((hp, HEAD_DIM, HEAD_DIM), f32)]


def _lanes(i):
    return pl.ds(i * HEAD_DIM, HEAD_DIM)


def _stack_heads(per_head):
    return tuple(jnp.stack(vals) for vals in zip(*per_head))


def _hgrn2_fwd(h4, lb, nw):
    _, s, d = h4.shape
    h, n, ncb, nblk, t, hp = _scan_dims(s, d)
    sp = _scan_specs(t, nblk, ncb, hp, False)

    def body(h_ref, lb_ref, nw_ref, y_ref, st_ref, st_scr):
        @pl.when(pl.program_id(1) == 0)
        def _():
            st_scr[...] = jnp.zeros_like(st_scr)

        def step(c, carry):
            r = _rows(c)
            args = [(h_ref[0, r, _lanes(i)], h_ref[1, r, _lanes(i)], h_ref[2, r, _lanes(i)], h_ref[3, r, _lanes(i)],
                     lb_ref[:, _lanes(i)], nw_ref[...], st_scr[i]) for i in range(hp)]
            y, st_new = jax.vmap(_hgrn2_chunk)(*_stack_heads(args))
            for i, a in enumerate(args):
                st_ref[i, c] = a[-1]
                y_ref[r, _lanes(i)] = y[i].astype(y_ref.dtype)
                st_scr[i] = st_new[i]
            return carry

        lax.fori_loop(0, ncb, step, 0)

    return pl.pallas_call(
        body, grid=(h // hp, nblk), name="hgrn2_fwd",
        in_specs=[sp["grp"](4), sp["vec"](), sp["shared"]()],
        out_specs=[sp["col"](), sp["state"]()],
        out_shape=[jax.ShapeDtypeStruct((s, d), bf16), jax.ShapeDtypeStruct((h, n, HEAD_DIM, HEAD_DIM), f32)],
        scratch_shapes=_state_scratch(hp), compiler_params=_scan_params())(h4, lb, nw)


def _hgrn2_bwd(h4, lb, nw, states, dy, carry=None):
    _, s, d = h4.shape
    h, n, ncb, nblk, t, hp = _scan_dims(s, d)
    sp = _scan_specs(t, nblk, ncb, hp, True)
    grid = (h // hp, nblk)
    ni, no = (len(carry.arrays), len(carry.out_shapes)) if carry else (0, 0)

    def body(*refs):
        h_ref, lb_ref, nw_ref, st_ref, dy_ref = refs[:5]
        ins = refs[5:5 + ni]
        dh_ref, dlb_ref, dnw_ref = refs[5 + ni:8 + ni]
        outs = refs[8 + ni:8 + ni + no]
        dst_scr = refs[8 + ni + no]
        sems = refs[9 + ni + no:]
        first_block = pl.program_id(1) == 0
        _carry_at(carry, grid, "first", ins, outs, sems)

        @pl.when(first_block)
        def _():
            dst_scr[...] = jnp.zeros_like(dst_scr)
            dlb_ref[...] = jnp.zeros_like(dlb_ref)

        @pl.when(first_block & (pl.program_id(0) == 0))
        def _():
            dnw_ref[...] = jnp.zeros_like(dnw_ref)

        def step(k, carry):
            c = ncb - 1 - k
            r = _rows(c)
            args = [(h_ref[0, r, _lanes(i)], h_ref[1, r, _lanes(i)], h_ref[2, r, _lanes(i)], h_ref[3, r, _lanes(i)],
                     lb_ref[:, _lanes(i)], nw_ref[...], st_ref[i, c]) for i in range(hp)]
            cots = [(dy_ref[r, _lanes(i)].astype(f32), dst_scr[i]) for i in range(hp)]
            dq, df, di, dz, dlb, dnw, dst = jax.vjp(jax.vmap(_hgrn2_chunk), *_stack_heads(args))[1](_stack_heads(cots))
            dnw_sum = dnw_ref[...]
            for i in range(hp):
                for g, val in enumerate((dq, df, di, dz)):
                    dh_ref[g, r, _lanes(i)] = val[i].astype(dh_ref.dtype)
                dlb_ref[:, _lanes(i)] += dlb[i]
                dnw_sum = dnw_sum + dnw[i]
                dst_scr[i] = dst[i]
            dnw_ref[...] = dnw_sum
            return carry

        lax.fori_loop(0, ncb, step, 0)
        _carry_at(carry, grid, "last", ins, outs, sems)

    extra = _carry_layout(carry, 5, 3)
    res = pl.pallas_call(
        body, grid=grid, name="hgrn2_bwd",
        in_specs=[sp["grp"](4), sp["vec"](), sp["shared"](), sp["state"](), sp["col"]()] + extra.in_specs,
        out_specs=[sp["grp"](4), sp["vec"](), sp["shared"]()] + extra.out_specs,
        out_shape=[jax.ShapeDtypeStruct((4, s, d), bf16), jax.ShapeDtypeStruct((1, d), f32),
                   jax.ShapeDtypeStruct((1, HEAD_DIM), f32)] + extra.out_shapes,
        input_output_aliases=extra.aliases,
        scratch_shapes=_state_scratch(hp) + extra.sems, compiler_params=_scan_params())(
            h4, lb, nw, states, dy, *(carry.arrays if carry else []))
    return (*res[:3], res[3:]) if carry else res


def _gdn_fwd(c3, h4, bl, al, alog, dtb, nw):
    _, s, d = c3.shape
    h, n, ncb, nblk, t, hp = _scan_dims(s, d)
    sp = _scan_specs(t, nblk, ncb, hp, False)

    def body(c_ref, hz_ref, bl_ref, al_ref, alog_ref, dtb_ref, nw_ref, y_ref, st_ref, inv_ref, st_scr):
        @pl.when(pl.program_id(1) == 0)
        def _():
            st_scr[...] = jnp.zeros_like(st_scr)

        def step(c, carry):
            r = _rows(c)
            args = [(c_ref[0, r, _lanes(i)], c_ref[1, r, _lanes(i)], c_ref[2, r, _lanes(i)], hz_ref[r, _lanes(i)],
                     bl_ref[i, r, :], al_ref[i, r, :], alog_ref[i], dtb_ref[i], nw_ref[...], st_scr[i]) for i in range(hp)]
            y, st_new, inv = jax.vmap(functools.partial(_gdn_chunk, with_inverse=True))(*_stack_heads(args))
            for i, a in enumerate(args):
                st_ref[i, c] = a[-1]
                inv_ref[i, c] = inv[i]
                y_ref[r, _lanes(i)] = y[i].astype(y_ref.dtype)
                st_scr[i] = st_new[i]
            return carry

        lax.fori_loop(0, ncb, step, 0)

    return pl.pallas_call(
        body, grid=(h // hp, nblk), name="gdn_fwd",
        in_specs=[sp["grp"](3), sp["one"](3), sp["tok"](), sp["tok"](), sp["par"](), sp["par"](), sp["shared"]()],
        out_specs=[sp["col"](), sp["state"](), sp["inverse"]()],
        out_shape=[jax.ShapeDtypeStruct((s, d), bf16), jax.ShapeDtypeStruct((h, n, HEAD_DIM, HEAD_DIM), f32),
                   jax.ShapeDtypeStruct((h, n, CHUNK, CHUNK), f32)],
        scratch_shapes=_state_scratch(hp), compiler_params=_scan_params())(c3, h4, bl, al, alog, dtb, nw)


def _gdn_bwd(c3, h4, bl, al, alog, dtb, nw, states, inverses, dy):
    _, s, d = c3.shape
    h, n, ncb, nblk, t, hp = _scan_dims(s, d)
    sp = _scan_specs(t, nblk, ncb, hp, True)

    def chunk(*a):
        return _gdn_chunk(*a[:-1], inv=a[-1])

    def body(c_ref, hz_ref, bl_ref, al_ref, alog_ref, dtb_ref, nw_ref, st_ref, inv_ref, dy_ref,
             dc_ref, dz_ref, dbl_ref, dal_ref, dalog_ref, ddtb_ref, dnw_ref, dst_scr):
        first_block = pl.program_id(1) == 0

        @pl.when(first_block)
        def _():
            dst_scr[...] = jnp.zeros_like(dst_scr)
            dalog_ref[...] = jnp.zeros_like(dalog_ref)
            ddtb_ref[...] = jnp.zeros_like(ddtb_ref)

        @pl.when(first_block & (pl.program_id(0) == 0))
        def _():
            dnw_ref[...] = jnp.zeros_like(dnw_ref)

        def step(k, carry):
            c = ncb - 1 - k
            r = _rows(c)
            args = [(c_ref[0, r, _lanes(i)], c_ref[1, r, _lanes(i)], c_ref[2, r, _lanes(i)], hz_ref[r, _lanes(i)],
                     bl_ref[i, r, :], al_ref[i, r, :], alog_ref[i], dtb_ref[i], nw_ref[...], st_ref[i, c], inv_ref[i, c])
                    for i in range(hp)]
            cots = [(dy_ref[r, _lanes(i)].astype(f32), dst_scr[i]) for i in range(hp)]
            dq, dk, dv, dz, dbl, dal, dalog, ddtb, dnw, dst, _ = jax.vjp(
                jax.vmap(chunk), *_stack_heads(args))[1](_stack_heads(cots))
            dnw_sum = dnw_ref[...]
            for i in range(hp):
                for g, val in enumerate((dq, dk, dv)):
                    dc_ref[g, r, _lanes(i)] = val[i]
                dz_ref[r, _lanes(i)] = dz[i].astype(dz_ref.dtype)
                dbl_ref[i, r, :] = dbl[i]
                dal_ref[i, r, :] = dal[i]
                dalog_ref[i] += dalog[i]
                ddtb_ref[i] += ddtb[i]
                dnw_sum = dnw_sum + dnw[i]
                dst_scr[i] = dst[i]
            dnw_ref[...] = dnw_sum
            return carry

        lax.fori_loop(0, ncb, step, 0)

    tokens = jax.ShapeDtypeStruct((h, s, 1), f32)
    heads = jax.ShapeDtypeStruct((h, 1, 1), f32)
    return pl.pallas_call(
        body, grid=(h // hp, nblk), name="gdn_bwd",
        in_specs=[sp["grp"](3), sp["one"](3), sp["tok"](), sp["tok"](), sp["par"](), sp["par"](), sp["shared"](),
                  sp["state"](), sp["inverse"](), sp["col"]()],
        out_specs=[sp["grp"](3), sp["col"](), sp["tok"](), sp["tok"](), sp["par"](), sp["par"](), sp["shared"]()],
        out_shape=[jax.ShapeDtypeStruct((3, s, d), f32), jax.ShapeDtypeStruct((s, d), bf16), tokens, tokens, heads, heads,
                   jax.ShapeDtypeStruct((1, HEAD_DIM), f32)],
        scratch_shapes=_state_scratch(hp), compiler_params=_scan_params())(
            c3, h4, bl, al, alog, dtb, nw, states, inverses, dy)


def _tile(n, pref):
    return n if n <= pref else pref


def _mm(a, b, mode, *, name, out_dtype=f32, add=None, add_scale=1.0, b_g0=0, groups=None, carry=None, a_half=None):
    squeeze = a.ndim == 2 and b.ndim == 2
    a3 = a if a.ndim == 3 else a[None]
    b3 = b if b.ndim == 3 else b[None]
    if mode == "nt":
        g = groups or a3.shape[0]
        (_, m, r), (_, n, _) = a3.shape, b3.shape
    elif mode == "nn":
        g = groups or b3.shape[0]
        (_, m, r), (_, _, n) = a3.shape, b3.shape
    else:
        g = groups or b3.shape[0]
        (_, r, m), (_, _, n) = a3.shape, b3.shape
        if a_half is not None:
            m = m // 2
    tm, tn, tr = _tile(m, 1024), _tile(n, 1024), _tile(r, 2048)
    nr = r // tr
    a_grouped = a3.shape[0] > 1
    dn = {"nn": NN, "nt": NT, "tn": TN}[mode]

    if mode == "nt":
        grid = (m // tm, n // tn, g * nr)
        last = g * nr - 1
        kax = 2
        a_spec = pl.BlockSpec((None, tm, tr), lambda i, j, k: (k // nr, i, k % nr))
        b_spec = pl.BlockSpec((None, tn, tr), lambda i, j, k: (b_g0 + k // nr, j, k % nr))
        o_spec = pl.BlockSpec((tm, tn), lambda i, j, k: (i, j))
        add_spec = pl.BlockSpec((tm, tn), lambda i, j, k: (i, j))
        out_shape = jax.ShapeDtypeStruct((m, n), out_dtype)
        sem = ("parallel", "parallel", "arbitrary")
    else:
        grid = (g, m // tm, n // tn, nr)
        last = nr - 1
        kax = 3
        ag = (lambda q: q) if a_grouped else (lambda q: 0)
        if mode == "nn":
            a_spec = pl.BlockSpec((None, tm, tr), lambda q, i, j, k: (ag(q), i, k))
        else:
            def col0():
                if a_half is None:
                    return 0
                return (_my_core() if a_half == "mine" else 1 - _my_core()) * (m // tm)
            a_spec = pl.BlockSpec((None, tr, tm), lambda q, i, j, k: (ag(q), k, col0() + i))
        b_spec = pl.BlockSpec((None, tr, tn), lambda q, i, j, k: (b_g0 + q, k, j))
        o_spec = pl.BlockSpec((None, tm, tn), lambda q, i, j, k: (q, i, j))
        add_spec = None
        out_shape = jax.ShapeDtypeStruct((g, m, n), out_dtype)
        sem = ("parallel", "parallel", "parallel", "arbitrary")
        assert add is None

    n_main = 2 if add is None else 3
    ni = len(carry.arrays) if carry else 0
    no = len(carry.out_shapes) if carry else 0

    def body(*refs):
        a_ref, b_ref = refs[:2]
        add_ref = None if add is None else refs[2]
        ins = refs[n_main:n_main + ni]
        o_ref = refs[n_main + ni]
        outs = refs[n_main + ni + 1:n_main + ni + 1 + no]
        acc = refs[n_main + ni + 1 + no]
        sems = refs[n_main + ni + 2 + no:]
        k = pl.program_id(kax)
        _carry_at(carry, grid, "first", ins, outs, sems)

        @pl.when(k == 0)
        def _():
            acc[...] = jnp.zeros_like(acc)

        acc[...] += _bdot(a_ref[...], b_ref[...], dn)

        @pl.when(k == last)
        def _():
            res = acc[...]
            if add is not None:
                res = res + add_scale * add_ref[...]
            o_ref[...] = res.astype(o_ref.dtype)

        _carry_at(carry, grid, "last", ins, outs, sems)

    operands = [a3, b3] + ([] if add is None else [add])
    in_specs = [a_spec, b_spec] + ([] if add is None else [add_spec])
    extra = _carry_layout(carry, n_main, 1)
    if carry:
        sem = ("arbitrary",) * len(grid)
    out, *moved = pl.pallas_call(
        body, grid=grid, name=name, in_specs=in_specs + extra.in_specs, out_specs=[o_spec] + extra.out_specs,
        out_shape=[out_shape] + extra.out_shapes, input_output_aliases=extra.aliases,
        scratch_shapes=[pltpu.VMEM((tm, tn), f32)] + extra.sems,
        compiler_params=pltpu.CompilerParams(dimension_semantics=sem, vmem_limit_bytes=VMEM_LIMIT))(
            *operands, *(carry.arrays if carry else []))
    out = out[0] if (squeeze and mode != "nt") else out
    return (out, moved) if carry else out


def _row_block(rows, cols, target_bytes=1 << 20):
    if rows <= SUBLANES:
        return rows
    tr = SUBLANES
    while tr * 2 <= rows and tr * 2 * cols * 4 <= target_bytes and rows % (tr * 2) == 0:
        tr *= 2
    return tr


def _ew_params(n_axes=1):
    return pltpu.CompilerParams(dimension_semantics=("parallel",) * n_axes, vmem_limit_bytes=VMEM_LIMIT)


def _cast(x, dtype, name, into_chip_slot=False):
    r, c = x.shape
    tr = _row_block(r, c, 2 << 20)
    tr = max(tr, 16) if r >= 16 else tr
    spec = pl.BlockSpec((tr, c), lambda i: (i, 0))
    if into_chip_slot:
        out_spec = pl.BlockSpec((None, tr, c), lambda i: (_my_chip(), i, 0))
        out_shape = jax.ShapeDtypeStruct((N_CHIPS, r, c), dtype)
    else:
        out_spec, out_shape = spec, jax.ShapeDtypeStruct((r, c), dtype)

    def body(x_ref, o_ref):
        o_ref[...] = x_ref[...].astype(dtype)

    return pl.pallas_call(body, grid=(r // tr,), name=name, in_specs=[spec], out_specs=out_spec,
                          out_shape=out_shape, compiler_params=_ew_params())(x)


def _ln_stats(r):
    mu = jnp.mean(r, axis=-1, keepdims=True)
    var = jnp.mean(jnp.square(r - mu), axis=-1, keepdims=True)
    return mu, lax.rsqrt(var + LN_EPS)


def _ln_fwd(x, yo, g, b):
    s, d = x.shape
    tr = _row_block(s, d, 2 << 20)
    tr = max(tr, 16)
    big = pl.BlockSpec((tr, d), lambda i: (i, 0))
    vec = pl.BlockSpec((1, d), lambda i: (0, 0))

    def body(x_ref, yo_ref, g_ref, b_ref, o_ref, ob_ref):
        r = DEEPNORM_ALPHA * x_ref[...] + yo_ref[...]
        mu, rstd = _ln_stats(r)
        y = (r - mu) * rstd * g_ref[...] + b_ref[...]
        o_ref[...] = y
        ob_ref[...] = y.astype(bf16)

    return pl.pallas_call(
        body, grid=(s // tr,), name="ln_fwd", in_specs=[big, big, vec, vec], out_specs=[big, big],
        out_shape=[jax.ShapeDtypeStruct((s, d), f32), jax.ShapeDtypeStruct((s, d), bf16)],
        compiler_params=_ew_params())(x, yo, g, b)


def _ln_bwd(x, yo, g, *, dy=None, b=None, target=None, name):
    s, d = x.shape
    with_loss = target is not None
    tr = max(_row_block(s, d, 2 << 20), 16)
    big = pl.BlockSpec((tr, d), lambda i: (i, 0))
    vec = pl.BlockSpec((1, d), lambda i: (0, 0))
    lane = pl.BlockSpec((1, LANES), lambda i: (0, 0))

    def body(*refs):
        if with_loss:
            x_ref, yo_ref, g_ref, b_ref, t_ref, dr_ref, drb_ref, dg_ref, db_ref, loss_ref = refs
        else:
            x_ref, yo_ref, g_ref, dy_ref, dr_ref, drb_ref, dg_ref, db_ref = refs

        @pl.when(pl.program_id(0) == 0)
        def _():
            dg_ref[...] = jnp.zeros_like(dg_ref)
            db_ref[...] = jnp.zeros_like(db_ref)
            if with_loss:
                loss_ref[...] = jnp.zeros_like(loss_ref)

        r = DEEPNORM_ALPHA * x_ref[...] + yo_ref[...]
        mu, rstd = _ln_stats(r)
        xhat = (r - mu) * rstd
        if with_loss:
            err = xhat * g_ref[...] + b_ref[...] - t_ref[...]
            loss_ref[...] += jnp.sum(err * err) * (0.5 / d)
            dyv = err * (1.0 / d)
        else:
            dyv = dy_ref[...]
        dxhat = dyv * g_ref[...]
        m1 = jnp.mean(dxhat, axis=-1, keepdims=True)
        m2 = jnp.mean(dxhat * xhat, axis=-1, keepdims=True)
        dr = rstd * (dxhat - m1 - xhat * m2)
        dr_ref[...] = dr
        drb_ref[...] = dr.astype(bf16)
        dg_ref[...] += jnp.sum(dyv * xhat, axis=0, keepdims=True)
        db_ref[...] += jnp.sum(dyv, axis=0, keepdims=True)

    operands = (x, yo, g, b, target) if with_loss else (x, yo, g, dy)
    in_specs = [big, big, vec, vec, big] if with_loss else [big, big, vec, big]
    out_specs = [big, big, vec, vec] + ([lane] if with_loss else [])
    out_shape = [jax.ShapeDtypeStruct((s, d), f32), jax.ShapeDtypeStruct((s, d), bf16),
                 jax.ShapeDtypeStruct((1, d), f32), jax.ShapeDtypeStruct((1, d), f32)]
    if with_loss:
        out_shape.append(jax.ShapeDtypeStruct((1, LANES), f32))
    return pl.pallas_call(
        body, grid=(s // tr,), name=name, in_specs=in_specs, out_specs=out_specs, out_shape=out_shape,
        compiler_params=pltpu.CompilerParams(dimension_semantics=("arbitrary",), vmem_limit_bytes=VMEM_LIMIT))(*operands)


def _lower_bound(a_lb):
    _, d = a_lb.shape

    def body(a_ref, o_ref):
        a = a_ref[...]
        ex = jnp.exp(a - jnp.max(a, axis=0, keepdims=True))
        p = ex / jnp.sum(ex, axis=0, keepdims=True)
        o_ref[...] = jnp.sum(jnp.where(_iota(p.shape, 0) == 0, p, 0.0), axis=0, keepdims=True)

    return pl.pallas_call(body, name="lower_bound", in_specs=[VMEM], out_specs=VMEM,
                          out_shape=jax.ShapeDtypeStruct((1, d), f32))(a_lb)


def _shift_rows(ext, k, rows):
    return pltpu.roll(ext, k, axis=0)[SUBLANES:SUBLANES + rows]


def _conv_tiles(s, d):
    return _tile(s, 512), _tile(d, 512)


def _conv_fwd(h4, w):
    _, s, d = h4.shape
    tr, tc = _conv_tiles(s, d)
    nj = d // tc
    hb = tr // SUBLANES
    cur = pl.BlockSpec((None, tr, tc), lambda g, j, i: (g, i, j))
    prev = pl.BlockSpec((None, SUBLANES, tc), lambda g, j, i: (g, jnp.maximum(i * hb - 1, 0), j))
    wsp = pl.BlockSpec((CONV_WIDTH, tc), lambda g, j, i: (0, g * nj + j))

    def body(x_ref, p_ref, w_ref, o_ref):
        x = x_ref[...]
        halo = jnp.where(pl.program_id(2) == 0, 0.0, p_ref[...])
        ext = jnp.concatenate([halo, x], axis=0)
        acc = w_ref[CONV_WIDTH - 1:CONV_WIDTH, :] * x
        for k in range(1, CONV_WIDTH):
            acc = acc + w_ref[CONV_WIDTH - 1 - k:CONV_WIDTH - k, :] * _shift_rows(ext, k, tr)
        o_ref[...] = acc

    return pl.pallas_call(
        body, grid=(3, nj, s // tr), name="conv_fwd", in_specs=[cur, prev, wsp], out_specs=cur,
        out_shape=jax.ShapeDtypeStruct((3, s, d), f32), compiler_params=_ew_params(3))(h4, h4, w)


def _conv_bwd(dy3, h4, w):
    _, s, d = dy3.shape
    tr, tc = _conv_tiles(s, d)
    nj = d // tc
    hb = tr // SUBLANES
    ni = s // tr
    cur = pl.BlockSpec((None, tr, tc), lambda g, j, i: (g, i, j))
    prev = pl.BlockSpec((None, SUBLANES, tc), lambda g, j, i: (g, jnp.maximum(i * hb - 1, 0), j))
    nxt = pl.BlockSpec((None, SUBLANES, tc), lambda g, j, i: (g, jnp.minimum((i + 1) * hb, s // SUBLANES - 1), j))
    wsp = pl.BlockSpec((CONV_WIDTH, tc), lambda g, j, i: (0, g * nj + j))

    def body(dy_ref, dn_ref, x_ref, p_ref, w_ref, dx_ref, dw_ref):
        i = pl.program_id(2)

        @pl.when(i == 0)
        def _():
            dw_ref[...] = jnp.zeros_like(dw_ref)

        dy = dy_ref[...]
        x = x_ref[...]
        tail = jnp.where(i == ni - 1, 0.0, dn_ref[...])
        dext = jnp.concatenate([dy, tail], axis=0)
        halo = jnp.where(i == 0, 0.0, p_ref[...])
        xext = jnp.concatenate([halo, x], axis=0)
        acc = w_ref[CONV_WIDTH - 1:CONV_WIDTH, :] * dy
        dw_ref[CONV_WIDTH - 1:CONV_WIDTH, :] += jnp.sum(dy * x, axis=0, keepdims=True)
        for k in range(1, CONV_WIDTH):
            ahead = pltpu.roll(dext, tr + SUBLANES - k, axis=0)[:tr]
            acc = acc + w_ref[CONV_WIDTH - 1 - k:CONV_WIDTH - k, :] * ahead
            dw_ref[CONV_WIDTH - 1 - k:CONV_WIDTH - k, :] += jnp.sum(dy * _shift_rows(xext, k, tr), axis=0, keepdims=True)
        dx_ref[...] = acc.astype(dx_ref.dtype)

    return pl.pallas_call(
        body, grid=(3, nj, ni), name="conv_bwd", in_specs=[cur, nxt, cur, prev, wsp], out_specs=[cur, wsp],
        out_shape=[jax.ShapeDtypeStruct((3, s, d), bf16), jax.ShapeDtypeStruct((CONV_WIDTH, 3 * d), f32)],
        compiler_params=pltpu.CompilerParams(dimension_semantics=("parallel", "parallel", "arbitrary"),
                                             vmem_limit_bytes=VMEM_LIMIT))(dy3, dy3, h4, h4, w)


def _my_core():
    return lax.axis_index("c")


def _my_chip():
    return 2 * lax.axis_index("x") + lax.axis_index("y")


def _chip_sum(halves, other, name):
    _, r2, c = other.shape
    tr = max(_row_block(r2, c), 16)
    own = pl.BlockSpec((None, tr, c), lambda p, i: (2 * p + _my_core(), i, 0))
    blk = pl.BlockSpec((None, tr, c), lambda p, i: (p, i, 0))

    def body(a_ref, b_ref, o_ref):
        o_ref[...] = (a_ref[...] + b_ref[...]).astype(o_ref.dtype)

    return pl.pallas_call(
        body, name=name, grid=(N_CHIPS, r2 // tr), in_specs=[own, blk], out_specs=blk,
        out_shape=jax.ShapeDtypeStruct(other.shape, bf16), compiler_params=_ew_params(2))(halves, other)


def _final_sum(halves, other, remote, name):
    _, r2, c = remote.shape
    tr = max(_row_block(r2, c), 16)
    own = pl.BlockSpec((None, tr, c), lambda i: (2 * _my_chip() + _my_core(), i, 0))
    sib = pl.BlockSpec((None, tr, c), lambda i: (_my_chip(), i, 0))
    slot = lambda j: pl.BlockSpec((None, tr, c), lambda i: (j, i, 0))

    def body(a_ref, b_ref, r0_ref, r1_ref, r2_ref, o_ref):
        acc = a_ref[...] + b_ref[...]
        for ref in (r0_ref, r1_ref, r2_ref):
            acc = acc + ref[...].astype(f32)
        o_ref[...] = acc

    return pl.pallas_call(
        body, name=name, grid=(r2 // tr,), in_specs=[own, sib, slot(0), slot(1), slot(2)],
        out_specs=pl.BlockSpec((None, tr, c), lambda i: (_my_core(), i, 0)),
        out_shape=jax.ShapeDtypeStruct((2, r2, c), f32), compiler_params=_ew_params())(halves, other, remote, remote, remote)


def _adamw(w, g, m, v, name):
    r, c = w.shape
    tr = _row_block(r, c)
    spec = pl.BlockSpec((tr, c), lambda i: (i, 0))

    def body(w_ref, g_ref, m_ref, v_ref, d_ref, nm_ref, nv_ref):
        gv = g_ref[...]
        nm = ADAM_B1 * m_ref[...] + (1.0 - ADAM_B1) * gv
        nv = ADAM_B2 * v_ref[...] + (1.0 - ADAM_B2) * jnp.square(gv)
        m_hat = nm / (1.0 - ADAM_B1 ** ADAM_STEP)
        v_hat = nv / (1.0 - ADAM_B2 ** ADAM_STEP)
        d_ref[...] = -ADAM_LR * (m_hat / (jnp.sqrt(v_hat) + ADAM_EPS) + ADAM_WD * w_ref[...])
        nm_ref[...] = nm
        nv_ref[...] = nv

    out = jax.ShapeDtypeStruct((r, c), f32)
    return pl.pallas_call(body, grid=(r // tr,), name=name, in_specs=[spec] * 4, out_specs=[spec] * 3,
                          out_shape=[out, out, out], compiler_params=_ew_params())(w, g, m, v)


class _Comm(NamedTuple):
    arrays: list
    aliases: dict
    out_shapes: list
    sems: list
    start: Callable
    finish: Callable


class _CarryLayout(NamedTuple):
    in_specs: list
    out_specs: list
    out_shapes: list
    aliases: dict
    sems: list


def _carry_layout(comm, n_in, n_out):
    if comm is None:
        return _CarryLayout([], [], [], {}, [])
    return _CarryLayout([ANY] * len(comm.arrays), [ANY] * len(comm.out_shapes), list(comm.out_shapes),
                        {n_in + i: n_out + o for i, o in comm.aliases.items()}, list(comm.sems))


def _carry_at(comm, grid, edge, ins, outs, sems):
    if comm is None:
        return
    hit = [pl.program_id(ax) == (0 if edge == "first" else extent - 1) for ax, extent in enumerate(grid)]

    @pl.when(functools.reduce(jnp.logical_and, hit))
    def _():
        (comm.start if edge == "first" else comm.finish)(ins, outs, sems)


def _compose(first, second):
    na, no, ns = len(first.arrays), len(first.out_shapes), len(first.sems)

    def both(which):
        def run(ins, outs, sems):
            getattr(first, which)(ins[:na], outs[:no], sems[:ns])
            getattr(second, which)(ins[na:], outs[no:], sems[ns:])
        return run

    aliases = dict(first.aliases) | {na + i: no + o for i, o in second.aliases.items()}
    return _Comm(first.arrays + second.arrays, aliases, first.out_shapes + second.out_shapes, first.sems + second.sems,
                 both("start"), both("finish"))


def _run_comm(comm, name):
    ni, no = len(comm.arrays), len(comm.out_shapes)

    def body(*refs):
        ins, outs, sems = refs[:ni], refs[ni:ni + no], refs[ni + no:]
        comm.start(ins, outs, sems)
        comm.finish(ins, outs, sems)

    return pl.pallas_call(body, name=name, in_specs=[ANY] * ni, out_specs=[ANY] * no, out_shape=comm.out_shapes,
                          input_output_aliases=comm.aliases, scratch_shapes=comm.sems)(*comm.arrays)


def _place():
    return lax.axis_index("x"), lax.axis_index("y"), lax.axis_index("c")


def _chip_peers(x, y):
    out = []
    for fx, fy in ((1, 0), (0, 1), (1, 1)):
        px, py = x ^ fx, y ^ fy
        out.append((px, py, 2 * px + py))
    return out


def _all_gather(arrays, split):
    na = len(arrays)

    def part(ref, a, block, which):
        if not split[a]:
            return ref.at[block]
        rows = arrays[a].shape[1] // 2
        return ref.at[block, pl.ds(which * rows, rows), :]

    def ici_sends(outs, sems):
        send_sems, recv_sems = sems
        x, y, c = _place()
        me = 2 * x + y
        out = []
        for a in range(na):
            for j, (px, py, _) in enumerate(_chip_peers(x, y)):
                mine = part(outs[a], a, me, c)
                out.append(pltpu.make_async_remote_copy(mine, mine, send_sems.at[a, j], recv_sems.at[a, j],
                                                        device_id=(px, py, c), device_id_type=MESH))
        return out

    def forwards(outs, sems, which_core):
        send_sems, recv_sems = sems
        x, y, c = _place()
        out = []
        for a in range(na):
            if split[a]:
                for j, (_, _, pid) in enumerate(_chip_peers(x, y)):
                    got = part(outs[a], a, pid, which_core)
                    out.append(pltpu.make_async_remote_copy(got, got, send_sems.at[a, 3 + j], recv_sems.at[a, 3 + j],
                                                            device_id=(x, y, 1 - c), device_id_type=MESH))
        return out

    def arrivals(outs, sems):
        send_sems, recv_sems = sems
        x, y, c = _place()
        out = []
        for a in range(na):
            for j, (px, py, pid) in enumerate(_chip_peers(x, y)):
                got = part(outs[a], a, pid, c)
                out.append(pltpu.make_async_remote_copy(got, got, send_sems.at[a, j], recv_sems.at[a, j],
                                                        device_id=(px, py, c), device_id_type=MESH))
        return out

    def finish(ins, outs, sems):
        c = lax.axis_index("c")
        for cp in arrivals(outs, sems):
            cp.wait_recv()
        mine = forwards(outs, sems, c)
        for cp in mine:
            cp.start()
        for cp in forwards(outs, sems, 1 - c):
            cp.wait_recv()
        for cp in ici_sends(outs, sems) + mine:
            cp.wait_send()

    return _Comm(
        arrays=list(arrays), aliases={a: a for a in range(na)},
        out_shapes=[jax.ShapeDtypeStruct(a.shape, a.dtype) for a in arrays],
        sems=[pltpu.SemaphoreType.DMA((na, 6)), pltpu.SemaphoreType.DMA((na, 6))],
        start=lambda ins, outs, sems: [cp.start() for cp in ici_sends(outs, sems)], finish=finish)


def _swap_halves(arrays):
    na = len(arrays)

    def copies(ins, outs, sems):
        send_sems, recv_sems = sems
        x, y, c = _place()
        out = []
        for a in range(na):
            rows = arrays[a].shape[1] // 2
            src = ins[a].at[:, pl.ds((1 - c) * rows, rows), :]
            out.append(pltpu.make_async_remote_copy(src, outs[a], send_sems.at[a], recv_sems.at[a],
                                                    device_id=(x, y, 1 - c), device_id_type=MESH))
        return out

    return _Comm(
        arrays=list(arrays), aliases={},
        out_shapes=[jax.ShapeDtypeStruct((a.shape[0], a.shape[1] // 2, a.shape[2]), a.dtype) for a in arrays],
        sems=[pltpu.SemaphoreType.DMA((na,)), pltpu.SemaphoreType.DMA((na,))],
        start=lambda ins, outs, sems: [cp.start() for cp in copies(ins, outs, sems)],
        finish=lambda ins, outs, sems: [cp.wait() for cp in copies(ins, outs, sems)])


def _send_to_sibling(arrays):
    na = len(arrays)

    def copies(ins, outs, sems):
        send_sems, recv_sems = sems
        x, y, c = _place()
        return [pltpu.make_async_remote_copy(ins[a], outs[a], send_sems.at[a], recv_sems.at[a],
                                             device_id=(x, y, 1 - c), device_id_type=MESH) for a in range(na)]

    return _Comm(
        arrays=list(arrays), aliases={}, out_shapes=[jax.ShapeDtypeStruct(a.shape, a.dtype) for a in arrays],
        sems=[pltpu.SemaphoreType.DMA((na,)), pltpu.SemaphoreType.DMA((na,))],
        start=lambda ins, outs, sems: [cp.start() for cp in copies(ins, outs, sems)],
        finish=lambda ins, outs, sems: [cp.wait() for cp in copies(ins, outs, sems)])


def _pair_sum(keep, other, name):
    _, r2, c = keep.shape
    tr = max(_row_block(r2, c), 16)
    blk = pl.BlockSpec((None, tr, c), lambda p, i: (p, i, 0))

    def body(a_ref, b_ref, o_ref):
        o_ref[...] = (a_ref[...] + b_ref[...]).astype(o_ref.dtype)

    return pl.pallas_call(body, name=name, grid=(N_CHIPS, r2 // tr), in_specs=[blk, blk], out_specs=blk,
                          out_shape=jax.ShapeDtypeStruct(keep.shape, bf16), compiler_params=_ew_params(2))(keep, other)


def _final_sum_pair(keep, other, remote, name):
    _, r2, c = remote.shape
    tr = max(_row_block(r2, c), 16)
    own = pl.BlockSpec((None, tr, c), lambda i: (_my_chip(), i, 0))
    slot = lambda j: pl.BlockSpec((None, tr, c), lambda i: (j, i, 0))

    def body(a_ref, b_ref, r0_ref, r1_ref, r2_ref, o_ref):
        acc = a_ref[...] + b_ref[...]
        for ref in (r0_ref, r1_ref, r2_ref):
            acc = acc + ref[...].astype(f32)
        o_ref[...] = acc

    return pl.pallas_call(
        body, name=name, grid=(r2 // tr,), in_specs=[own, own, slot(0), slot(1), slot(2)],
        out_specs=pl.BlockSpec((None, tr, c), lambda i: (_my_core(), i, 0)),
        out_shape=jax.ShapeDtypeStruct((2, r2, c), f32), compiler_params=_ew_params())(keep, other, remote, remote, remote)


def _scatter_blocks(arrays):
    na = len(arrays)

    def copies(ins, outs, sems):
        send_sems, recv_sems = sems
        x, y, c = _place()
        out = []
        for a in range(na):
            for j, (px, py, pid) in enumerate(_chip_peers(x, y)):
                out.append(pltpu.make_async_remote_copy(ins[a].at[pid], outs[a].at[j], send_sems.at[a, j],
                                                        recv_sems.at[a, j], device_id=(px, py, c), device_id_type=MESH))
        return out

    return _Comm(
        arrays=list(arrays), aliases={},
        out_shapes=[jax.ShapeDtypeStruct((3,) + a.shape[1:], a.dtype) for a in arrays],
        sems=[pltpu.SemaphoreType.DMA((na, 3)), pltpu.SemaphoreType.DMA((na, 3))],
        start=lambda ins, outs, sems: [cp.start() for cp in copies(ins, outs, sems)],
        finish=lambda ins, outs, sems: [cp.wait() for cp in copies(ins, outs, sems)])


def _join_halves(arrays):
    na = len(arrays)

    def copies(outs, sems, slot):
        send_sems, recv_sems = sems
        x, y, c = _place()
        which = c if slot == "mine" else 1 - c
        return [pltpu.make_async_remote_copy(outs[a].at[which], outs[a].at[which], send_sems.at[a], recv_sems.at[a],
                                             device_id=(x, y, 1 - c), device_id_type=MESH) for a in range(na)]

    def finish(ins, outs, sems):
        for cp in copies(outs, sems, "mine"):
            cp.wait_send()
        for cp in copies(outs, sems, "sibling's"):
            cp.wait_recv()

    return _Comm(
        arrays=list(arrays), aliases={a: a for a in range(na)},
        out_shapes=[jax.ShapeDtypeStruct(a.shape, a.dtype) for a in arrays],
        sems=[pltpu.SemaphoreType.DMA((na,)), pltpu.SemaphoreType.DMA((na,))],
        start=lambda ins, outs, sems: [cp.start() for cp in copies(outs, sems, "mine")], finish=finish)


SMALL_ROWS = 24


def _all_reduce_small(a_lb, dlb, pieces):
    d = a_lb.shape[1]
    npc = len(pieces)

    def body(*refs):
        alb_ref, dlb_ref = refs[:2]
        piece_refs = refs[2:2 + npc]
        out_ref, pack, slots, send_sems, recv_sems = refs[2 + npc:]
        x, y, c = _place()
        me = 4 * x + 2 * y + c
        pack[...] = jnp.zeros_like(pack)
        a = alb_ref[...]
        ex = jnp.exp(a - jnp.max(a, axis=0, keepdims=True))
        p = ex / jnp.sum(ex, axis=0, keepdims=True)
        first = (_iota(p.shape, 0) == 0).astype(f32)
        p0 = jnp.sum(p * first, axis=0, keepdims=True)
        pack[0:3, :] = p * dlb_ref[...] * (first - p0)
        for ref, (arr, r0) in zip(piece_refs, pieces):
            pack[r0:r0 + arr.shape[0], 0:arr.shape[1]] = ref[...]
        slots[me] = pack[...]
        copies = []
        for k in range(1, N_DEV):
            peer = (x ^ (k >> 2), y ^ ((k >> 1) & 1), c ^ (k & 1))
            cp = pltpu.make_async_remote_copy(pack, slots.at[me], send_sems.at[k - 1], recv_sems.at[k - 1],
                                              device_id=peer, device_id_type=MESH)
            cp.start()
            copies.append(cp)
        for k in range(1, N_DEV):
            src = (x ^ (k >> 2)) * 4 + (y ^ ((k >> 1) & 1)) * 2 + (c ^ (k & 1))
            pltpu.make_async_remote_copy(pack, slots.at[src], send_sems.at[k - 1], recv_sems.at[k - 1],
                                         device_id=(x, y, c), device_id_type=MESH).wait_recv()
        for cp in copies:
            cp.wait_send()
        acc = slots[0]
        for i in range(1, N_DEV):
            acc = acc + slots[i]
        out_ref[...] = acc

    return pl.pallas_call(
        body, name="all_reduce_small", in_specs=[VMEM] * (2 + npc), out_specs=VMEM,
        out_shape=jax.ShapeDtypeStruct((SMALL_ROWS, d), f32),
        scratch_shapes=[pltpu.VMEM((SMALL_ROWS, d), f32), pltpu.VMEM((N_DEV, SMALL_ROWS, d), f32),
                        pltpu.SemaphoreType.DMA((N_DEV - 1,)), pltpu.SemaphoreType.DMA((N_DEV - 1,))],
        compiler_params=pltpu.CompilerParams(vmem_limit_bytes=VMEM_LIMIT))(
            a_lb, dlb, *[p[0] for p in pieces])


def _as_halves(g):
    return g.reshape(2 * g.shape[0], g.shape[1] // 2, g.shape[2])


def _chip_sums(grads, from_sibling, tag):
    return [_chip_sum(_as_halves(g), t, f"grad_chip_sum_{tag}{a}") for a, (g, t) in enumerate(zip(grads, from_sibling))]


def _final_sums(grads, from_sibling, from_chips, tag):
    return [_final_sum(_as_halves(g), t, r, f"grad_final_sum_{tag}{a}")
            for a, (g, t, r) in enumerate(zip(grads, from_sibling, from_chips))]


def kernel(x, a_w_in, a_lower_bounds, a_norm_w, a_w_out, b_w_in, b_conv_w, b_a_log, b_dt_bias, b_norm_w, b_w_out, ln_g, ln_b, loss_target, m_a_w_in, m_a_lower_bounds, m_a_norm_w, m_a_w_out, m_b_w_in, m_b_conv_w, m_b_a_log, m_b_dt_bias, m_b_norm_w, m_b_w_out, m_ln_g, m_ln_b, v_a_w_in, v_a_lower_bounds, v_a_norm_w, v_a_w_out, v_b_w_in, v_b_conv_w, v_b_a_log, v_b_dt_bias, v_b_norm_w, v_b_w_out, v_ln_g, v_ln_b):
    _, s, d = x.shape
    h = d // HEAD_DIM
    wb = b_w_in.shape[2]
    x2, target = x[0], loss_target[0]
    chip = 2 * lax.axis_index("x") + lax.axis_index("y")

    shards = [_cast(a_w_in[0], bf16, "cast_a_w_in", True), _cast(a_w_out[0], bf16, "cast_a_w_out", True),
              _cast(b_w_in[0], bf16, "cast_b_w_in", True), _cast(b_w_out[0], bf16, "cast_b_w_out", True),
              _cast(b_conv_w[0], f32, "copy_b_conv_w", True)]
    sh_a_in, sh_a_out, sh_b_in, sh_b_out, sh_conv = shards
    wa4, wa_out_g, conv_g = _run_comm(_all_gather([sh_a_in, sh_a_out, sh_conv], [True, True, False]), "all_gather_layer_a")
    wa_out = wa_out_g.reshape(d, d)
    conv_w = conv_g.transpose(1, 0, 2).reshape(CONV_WIDTH, 3 * d)

    x_bf = _cast(x2, bf16, "cast_x")
    lb = _lower_bound(a_lower_bounds)
    h4a, (wb_g,) = _mm(x_bf, wa4, "nn", name="a_in_fwd", carry=_all_gather([sh_b_in], [True]))
    wb2d = wb_g.transpose(1, 0, 2).reshape(d, N_CHIPS * wb)
    wb4 = wb2d[:, :4 * d].reshape(d, 4, d).transpose(1, 0, 2)
    wba = jnp.pad(wb2d[:, 4 * d:], ((0, 0), (0, LANES - 2 * h)))
    ya, st_a = _hgrn2_fwd(h4a, lb, a_norm_w)
    yo_a = _mm(ya, wa_out, "nn", name="a_out_fwd")
    x1, x1_bf = _ln_fwd(x2, yo_a, ln_g[0:1], ln_b[0:1])
    h4b, (wb_out_g,) = _mm(x1_bf, wb4, "nn", name="b_in_fwd", carry=_all_gather([sh_b_out], [True]))
    wb_out = wb_out_g.reshape(d, d)
    ba = _mm(x1_bf, wba, "nn", name="b_gate_fwd")
    c3 = _conv_fwd(h4b, conv_w)
    bl = ba[:, :h].T.reshape(h, s, 1)
    al = ba[:, h:2 * h].T.reshape(h, s, 1)
    alog, dtb = b_a_log.reshape(h, 1, 1), b_dt_bias.reshape(h, 1, 1)
    yb, st_b, inv_b = _gdn_fwd(c3, h4b, bl, al, alog, dtb, b_norm_w)
    yo_b = _mm(yb, wb_out, "nn", name="b_out_fwd")

    dr2, dr2_bf, dg2, db2, loss = _ln_bwd(x1, yo_b, ln_g[1:2], b=ln_b[1:2], target=target, name="ln2_loss_bwd")
    g_wb_out = _mm(yb, dr2_bf, "tn", name="b_out_dw")
    dyb = _mm(dr2_bf, wb_out, "nt", name="b_out_dx")
    dc3, dhz, dbl, dal, dalog, ddtb, dnw_b = _gdn_bwd(c3, h4b, bl, al, alog, dtb, b_norm_w, st_b, inv_b, dyb)
    dh3, dconv = _conv_bwd(dc3, h4b, conv_w)
    dba = jnp.pad(jnp.concatenate([dbl.reshape(h, s).T, dal.reshape(h, s).T], axis=1), ((0, 0), (0, LANES - 2 * h)))
    g_wb3 = _mm(x1_bf, dh3, "tn", name="b_in_dw_qkv")
    g_wbz = _mm(x1_bf, dhz, "tn", name="b_in_dw_z")
    g_wba = _mm(x1_bf, dba, "tn", name="b_in_dw_gate")
    dx1 = _mm(dh3, wb4, "nt", name="b_in_dx_qkv", groups=3, add=dr2, add_scale=DEEPNORM_ALPHA)
    dx1 = _mm(dhz, wb4, "nt", name="b_in_dx_z", b_g0=3, groups=1, add=dx1)
    dx1 = _mm(dba, wba, "nt", name="b_in_dx_gate", add=dx1)
    dr1, dr1_bf, dg1, db1 = _ln_bwd(x2, yo_a, ln_g[0:1], dy=dx1, name="ln1_bwd")
    g_wb2d = jnp.concatenate([g_wb3[0], g_wb3[1], g_wb3[2], g_wbz, g_wba[:, :2 * h]], axis=1)
    grads_b = [g_wb2d.reshape(d, N_CHIPS, wb).transpose(1, 0, 2), g_wb_out.reshape(N_CHIPS, d // N_CHIPS, d)]
    g_wa_out, sib_b = _mm(ya, dr1_bf, "tn", name="a_out_dw", carry=_swap_halves(grads_b))
    chip_b = _chip_sums(grads_b, sib_b, "b")
    dya = _mm(dr1_bf, wa_out, "nt", name="a_out_dx")
    dh4, dlb, dnw_a, far_b = _hgrn2_bwd(h4a, lb, a_norm_w, st_a, dya, carry=_scatter_blocks(chip_b))
    grads_a_out = [g_wa_out.reshape(N_CHIPS, d // N_CHIPS, d)]
    g_wa4_away = _mm(x_bf, dh4, "tn", name="a_in_dw_away", a_half="other")
    g_wa4_keep, (sib_a_in, sib_a_out) = _mm(x_bf, dh4, "tn", name="a_in_dw_keep", a_half="mine",
                                            carry=_compose(_send_to_sibling([g_wa4_away]), _swap_halves(grads_a_out)))
    chip_a = [_pair_sum(g_wa4_keep, sib_a_in, "grad_chip_sum_a_in")] + _chip_sums(grads_a_out, [sib_a_out], "a_out")
    grad_x, far_a = _mm(dh4, wa4, "nt", name="a_in_dx", add=dr1, add_scale=DEEPNORM_ALPHA, carry=_scatter_blocks(chip_a))

    finals = ([_final_sum_pair(g_wa4_keep, sib_a_in, far_a[0], "grad_final_sum_a_in")]
              + _final_sums(grads_a_out, [sib_a_out], far_a[1:], "a_out") + _final_sums(grads_b, sib_b, far_b, "b"))
    big = [j.reshape(2 * j.shape[1], j.shape[2]) for j in _run_comm(_join_halves(finals), "grad_join_halves")]
    small = _all_reduce_small(a_lower_bounds, dlb, [
        (dg1, 3), (dg2, 4), (db1, 5), (db2, 6), (dnw_a, 7), (dnw_b, 8), (dalog.reshape(1, h), 9), (ddtb.reshape(1, h), 10),
        (loss, 11), (dconv.reshape(3 * CONV_WIDTH, d), 12)])
    loss_out = small[11, 0]
    cw = 3 * d // N_CHIPS
    g_conv = lax.dynamic_slice_in_dim(small[12:24].reshape(CONV_WIDTH, 3 * d), chip * cw, cw, axis=1)
    grads = {
        "a_w_in": big[0], "a_lower_bounds": small[0:3], "a_norm_w": small[7:8, :HEAD_DIM], "a_w_out": big[1],
        "b_w_in": big[2], "b_conv_w": g_conv, "b_a_log": small[9:10, :h], "b_dt_bias": small[10:11, :h],
        "b_norm_w": small[8:9, :HEAD_DIM], "b_w_out": big[3],
        "ln_g": jnp.concatenate([small[3:4], small[4:5]], axis=0), "ln_b": jnp.concatenate([small[5:6], small[6:7]], axis=0),
    }

    weights = dict(a_w_in=a_w_in, a_lower_bounds=a_lower_bounds, a_norm_w=a_norm_w, a_w_out=a_w_out, b_w_in=b_w_in,
                   b_conv_w=b_conv_w, b_a_log=b_a_log, b_dt_bias=b_dt_bias, b_norm_w=b_norm_w, b_w_out=b_w_out,
                   ln_g=ln_g, ln_b=ln_b)
    m_in = dict(a_w_in=m_a_w_in, a_lower_bounds=m_a_lower_bounds, a_norm_w=m_a_norm_w, a_w_out=m_a_w_out, b_w_in=m_b_w_in,
                b_conv_w=m_b_conv_w, b_a_log=m_b_a_log, b_dt_bias=m_b_dt_bias, b_norm_w=m_b_norm_w, b_w_out=m_b_w_out,
                ln_g=m_ln_g, ln_b=m_ln_b)
    v_in = dict(a_w_in=v_a_w_in, a_lower_bounds=v_a_lower_bounds, a_norm_w=v_a_norm_w, a_w_out=v_a_w_out, b_w_in=v_b_w_in,
                b_conv_w=v_b_conv_w, b_a_log=v_b_a_log, b_dt_bias=v_b_dt_bias, b_norm_w=v_b_norm_w, b_w_out=v_b_w_out,
                ln_g=v_ln_g, ln_b=v_ln_b)
    out_g, out_d, out_m, out_v = [], [], [], []
    for name, w in weights.items():
        shape2 = w.shape[-2:]
        g = grads[name].reshape(shape2)
        delta, nm, nv = _adamw(w.reshape(shape2), g, m_in[name].reshape(shape2), v_in[name].reshape(shape2), "adamw_" + name)
        out_g.append(g.reshape(w.shape))
        out_d.append(delta.reshape(w.shape))
        out_m.append(nm.reshape(w.shape))
        out_v.append(nv.reshape(w.shape))
    return (loss_out, grad_x.reshape(x.shape), *out_g, *out_d, *out_m, *out_v)
```

```python
import functools
import math
from typing import Callable, NamedTuple

import jax
import jax.numpy as jnp
from jax import lax
from jax.experimental import pallas as pl
from jax.experimental.pallas import tpu as pltpu

f32 = jnp.float32
bf16 = jnp.bfloat16
HIGHEST = lax.Precision.HIGHEST

HEAD_DIM = 128
CHUNK = 64
SUB = 16
DEPTH = 2
DEEPNORM_ALPHA = (2.0 * DEPTH) ** 0.25
LN_EPS = 1e-5
RMS_EPS = 1e-6
L2_EPS = 1e-6
ADAM_LR, ADAM_B1, ADAM_B2, ADAM_EPS, ADAM_WD, ADAM_STEP = 0.001, 0.9, 0.999, 1e-08, 0.01, 10
CONV_WIDTH = 4
EXP_CLAMP = 40.0
VMEM_LIMIT = 56 << 20
MXU_DTYPE = bf16
SUBLANES = 8
LANES = 128
N_CHIPS = 4
N_DEV = 8

NN = (((1,), (0,)), ((), ()))
NT = (((1,), (1,)), ((), ()))
TN = (((0,), (0,)), ((), ()))
MESH = pl.DeviceIdType.MESH
ANY = pl.BlockSpec(memory_space=pl.ANY)
VMEM = pl.BlockSpec(memory_space=pltpu.VMEM)


def _bdot(a, b, dn):
    return lax.dot_general(a.astype(MXU_DTYPE), b.astype(MXU_DTYPE), dn, preferred_element_type=f32)


def _hdot(a, b, dn):
    return lax.dot_general(a, b, dn, precision=lax.Precision.HIGH, preferred_element_type=f32)


def _iota(shape, dim):
    return lax.broadcasted_iota(jnp.int32, shape, dim)


def _same_sub(a, b):
    shift = int(math.log2(SUB))
    return jnp.right_shift(a, shift) == jnp.right_shift(b, shift)


def _silu(x):
    return x * jax.nn.sigmoid(x)


def _gated_rmsnorm(o, hz, nw):
    return o * lax.rsqrt(jnp.mean(o * o, axis=-1, keepdims=True) + RMS_EPS) * nw * _silu(hz)


def _hgrn2_chunk(hq, hf, hi, hz, lb, nw, st):
    c = CHUNK
    q = _silu(hq)
    forget = lb + (1.0 - lb) * jax.nn.sigmoid(hf)
    k = 1.0 - forget
    g = jnp.log(forget)
    tri = (_iota((c, c), 1) <= _iota((c, c), 0)).astype(f32)
    cum = _hdot(tri, g, NN)
    total = jnp.sum(g, axis=0, keepdims=True)
    nb = c // SUB
    blk = [cum[SUB * j:SUB * (j + 1)] for j in range(nb)]
    r16 = _iota((SUB, HEAD_DIM), 0)

    def row(j, i):
        return jnp.sum(jnp.where(r16 == i, blk[j], 0.0), axis=0, keepdims=True)

    c15, c31, c47 = row(0, SUB - 1), row(1, SUB - 1), row(2, SUB - 1)
    z = jnp.zeros((SUB, HEAD_DIM), f32)
    e = jnp.exp
    q32 = jnp.concatenate([z, z, e(blk[2] - c31), e(blk[3] - c31)], axis=0)
    k32 = jnp.concatenate([e(c31 - blk[0]), e(c31 - blk[1]), z, z], axis=0)
    q16a = jnp.concatenate([z, e(blk[1] - c15), z, z], axis=0)
    k16a = jnp.concatenate([e(c15 - blk[0]), z, z, z], axis=0)
    q16b = jnp.concatenate([z, z, z, e(blk[3] - c47)], axis=0)
    k16b = jnp.concatenate([z, z, e(c47 - blk[2]), z], axis=0)
    dmid = jnp.concatenate([blk[j] - row(j, SUB // 2 - 1) for j in range(nb)], axis=0)
    qd = e(jnp.minimum(dmid, EXP_CLAMP))
    kd = e(jnp.minimum(-dmid, EXP_CLAMP))
    q2 = jnp.concatenate([q * q32, q * q16a, q * q16b], axis=1)
    k2 = jnp.concatenate([k * k32, k * k16a, k * k16b], axis=1)
    rr, cc = _iota((c, c), 0), _iota((c, c), 1)
    diag = _same_sub(rr, cc) & (cc <= rr)
    att = _bdot(q2, k2, NT) + jnp.where(diag, _bdot(q * qd, k * kd, NT), 0.0)
    o = _bdot(att, hi, NN) + _bdot(q * e(cum), st, NT)
    st_new = st * e(total) + _bdot(hi, k * e(total - cum), TN)
    return _gated_rmsnorm(o, hz, nw), st_new


def _unit_lower_inverse(lower):
    c = CHUNK
    rr, cc = _iota((c, c), 0), _iota((c, c), 1)
    eye = (rr == cc).astype(f32)
    same = _same_sub(rr, cc)
    ld = jnp.where(same, lower, 0.0)
    off = lower - ld
    x = eye - ld
    p = ld
    for _ in range(int(math.log2(SUB)) - 1):
        p = _hdot(p, p, NN)
        x = x + _hdot(x, p, NN)
    n = _hdot(x, off, NN)
    y = eye - n
    p = n
    for _ in range(int(math.log2(c // SUB)) - 1):
        p = _hdot(p, p, NN)
        y = y + _hdot(y, p, NN)
    return _hdot(y, x, NN)


@jax.custom_vjp
def _known_inverse(lower, inv):
    return inv


def _known_inverse_fwd(lower, inv):
    return inv, inv


def _known_inverse_bwd(inv, dinv):
    return -_hdot(_hdot(inv, dinv, TN), inv, NT), jnp.zeros_like(inv)


_known_inverse.defvjp(_known_inverse_fwd, _known_inverse_bwd)


def _gdn_chunk(cq, ck, cv, hz, bl, al, alog, dtb, nw, st, inv=None, with_inverse=False):
    c = CHUNK
    sq, sk, v = _silu(cq), _silu(ck), _silu(cv)
    q = sq * lax.rsqrt(jnp.sum(sq * sq, axis=-1, keepdims=True) + L2_EPS) * (HEAD_DIM ** -0.5)
    k = sk * lax.rsqrt(jnp.sum(sk * sk, axis=-1, keepdims=True) + L2_EPS)
    beta = jax.nn.sigmoid(bl)
    xs = al + dtb
    pos = xs > 0.0
    g = -jnp.exp(alog) * (jnp.where(pos, xs, 0.0) + jnp.log(1.0 + jnp.exp(jnp.where(pos, -xs, xs))))
    rr, cc = _iota((c, c), 0), _iota((c, c), 1)
    tri = (cc <= rr).astype(f32)
    gc = jnp.broadcast_to(g, (c, c))
    cum_c = _hdot(tri, gc, NN)
    cum_r = _hdot(gc, (rr <= cc).astype(f32), TN)
    g128 = jnp.broadcast_to(g, (c, HEAD_DIM))
    cum = _hdot(tri, g128, NN)
    total = jnp.sum(g128, axis=0, keepdims=True)
    diff = cum_c - cum_r
    ninf = -jnp.inf
    lower = beta * _bdot(k, k, NT) * jnp.exp(jnp.where(cc < rr, diff, ninf))
    rhs = jnp.concatenate([beta * v, (beta * jnp.exp(cum)) * k], axis=1)
    minv = _unit_lower_inverse(lower) if inv is None else _known_inverse(lower, inv)
    sol = _hdot(minv, rhs, NN)
    u0, w = sol[:, :HEAD_DIM], sol[:, HEAD_DIM:]
    qk = _bdot(q, k, NT) * jnp.exp(jnp.where(cc <= rr, diff, ninf))
    u = u0 - _bdot(w, st, NN)
    o = _bdot(q * jnp.exp(cum), st, NN) + _bdot(qk, u, NN)
    st_new = st * jnp.exp(total) + _bdot(k * jnp.exp(total - cum), u, TN)
    y = _gated_rmsnorm(o, hz, nw)
    return (y, st_new, minv) if with_inverse else (y, st_new)


def _rows(c):
    return pl.ds(pl.multiple_of(c * CHUNK, CHUNK), CHUNK)


HEADS_PER_STEP = 8
CHUNKS_PER_STEP = 4


def _scan_dims(s, d):
    h, n = d // HEAD_DIM, s // CHUNK
    ncb = min(n, CHUNKS_PER_STEP)
    hp = min(h, HEADS_PER_STEP)
    return h, n, ncb, n // ncb, ncb * CHUNK, hp


def _scan_specs(t, nblk, ncb, hp, reverse):
    blk = (lambda j: nblk - 1 - j) if reverse else (lambda j: j)
    w = hp * HEAD_DIM
    grp = lambda g: pl.BlockSpec((g, t, w), lambda h, j: (0, blk(j), h))
    one = lambda g: pl.BlockSpec((None, t, w), lambda h, j: (g, blk(j), h))
    col = lambda: pl.BlockSpec((t, w), lambda h, j: (blk(j), h))
    vec = lambda: pl.BlockSpec((1, w), lambda h, j: (0, h))
    shared = lambda: pl.BlockSpec((1, HEAD_DIM), lambda h, j: (0, 0))
    state = lambda: pl.BlockSpec((hp, ncb, HEAD_DIM, HEAD_DIM), lambda h, j: (h, blk(j), 0, 0))
    inverse = lambda: pl.BlockSpec((hp, ncb, CHUNK, CHUNK), lambda h, j: (h, blk(j), 0, 0))
    tok = lambda: pl.BlockSpec((hp, t, 1), lambda h, j: (h, blk(j), 0))
    par = lambda: pl.BlockSpec((hp, 1, 1), lambda h, j: (h, 0, 0))
    return dict(grp=grp, one=one, col=col, vec=vec, shared=shared, state=state, inverse=inverse, tok=tok, par=par)


def _scan_params():
    return pltpu.CompilerParams(dimension_semantics=("arbitrary", "arbitrary"), vmem_limit_bytes=VMEM_LIMIT)


def _state_scratch(hp):
    return [pltpu.VMEM((hp, HEAD_DIM, HEAD_DIM), f32)]


def _lanes(i):
    return pl.ds(i * HEAD_DIM, HEAD_DIM)


def _stack_heads(per_head):
    return tuple(jnp.stack(vals) for vals in zip(*per_head))


def _hgrn2_fwd(h4, lb, nw, carry=None):
    _, s, d = h4.shape
    h, n, ncb, nblk, t, hp = _scan_dims(s, d)
    sp = _scan_specs(t, nblk, ncb, hp, False)
    grid = (h // hp, nblk)
    ni, no = (len(carry.arrays), len(carry.out_shapes)) if carry else (0, 0)

    def body(*refs):
        h_ref, lb_ref, nw_ref = refs[:3]
        ins = refs[3:3 + ni]
        y_ref, st_ref = refs[3 + ni:5 + ni]
        outs = refs[5 + ni:5 + ni + no]
        st_scr = refs[5 + ni + no]
        sems = refs[6 + ni + no:]
        _carry_at(carry, grid, "first", ins, outs, sems)

        @pl.when(pl.program_id(1) == 0)
        def _():
            st_scr[...] = jnp.zeros_like(st_scr)

        def step(c, carry):
            r = _rows(c)
            args = [(h_ref[0, r, _lanes(i)], h_ref[1, r, _lanes(i)], h_ref[2, r, _lanes(i)], h_ref[3, r, _lanes(i)],
                     lb_ref[:, _lanes(i)], nw_ref[...], st_scr[i]) for i in range(hp)]
            y, st_new = jax.vmap(_hgrn2_chunk)(*_stack_heads(args))
            for i, a in enumerate(args):
                st_ref[i, c] = a[-1]
                y_ref[r, _lanes(i)] = y[i].astype(y_ref.dtype)
                st_scr[i] = st_new[i]
            return carry

        lax.fori_loop(0, ncb, step, 0)
        _carry_at(carry, grid, "last", ins, outs, sems)

    extra = _carry_layout(carry, 3, 2)
    res = pl.pallas_call(
        body, grid=grid, name="hgrn2_fwd",
        in_specs=[sp["grp"](4), sp["vec"](), sp["shared"]()] + extra.in_specs,
        out_specs=[sp["col"](), sp["state"]()] + extra.out_specs,
        out_shape=[jax.ShapeDtypeStruct((s, d), bf16), jax.ShapeDtypeStruct((h, n, HEAD_DIM, HEAD_DIM), f32)]
        + extra.out_shapes,
        input_output_aliases=extra.aliases,
        scratch_shapes=_state_scratch(hp) + extra.sems, compiler_params=_scan_params())(
            h4, lb, nw, *(carry.arrays if carry else []))
    return (*res[:2], res[2:]) if carry else res


def _hgrn2_bwd(h4, lb, nw, states, dy, carry=None):
    _, s, d = h4.shape
    h, n, ncb, nblk, t, hp = _scan_dims(s, d)
    sp = _scan_specs(t, nblk, ncb, hp, True)
    grid = (h // hp, nblk)
    ni, no = (len(carry.arrays), len(carry.out_shapes)) if carry else (0, 0)

    def body(*refs):
        h_ref, lb_ref, nw_ref, st_ref, dy_ref = refs[:5]
        ins = refs[5:5 + ni]
        dh_ref, dlb_ref, dnw_ref = refs[5 + ni:8 + ni]
        outs = refs[8 + ni:8 + ni + no]
        dst_scr = refs[8 + ni + no]
        sems = refs[9 + ni + no:]
        first_block = pl.program_id(1) == 0
        _carry_at(carry, grid, "first", ins, outs, sems)

        @pl.when(first_block)
        def _():
            dst_scr[...] = jnp.zeros_like(dst_scr)
            dlb_ref[...] = jnp.zeros_like(dlb_ref)

        @pl.when(first_block & (pl.program_id(0) == 0))
        def _():
            dnw_ref[...] = jnp.zeros_like(dnw_ref)

        def step(k, carry):
            c = ncb - 1 - k
            r = _rows(c)
            args = [(h_ref[0, r, _lanes(i)], h_ref[1, r, _lanes(i)], h_ref[2, r, _lanes(i)], h_ref[3, r, _lanes(i)],
                     lb_ref[:, _lanes(i)], nw_ref[...], st_ref[i, c]) for i in range(hp)]
            cots = [(dy_ref[r, _lanes(i)].astype(f32), dst_scr[i]) for i in range(hp)]
            dq, df, di, dz, dlb, dnw, dst = jax.vjp(jax.vmap(_hgrn2_chunk), *_stack_heads(args))[1](_stack_heads(cots))
            dnw_sum = dnw_ref[...]
            for i in range(hp):
                for g, val in enumerate((dq, df, di, dz)):
                    dh_ref[g, r, _lanes(i)] = val[i].astype(dh_ref.dtype)
                dlb_ref[:, _lanes(i)] += dlb[i]
                dnw_sum = dnw_sum + dnw[i]
                dst_scr[i] = dst[i]
            dnw_ref[...] = dnw_sum
            return carry

        lax.fori_loop(0, ncb, step, 0)
        _carry_at(carry, grid, "last", ins, outs, sems)

    extra = _carry_layout(carry, 5, 3)
    res = pl.pallas_call(
        body, grid=grid, name="hgrn2_bwd",
        in_specs=[sp["grp"](4), sp["vec"](), sp["shared"](), sp["state"](), sp["col"]()] + extra.in_specs,
        out_specs=[sp["grp"](4), sp["vec"](), sp["shared"]()] + extra.out_specs,
        out_shape=[jax.ShapeDtypeStruct((4, s, d), bf16), jax.ShapeDtypeStruct((1, d), f32),
                   jax.ShapeDtypeStruct((1, HEAD_DIM), f32)] + extra.out_shapes,
        input_output_aliases=extra.aliases,
        scratch_shapes=_state_scratch(hp) + extra.sems, compiler_params=_scan_params())(
            h4, lb, nw, states, dy, *(carry.arrays if carry else []))
    return (*res[:3], res[3:]) if carry else res


def _gdn_fwd(c3, h4, bl, al, alog, dtb, nw):
    _, s, d = c3.shape
    h, n, ncb, nblk, t, hp = _scan_dims(s, d)
    sp = _scan_specs(t, nblk, ncb, hp, False)

    def body(c_ref, hz_ref, bl_ref, al_ref, alog_ref, dtb_ref, nw_ref, y_ref, st_ref, inv_ref, st_scr):
        @pl.when(pl.program_id(1) == 0)
        def _():
            st_scr[...] = jnp.zeros_like(st_scr)

        def step(c, carry):
            r = _rows(c)
            args = [(c_ref[0, r, _lanes(i)], c_ref[1, r, _lanes(i)], c_ref[2, r, _lanes(i)], hz_ref[r, _lanes(i)],
                     bl_ref[i, r, :], al_ref[i, r, :], alog_ref[i], dtb_ref[i], nw_ref[...], st_scr[i]) for i in range(hp)]
            y, st_new, inv = jax.vmap(functools.partial(_gdn_chunk, with_inverse=True))(*_stack_heads(args))
            for i, a in enumerate(args):
                st_ref[i, c] = a[-1]
                inv_ref[i, c] = inv[i]
                y_ref[r, _lanes(i)] = y[i].astype(y_ref.dtype)
                st_scr[i] = st_new[i]
            return carry

        lax.fori_loop(0, ncb, step, 0)

    return pl.pallas_call(
        body, grid=(h // hp, nblk), name="gdn_fwd",
        in_specs=[sp["grp"](3), sp["one"](3), sp["tok"](), sp["tok"](), sp["par"](), sp["par"](), sp["shared"]()],
        out_specs=[sp["col"](), sp["state"](), sp["inverse"]()],
        out_shape=[jax.ShapeDtypeStruct((s, d), bf16), jax.ShapeDtypeStruct((h, n, HEAD_DIM, HEAD_DIM), f32),
                   jax.ShapeDtypeStruct((h, n, CHUNK, CHUNK), f32)],
        scratch_shapes=_state_scratch(hp), compiler_params=_scan_params())(c3, h4, bl, al, alog, dtb, nw)


def _gdn_bwd(c3, h4, bl, al, alog, dtb, nw, states, inverses, dy):
    _, s, d = c3.shape
    h, n, ncb, nblk, t, hp = _scan_dims(s, d)
    sp = _scan_specs(t, nblk, ncb, hp, True)

    def chunk(*a):
        return _gdn_chunk(*a[:-1], inv=a[-1])

    def body(c_ref, hz_ref, bl_ref, al_ref, alog_ref, dtb_ref, nw_ref, st_ref, inv_ref, dy_ref,
             dc_ref, dz_ref, dbl_ref, dal_ref, dalog_ref, ddtb_ref, dnw_ref, dst_scr):
        first_block = pl.program_id(1) == 0

        @pl.when(first_block)
        def _():
            dst_scr[...] = jnp.zeros_like(dst_scr)
            dalog_ref[...] = jnp.zeros_like(dalog_ref)
            ddtb_ref[...] = jnp.zeros_like(ddtb_ref)

        @pl.when(first_block & (pl.program_id(0) == 0))
        def _():
            dnw_ref[...] = jnp.zeros_like(dnw_ref)

        def step(k, carry):
            c = ncb - 1 - k
            r = _rows(c)
            args = [(c_ref[0, r, _lanes(i)], c_ref[1, r, _lanes(i)], c_ref[2, r, _lanes(i)], hz_ref[r, _lanes(i)],
                     bl_ref[i, r, :], al_ref[i, r, :], alog_ref[i], dtb_ref[i], nw_ref[...], st_ref[i, c], inv_ref[i, c])
                    for i in range(hp)]
            cots = [(dy_ref[r, _lanes(i)].astype(f32), dst_scr[i]) for i in range(hp)]
            dq, dk, dv, dz, dbl, dal, dalog, ddtb, dnw, dst, _ = jax.vjp(
                jax.vmap(chunk), *_stack_heads(args))[1](_stack_heads(cots))
            dnw_sum = dnw_ref[...]
            for i in range(hp):
                for g, val in enumerate((dq, dk, dv)):
                    dc_ref[g, r, _lanes(i)] = val[i]
                dz_ref[r, _lanes(i)] = dz[i].astype(dz_ref.dtype)
                dbl_ref[i, r, :] = dbl[i]
                dal_ref[i, r, :] = dal[i]
                dalog_ref[i] += dalog[i]
                ddtb_ref[i] += ddtb[i]
                dnw_sum = dnw_sum + dnw[i]
                dst_scr[i] = dst[i]
            dnw_ref[...] = dnw_sum
            return carry

        lax.fori_loop(0, ncb, step, 0)

    tokens = jax.ShapeDtypeStruct((h, s, 1), f32)
    heads = jax.ShapeDtypeStruct((h, 1, 1), f32)
    return pl.pallas_call(
        body, grid=(h // hp, nblk), name="gdn_bwd",
        in_specs=[sp["grp"](3), sp["one"](3), sp["tok"](), sp["tok"](), sp["par"](), sp["par"](), sp["shared"](),
                  sp["state"](), sp["inverse"](), sp["col"]()],
        out_specs=[sp["grp"](3), sp["col"](), sp["tok"](), sp["tok"](), sp["par"](), sp["par"](), sp["shared"]()],
        out_shape=[jax.ShapeDtypeStruct((3, s, d), f32), jax.ShapeDtypeStruct((s, d), bf16), tokens, tokens, heads, heads,
                   jax.ShapeDtypeStruct((1, HEAD_DIM), f32)],
        scratch_shapes=_state_scratch(hp), compiler_params=_scan_params())(
            c3, h4, bl, al, alog, dtb, nw, states, inverses, dy)


def _tile(n, pref):
    return n if n <= pref else pref


def _mm(a, b, mode, *, name, out_dtype=f32, add=None, add_scale=1.0, b_g0=0, groups=None, carry=None, a_half=None,
        n_half=None, into=None):
    squeeze = a.ndim == 2 and b.ndim == 2
    a3 = a if a.ndim == 3 else a[None]
    b3 = b if b.ndim == 3 else b[None]
    if mode == "nt":
        g = groups or a3.shape[0]
        (_, m, r), (_, n, _) = a3.shape, b3.shape
    elif mode == "nn":
        g = groups or b3.shape[0]
        (_, m, r), (_, _, n) = a3.shape, b3.shape
        if n_half is not None:
            n = n // 2
    else:
        g = groups or b3.shape[0]
        (_, r, m), (_, _, n) = a3.shape, b3.shape
        if a_half is not None:
            m = m // 2
    tm, tn, tr = _tile(m, 1024), _tile(n, 1024), _tile(r, 2048)
    nr = r // tr
    a_grouped = a3.shape[0] > 1
    dn = {"nn": NN, "nt": NT, "tn": TN}[mode]

    if mode == "nt":
        grid = (m // tm, n // tn, g * nr)
        last = g * nr - 1
        kax = 2
        a_spec = pl.BlockSpec((None, tm, tr), lambda i, j, k: (k // nr, i, k % nr))
        b_spec = pl.BlockSpec((None, tn, tr), lambda i, j, k: (b_g0 + k // nr, j, k % nr))
        o_spec = pl.BlockSpec((tm, tn), lambda i, j, k: (i, j))
        add_spec = pl.BlockSpec((tm, tn), lambda i, j, k: (i, j))
        out_shape = jax.ShapeDtypeStruct((m, n), out_dtype)
        sem = ("parallel", "parallel", "arbitrary")
    else:
        grid = (g, m // tm, n // tn, nr)
        last = nr - 1
        kax = 3
        ag = (lambda q: q) if a_grouped else (lambda q: 0)
        if mode == "nn":
            a_spec = pl.BlockSpec((None, tm, tr), lambda q, i, j, k: (ag(q), i, k))
        else:
            def col0():
                if a_half is None:
                    return 0
                return (_my_core() if a_half == "mine" else 1 - _my_core()) * (m // tm)
            a_spec = pl.BlockSpec((None, tr, tm), lambda q, i, j, k: (ag(q), k, col0() + i))
        off = 0 if n_half is None else n_half * (n // tn)
        b_spec = pl.BlockSpec((None, tr, tn), lambda q, i, j, k: (b_g0 + q, k, off + j))
        o_spec = pl.BlockSpec((None, tm, tn), lambda q, i, j, k: (q, i, off + j))
        add_spec = ANY
        out_shape = jax.ShapeDtypeStruct((g, m, n if n_half is None else 2 * n), out_dtype)
        sem = ("parallel", "parallel", "parallel", "arbitrary")
        assert add is None
        add = into

    n_main = 2 if add is None else 3
    ni = len(carry.arrays) if carry else 0
    no = len(carry.out_shapes) if carry else 0

    def body(*refs):
        a_ref, b_ref = refs[:2]
        add_ref = None if add is None else refs[2]
        ins = refs[n_main:n_main + ni]
        o_ref = refs[n_main + ni]
        outs = refs[n_main + ni + 1:n_main + ni + 1 + no]
        acc = refs[n_main + ni + 1 + no]
        sems = refs[n_main + ni + 2 + no:]
        k = pl.program_id(kax)
        _carry_at(carry, grid, "first", ins, outs, sems)

        @pl.when(k == 0)
        def _():
            acc[...] = jnp.zeros_like(acc)

        acc[...] += _bdot(a_ref[...], b_ref[...], dn)

        @pl.when(k == last)
        def _():
            res = acc[...]
            if add is not None and mode == "nt":
                res = res + add_scale * add_ref[...]
            o_ref[...] = res.astype(o_ref.dtype)

        _carry_at(carry, grid, "last", ins, outs, sems)

    operands = [a3, b3] + ([] if add is None else [add])
    in_specs = [a_spec, b_spec] + ([] if add is None else [add_spec])
    extra = _carry_layout(carry, n_main, 1)
    if carry:
        sem = ("arbitrary",) * len(grid)
    out, *moved = pl.pallas_call(
        body, grid=grid, name=name, in_specs=in_specs + extra.in_specs, out_specs=[o_spec] + extra.out_specs,
        out_shape=[out_shape] + extra.out_shapes,
        input_output_aliases=extra.aliases | ({2: 0} if into is not None else {}),
        scratch_shapes=[pltpu.VMEM((tm, tn), f32)] + extra.sems,
        compiler_params=pltpu.CompilerParams(dimension_semantics=sem, vmem_limit_bytes=VMEM_LIMIT))(
            *operands, *(carry.arrays if carry else []))
    out = out[0] if (squeeze and mode != "nt") else out
    return (out, moved) if carry else out


def _row_block(rows, cols, target_bytes=1 << 20):
    if rows <= SUBLANES:
        return rows
    tr = SUBLANES
    while tr * 2 <= rows and tr * 2 * cols * 4 <= target_bytes and rows % (tr * 2) == 0:
        tr *= 2
    return tr


def _ew_params(n_axes=1):
    return pltpu.CompilerParams(dimension_semantics=("parallel",) * n_axes, vmem_limit_bytes=VMEM_LIMIT)


def _cast(x, dtype, name, into_chip_slot=False):
    r, c = x.shape
    tr = _row_block(r, c, 2 << 20)
    tr = max(tr, 16) if r >= 16 else tr
    spec = pl.BlockSpec((tr, c), lambda i: (i, 0))
    if into_chip_slot:
        out_spec = pl.BlockSpec((None, tr, c), lambda i: (_my_chip(), i, 0))
        out_shape = jax.ShapeDtypeStruct((N_CHIPS, r, c), dtype)
    else:
        out_spec, out_shape = spec, jax.ShapeDtypeStruct((r, c), dtype)

    def body(x_ref, o_ref):
        o_ref[...] = x_ref[...].astype(dtype)

    return pl.pallas_call(body, grid=(r // tr,), name=name, in_specs=[spec], out_specs=out_spec,
                          out_shape=out_shape, compiler_params=_ew_params())(x)


def _ln_stats(r):
    mu = jnp.mean(r, axis=-1, keepdims=True)
    var = jnp.mean(jnp.square(r - mu), axis=-1, keepdims=True)
    return mu, lax.rsqrt(var + LN_EPS)


def _ln_fwd(x, yo, g, b):
    s, d = x.shape
    tr = _row_block(s, d, 2 << 20)
    tr = max(tr, 16)
    big = pl.BlockSpec((tr, d), lambda i: (i, 0))
    vec = pl.BlockSpec((1, d), lambda i: (0, 0))

    def body(x_ref, yo_ref, g_ref, b_ref, o_ref, ob_ref):
        r = DEEPNORM_ALPHA * x_ref[...] + yo_ref[...]
        mu, rstd = _ln_stats(r)
        y = (r - mu) * rstd * g_ref[...] + b_ref[...]
        o_ref[...] = y
        ob_ref[...] = y.astype(bf16)

    return pl.pallas_call(
        body, grid=(s // tr,), name="ln_fwd", in_specs=[big, big, vec, vec], out_specs=[big, big],
        out_shape=[jax.ShapeDtypeStruct((s, d), f32), jax.ShapeDtypeStruct((s, d), bf16)],
        compiler_params=_ew_params())(x, yo, g, b)


def _ln_bwd(x, yo, g, *, dy=None, b=None, target=None, name):
    s, d = x.shape
    with_loss = target is not None
    tr = max(_row_block(s, d, 2 << 20), 16)
    big = pl.BlockSpec((tr, d), lambda i: (i, 0))
    vec = pl.BlockSpec((1, d), lambda i: (0, 0))
    lane = pl.BlockSpec((1, LANES), lambda i: (0, 0))

    def body(*refs):
        if with_loss:
            x_ref, yo_ref, g_ref, b_ref, t_ref, dr_ref, drb_ref, dg_ref, db_ref, loss_ref = refs
        else:
            x_ref, yo_ref, g_ref, dy_ref, dr_ref, drb_ref, dg_ref, db_ref = refs

        @pl.when(pl.program_id(0) == 0)
        def _():
            dg_ref[...] = jnp.zeros_like(dg_ref)
            db_ref[...] = jnp.zeros_like(db_ref)
            if with_loss:
                loss_ref[...] = jnp.zeros_like(loss_ref)

        r = DEEPNORM_ALPHA * x_ref[...] + yo_ref[...]
        mu, rstd = _ln_stats(r)
        xhat = (r - mu) * rstd
        if with_loss:
            err = xhat * g_ref[...] + b_ref[...] - t_ref[...]
            loss_ref[...] += jnp.sum(err * err) * (0.5 / d)
            dyv = err * (1.0 / d)
        else:
            dyv = dy_ref[...]
        dxhat = dyv * g_ref[...]
        m1 = jnp.mean(dxhat, axis=-1, keepdims=True)
        m2 = jnp.mean(dxhat * xhat, axis=-1, keepdims=True)
        dr = rstd * (dxhat - m1 - xhat * m2)
        dr_ref[...] = dr
        drb_ref[...] = dr.astype(bf16)
        dg_ref[...] += jnp.sum(dyv * xhat, axis=0, keepdims=True)
        db_ref[...] += jnp.sum(dyv, axis=0, keepdims=True)

    operands = (x, yo, g, b, target) if with_loss else (x, yo, g, dy)
    in_specs = [big, big, vec, vec, big] if with_loss else [big, big, vec, big]
    out_specs = [big, big, vec, vec] + ([lane] if with_loss else [])
    out_shape = [jax.ShapeDtypeStruct((s, d), f32), jax.ShapeDtypeStruct((s, d), bf16),
                 jax.ShapeDtypeStruct((1, d), f32), jax.ShapeDtypeStruct((1, d), f32)]
    if with_loss:
        out_shape.append(jax.ShapeDtypeStruct((1, LANES), f32))
    return pl.pallas_call(
        body, grid=(s // tr,), name=name, in_specs=in_specs, out_specs=out_specs, out_shape=out_shape,
        compiler_params=pltpu.CompilerParams(dimension_semantics=("arbitrary",), vmem_limit_bytes=VMEM_LIMIT))(*operands)


def _lower_bound(a_lb):
    _, d = a_lb.shape

    def body(a_ref, o_ref):
        a = a_ref[...]
        ex = jnp.exp(a - jnp.max(a, axis=0, keepdims=True))
        p = ex / jnp.sum(ex, axis=0, keepdims=True)
        o_ref[...] = jnp.sum(jnp.where(_iota(p.shape, 0) == 0, p, 0.0), axis=0, keepdims=True)

    return pl.pallas_call(body, name="lower_bound", in_specs=[VMEM], out_specs=VMEM,
                          out_shape=jax.ShapeDtypeStruct((1, d), f32))(a_lb)


def _shift_rows(ext, k, rows):
    return pltpu.roll(ext, k, axis=0)[SUBLANES:SUBLANES + rows]


def _conv_tiles(s, d):
    return _tile(s, 512), _tile(d, 512)


def _conv_fwd(h4, w):
    _, s, d = h4.shape
    tr, tc = _conv_tiles(s, d)
    nj = d // tc
    hb = tr // SUBLANES
    cur = pl.BlockSpec((None, tr, tc), lambda g, j, i: (g, i, j))
    prev = pl.BlockSpec((None, SUBLANES, tc), lambda g, j, i: (g, jnp.maximum(i * hb - 1, 0), j))
    wsp = pl.BlockSpec((CONV_WIDTH, tc), lambda g, j, i: (0, g * nj + j))

    def body(x_ref, p_ref, w_ref, o_ref):
        x = x_ref[...]
        halo = jnp.where(pl.program_id(2) == 0, 0.0, p_ref[...])
        ext = jnp.concatenate([halo, x], axis=0)
        acc = w_ref[CONV_WIDTH - 1:CONV_WIDTH, :] * x
        for k in range(1, CONV_WIDTH):
            acc = acc + w_ref[CONV_WIDTH - 1 - k:CONV_WIDTH - k, :] * _shift_rows(ext, k, tr)
        o_ref[...] = acc

    return pl.pallas_call(
        body, grid=(3, nj, s // tr), name="conv_fwd", in_specs=[cur, prev, wsp], out_specs=cur,
        out_shape=jax.ShapeDtypeStruct((3, s, d), f32), compiler_params=_ew_params(3))(h4, h4, w)


def _conv_bwd(dy3, h4, w):
    _, s, d = dy3.shape
    tr, tc = _conv_tiles(s, d)
    nj = d // tc
    hb = tr // SUBLANES
    ni = s // tr
    cur = pl.BlockSpec((None, tr, tc), lambda g, j, i: (g, i, j))
    prev = pl.BlockSpec((None, SUBLANES, tc), lambda g, j, i: (g, jnp.maximum(i * hb - 1, 0), j))
    nxt = pl.BlockSpec((None, SUBLANES, tc), lambda g, j, i: (g, jnp.minimum((i + 1) * hb, s // SUBLANES - 1), j))
    wsp = pl.BlockSpec((CONV_WIDTH, tc), lambda g, j, i: (0, g * nj + j))

    def body(dy_ref, dn_ref, x_ref, p_ref, w_ref, dx_ref, dw_ref):
        i = pl.program_id(2)

        @pl.when(i == 0)
        def _():
            dw_ref[...] = jnp.zeros_like(dw_ref)

        dy = dy_ref[...]
        x = x_ref[...]
        tail = jnp.where(i == ni - 1, 0.0, dn_ref[...])
        dext = jnp.concatenate([dy, tail], axis=0)
        halo = jnp.where(i == 0, 0.0, p_ref[...])
        xext = jnp.concatenate([halo, x], axis=0)
        acc = w_ref[CONV_WIDTH - 1:CONV_WIDTH, :] * dy
        dw_ref[CONV_WIDTH - 1:CONV_WIDTH, :] += jnp.sum(dy * x, axis=0, keepdims=True)
        for k in range(1, CONV_WIDTH):
            ahead = pltpu.roll(dext, tr + SUBLANES - k, axis=0)[:tr]
            acc = acc + w_ref[CONV_WIDTH - 1 - k:CONV_WIDTH - k, :] * ahead
            dw_ref[CONV_WIDTH - 1 - k:CONV_WIDTH - k, :] += jnp.sum(dy * _shift_rows(xext, k, tr), axis=0, keepdims=True)
        dx_ref[...] = acc.astype(dx_ref.dtype)

    return pl.pallas_call(
        body, grid=(3, nj, ni), name="conv_bwd", in_specs=[cur, nxt, cur, prev, wsp], out_specs=[cur, wsp],
        out_shape=[jax.ShapeDtypeStruct((3, s, d), bf16), jax.ShapeDtypeStruct((CONV_WIDTH, 3 * d), f32)],
        compiler_params=pltpu.CompilerParams(dimension_semantics=("parallel", "parallel", "arbitrary"),
                                             vmem_limit_bytes=VMEM_LIMIT))(dy3, dy3, h4, h4, w)


def _my_core():
    return lax.axis_index("c")


def _my_chip():
    return 2 * lax.axis_index("x") + lax.axis_index("y")


def _chip_sum(halves, other, name):
    _, r2, c = other.shape
    tr = max(_row_block(r2, c), 16)
    own = pl.BlockSpec((None, tr, c), lambda p, i: (2 * p + _my_core(), i, 0))
    blk = pl.BlockSpec((None, tr, c), lambda p, i: (p, i, 0))

    def body(a_ref, b_ref, o_ref):
        o_ref[...] = (a_ref[...] + b_ref[...]).astype(o_ref.dtype)

    return pl.pallas_call(
        body, name=name, grid=(N_CHIPS, r2 // tr), in_specs=[own, blk], out_specs=blk,
        out_shape=jax.ShapeDtypeStruct(other.shape, bf16), compiler_params=_ew_params(2))(halves, other)


def _final_sum(halves, other, remote, name):
    _, r2, c = remote.shape
    tr = max(_row_block(r2, c), 16)
    own = pl.BlockSpec((None, tr, c), lambda i: (2 * _my_chip() + _my_core(), i, 0))
    sib = pl.BlockSpec((None, tr, c), lambda i: (_my_chip(), i, 0))
    slot = lambda j: pl.BlockSpec((None, tr, c), lambda i: (j, i, 0))

    def body(a_ref, b_ref, r0_ref, r1_ref, r2_ref, o_ref):
        acc = a_ref[...] + b_ref[...]
        for ref in (r0_ref, r1_ref, r2_ref):
            acc = acc + ref[...].astype(f32)
        o_ref[...] = acc

    return pl.pallas_call(
        body, name=name, grid=(r2 // tr,), in_specs=[own, sib, slot(0), slot(1), slot(2)],
        out_specs=pl.BlockSpec((None, tr, c), lambda i: (_my_core(), i, 0)),
        out_shape=jax.ShapeDtypeStruct((2, r2, c), f32), compiler_params=_ew_params())(halves, other, remote, remote, remote)


def _adamw(w, g, m, v, name):
    r, c = w.shape
    tr = _row_block(r, c)
    spec = pl.BlockSpec((tr, c), lambda i: (i, 0))

    def body(w_ref, g_ref, m_ref, v_ref, go_ref, d_ref, nm_ref, nv_ref):
        gv = g_ref[...]
        go_ref[...] = gv
        nm = ADAM_B1 * m_ref[...] + (1.0 - ADAM_B1) * gv
        nv = ADAM_B2 * v_ref[...] + (1.0 - ADAM_B2) * jnp.square(gv)
        m_hat = nm / (1.0 - ADAM_B1 ** ADAM_STEP)
        v_hat = nv / (1.0 - ADAM_B2 ** ADAM_STEP)
        d_ref[...] = -ADAM_LR * (m_hat / (jnp.sqrt(v_hat) + ADAM_EPS) + ADAM_WD * w_ref[...])
        nm_ref[...] = nm
        nv_ref[...] = nv

    out = jax.ShapeDtypeStruct((r, c), f32)
    return pl.pallas_call(body, grid=(r // tr,), name=name, in_specs=[spec] * 4, out_specs=[spec] * 4,
                          out_shape=[out, out, out, out], compiler_params=_ew_params())(w, g, m, v)


class _Comm(NamedTuple):
    arrays: list
    aliases: dict
    out_shapes: list
    sems: list
    start: Callable
    finish: Callable


class _CarryLayout(NamedTuple):
    in_specs: list
    out_specs: list
    out_shapes: list
    aliases: dict
    sems: list


def _carry_layout(comm, n_in, n_out):
    if comm is None:
        return _CarryLayout([], [], [], {}, [])
    return _CarryLayout([ANY] * len(comm.arrays), [ANY] * len(comm.out_shapes), list(comm.out_shapes),
                        {n_in + i: n_out + o for i, o in comm.aliases.items()}, list(comm.sems))


def _carry_at(comm, grid, edge, ins, outs, sems):
    if comm is None:
        return
    hit = [pl.program_id(ax) == (0 if edge == "first" else extent - 1) for ax, extent in enumerate(grid)]

    @pl.when(functools.reduce(jnp.logical_and, hit))
    def _():
        (comm.start if edge == "first" else comm.finish)(ins, outs, sems)


def _compose(first, second):
    na, no, ns = len(first.arrays), len(first.out_shapes), len(first.sems)

    def both(which):
        def run(ins, outs, sems):
            getattr(first, which)(ins[:na], outs[:no], sems[:ns])
            getattr(second, which)(ins[na:], outs[no:], sems[ns:])
        return run

    aliases = dict(first.aliases) | {na + i: no + o for i, o in second.aliases.items()}
    return _Comm(first.arrays + second.arrays, aliases, first.out_shapes + second.out_shapes, first.sems + second.sems,
                 both("start"), both("finish"))


def _run_comm(comm, name):
    ni, no = len(comm.arrays), len(comm.out_shapes)

    def body(*refs):
        ins, outs, sems = refs[:ni], refs[ni:ni + no], refs[ni + no:]
        comm.start(ins, outs, sems)
        comm.finish(ins, outs, sems)

    return pl.pallas_call(body, name=name, in_specs=[ANY] * ni, out_specs=[ANY] * no, out_shape=comm.out_shapes,
                          input_output_aliases=comm.aliases, scratch_shapes=comm.sems)(*comm.arrays)


def _place():
    return lax.axis_index("x"), lax.axis_index("y"), lax.axis_index("c")


def _chip_peers(x, y):
    out = []
    for fx, fy in ((1, 0), (0, 1), (1, 1)):
        px, py = x ^ fx, y ^ fy
        out.append((px, py, 2 * px + py))
    return out


def _all_gather(arrays, split, cols=None):
    na = len(arrays)

    def part(ref, a, block, which):
        width = slice(None) if cols is None or cols[a] is None else pl.ds(*cols[a])
        if not split[a]:
            return ref.at[block, :, width]
        rows = arrays[a].shape[1] // 2
        return ref.at[block, pl.ds(which * rows, rows), width]

    def ici_sends(outs, sems):
        send_sems, recv_sems = sems
        x, y, c = _place()
        me = 2 * x + y
        out = []
        for a in range(na):
            for j, (px, py, _) in enumerate(_chip_peers(x, y)):
                mine = part(outs[a], a, me, c)
                out.append(pltpu.make_async_remote_copy(mine, mine, send_sems.at[a, j], recv_sems.at[a, j],
                                                        device_id=(px, py, c), device_id_type=MESH))
        return out

    def forwards(outs, sems, which_core):
        send_sems, recv_sems = sems
        x, y, c = _place()
        out = []
        for a in range(na):
            if split[a]:
                for j, (_, _, pid) in enumerate(_chip_peers(x, y)):
                    got = part(outs[a], a, pid, which_core)
                    out.append(pltpu.make_async_remote_copy(got, got, send_sems.at[a, 3 + j], recv_sems.at[a, 3 + j],
                                                            device_id=(x, y, 1 - c), device_id_type=MESH))
        return out

    def arrivals(outs, sems):
        send_sems, recv_sems = sems
        x, y, c = _place()
        out = []
        for a in range(na):
            for j, (px, py, pid) in enumerate(_chip_peers(x, y)):
                got = part(outs[a], a, pid, c)
                out.append(pltpu.make_async_remote_copy(got, got, send_sems.at[a, j], recv_sems.at[a, j],
                                                        device_id=(px, py, c), device_id_type=MESH))
        return out

    def finish(ins, outs, sems):
        c = lax.axis_index("c")
        for cp in arrivals(outs, sems):
            cp.wait_recv()
        mine = forwards(outs, sems, c)
        for cp in mine:
            cp.start()
        for cp in forwards(outs, sems, 1 - c):
            cp.wait_recv()
        for cp in ici_sends(outs, sems) + mine:
            cp.wait_send()

    return _Comm(
        arrays=list(arrays), aliases={a: a for a in range(na)},
        out_shapes=[jax.ShapeDtypeStruct(a.shape, a.dtype) for a in arrays],
        sems=[pltpu.SemaphoreType.DMA((na, 6)), pltpu.SemaphoreType.DMA((na, 6))],
        start=lambda ins, outs, sems: [cp.start() for cp in ici_sends(outs, sems)], finish=finish)


def _swap_halves(arrays):
    na = len(arrays)

    def copies(ins, outs, sems):
        send_sems, recv_sems = sems
        x, y, c = _place()
        out = []
        for a in range(na):
            rows = arrays[a].shape[1] // 2
            src = ins[a].at[:, pl.ds((1 - c) * rows, rows), :]
            out.append(pltpu.make_async_remote_copy(src, outs[a], send_sems.at[a], recv_sems.at[a],
                                                    device_id=(x, y, 1 - c), device_id_type=MESH))
        return out

    return _Comm(
        arrays=list(arrays), aliases={},
        out_shapes=[jax.ShapeDtypeStruct((a.shape[0], a.shape[1] // 2, a.shape[2]), a.dtype) for a in arrays],
        sems=[pltpu.SemaphoreType.DMA((na,)), pltpu.SemaphoreType.DMA((na,))],
        start=lambda ins, outs, sems: [cp.start() for cp in copies(ins, outs, sems)],
        finish=lambda ins, outs, sems: [cp.wait() for cp in copies(ins, outs, sems)])


def _send_to_sibling(arrays):
    na = len(arrays)

    def copies(ins, outs, sems):
        send_sems, recv_sems = sems
        x, y, c = _place()
        return [pltpu.make_async_remote_copy(ins[a], outs[a], send_sems.at[a], recv_sems.at[a],
                                             device_id=(x, y, 1 - c), device_id_type=MESH) for a in range(na)]

    return _Comm(
        arrays=list(arrays), aliases={}, out_shapes=[jax.ShapeDtypeStruct(a.shape, a.dtype) for a in arrays],
        sems=[pltpu.SemaphoreType.DMA((na,)), pltpu.SemaphoreType.DMA((na,))],
        start=lambda ins, outs, sems: [cp.start() for cp in copies(ins, outs, sems)],
        finish=lambda ins, outs, sems: [cp.wait() for cp in copies(ins, outs, sems)])


def _pair_sum(keep, other, name):
    _, r2, c = keep.shape
    tr = max(_row_block(r2, c), 16)
    blk = pl.BlockSpec((None, tr, c), lambda p, i: (p, i, 0))

    def body(a_ref, b_ref, o_ref):
        o_ref[...] = (a_ref[...] + b_ref[...]).astype(o_ref.dtype)

    return pl.pallas_call(body, name=name, grid=(N_CHIPS, r2 // tr), in_specs=[blk, blk], out_specs=blk,
                          out_shape=jax.ShapeDtypeStruct(keep.shape, bf16), compiler_params=_ew_params(2))(keep, other)


def _final_sum_pair(keep, other, remote, name):
    _, r2, c = remote.shape
    tr = max(_row_block(r2, c), 16)
    own = pl.BlockSpec((None, tr, c), lambda i: (_my_chip(), i, 0))
    slot = lambda j: pl.BlockSpec((None, tr, c), lambda i: (j, i, 0))

    def body(a_ref, b_ref, r0_ref, r1_ref, r2_ref, o_ref):
        acc = a_ref[...] + b_ref[...]
        for ref in (r0_ref, r1_ref, r2_ref):
            acc = acc + ref[...].astype(f32)
        o_ref[...] = acc

    return pl.pallas_call(
        body, name=name, grid=(r2 // tr,), in_specs=[own, own, slot(0), slot(1), slot(2)],
        out_specs=pl.BlockSpec((None, tr, c), lambda i: (_my_core(), i, 0)),
        out_shape=jax.ShapeDtypeStruct((2, r2, c), f32), compiler_params=_ew_params())(keep, other, remote, remote, remote)


def _scatter_blocks(arrays):
    na = len(arrays)

    def copies(ins, outs, sems):
        send_sems, recv_sems = sems
        x, y, c = _place()
        out = []
        for a in range(na):
            for j, (px, py, pid) in enumerate(_chip_peers(x, y)):
                out.append(pltpu.make_async_remote_copy(ins[a].at[pid], outs[a].at[j], send_sems.at[a, j],
                                                        recv_sems.at[a, j], device_id=(px, py, c), device_id_type=MESH))
        return out

    return _Comm(
        arrays=list(arrays), aliases={},
        out_shapes=[jax.ShapeDtypeStruct((3,) + a.shape[1:], a.dtype) for a in arrays],
        sems=[pltpu.SemaphoreType.DMA((na, 3)), pltpu.SemaphoreType.DMA((na, 3))],
        start=lambda ins, outs, sems: [cp.start() for cp in copies(ins, outs, sems)],
        finish=lambda ins, outs, sems: [cp.wait() for cp in copies(ins, outs, sems)])


def _join_halves(arrays):
    na = len(arrays)

    def copies(outs, sems, slot):
        send_sems, recv_sems = sems
        x, y, c = _place()
        which = c if slot == "mine" else 1 - c
        return [pltpu.make_async_remote_copy(outs[a].at[which], outs[a].at[which], send_sems.at[a], recv_sems.at[a],
                                             device_id=(x, y, 1 - c), device_id_type=MESH) for a in range(na)]

    def finish(ins, outs, sems):
        for cp in copies(outs, sems, "mine"):
            cp.wait_send()
        for cp in copies(outs, sems, "sibling's"):
            cp.wait_recv()

    return _Comm(
        arrays=list(arrays), aliases={a: a for a in range(na)},
        out_shapes=[jax.ShapeDtypeStruct(a.shape, a.dtype) for a in arrays],
        sems=[pltpu.SemaphoreType.DMA((na,)), pltpu.SemaphoreType.DMA((na,))],
        start=lambda ins, outs, sems: [cp.start() for cp in copies(outs, sems, "mine")], finish=finish)


SMALL_ROWS = 24


def _all_reduce_small(a_lb, dlb, pieces):
    d = a_lb.shape[1]
    npc = len(pieces)

    def body(*refs):
        alb_ref, dlb_ref = refs[:2]
        piece_refs = refs[2:2 + npc]
        out_ref, pack, slots, send_sems, recv_sems = refs[2 + npc:]
        x, y, c = _place()
        me = 4 * x + 2 * y + c
        pack[...] = jnp.zeros_like(pack)
        a = alb_ref[...]
        ex = jnp.exp(a - jnp.max(a, axis=0, keepdims=True))
        p = ex / jnp.sum(ex, axis=0, keepdims=True)
        first = (_iota(p.shape, 0) == 0).astype(f32)
        p0 = jnp.sum(p * first, axis=0, keepdims=True)
        pack[0:3, :] = p * dlb_ref[...] * (first - p0)
        for ref, (arr, r0) in zip(piece_refs, pieces):
            pack[r0:r0 + arr.shape[0], 0:arr.shape[1]] = ref[...]
        slots[me] = pack[...]
        copies = []
        for k in range(1, N_DEV):
            peer = (x ^ (k >> 2), y ^ ((k >> 1) & 1), c ^ (k & 1))
            cp = pltpu.make_async_remote_copy(pack, slots.at[me], send_sems.at[k - 1], recv_sems.at[k - 1],
                                              device_id=peer, device_id_type=MESH)
            cp.start()
            copies.append(cp)
        for k in range(1, N_DEV):
            src = (x ^ (k >> 2)) * 4 + (y ^ ((k >> 1) & 1)) * 2 + (c ^ (k & 1))
            pltpu.make_async_remote_copy(pack, slots.at[src], send_sems.at[k - 1], recv_sems.at[k - 1],
                                         device_id=(x, y, c), device_id_type=MESH).wait_recv()
        for cp in copies:
            cp.wait_send()
        acc = slots[0]
        for i in range(1, N_DEV):
            acc = acc + slots[i]
        out_ref[...] = acc

    return pl.pallas_call(
        body, name="all_reduce_small", in_specs=[VMEM] * (2 + npc), out_specs=VMEM,
        out_shape=jax.ShapeDtypeStruct((SMALL_ROWS, d), f32),
        scratch_shapes=[pltpu.VMEM((SMALL_ROWS, d), f32), pltpu.VMEM((N_DEV, SMALL_ROWS, d), f32),
                        pltpu.SemaphoreType.DMA((N_DEV - 1,)), pltpu.SemaphoreType.DMA((N_DEV - 1,))],
        compiler_params=pltpu.CompilerParams(vmem_limit_bytes=VMEM_LIMIT))(
            a_lb, dlb, *[p[0] for p in pieces])


def _as_halves(g):
    return g.reshape(2 * g.shape[0], g.shape[1] // 2, g.shape[2])


def _chip_sums(grads, from_sibling, tag):
    return [_chip_sum(_as_halves(g), t, f"grad_chip_sum_{tag}{a}") for a, (g, t) in enumerate(zip(grads, from_sibling))]


def _final_sums(grads, from_sibling, from_chips, tag):
    return [_final_sum(_as_halves(g), t, r, f"grad_final_sum_{tag}{a}")
            for a, (g, t, r) in enumerate(zip(grads, from_sibling, from_chips))]


def kernel(x, a_w_in, a_lower_bounds, a_norm_w, a_w_out, b_w_in, b_conv_w, b_a_log, b_dt_bias, b_norm_w, b_w_out, ln_g, ln_b, loss_target, m_a_w_in, m_a_lower_bounds, m_a_norm_w, m_a_w_out, m_b_w_in, m_b_conv_w, m_b_a_log, m_b_dt_bias, m_b_norm_w, m_b_w_out, m_ln_g, m_ln_b, v_a_w_in, v_a_lower_bounds, v_a_norm_w, v_a_w_out, v_b_w_in, v_b_conv_w, v_b_a_log, v_b_dt_bias, v_b_norm_w, v_b_w_out, v_ln_g, v_ln_b):
    _, s, d = x.shape
    h = d // HEAD_DIM
    wb = b_w_in.shape[2]
    x2, target = x[0], loss_target[0]
    chip = 2 * lax.axis_index("x") + lax.axis_index("y")

    shards = [_cast(a_w_in[0], bf16, "cast_a_w_in", True), _cast(a_w_out[0], bf16, "cast_a_w_out", True),
              _cast(b_w_in[0], bf16, "cast_b_w_in", True), _cast(b_w_out[0], bf16, "cast_b_w_out", True),
              _cast(b_conv_w[0], f32, "copy_b_conv_w", True)]
    sh_a_in, sh_a_out, sh_b_in, sh_b_out, sh_conv = shards
    left, right = (0, d // 2), (d // 2, d // 2)
    wa4, conv_g = _run_comm(_all_gather([sh_a_in, sh_conv], [True, False], [left, None]), "all_gather_first")
    conv_w = conv_g.transpose(1, 0, 2).reshape(CONV_WIDTH, 3 * d)

    x_bf = _cast(x2, bf16, "cast_x")
    lb = _lower_bound(a_lower_bounds)
    h4a, (wa4,) = _mm(x_bf, wa4, "nn", name="a_in_fwd_left", n_half=0, carry=_all_gather([wa4], [True], [right]))
    h4a, (wb_g,) = _mm(x_bf, wa4, "nn", name="a_in_fwd_right", n_half=1, into=h4a, carry=_all_gather([sh_b_in], [True]))
    wb2d = wb_g.transpose(1, 0, 2).reshape(d, N_CHIPS * wb)
    wb4 = wb2d[:, :4 * d].reshape(d, 4, d).transpose(1, 0, 2)
    wba = jnp.pad(wb2d[:, 4 * d:], ((0, 0), (0, LANES - 2 * h)))
    ya, st_a, (wa_out_g,) = _hgrn2_fwd(h4a, lb, a_norm_w, carry=_all_gather([sh_a_out], [True]))
    wa_out = wa_out_g.reshape(d, d)
    yo_a = _mm(ya, wa_out, "nn", name="a_out_fwd")
    x1, x1_bf = _ln_fwd(x2, yo_a, ln_g[0:1], ln_b[0:1])
    h4b, (wb_out_g,) = _mm(x1_bf, wb4, "nn", name="b_in_fwd", carry=_all_gather([sh_b_out], [True]))
    wb_out = wb_out_g.reshape(d, d)
    ba = _mm(x1_bf, wba, "nn", name="b_gate_fwd")
    c3 = _conv_fwd(h4b, conv_w)
    bl = ba[:, :h].T.reshape(h, s, 1)
    al = ba[:, h:2 * h].T.reshape(h, s, 1)
    alog, dtb = b_a_log.reshape(h, 1, 1), b_dt_bias.reshape(h, 1, 1)
    yb, st_b, inv_b = _gdn_fwd(c3, h4b, bl, al, alog, dtb, b_norm_w)
    yo_b = _mm(yb, wb_out, "nn", name="b_out_fwd")

    dr2, dr2_bf, dg2, db2, loss = _ln_bwd(x1, yo_b, ln_g[1:2], b=ln_b[1:2], target=target, name="ln2_loss_bwd")
    g_wb_out = _mm(yb, dr2_bf, "tn", name="b_out_dw")
    dyb = _mm(dr2_bf, wb_out, "nt", name="b_out_dx")
    dc3, dhz, dbl, dal, dalog, ddtb, dnw_b = _gdn_bwd(c3, h4b, bl, al, alog, dtb, b_norm_w, st_b, inv_b, dyb)
    dh3, dconv = _conv_bwd(dc3, h4b, conv_w)
    dba = jnp.pad(jnp.concatenate([dbl.reshape(h, s).T, dal.reshape(h, s).T], axis=1), ((0, 0), (0, LANES - 2 * h)))
    g_wb3 = _mm(x1_bf, dh3, "tn", name="b_in_dw_qkv")
    g_wbz = _mm(x1_bf, dhz, "tn", name="b_in_dw_z")
    g_wba = _mm(x1_bf, dba, "tn", name="b_in_dw_gate")
    dx1 = _mm(dh3, wb4, "nt", name="b_in_dx_qkv", groups=3, add=dr2, add_scale=DEEPNORM_ALPHA)
    dx1 = _mm(dhz, wb4, "nt", name="b_in_dx_z", b_g0=3, groups=1, add=dx1)
    dx1 = _mm(dba, wba, "nt", name="b_in_dx_gate", add=dx1)
    dr1, dr1_bf, dg1, db1 = _ln_bwd(x2, yo_a, ln_g[0:1], dy=dx1, name="ln1_bwd")
    g_wb2d = jnp.concatenate([g_wb3[0], g_wb3[1], g_wb3[2], g_wbz, g_wba[:, :2 * h]], axis=1)
    grads_b = [g_wb2d.reshape(d, N_CHIPS, wb).transpose(1, 0, 2), g_wb_out.reshape(N_CHIPS, d // N_CHIPS, d)]
    g_wa_out, sib_b = _mm(ya, dr1_bf, "tn", name="a_out_dw", carry=_swap_halves(grads_b))
    chip_b = _chip_sums(grads_b, sib_b, "b")
    dya = _mm(dr1_bf, wa_out, "nt", name="a_out_dx")
    dh4, dlb, dnw_a, far_b = _hgrn2_bwd(h4a, lb, a_norm_w, st_a, dya, carry=_scatter_blocks(chip_b))
    grads_a_out = [g_wa_out.reshape(N_CHIPS, d // N_CHIPS, d)]
    g_wa4_away = _mm(x_bf, dh4, "tn", name="a_in_dw_away", a_half="other")
    g_wa4_keep, (sib_a_in, sib_a_out) = _mm(x_bf, dh4, "tn", name="a_in_dw_keep", a_half="mine",
                                            carry=_compose(_send_to_sibling([g_wa4_away]), _swap_halves(grads_a_out)))
    chip_a = [_pair_sum(g_wa4_keep, sib_a_in, "grad_chip_sum_a_in")] + _chip_sums(grads_a_out, [sib_a_out], "a_out")
    grad_x, far_a = _mm(dh4, wa4, "nt", name="a_in_dx", add=dr1, add_scale=DEEPNORM_ALPHA, carry=_scatter_blocks(chip_a))

    finals = ([_final_sum_pair(g_wa4_keep, sib_a_in, far_a[0], "grad_final_sum_a_in")]
              + _final_sums(grads_a_out, [sib_a_out], far_a[1:], "a_out") + _final_sums(grads_b, sib_b, far_b, "b"))
    big = [j.reshape(2 * j.shape[1], j.shape[2]) for j in _run_comm(_join_halves(finals), "grad_join_halves")]
    small = _all_reduce_small(a_lower_bounds, dlb, [
        (dg1, 3), (dg2, 4), (db1, 5), (db2, 6), (dnw_a, 7), (dnw_b, 8), (dalog.reshape(1, h), 9), (ddtb.reshape(1, h), 10),
        (loss, 11), (dconv.reshape(3 * CONV_WIDTH, d), 12)])
    loss_out = small[11, 0]
    cw = 3 * d // N_CHIPS
    g_conv = lax.dynamic_slice_in_dim(small[12:24].reshape(CONV_WIDTH, 3 * d), chip * cw, cw, axis=1)
    grads = {
        "a_w_in": big[0], "a_lower_bounds": small[0:3], "a_norm_w": small[7:8, :HEAD_DIM], "a_w_out": big[1],
        "b_w_in": big[2], "b_conv_w": g_conv, "b_a_log": small[9:10, :h], "b_dt_bias": small[10:11, :h],
        "b_norm_w": small[8:9, :HEAD_DIM], "b_w_out": big[3],
        "ln_g": jnp.concatenate([small[3:4], small[4:5]], axis=0), "ln_b": jnp.concatenate([small[5:6], small[6:7]], axis=0),
    }

    weights = dict(a_w_in=a_w_in, a_lower_bounds=a_lower_bounds, a_norm_w=a_norm_w, a_w_out=a_w_out, b_w_in=b_w_in,
                   b_conv_w=b_conv_w, b_a_log=b_a_log, b_dt_bias=b_dt_bias, b_norm_w=b_norm_w, b_w_out=b_w_out,
                   ln_g=ln_g, ln_b=ln_b)
    m_in = dict(a_w_in=m_a_w_in, a_lower_bounds=m_a_lower_bounds, a_norm_w=m_a_norm_w, a_w_out=m_a_w_out, b_w_in=m_b_w_in,
                b_conv_w=m_b_conv_w, b_a_log=m_b_a_log, b_dt_bias=m_b_dt_bias, b_norm_w=m_b_norm_w, b_w_out=m_b_w_out,
                ln_g=m_ln_g, ln_b=m_ln_b)
    v_in = dict(a_w_in=v_a_w_in, a_lower_bounds=v_a_lower_bounds, a_norm_w=v_a_norm_w, a_w_out=v_a_w_out, b_w_in=v_b_w_in,
                b_conv_w=v_b_conv_w, b_a_log=v_b_a_log, b_dt_bias=v_b_dt_bias, b_norm_w=v_b_norm_w, b_w_out=v_b_w_out,
                ln_g=v_ln_g, ln_b=v_ln_b)
    out_g, out_d, out_m, out_v = [], [], [], []
    for name, w in weights.items():
        shape2 = w.shape[-2:]
        g, delta, nm, nv = _adamw(w.reshape(shape2), grads[name].reshape(shape2), m_in[name].reshape(shape2),
                                  v_in[name].reshape(shape2), "adamw_" + name)
        out_g.append(g.reshape(w.shape))
        out_d.append(delta.reshape(w.shape))
        out_m.append(nm.reshape(w.shape))
        out_v.append(nv.reshape(w.shape))
    return (loss_out, grad_x.reshape(x.shape), *out_g, *out_d, *out_m, *out_v)
```

```python
import functools
import math
from typing import Callable, NamedTuple

import jax
import jax.numpy as jnp
from jax import lax
from jax.experimental import pallas as pl
from jax.experimental.pallas import tpu as pltpu

f32 = jnp.float32
bf16 = jnp.bfloat16
HIGHEST = lax.Precision.HIGHEST

HEAD_DIM = 128
CHUNK = 64
SUB = 16
DEPTH = 2
DEEPNORM_ALPHA = (2.0 * DEPTH) ** 0.25
LN_EPS = 1e-5
RMS_EPS = 1e-6
L2_EPS = 1e-6
ADAM_LR, ADAM_B1, ADAM_B2, ADAM_EPS, ADAM_WD, ADAM_STEP = 0.001, 0.9, 0.999, 1e-08, 0.01, 10
CONV_WIDTH = 4
EXP_CLAMP = 40.0
VMEM_LIMIT = 56 << 20
MXU_DTYPE = bf16
SUBLANES = 8
LANES = 128
N_CHIPS = 4
N_DEV = 8

NN = (((1,), (0,)), ((), ()))
NT = (((1,), (1,)), ((), ()))
TN = (((0,), (0,)), ((), ()))
MESH = pl.DeviceIdType.MESH
ANY = pl.BlockSpec(memory_space=pl.ANY)
VMEM = pl.BlockSpec(memory_space=pltpu.VMEM)


def _bdot(a, b, dn):
    return lax.dot_general(a.astype(MXU_DTYPE), b.astype(MXU_DTYPE), dn, preferred_element_type=f32)


def _hdot(a, b, dn):
    return lax.dot_general(a, b, dn, precision=lax.Precision.HIGH, preferred_element_type=f32)


def _iota(shape, dim):
    return lax.broadcasted_iota(jnp.int32, shape, dim)


def _same_sub(a, b):
    shift = int(math.log2(SUB))
    return jnp.right_shift(a, shift) == jnp.right_shift(b, shift)


def _silu(x):
    return x * jax.nn.sigmoid(x)


def _gated_rmsnorm(o, hz, nw):
    return o * lax.rsqrt(jnp.mean(o * o, axis=-1, keepdims=True) + RMS_EPS) * nw * _silu(hz)


def _hgrn2_chunk(hq, hf, hi, hz, lb, nw, st):
    c = CHUNK
    q = _silu(hq)
    forget = lb + (1.0 - lb) * jax.nn.sigmoid(hf)
    k = 1.0 - forget
    g = jnp.log(forget)
    tri = (_iota((c, c), 1) <= _iota((c, c), 0)).astype(f32)
    cum = _hdot(tri, g, NN)
    total = jnp.sum(g, axis=0, keepdims=True)
    nb = c // SUB
    blk = [cum[SUB * j:SUB * (j + 1)] for j in range(nb)]
    r16 = _iota((SUB, HEAD_DIM), 0)

    def row(j, i):
        return jnp.sum(jnp.where(r16 == i, blk[j], 0.0), axis=0, keepdims=True)

    c15, c31, c47 = row(0, SUB - 1), row(1, SUB - 1), row(2, SUB - 1)
    z = jnp.zeros((SUB, HEAD_DIM), f32)
    e = jnp.exp
    q32 = jnp.concatenate([z, z, e(blk[2] - c31), e(blk[3] - c31)], axis=0)
    k32 = jnp.concatenate([e(c31 - blk[0]), e(c31 - blk[1]), z, z], axis=0)
    q16a = jnp.concatenate([z, e(blk[1] - c15), z, z], axis=0)
    k16a = jnp.concatenate([e(c15 - blk[0]), z, z, z], axis=0)
    q16b = jnp.concatenate([z, z, z, e(blk[3] - c47)], axis=0)
    k16b = jnp.concatenate([z, z, e(c47 - blk[2]), z], axis=0)
    dmid = jnp.concatenate([blk[j] - row(j, SUB // 2 - 1) for j in range(nb)], axis=0)
    qd = e(jnp.minimum(dmid, EXP_CLAMP))
    kd = e(jnp.minimum(-dmid, EXP_CLAMP))
    q2 = jnp.concatenate([q * q32, q * q16a, q * q16b], axis=1)
    k2 = jnp.concatenate([k * k32, k * k16a, k * k16b], axis=1)
    rr, cc = _iota((c, c), 0), _iota((c, c), 1)
    diag = _same_sub(rr, cc) & (cc <= rr)
    att = _bdot(q2, k2, NT) + jnp.where(diag, _bdot(q * qd, k * kd, NT), 0.0)
    o = _bdot(att, hi, NN) + _bdot(q * e(cum), st, NT)
    st_new = st * e(total) + _bdot(hi, k * e(total - cum), TN)
    return _gated_rmsnorm(o, hz, nw), st_new


def _unit_lower_inverse(lower):
    c = CHUNK
    rr, cc = _iota((c, c), 0), _iota((c, c), 1)
    eye = (rr == cc).astype(f32)
    same = _same_sub(rr, cc)
    ld = jnp.where(same, lower, 0.0)
    off = lower - ld
    x = eye - ld
    p = ld
    for _ in range(int(math.log2(SUB)) - 1):
        p = _hdot(p, p, NN)
        x = x + _hdot(x, p, NN)
    n = _hdot(x, off, NN)
    y = eye - n
    p = n
    for _ in range(int(math.log2(c // SUB)) - 1):
        p = _hdot(p, p, NN)
        y = y + _hdot(y, p, NN)
    return _hdot(y, x, NN)


@jax.custom_vjp
def _known_inverse(lower, inv):
    return inv


def _known_inverse_fwd(lower, inv):
    return inv, inv


def _known_inverse_bwd(inv, dinv):
    return -_hdot(_hdot(inv, dinv, TN), inv, NT), jnp.zeros_like(inv)


_known_inverse.defvjp(_known_inverse_fwd, _known_inverse_bwd)


def _gdn_chunk(cq, ck, cv, hz, bl, al, alog, dtb, nw, st, inv=None, with_inverse=False):
    c = CHUNK
    sq, sk, v = _silu(cq), _silu(ck), _silu(cv)
    q = sq * lax.rsqrt(jnp.sum(sq * sq, axis=-1, keepdims=True) + L2_EPS) * (HEAD_DIM ** -0.5)
    k = sk * lax.rsqrt(jnp.sum(sk * sk, axis=-1, keepdims=True) + L2_EPS)
    beta = jax.nn.sigmoid(bl)
    xs = al + dtb
    pos = xs > 0.0
    g = -jnp.exp(alog) * (jnp.where(pos, xs, 0.0) + jnp.log(1.0 + jnp.exp(jnp.where(pos, -xs, xs))))
    rr, cc = _iota((c, c), 0), _iota((c, c), 1)
    tri = (cc <= rr).astype(f32)
    gc = jnp.broadcast_to(g, (c, c))
    cum_c = _hdot(tri, gc, NN)
    cum_r = _hdot(gc, (rr <= cc).astype(f32), TN)
    g128 = jnp.broadcast_to(g, (c, HEAD_DIM))
    cum = _hdot(tri, g128, NN)
    total = jnp.sum(g128, axis=0, keepdims=True)
    diff = cum_c - cum_r
    ninf = -jnp.inf
    lower = beta * _bdot(k, k, NT) * jnp.exp(jnp.where(cc < rr, diff, ninf))
    rhs = jnp.concatenate([beta * v, (beta * jnp.exp(cum)) * k], axis=1)
    minv = _unit_lower_inverse(lower) if inv is None else _known_inverse(lower, inv)
    sol = _hdot(minv, rhs, NN)
    u0, w = sol[:, :HEAD_DIM], sol[:, HEAD_DIM:]
    qk = _bdot(q, k, NT) * jnp.exp(jnp.where(cc <= rr, diff, ninf))
    u = u0 - _bdot(w, st, NN)
    o = _bdot(q * jnp.exp(cum), st, NN) + _bdot(qk, u, NN)
    st_new = st * jnp.exp(total) + _bdot(k * jnp.exp(total - cum), u, TN)
    y = _gated_rmsnorm(o, hz, nw)
    return (y, st_new, minv) if with_inverse else (y, st_new)


def _rows(c):
    return pl.ds(pl.multiple_of(c * CHUNK, CHUNK), CHUNK)


HEADS_PER_STEP = 8
CHUNKS_PER_STEP = 4


def _scan_dims(s, d):
    h, n = d // HEAD_DIM, s // CHUNK
    ncb = min(n, CHUNKS_PER_STEP)
    hp = min(h, HEADS_PER_STEP)
    return h, n, ncb, n // ncb, ncb * CHUNK, hp


def _scan_specs(t, nblk, ncb, hp, reverse):
    blk = (lambda j: nblk - 1 - j) if reverse else (lambda j: j)
    w = hp * HEAD_DIM
    grp = lambda g: pl.BlockSpec((g, t, w), lambda h, j: (0, blk(j), h))
    one = lambda g: pl.BlockSpec((None, t, w), lambda h, j: (g, blk(j), h))
    col = lambda: pl.BlockSpec((t, w), lambda h, j: (blk(j), h))
    vec = lambda: pl.BlockSpec((1, w), lambda h, j: (0, h))
    shared = lambda: pl.BlockSpec((1, HEAD_DIM), lambda h, j: (0, 0))
    state = lambda: pl.BlockSpec((hp, ncb, HEAD_DIM, HEAD_DIM), lambda h, j: (h, blk(j), 0, 0))
    inverse = lambda: pl.BlockSpec((hp, ncb, CHUNK, CHUNK), lambda h, j: (h, blk(j), 0, 0))
    tok = lambda: pl.BlockSpec((hp, t, 1), lambda h, j: (h, blk(j), 0))
    par = lambda: pl.BlockSpec((hp, 1, 1), lambda h, j: (h, 0, 0))
    return dict(grp=grp, one=one, col=col, vec=vec, shared=shared, state=state, inverse=inverse, tok=tok, par=par)


def _scan_params():
    return pltpu.CompilerParams(dimension_semantics=("arbitrary", "arbitrary"), vmem_limit_bytes=VMEM_LIMIT)


def _state_scratch(hp):
    return [pltpu.VMEM((hp, HEAD_DIM, HEAD_DIM), f32)]


def _lanes(i):
    return pl.ds(i * HEAD_DIM, HEAD_DIM)


def _stack_heads(per_head):
    return tuple(jnp.stack(vals) for vals in zip(*per_head))


def _hgrn2_fwd(h4, lb, nw, carry=None):
    _, s, d = h4.shape
    h, n, ncb, nblk, t, hp = _scan_dims(s, d)
    sp = _scan_specs(t, nblk, ncb, hp, False)
    grid = (h // hp, nblk)
    ni, no = (len(carry.arrays), len(carry.out_shapes)) if carry else (0, 0)

    def body(*refs):
        h_ref, lb_ref, nw_ref = refs[:3]
        ins = refs[3:3 + ni]
        y_ref, st_ref = refs[3 + ni:5 + ni]
        outs = refs[5 + ni:5 + ni + no]
        st_scr = refs[5 + ni + no]
        sems = refs[6 + ni + no:]
        _carry_at(carry, grid, "first", ins, outs, sems)

        @pl.when(pl.program_id(1) == 0)
        def _():
            st_scr[...] = jnp.zeros_like(st_scr)

        def step(c, carry):
            r = _rows(c)
            args = [(h_ref[0, r, _lanes(i)], h_ref[1, r, _lanes(i)], h_ref[2, r, _lanes(i)], h_ref[3, r, _lanes(i)],
                     lb_ref[:, _lanes(i)], nw_ref[...], st_scr[i]) for i in range(hp)]
            y, st_new = jax.vmap(_hgrn2_chunk)(*_stack_heads(args))
            for i, a in enumerate(args):
                st_ref[i, c] = a[-1]
                y_ref[r, _lanes(i)] = y[i].astype(y_ref.dtype)
                st_scr[i] = st_new[i]
            return carry

        lax.fori_loop(0, ncb, step, 0)
        _carry_at(carry, grid, "last", ins, outs, sems)

    extra = _carry_layout(carry, 3, 2)
    res = pl.pallas_call(
        body, grid=grid, name="hgrn2_fwd",
        in_specs=[sp["grp"](4), sp["vec"](), sp["shared"]()] + extra.in_specs,
        out_specs=[sp["col"](), sp["state"]()] + extra.out_specs,
        out_shape=[jax.ShapeDtypeStruct((s, d), bf16), jax.ShapeDtypeStruct((h, n, HEAD_DIM, HEAD_DIM), f32)]
        + extra.out_shapes,
        input_output_aliases=extra.aliases,
        scratch_shapes=_state_scratch(hp) + extra.sems, compiler_params=_scan_params())(
            h4, lb, nw, *(carry.arrays if carry else []))
    return (*res[:2], res[2:]) if carry else res


def _hgrn2_bwd(h4, lb, nw, states, dy, carry=None):
    _, s, d = h4.shape
    h, n, ncb, nblk, t, hp = _scan_dims(s, d)
    sp = _scan_specs(t, nblk, ncb, hp, True)
    grid = (h // hp, nblk)
    ni, no = (len(carry.arrays), len(carry.out_shapes)) if carry else (0, 0)

    def body(*refs):
        h_ref, lb_ref, nw_ref, st_ref, dy_ref = refs[:5]
        ins = refs[5:5 + ni]
        dh_ref, dlb_ref, dnw_ref = refs[5 + ni:8 + ni]
        outs = refs[8 + ni:8 + ni + no]
        dst_scr = refs[8 + ni + no]
        sems = refs[9 + ni + no:]
        first_block = pl.program_id(1) == 0
        _carry_at(carry, grid, "first", ins, outs, sems)

        @pl.when(first_block)
        def _():
            dst_scr[...] = jnp.zeros_like(dst_scr)
            dlb_ref[...] = jnp.zeros_like(dlb_ref)

        @pl.when(first_block & (pl.program_id(0) == 0))
        def _():
            dnw_ref[...] = jnp.zeros_like(dnw_ref)

        def step(k, carry):
            c = ncb - 1 - k
            r = _rows(c)
            args = [(h_ref[0, r, _lanes(i)], h_ref[1, r, _lanes(i)], h_ref[2, r, _lanes(i)], h_ref[3, r, _lanes(i)],
                     lb_ref[:, _lanes(i)], nw_ref[...], st_ref[i, c]) for i in range(hp)]
            cots = [(dy_ref[r, _lanes(i)].astype(f32), dst_scr[i]) for i in range(hp)]
            dq, df, di, dz, dlb, dnw, dst = jax.vjp(jax.vmap(_hgrn2_chunk), *_stack_heads(args))[1](_stack_heads(cots))
            dnw_sum = dnw_ref[...]
            for i in range(hp):
                for g, val in enumerate((dq, df, di, dz)):
                    dh_ref[g, r, _lanes(i)] = val[i].astype(dh_ref.dtype)
                dlb_ref[:, _lanes(i)] += dlb[i]
                dnw_sum = dnw_sum + dnw[i]
                dst_scr[i] = dst[i]
            dnw_ref[...] = dnw_sum
            return carry

        lax.fori_loop(0, ncb, step, 0)
        _carry_at(carry, grid, "last", ins, outs, sems)

    extra = _carry_layout(carry, 5, 3)
    res = pl.pallas_call(
        body, grid=grid, name="hgrn2_bwd",
        in_specs=[sp["grp"](4), sp["vec"](), sp["shared"](), sp["state"](), sp["col"]()] + extra.in_specs,
        out_specs=[sp["grp"](4), sp["vec"](), sp["shared"]()] + extra.out_specs,
        out_shape=[jax.ShapeDtypeStruct((4, s, d), bf16), jax.ShapeDtypeStruct((1, d), f32),
                   jax.ShapeDtypeStruct((1, HEAD_DIM), f32)] + extra.out_shapes,
        input_output_aliases=extra.aliases,
        scratch_shapes=_state_scratch(hp) + extra.sems, compiler_params=_scan_params())(
            h4, lb, nw, states, dy, *(carry.arrays if carry else []))
    return (*res[:3], res[3:]) if carry else res


def _gdn_fwd(c3, h4, bl, al, alog, dtb, nw):
    _, s, d = c3.shape
    h, n, ncb, nblk, t, hp = _scan_dims(s, d)
    sp = _scan_specs(t, nblk, ncb, hp, False)

    def body(c_ref, hz_ref, bl_ref, al_ref, alog_ref, dtb_ref, nw_ref, y_ref, st_ref, inv_ref, st_scr):
        @pl.when(pl.program_id(1) == 0)
        def _():
            st_scr[...] = jnp.zeros_like(st_scr)

        def step(c, carry):
            r = _rows(c)
            args = [(c_ref[0, r, _lanes(i)], c_ref[1, r, _lanes(i)], c_ref[2, r, _lanes(i)], hz_ref[r, _lanes(i)],
                     bl_ref[i, r, :], al_ref[i, r, :], alog_ref[i], dtb_ref[i], nw_ref[...], st_scr[i]) for i in range(hp)]
            y, st_new, inv = jax.vmap(functools.partial(_gdn_chunk, with_inverse=True))(*_stack_heads(args))
            for i, a in enumerate(args):
                st_ref[i, c] = a[-1]
                inv_ref[i, c] = inv[i]
                y_ref[r, _lanes(i)] = y[i].astype(y_ref.dtype)
                st_scr[i] = st_new[i]
            return carry

        lax.fori_loop(0, ncb, step, 0)

    return pl.pallas_call(
        body, grid=(h // hp, nblk), name="gdn_fwd",
        in_specs=[sp["grp"](3), sp["one"](3), sp["tok"](), sp["tok"](), sp["par"](), sp["par"](), sp["shared"]()],
        out_specs=[sp["col"](), sp["state"](), sp["inverse"]()],
        out_shape=[jax.ShapeDtypeStruct((s, d), bf16), jax.ShapeDtypeStruct((h, n, HEAD_DIM, HEAD_DIM), f32),
                   jax.ShapeDtypeStruct((h, n, CHUNK, CHUNK), f32)],
        scratch_shapes=_state_scratch(hp), compiler_params=_scan_params())(c3, h4, bl, al, alog, dtb, nw)


def _gdn_bwd(c3, h4, bl, al, alog, dtb, nw, states, inverses, dy):
    _, s, d = c3.shape
    h, n, ncb, nblk, t, hp = _scan_dims(s, d)
    sp = _scan_specs(t, nblk, ncb, hp, True)

    def chunk(*a):
        return _gdn_chunk(*a[:-1], inv=a[-1])

    def body(c_ref, hz_ref, bl_ref, al_ref, alog_ref, dtb_ref, nw_ref, st_ref, inv_ref, dy_ref,
             dc_ref, dz_ref, dbl_ref, dal_ref, dalog_ref, ddtb_ref, dnw_ref, dst_scr):
        first_block = pl.program_id(1) == 0

        @pl.when(first_block)
        def _():
            dst_scr[...] = jnp.zeros_like(dst_scr)
            dalog_ref[...] = jnp.zeros_like(dalog_ref)
            ddtb_ref[...] = jnp.zeros_like(ddtb_ref)

        @pl.when(first_block & (pl.program_id(0) == 0))
        def _():
            dnw_ref[...] = jnp.zeros_like(dnw_ref)

        def step(k, carry):
            c = ncb - 1 - k
            r = _rows(c)
            args = [(c_ref[0, r, _lanes(i)], c_ref[1, r, _lanes(i)], c_ref[2, r, _lanes(i)], hz_ref[r, _lanes(i)],
                     bl_ref[i, r, :], al_ref[i, r, :], alog_ref[i], dtb_ref[i], nw_ref[...], st_ref[i, c], inv_ref[i, c])
                    for i in range(hp)]
            cots = [(dy_ref[r, _lanes(i)].astype(f32), dst_scr[i]) for i in range(hp)]
            dq, dk, dv, dz, dbl, dal, dalog, ddtb, dnw, dst, _ = jax.vjp(
                jax.vmap(chunk), *_stack_heads(args))[1](_stack_heads(cots))
            dnw_sum = dnw_ref[...]
            for i in range(hp):
                for g, val in enumerate((dq, dk, dv)):
                    dc_ref[g, r, _lanes(i)] = val[i]
                dz_ref[r, _lanes(i)] = dz[i].astype(dz_ref.dtype)
                dbl_ref[i, r, :] = dbl[i]
                dal_ref[i, r, :] = dal[i]
                dalog_ref[i] += dalog[i]
                ddtb_ref[i] += ddtb[i]
                dnw_sum = dnw_sum + dnw[i]
                dst_scr[i] = dst[i]
            dnw_ref[...] = dnw_sum
            return carry

        lax.fori_loop(0, ncb, step, 0)

    tokens = jax.ShapeDtypeStruct((h, s, 1), f32)
    heads = jax.ShapeDtypeStruct((h, 1, 1), f32)
    return pl.pallas_call(
        body, grid=(h // hp, nblk), name="gdn_bwd",
        in_specs=[sp["grp"](3), sp["one"](3), sp["tok"](), sp["tok"](), sp["par"](), sp["par"](), sp["shared"](),
                  sp["state"](), sp["inverse"](), sp["col"]()],
        out_specs=[sp["grp"](3), sp["one"](3), sp["tok"](), sp["tok"](), sp["par"](), sp["par"](), sp["shared"]()],
        out_shape=[jax.ShapeDtypeStruct((3, s, d), f32), jax.ShapeDtypeStruct((4, s, d), bf16), tokens, tokens, heads, heads,
                   jax.ShapeDtypeStruct((1, HEAD_DIM), f32)],
        scratch_shapes=_state_scratch(hp), compiler_params=_scan_params())(
            c3, h4, bl, al, alog, dtb, nw, states, inverses, dy)


def _tile(n, pref):
    return n if n <= pref else pref


def _mm(a, b, mode, *, name, out_dtype=f32, add=None, add_scale=1.0, b_g0=0, groups=None, carry=None, a_half=None,
        n_half=None, into=None):
    squeeze = a.ndim == 2 and b.ndim == 2
    a3 = a if a.ndim == 3 else a[None]
    b3 = b if b.ndim == 3 else b[None]
    if mode == "nt":
        g = groups or a3.shape[0]
        (_, m, r), (_, n, _) = a3.shape, b3.shape
    elif mode == "nn":
        g = groups or b3.shape[0]
        (_, m, r), (_, _, n) = a3.shape, b3.shape
        if n_half is not None:
            n = n // 2
    else:
        g = groups or b3.shape[0]
        (_, r, m), (_, _, n) = a3.shape, b3.shape
        if a_half is not None:
            m = m // 2
    tm, tn, tr = _tile(m, 1024), _tile(n, 1024), _tile(r, 2048)
    nr = r // tr
    a_grouped = a3.shape[0] > 1
    dn = {"nn": NN, "nt": NT, "tn": TN}[mode]

    if mode == "nt":
        grid = (m // tm, n // tn, g * nr)
        last = g * nr - 1
        kax = 2
        a_spec = pl.BlockSpec((None, tm, tr), lambda i, j, k: (k // nr, i, k % nr))
        b_spec = pl.BlockSpec((None, tn, tr), lambda i, j, k: (b_g0 + k // nr, j, k % nr))
        o_spec = pl.BlockSpec((tm, tn), lambda i, j, k: (i, j))
        add_spec = pl.BlockSpec((tm, tn), lambda i, j, k: (i, j))
        out_shape = jax.ShapeDtypeStruct((m, n), out_dtype)
        sem = ("parallel", "parallel", "arbitrary")
    else:
        grid = (g, m // tm, n // tn, nr)
        last = nr - 1
        kax = 3
        ag = (lambda q: q) if a_grouped else (lambda q: 0)
        if mode == "nn":
            a_spec = pl.BlockSpec((None, tm, tr), lambda q, i, j, k: (ag(q), i, k))
        else:
            def col0():
                if a_half is None:
                    return 0
                return (_my_core() if a_half == "mine" else 1 - _my_core()) * (m // tm)
            a_spec = pl.BlockSpec((None, tr, tm), lambda q, i, j, k: (ag(q), k, col0() + i))
        off = 0 if n_half is None else n_half * (n // tn)
        b_spec = pl.BlockSpec((None, tr, tn), lambda q, i, j, k: (b_g0 + q, k, off + j))
        o_spec = pl.BlockSpec((None, tm, tn), lambda q, i, j, k: (q, i, off + j))
        add_spec = ANY
        out_shape = jax.ShapeDtypeStruct((g, m, n if n_half is None else 2 * n), out_dtype)
        sem = ("parallel", "parallel", "parallel", "arbitrary")
        assert add is None
        add = into

    n_main = 2 if add is None else 3
    ni = len(carry.arrays) if carry else 0
    no = len(carry.out_shapes) if carry else 0

    def body(*refs):
        a_ref, b_ref = refs[:2]
        add_ref = None if add is None else refs[2]
        ins = refs[n_main:n_main + ni]
        o_ref = refs[n_main + ni]
        outs = refs[n_main + ni + 1:n_main + ni + 1 + no]
        acc = refs[n_main + ni + 1 + no]
        sems = refs[n_main + ni + 2 + no:]
        k = pl.program_id(kax)
        _carry_at(carry, grid, "first", ins, outs, sems)

        @pl.when(k == 0)
        def _():
            acc[...] = jnp.zeros_like(acc)

        acc[...] += _bdot(a_ref[...], b_ref[...], dn)

        @pl.when(k == last)
        def _():
            res = acc[...]
            if add is not None and mode == "nt":
                res = res + add_scale * add_ref[...]
            o_ref[...] = res.astype(o_ref.dtype)

        _carry_at(carry, grid, "last", ins, outs, sems)

    operands = [a3, b3] + ([] if add is None else [add])
    in_specs = [a_spec, b_spec] + ([] if add is None else [add_spec])
    extra = _carry_layout(carry, n_main, 1)
    if carry:
        sem = ("arbitrary",) * len(grid)
    out, *moved = pl.pallas_call(
        body, grid=grid, name=name, in_specs=in_specs + extra.in_specs, out_specs=[o_spec] + extra.out_specs,
        out_shape=[out_shape] + extra.out_shapes,
        input_output_aliases=extra.aliases | ({2: 0} if into is not None else {}),
        scratch_shapes=[pltpu.VMEM((tm, tn), f32)] + extra.sems,
        compiler_params=pltpu.CompilerParams(dimension_semantics=sem, vmem_limit_bytes=VMEM_LIMIT))(
            *operands, *(carry.arrays if carry else []))
    out = out[0] if (squeeze and mode != "nt") else out
    return (out, moved) if carry else out


def _row_block(rows, cols, target_bytes=1 << 20):
    if rows <= SUBLANES:
        return rows
    tr = SUBLANES
    while tr * 2 <= rows and tr * 2 * cols * 4 <= target_bytes and rows % (tr * 2) == 0:
        tr *= 2
    return tr


def _ew_params(n_axes=1):
    return pltpu.CompilerParams(dimension_semantics=("parallel",) * n_axes, vmem_limit_bytes=VMEM_LIMIT)


def _cast(x, dtype, name, into_chip_slot=False):
    r, c = x.shape
    tr = _row_block(r, c, 2 << 20)
    tr = max(tr, 16) if r >= 16 else tr
    spec = pl.BlockSpec((tr, c), lambda i: (i, 0))
    if into_chip_slot:
        out_spec = pl.BlockSpec((None, tr, c), lambda i: (_my_chip(), i, 0))
        out_shape = jax.ShapeDtypeStruct((N_CHIPS, r, c), dtype)
    else:
        out_spec, out_shape = spec, jax.ShapeDtypeStruct((r, c), dtype)

    def body(x_ref, o_ref):
        o_ref[...] = x_ref[...].astype(dtype)

    return pl.pallas_call(body, grid=(r // tr,), name=name, in_specs=[spec], out_specs=out_spec,
                          out_shape=out_shape, compiler_params=_ew_params())(x)


def _merge_tails(main, tail):
    _, d, _ = main.shape
    tr = max(_row_block(d, LANES), 16)
    head = pl.BlockSpec((None, tr, LANES), lambda g, i: (g + 1, i, 0))
    rest = pl.BlockSpec((None, tr, LANES), lambda g, i: (g, i, 0))

    def body(m_ref, t_ref, o_ref):
        o_ref[...] = m_ref[...] + t_ref[...]

    return pl.pallas_call(body, grid=(N_CHIPS - 1, d // tr), name="merge_shard_tails", in_specs=[head, rest],
                          out_specs=head, out_shape=jax.ShapeDtypeStruct(main.shape, main.dtype),
                          input_output_aliases={0: 0}, compiler_params=_ew_params(2))(main, tail)


def _ln_stats(r):
    mu = jnp.mean(r, axis=-1, keepdims=True)
    var = jnp.mean(jnp.square(r - mu), axis=-1, keepdims=True)
    return mu, lax.rsqrt(var + LN_EPS)


def _ln_fwd(x, yo, g, b):
    s, d = x.shape
    tr = _row_block(s, d, 2 << 20)
    tr = max(tr, 16)
    big = pl.BlockSpec((tr, d), lambda i: (i, 0))
    vec = pl.BlockSpec((1, d), lambda i: (0, 0))

    def body(x_ref, yo_ref, g_ref, b_ref, o_ref, ob_ref):
        r = DEEPNORM_ALPHA * x_ref[...] + yo_ref[...]
        mu, rstd = _ln_stats(r)
        y = (r - mu) * rstd * g_ref[...] + b_ref[...]
        o_ref[...] = y
        ob_ref[...] = y.astype(bf16)

    return pl.pallas_call(
        body, grid=(s // tr,), name="ln_fwd", in_specs=[big, big, vec, vec], out_specs=[big, big],
        out_shape=[jax.ShapeDtypeStruct((s, d), f32), jax.ShapeDtypeStruct((s, d), bf16)],
        compiler_params=_ew_params())(x, yo, g, b)


def _ln_bwd(x, yo, g, *, dy=None, b=None, target=None, name):
    s, d = x.shape
    with_loss = target is not None
    tr = max(_row_block(s, d, 2 << 20), 16)
    big = pl.BlockSpec((tr, d), lambda i: (i, 0))
    vec = pl.BlockSpec((1, d), lambda i: (0, 0))
    lane = pl.BlockSpec((1, LANES), lambda i: (0, 0))

    def body(*refs):
        if with_loss:
            x_ref, yo_ref, g_ref, b_ref, t_ref, dr_ref, drb_ref, dg_ref, db_ref, loss_ref = refs
        else:
            x_ref, yo_ref, g_ref, dy_ref, dr_ref, drb_ref, dg_ref, db_ref = refs

        @pl.when(pl.program_id(0) == 0)
        def _():
            dg_ref[...] = jnp.zeros_like(dg_ref)
            db_ref[...] = jnp.zeros_like(db_ref)
            if with_loss:
                loss_ref[...] = jnp.zeros_like(loss_ref)

        r = DEEPNORM_ALPHA * x_ref[...] + yo_ref[...]
        mu, rstd = _ln_stats(r)
        xhat = (r - mu) * rstd
        if with_loss:
            err = xhat * g_ref[...] + b_ref[...] - t_ref[...]
            loss_ref[...] += jnp.sum(err * err) * (0.5 / d)
            dyv = err * (1.0 / d)
        else:
            dyv = dy_ref[...]
        dxhat = dyv * g_ref[...]
        m1 = jnp.mean(dxhat, axis=-1, keepdims=True)
        m2 = jnp.mean(dxhat * xhat, axis=-1, keepdims=True)
        dr = rstd * (dxhat - m1 - xhat * m2)
        dr_ref[...] = dr
        drb_ref[...] = dr.astype(bf16)
        dg_ref[...] += jnp.sum(dyv * xhat, axis=0, keepdims=True)
        db_ref[...] += jnp.sum(dyv, axis=0, keepdims=True)

    operands = (x, yo, g, b, target) if with_loss else (x, yo, g, dy)
    in_specs = [big, big, vec, vec, big] if with_loss else [big, big, vec, big]
    out_specs = [big, big, vec, vec] + ([lane] if with_loss else [])
    out_shape = [jax.ShapeDtypeStruct((s, d), f32), jax.ShapeDtypeStruct((s, d), bf16),
                 jax.ShapeDtypeStruct((1, d), f32), jax.ShapeDtypeStruct((1, d), f32)]
    if with_loss:
        out_shape.append(jax.ShapeDtypeStruct((1, LANES), f32))
    return pl.pallas_call(
        body, grid=(s // tr,), name=name, in_specs=in_specs, out_specs=out_specs, out_shape=out_shape,
        compiler_params=pltpu.CompilerParams(dimension_semantics=("arbitrary",), vmem_limit_bytes=VMEM_LIMIT))(*operands)


def _lower_bound(a_lb):
    _, d = a_lb.shape

    def body(a_ref, o_ref):
        a = a_ref[...]
        ex = jnp.exp(a - jnp.max(a, axis=0, keepdims=True))
        p = ex / jnp.sum(ex, axis=0, keepdims=True)
        o_ref[...] = jnp.sum(jnp.where(_iota(p.shape, 0) == 0, p, 0.0), axis=0, keepdims=True)

    return pl.pallas_call(body, name="lower_bound", in_specs=[VMEM], out_specs=VMEM,
                          out_shape=jax.ShapeDtypeStruct((1, d), f32))(a_lb)


def _shift_rows(ext, k, rows):
    return pltpu.roll(ext, k, axis=0)[SUBLANES:SUBLANES + rows]


def _conv_tiles(s, d):
    return _tile(s, 512), _tile(d, 512)


def _conv_fwd(h4, w):
    _, s, d = h4.shape
    tr, tc = _conv_tiles(s, d)
    nj = d // tc
    hb = tr // SUBLANES
    cur = pl.BlockSpec((None, tr, tc), lambda g, j, i: (g, i, j))
    prev = pl.BlockSpec((None, SUBLANES, tc), lambda g, j, i: (g, jnp.maximum(i * hb - 1, 0), j))
    wsp = pl.BlockSpec((CONV_WIDTH, tc), lambda g, j, i: (0, g * nj + j))

    def body(x_ref, p_ref, w_ref, o_ref):
        x = x_ref[...]
        halo = jnp.where(pl.program_id(2) == 0, 0.0, p_ref[...])
        ext = jnp.concatenate([halo, x], axis=0)
        acc = w_ref[CONV_WIDTH - 1:CONV_WIDTH, :] * x
        for k in range(1, CONV_WIDTH):
            acc = acc + w_ref[CONV_WIDTH - 1 - k:CONV_WIDTH - k, :] * _shift_rows(ext, k, tr)
        o_ref[...] = acc

    return pl.pallas_call(
        body, grid=(3, nj, s // tr), name="conv_fwd", in_specs=[cur, prev, wsp], out_specs=cur,
        out_shape=jax.ShapeDtypeStruct((3, s, d), f32), compiler_params=_ew_params(3))(h4, h4, w)


def _conv_bwd(dy3, h4, w, dh4):
    _, s, d = dy3.shape
    tr, tc = _conv_tiles(s, d)
    nj = d // tc
    hb = tr // SUBLANES
    ni = s // tr
    cur = pl.BlockSpec((None, tr, tc), lambda g, j, i: (g, i, j))
    prev = pl.BlockSpec((None, SUBLANES, tc), lambda g, j, i: (g, jnp.maximum(i * hb - 1, 0), j))
    nxt = pl.BlockSpec((None, SUBLANES, tc), lambda g, j, i: (g, jnp.minimum((i + 1) * hb, s // SUBLANES - 1), j))
    wsp = pl.BlockSpec((CONV_WIDTH, tc), lambda g, j, i: (0, g * nj + j))

    def body(dy_ref, dn_ref, x_ref, p_ref, w_ref, _, dx_ref, dw_ref):
        i = pl.program_id(2)

        @pl.when(i == 0)
        def _():
            dw_ref[...] = jnp.zeros_like(dw_ref)

        dy = dy_ref[...]
        x = x_ref[...]
        tail = jnp.where(i == ni - 1, 0.0, dn_ref[...])
        dext = jnp.concatenate([dy, tail], axis=0)
        halo = jnp.where(i == 0, 0.0, p_ref[...])
        xext = jnp.concatenate([halo, x], axis=0)
        acc = w_ref[CONV_WIDTH - 1:CONV_WIDTH, :] * dy
        dw_ref[CONV_WIDTH - 1:CONV_WIDTH, :] += jnp.sum(dy * x, axis=0, keepdims=True)
        for k in range(1, CONV_WIDTH):
            ahead = pltpu.roll(dext, tr + SUBLANES - k, axis=0)[:tr]
            acc = acc + w_ref[CONV_WIDTH - 1 - k:CONV_WIDTH - k, :] * ahead
            dw_ref[CONV_WIDTH - 1 - k:CONV_WIDTH - k, :] += jnp.sum(dy * _shift_rows(xext, k, tr), axis=0, keepdims=True)
        dx_ref[...] = acc.astype(dx_ref.dtype)

    return pl.pallas_call(
        body, grid=(3, nj, ni), name="conv_bwd", in_specs=[cur, nxt, cur, prev, wsp, ANY], out_specs=[cur, wsp],
        out_shape=[jax.ShapeDtypeStruct((4, s, d), bf16), jax.ShapeDtypeStruct((CONV_WIDTH, 3 * d), f32)],
        input_output_aliases={5: 0},
        compiler_params=pltpu.CompilerParams(dimension_semantics=("parallel", "parallel", "arbitrary"),
                                             vmem_limit_bytes=VMEM_LIMIT))(dy3, dy3, h4, h4, w, dh4)


def _my_core():
    return lax.axis_index("c")


def _my_chip():
    return 2 * lax.axis_index("x") + lax.axis_index("y")


def _chip_sum(halves, other, name):
    _, r2, c = other.shape
    tr = max(_row_block(r2, c), 16)
    own = pl.BlockSpec((None, tr, c), lambda p, i: (2 * p + _my_core(), i, 0))
    blk = pl.BlockSpec((None, tr, c), lambda p, i: (p, i, 0))

    def body(a_ref, b_ref, o_ref):
        o_ref[...] = (a_ref[...] + b_ref[...]).astype(o_ref.dtype)

    return pl.pallas_call(
        body, name=name, grid=(N_CHIPS, r2 // tr), in_specs=[own, blk], out_specs=blk,
        out_shape=jax.ShapeDtypeStruct(other.shape, bf16), compiler_params=_ew_params(2))(halves, other)


def _final_sum(halves, other, remote, name):
    _, r2, c = remote.shape
    tr = max(_row_block(r2, c), 16)
    own = pl.BlockSpec((None, tr, c), lambda i: (2 * _my_chip() + _my_core(), i, 0))
    sib = pl.BlockSpec((None, tr, c), lambda i: (_my_chip(), i, 0))
    slot = lambda j: pl.BlockSpec((None, tr, c), lambda i: (j, i, 0))

    def body(a_ref, b_ref, r0_ref, r1_ref, r2_ref, o_ref):
        acc = a_ref[...] + b_ref[...]
        for ref in (r0_ref, r1_ref, r2_ref):
            acc = acc + ref[...].astype(f32)
        o_ref[...] = acc

    return pl.pallas_call(
        body, name=name, grid=(r2 // tr,), in_specs=[own, sib, slot(0), slot(1), slot(2)],
        out_specs=pl.BlockSpec((None, tr, c), lambda i: (_my_core(), i, 0)),
        out_shape=jax.ShapeDtypeStruct((2, r2, c), f32), compiler_params=_ew_params())(halves, other, remote, remote, remote)


def _adamw(w, g, m, v, name):
    r, c = w.shape
    tr = _row_block(r, c)
    spec = pl.BlockSpec((tr, c), lambda i: (i, 0))

    def body(w_ref, g_ref, m_ref, v_ref, go_ref, d_ref, nm_ref, nv_ref):
        gv = g_ref[...]
        go_ref[...] = gv
        nm = ADAM_B1 * m_ref[...] + (1.0 - ADAM_B1) * gv
        nv = ADAM_B2 * v_ref[...] + (1.0 - ADAM_B2) * jnp.square(gv)
        m_hat = nm / (1.0 - ADAM_B1 ** ADAM_STEP)
        v_hat = nv / (1.0 - ADAM_B2 ** ADAM_STEP)
        d_ref[...] = -ADAM_LR * (m_hat / (jnp.sqrt(v_hat) + ADAM_EPS) + ADAM_WD * w_ref[...])
        nm_ref[...] = nm
        nv_ref[...] = nv

    out = jax.ShapeDtypeStruct((r, c), f32)
    return pl.pallas_call(body, grid=(r // tr,), name=name, in_specs=[spec] * 4, out_specs=[spec] * 4,
                          out_shape=[out, out, out, out], compiler_params=_ew_params())(w, g, m, v)


class _Comm(NamedTuple):
    arrays: list
    aliases: dict
    out_shapes: list
    sems: list
    start: Callable
    finish: Callable


class _CarryLayout(NamedTuple):
    in_specs: list
    out_specs: list
    out_shapes: list
    aliases: dict
    sems: list


def _carry_layout(comm, n_in, n_out):
    if comm is None:
        return _CarryLayout([], [], [], {}, [])
    return _CarryLayout([ANY] * len(comm.arrays), [ANY] * len(comm.out_shapes), list(comm.out_shapes),
                        {n_in + i: n_out + o for i, o in comm.aliases.items()}, list(comm.sems))


def _carry_at(comm, grid, edge, ins, outs, sems):
    if comm is None:
        return
    hit = [pl.program_id(ax) == (0 if edge == "first" else extent - 1) for ax, extent in enumerate(grid)]

    @pl.when(functools.reduce(jnp.logical_and, hit))
    def _():
        (comm.start if edge == "first" else comm.finish)(ins, outs, sems)


def _compose(first, second):
    na, no, ns = len(first.arrays), len(first.out_shapes), len(first.sems)

    def both(which):
        def run(ins, outs, sems):
            getattr(first, which)(ins[:na], outs[:no], sems[:ns])
            getattr(second, which)(ins[na:], outs[no:], sems[ns:])
        return run

    aliases = dict(first.aliases) | {na + i: no + o for i, o in second.aliases.items()}
    return _Comm(first.arrays + second.arrays, aliases, first.out_shapes + second.out_shapes, first.sems + second.sems,
                 both("start"), both("finish"))


def _run_comm(comm, name):
    ni, no = len(comm.arrays), len(comm.out_shapes)

    def body(*refs):
        ins, outs, sems = refs[:ni], refs[ni:ni + no], refs[ni + no:]
        comm.start(ins, outs, sems)
        comm.finish(ins, outs, sems)

    return pl.pallas_call(body, name=name, in_specs=[ANY] * ni, out_specs=[ANY] * no, out_shape=comm.out_shapes,
                          input_output_aliases=comm.aliases, scratch_shapes=comm.sems)(*comm.arrays)


def _place():
    return lax.axis_index("x"), lax.axis_index("y"), lax.axis_index("c")


def _chip_peers(x, y):
    out = []
    for fx, fy in ((1, 0), (0, 1), (1, 1)):
        px, py = x ^ fx, y ^ fy
        out.append((px, py, 2 * px + py))
    return out


def _all_gather(arrays, split, cols=None):
    na = len(arrays)

    def part(ref, a, block, which):
        width = slice(None) if cols is None or cols[a] is None else pl.ds(*cols[a])
        if not split[a]:
            return ref.at[block, :, width]
        rows = arrays[a].shape[1] // 2
        return ref.at[block, pl.ds(which * rows, rows), width]

    def ici_sends(outs, sems):
        send_sems, recv_sems = sems
        x, y, c = _place()
        me = 2 * x + y
        out = []
        for a in range(na):
            for j, (px, py, _) in enumerate(_chip_peers(x, y)):
                mine = part(outs[a], a, me, c)
                out.append(pltpu.make_async_remote_copy(mine, mine, send_sems.at[a, j], recv_sems.at[a, j],
                                                        device_id=(px, py, c), device_id_type=MESH))
        return out

    def forwards(outs, sems, which_core):
        send_sems, recv_sems = sems
        x, y, c = _place()
        out = []
        for a in range(na):
            if split[a]:
                for j, (_, _, pid) in enumerate(_chip_peers(x, y)):
                    got = part(outs[a], a, pid, which_core)
                    out.append(pltpu.make_async_remote_copy(got, got, send_sems.at[a, 3 + j], recv_sems.at[a, 3 + j],
                                                            device_id=(x, y, 1 - c), device_id_type=MESH))
        return out

    def arrivals(outs, sems):
        send_sems, recv_sems = sems
        x, y, c = _place()
        out = []
        for a in range(na):
            for j, (px, py, pid) in enumerate(_chip_peers(x, y)):
                got = part(outs[a], a, pid, c)
                out.append(pltpu.make_async_remote_copy(got, got, send_sems.at[a, j], recv_sems.at[a, j],
                                                        device_id=(px, py, c), device_id_type=MESH))
        return out

    def finish(ins, outs, sems):
        c = lax.axis_index("c")
        for cp in arrivals(outs, sems):
            cp.wait_recv()
        mine = forwards(outs, sems, c)
        for cp in mine:
            cp.start()
        for cp in forwards(outs, sems, 1 - c):
            cp.wait_recv()
        for cp in ici_sends(outs, sems) + mine:
            cp.wait_send()

    return _Comm(
        arrays=list(arrays), aliases={a: a for a in range(na)},
        out_shapes=[jax.ShapeDtypeStruct(a.shape, a.dtype) for a in arrays],
        sems=[pltpu.SemaphoreType.DMA((na, 6)), pltpu.SemaphoreType.DMA((na, 6))],
        start=lambda ins, outs, sems: [cp.start() for cp in ici_sends(outs, sems)], finish=finish)


def _swap_halves(arrays):
    na = len(arrays)

    def copies(ins, outs, sems):
        send_sems, recv_sems = sems
        x, y, c = _place()
        out = []
        for a in range(na):
            rows = arrays[a].shape[1] // 2
            src = ins[a].at[:, pl.ds((1 - c) * rows, rows), :]
            out.append(pltpu.make_async_remote_copy(src, outs[a], send_sems.at[a], recv_sems.at[a],
                                                    device_id=(x, y, 1 - c), device_id_type=MESH))
        return out

    return _Comm(
        arrays=list(arrays), aliases={},
        out_shapes=[jax.ShapeDtypeStruct((a.shape[0], a.shape[1] // 2, a.shape[2]), a.dtype) for a in arrays],
        sems=[pltpu.SemaphoreType.DMA((na,)), pltpu.SemaphoreType.DMA((na,))],
        start=lambda ins, outs, sems: [cp.start() for cp in copies(ins, outs, sems)],
        finish=lambda ins, outs, sems: [cp.wait() for cp in copies(ins, outs, sems)])


def _send_to_sibling(arrays):
    na = len(arrays)

    def copies(ins, outs, sems):
        send_sems, recv_sems = sems
        x, y, c = _place()
        return [pltpu.make_async_remote_copy(ins[a], outs[a], send_sems.at[a], recv_sems.at[a],
                                             device_id=(x, y, 1 - c), device_id_type=MESH) for a in range(na)]

    return _Comm(
        arrays=list(arrays), aliases={}, out_shapes=[jax.ShapeDtypeStruct(a.shape, a.dtype) for a in arrays],
        sems=[pltpu.SemaphoreType.DMA((na,)), pltpu.SemaphoreType.DMA((na,))],
        start=lambda ins, outs, sems: [cp.start() for cp in copies(ins, outs, sems)],
        finish=lambda ins, outs, sems: [cp.wait() for cp in copies(ins, outs, sems)])


def _pair_sum(keep, other, name):
    _, r2, c = keep.shape
    tr = max(_row_block(r2, c), 16)
    blk = pl.BlockSpec((None, tr, c), lambda p, i: (p, i, 0))

    def body(a_ref, b_ref, o_ref):
        o_ref[...] = (a_ref[...] + b_ref[...]).astype(o_ref.dtype)

    return pl.pallas_call(body, name=name, grid=(N_CHIPS, r2 // tr), in_specs=[blk, blk], out_specs=blk,
                          out_shape=jax.ShapeDtypeStruct(keep.shape, bf16), compiler_params=_ew_params(2))(keep, other)


def _final_sum_pair(keep, other, remote, name):
    _, r2, c = remote.shape
    tr = max(_row_block(r2, c), 16)
    own = pl.BlockSpec((None, tr, c), lambda i: (_my_chip(), i, 0))
    slot = lambda j: pl.BlockSpec((None, tr, c), lambda i: (j, i, 0))

    def body(a_ref, b_ref, r0_ref, r1_ref, r2_ref, o_ref):
        acc = a_ref[...] + b_ref[...]
        for ref in (r0_ref, r1_ref, r2_ref):
            acc = acc + ref[...].astype(f32)
        o_ref[...] = acc

    return pl.pallas_call(
        body, name=name, grid=(r2 // tr,), in_specs=[own, own, slot(0), slot(1), slot(2)],
        out_specs=pl.BlockSpec((None, tr, c), lambda i: (_my_core(), i, 0)),
        out_shape=jax.ShapeDtypeStruct((2, r2, c), f32), compiler_params=_ew_params())(keep, other, remote, remote, remote)


def _scatter_blocks(arrays):
    na = len(arrays)

    def copies(ins, outs, sems):
        send_sems, recv_sems = sems
        x, y, c = _place()
        out = []
        for a in range(na):
            for j, (px, py, pid) in enumerate(_chip_peers(x, y)):
                out.append(pltpu.make_async_remote_copy(ins[a].at[pid], outs[a].at[j], send_sems.at[a, j],
                                                        recv_sems.at[a, j], device_id=(px, py, c), device_id_type=MESH))
        return out

    return _Comm(
        arrays=list(arrays), aliases={},
        out_shapes=[jax.ShapeDtypeStruct((3,) + a.shape[1:], a.dtype) for a in arrays],
        sems=[pltpu.SemaphoreType.DMA((na, 3)), pltpu.SemaphoreType.DMA((na, 3))],
        start=lambda ins, outs, sems: [cp.start() for cp in copies(ins, outs, sems)],
        finish=lambda ins, outs, sems: [cp.wait() for cp in copies(ins, outs, sems)])


def _join_halves(arrays):
    na = len(arrays)

    def copies(outs, sems, slot):
        send_sems, recv_sems = sems
        x, y, c = _place()
        which = c if slot == "mine" else 1 - c
        return [pltpu.make_async_remote_copy(outs[a].at[which], outs[a].at[which], send_sems.at[a], recv_sems.at[a],
                                             device_id=(x, y, 1 - c), device_id_type=MESH) for a in range(na)]

    def finish(ins, outs, sems):
        for cp in copies(outs, sems, "mine"):
            cp.wait_send()
        for cp in copies(outs, sems, "sibling's"):
            cp.wait_recv()

    return _Comm(
        arrays=list(arrays), aliases={a: a for a in range(na)},
        out_shapes=[jax.ShapeDtypeStruct(a.shape, a.dtype) for a in arrays],
        sems=[pltpu.SemaphoreType.DMA((na,)), pltpu.SemaphoreType.DMA((na,))],
        start=lambda ins, outs, sems: [cp.start() for cp in copies(outs, sems, "mine")], finish=finish)


SMALL_ROWS = 24


def _all_reduce_small(a_lb, dlb, pieces):
    d = a_lb.shape[1]
    npc = len(pieces)

    def body(*refs):
        alb_ref, dlb_ref = refs[:2]
        piece_refs = refs[2:2 + npc]
        out_ref, pack, slots, send_sems, recv_sems = refs[2 + npc:]
        x, y, c = _place()
        me = 4 * x + 2 * y + c
        pack[...] = jnp.zeros_like(pack)
        a = alb_ref[...]
        ex = jnp.exp(a - jnp.max(a, axis=0, keepdims=True))
        p = ex / jnp.sum(ex, axis=0, keepdims=True)
        first = (_iota(p.shape, 0) == 0).astype(f32)
        p0 = jnp.sum(p * first, axis=0, keepdims=True)
        pack[0:3, :] = p * dlb_ref[...] * (first - p0)
        for ref, (arr, r0) in zip(piece_refs, pieces):
            pack[r0:r0 + arr.shape[0], 0:arr.shape[1]] = ref[...]
        slots[me] = pack[...]
        copies = []
        for k in range(1, N_DEV):
            peer = (x ^ (k >> 2), y ^ ((k >> 1) & 1), c ^ (k & 1))
            cp = pltpu.make_async_remote_copy(pack, slots.at[me], send_sems.at[k - 1], recv_sems.at[k - 1],
                                              device_id=peer, device_id_type=MESH)
            cp.start()
            copies.append(cp)
        for k in range(1, N_DEV):
            src = (x ^ (k >> 2)) * 4 + (y ^ ((k >> 1) & 1)) * 2 + (c ^ (k & 1))
            pltpu.make_async_remote_copy(pack, slots.at[src], send_sems.at[k - 1], recv_sems.at[k - 1],
                                         device_id=(x, y, c), device_id_type=MESH).wait_recv()
        for cp in copies:
            cp.wait_send()
        acc = slots[0]
        for i in range(1, N_DEV):
            acc = acc + slots[i]
        out_ref[...] = acc

    return pl.pallas_call(
        body, name="all_reduce_small", in_specs=[VMEM] * (2 + npc), out_specs=VMEM,
        out_shape=jax.ShapeDtypeStruct((SMALL_ROWS, d), f32),
        scratch_shapes=[pltpu.VMEM((SMALL_ROWS, d), f32), pltpu.VMEM((N_DEV, SMALL_ROWS, d), f32),
                        pltpu.SemaphoreType.DMA((N_DEV - 1,)), pltpu.SemaphoreType.DMA((N_DEV - 1,))],
        compiler_params=pltpu.CompilerParams(vmem_limit_bytes=VMEM_LIMIT))(
            a_lb, dlb, *[p[0] for p in pieces])


def _as_halves(g):
    return g.reshape(2 * g.shape[0], g.shape[1] // 2, g.shape[2])


def _chip_sums(grads, from_sibling, tag):
    return [_chip_sum(_as_halves(g), t, f"grad_chip_sum_{tag}{a}") for a, (g, t) in enumerate(zip(grads, from_sibling))]


def _final_sums(grads, from_sibling, from_chips, tag):
    return [_final_sum(_as_halves(g), t, r, f"grad_final_sum_{tag}{a}")
            for a, (g, t, r) in enumerate(zip(grads, from_sibling, from_chips))]


def kernel(x, a_w_in, a_lower_bounds, a_norm_w, a_w_out, b_w_in, b_conv_w, b_a_log, b_dt_bias, b_norm_w, b_w_out, ln_g, ln_b, loss_target, m_a_w_in, m_a_lower_bounds, m_a_norm_w, m_a_w_out, m_b_w_in, m_b_conv_w, m_b_a_log, m_b_dt_bias, m_b_norm_w, m_b_w_out, m_ln_g, m_ln_b, v_a_w_in, v_a_lower_bounds, v_a_norm_w, v_a_w_out, v_b_w_in, v_b_conv_w, v_b_a_log, v_b_dt_bias, v_b_norm_w, v_b_w_out, v_ln_g, v_ln_b):
    _, s, d = x.shape
    h = d // HEAD_DIM
    wb = b_w_in.shape[2]
    x2, target = x[0], loss_target[0]
    chip = 2 * lax.axis_index("x") + lax.axis_index("y")

    e = wb - d
    shard_b = b_w_in[0]
    main_b = lax.dynamic_update_slice(jnp.zeros((d, d + N_CHIPS * e), f32), shard_b[:, :d], (0, e * chip))[:, :d]
    tail_b = lax.dynamic_slice(jnp.pad(shard_b, ((0, 0), (0, LANES))), (0, d - e * chip), (d, LANES))
    tail_b = jnp.where(lax.broadcasted_iota(jnp.int32, (1, LANES), 1) < e * (chip + 1), tail_b, 0.0)
    shards = [_cast(a_w_in[0], bf16, "cast_a_w_in", True), _cast(a_w_out[0], bf16, "cast_a_w_out", True),
              _cast(main_b, bf16, "cast_b_w_in", True), _cast(tail_b, bf16, "cast_b_w_in_tail", True),
              _cast(b_w_out[0], bf16, "cast_b_w_out", True), _cast(b_conv_w[0], f32, "copy_b_conv_w", True)]
    sh_a_in, sh_a_out, sh_b_in, sh_b_tail, sh_b_out, sh_conv = shards
    left, right = (0, d // 2), (d // 2, d // 2)
    wa4, conv_g = _run_comm(_all_gather([sh_a_in, sh_conv], [True, False], [left, None]), "all_gather_first")
    conv_w = conv_g.transpose(1, 0, 2).reshape(CONV_WIDTH, 3 * d)

    x_bf = _cast(x2, bf16, "cast_x")
    lb = _lower_bound(a_lower_bounds)
    h4a, (wa4,) = _mm(x_bf, wa4, "nn", name="a_in_fwd_left", n_half=0, carry=_all_gather([wa4], [True], [right]))
    h4a, (wb_main, wb_tail) = _mm(x_bf, wa4, "nn", name="a_in_fwd_right", n_half=1, into=h4a,
                                  carry=_all_gather([sh_b_in, sh_b_tail], [True, True]))
    wb4 = _merge_tails(wb_main, wb_tail)
    wba = wb_tail[N_CHIPS - 1]
    ya, st_a, (wa_out_g,) = _hgrn2_fwd(h4a, lb, a_norm_w, carry=_all_gather([sh_a_out], [True]))
    wa_out = wa_out_g.reshape(d, d)
    yo_a = _mm(ya, wa_out, "nn", name="a_out_fwd")
    x1, x1_bf = _ln_fwd(x2, yo_a, ln_g[0:1], ln_b[0:1])
    h4b, (wb_out_g,) = _mm(x1_bf, wb4, "nn", name="b_in_fwd", carry=_all_gather([sh_b_out], [True]))
    wb_out = wb_out_g.reshape(d, d)
    ba = _mm(x1_bf, wba, "nn", name="b_gate_fwd")
    c3 = _conv_fwd(h4b, conv_w)
    bl = ba[:, :h].T.reshape(h, s, 1)
    al = ba[:, h:2 * h].T.reshape(h, s, 1)
    alog, dtb = b_a_log.reshape(h, 1, 1), b_dt_bias.reshape(h, 1, 1)
    yb, st_b, inv_b = _gdn_fwd(c3, h4b, bl, al, alog, dtb, b_norm_w)
    yo_b = _mm(yb, wb_out, "nn", name="b_out_fwd")

    dr2, dr2_bf, dg2, db2, loss = _ln_bwd(x1, yo_b, ln_g[1:2], b=ln_b[1:2], target=target, name="ln2_loss_bwd")
    g_wb_out = _mm(yb, dr2_bf, "tn", name="b_out_dw")
    dyb = _mm(dr2_bf, wb_out, "nt", name="b_out_dx")
    dc3, dh4b, dbl, dal, dalog, ddtb, dnw_b = _gdn_bwd(c3, h4b, bl, al, alog, dtb, b_norm_w, st_b, inv_b, dyb)
    dh4b, dconv = _conv_bwd(dc3, h4b, conv_w, dh4b)
    dba = jnp.pad(jnp.concatenate([dbl.reshape(h, s).T, dal.reshape(h, s).T], axis=1), ((0, 0), (0, LANES - 2 * h)))
    g_wb4 = _mm(x1_bf, dh4b, "tn", name="b_in_dw")
    g_wba = _mm(x1_bf, dba, "tn", name="b_in_dw_gate")
    dx1 = _mm(dh4b, wb4, "nt", name="b_in_dx", add=dr2, add_scale=DEEPNORM_ALPHA)
    dx1 = _mm(dba, wba, "nt", name="b_in_dx_gate", add=dx1)
    dr1, dr1_bf, dg1, db1 = _ln_bwd(x2, yo_a, ln_g[0:1], dy=dx1, name="ln1_bwd")
    g_tail = jnp.concatenate([g_wb4[1:, :, :LANES], g_wba[None]], axis=0)
    grads_b = [g_wb4, g_tail, g_wb_out.reshape(N_CHIPS, d // N_CHIPS, d)]
    g_wa_out, sib_b = _mm(ya, dr1_bf, "tn", name="a_out_dw", carry=_swap_halves(grads_b))
    chip_b = _chip_sums(grads_b, sib_b, "b")
    dya = _mm(dr1_bf, wa_out, "nt", name="a_out_dx")
    dh4, dlb, dnw_a, far_b = _hgrn2_bwd(h4a, lb, a_norm_w, st_a, dya, carry=_scatter_blocks(chip_b))
    grads_a_out = [g_wa_out.reshape(N_CHIPS, d // N_CHIPS, d)]
    g_wa4_away = _mm(x_bf, dh4, "tn", name="a_in_dw_away", a_half="other")
    g_wa4_keep, (sib_a_in, sib_a_out) = _mm(x_bf, dh4, "tn", name="a_in_dw_keep", a_half="mine",
                                            carry=_compose(_send_to_sibling([g_wa4_away]), _swap_halves(grads_a_out)))
    chip_a = [_pair_sum(g_wa4_keep, sib_a_in, "grad_chip_sum_a_in")] + _chip_sums(grads_a_out, [sib_a_out], "a_out")
    grad_x, far_a = _mm(dh4, wa4, "nt", name="a_in_dx", add=dr1, add_scale=DEEPNORM_ALPHA, carry=_scatter_blocks(chip_a))

    finals = ([_final_sum_pair(g_wa4_keep, sib_a_in, far_a[0], "grad_final_sum_a_in")]
              + _final_sums(grads_a_out, [sib_a_out], far_a[1:], "a_out") + _final_sums(grads_b, sib_b, far_b, "b"))
    big = [j.reshape(2 * j.shape[1], j.shape[2]) for j in _run_comm(_join_halves(finals), "grad_join_halves")]
    small = _all_reduce_small(a_lower_bounds, dlb, [
        (dg1, 3), (dg2, 4), (db1, 5), (db2, 6), (dnw_a, 7), (dnw_b, 8), (dalog.reshape(1, h), 9), (ddtb.reshape(1, h), 10),
        (loss, 11), (dconv.reshape(3 * CONV_WIDTH, d), 12)])
    loss_out = small[11, 0]
    cw = 3 * d // N_CHIPS
    g_conv = lax.dynamic_slice_in_dim(small[12:24].reshape(CONV_WIDTH, 3 * d), chip * cw, cw, axis=1)
    g_b_in = lax.dynamic_slice(jnp.concatenate([big[2], big[3]], axis=1), (0, e * chip), (d, wb))
    grads = {
        "a_w_in": big[0], "a_lower_bounds": small[0:3], "a_norm_w": small[7:8, :HEAD_DIM], "a_w_out": big[1],
        "b_w_in": g_b_in, "b_conv_w": g_conv, "b_a_log": small[9:10, :h], "b_dt_bias": small[10:11, :h],
        "b_norm_w": small[8:9, :HEAD_DIM], "b_w_out": big[4],
        "ln_g": jnp.concatenate([small[3:4], small[4:5]], axis=0), "ln_b": jnp.concatenate([small[5:6], small[6:7]], axis=0),
    }

    weights = dict(a_w_in=a_w_in, a_lower_bounds=a_lower_bounds, a_norm_w=a_norm_w, a_w_out=a_w_out, b_w_in=b_w_in,
                   b_conv_w=b_conv_w, b_a_log=b_a_log, b_dt_bias=b_dt_bias, b_norm_w=b_norm_w, b_w_out=b_w_out,
                   ln_g=ln_g, ln_b=ln_b)
    m_in = dict(a_w_in=m_a_w_in, a_lower_bounds=m_a_lower_bounds, a_norm_w=m_a_norm_w, a_w_out=m_a_w_out, b_w_in=m_b_w_in,
                b_conv_w=m_b_conv_w, b_a_log=m_b_a_log, b_dt_bias=m_b_dt_bias, b_norm_w=m_b_norm_w, b_w_out=m_b_w_out,
                ln_g=m_ln_g, ln_b=m_ln_b)
    v_in = dict(a_w_in=v_a_w_in, a_lower_bounds=v_a_lower_bounds, a_norm_w=v_a_norm_w, a_w_out=v_a_w_out, b_w_in=v_b_w_in,
                b_conv_w=v_b_conv_w, b_a_log=v_b_a_log, b_dt_bias=v_b_dt_bias, b_norm_w=v_b_norm_w, b_w_out=v_b_w_out,
                ln_g=v_ln_g, ln_b=v_ln_b)
    out_g, out_d, out_m, out_v = [], [], [], []
    for name, w in weights.items():
        shape2 = w.shape[-2:]
        g, delta, nm, nv = _adamw(w.reshape(shape2), grads[name].reshape(shape2), m_in[name].reshape(shape2),
                                  v_in[name].reshape(shape2), "adamw_" + name)
        out_g.append(g.reshape(w.shape))
        out_d.append(delta.reshape(w.shape))
        out_m.append(nm.reshape(w.shape))
        out_v.append(nv.reshape(w.shape))
    return (loss_out, grad_x.reshape(x.shape), *out_g, *out_d, *out_m, *out_v)
```

```python
import functools
import math
from typing import Callable, NamedTuple

import jax
import jax.numpy as jnp
from jax import lax
from jax.experimental import pallas as pl
from jax.experimental.pallas import tpu as pltpu

f32 = jnp.float32
bf16 = jnp.bfloat16
HIGHEST = lax.Precision.HIGHEST

HEAD_DIM = 128
CHUNK = 64
SUB = 16
DEPTH = 2
DEEPNORM_ALPHA = (2.0 * DEPTH) ** 0.25
LN_EPS = 1e-5
RMS_EPS = 1e-6
L2_EPS = 1e-6
ADAM_LR, ADAM_B1, ADAM_B2, ADAM_EPS, ADAM_WD, ADAM_STEP = 0.001, 0.9, 0.999, 1e-08, 0.01, 10
CONV_WIDTH = 4
EXP_CLAMP = 40.0
VMEM_LIMIT = 56 << 20
MXU_DTYPE = bf16
SUBLANES = 8
LANES = 128
N_CHIPS = 4
N_DEV = 8

NN = (((1,), (0,)), ((), ()))
NT = (((1,), (1,)), ((), ()))
TN = (((0,), (0,)), ((), ()))
MESH = pl.DeviceIdType.MESH
ANY = pl.BlockSpec(memory_space=pl.ANY)
VMEM = pl.BlockSpec(memory_space=pltpu.VMEM)


def _bdot(a, b, dn):
    return lax.dot_general(a.astype(MXU_DTYPE), b.astype(MXU_DTYPE), dn, preferred_element_type=f32)


def _hdot(a, b, dn):
    return lax.dot_general(a, b, dn, precision=lax.Precision.HIGH, preferred_element_type=f32)


def _iota(shape, dim):
    return lax.broadcasted_iota(jnp.int32, shape, dim)


def _same_sub(a, b):
    shift = int(math.log2(SUB))
    return jnp.right_shift(a, shift) == jnp.right_shift(b, shift)


def _silu(x):
    return x * jax.nn.sigmoid(x)


def _gated_rmsnorm(o, hz, nw):
    return o * lax.rsqrt(jnp.mean(o * o, axis=-1, keepdims=True) + RMS_EPS) * nw * _silu(hz)


def _hgrn2_chunk(hq, hf, hi, hz, lb, nw, st):
    c = CHUNK
    q = _silu(hq)
    forget = lb + (1.0 - lb) * jax.nn.sigmoid(hf)
    k = 1.0 - forget
    g = jnp.log(forget)
    tri = (_iota((c, c), 1) <= _iota((c, c), 0)).astype(f32)
    cum = _hdot(tri, g, NN)
    total = jnp.sum(g, axis=0, keepdims=True)
    nb = c // SUB
    blk = [cum[SUB * j:SUB * (j + 1)] for j in range(nb)]
    r16 = _iota((SUB, HEAD_DIM), 0)

    def row(j, i):
        return jnp.sum(jnp.where(r16 == i, blk[j], 0.0), axis=0, keepdims=True)

    c15, c31, c47 = row(0, SUB - 1), row(1, SUB - 1), row(2, SUB - 1)
    z = jnp.zeros((SUB, HEAD_DIM), f32)
    e = jnp.exp
    q32 = jnp.concatenate([z, z, e(blk[2] - c31), e(blk[3] - c31)], axis=0)
    k32 = jnp.concatenate([e(c31 - blk[0]), e(c31 - blk[1]), z, z], axis=0)
    q16a = jnp.concatenate([z, e(blk[1] - c15), z, z], axis=0)
    k16a = jnp.concatenate([e(c15 - blk[0]), z, z, z], axis=0)
    q16b = jnp.concatenate([z, z, z, e(blk[3] - c47)], axis=0)
    k16b = jnp.concatenate([z, z, e(c47 - blk[2]), z], axis=0)
    dmid = jnp.concatenate([blk[j] - row(j, SUB // 2 - 1) for j in range(nb)], axis=0)
    qd = e(jnp.minimum(dmid, EXP_CLAMP))
    kd = e(jnp.minimum(-dmid, EXP_CLAMP))
    q2 = jnp.concatenate([q * q32, q * q16a, q * q16b], axis=1)
    k2 = jnp.concatenate([k * k32, k * k16a, k * k16b], axis=1)
    rr, cc = _iota((c, c), 0), _iota((c, c), 1)
    diag = _same_sub(rr, cc) & (cc <= rr)
    att = _bdot(q2, k2, NT) + jnp.where(diag, _bdot(q * qd, k * kd, NT), 0.0)
    o = _bdot(att, hi, NN) + _bdot(q * e(cum), st, NT)
    st_new = st * e(total) + _bdot(hi, k * e(total - cum), TN)
    return _gated_rmsnorm(o, hz, nw), st_new


def _unit_lower_inverse(lower):
    c = CHUNK
    rr, cc = _iota((c, c), 0), _iota((c, c), 1)
    eye = (rr == cc).astype(f32)
    same = _same_sub(rr, cc)
    ld = jnp.where(same, lower, 0.0)
    off = lower - ld
    x = eye - ld
    p = ld
    for _ in range(int(math.log2(SUB)) - 1):
        p = _hdot(p, p, NN)
        x = x + _hdot(x, p, NN)
    n = _hdot(x, off, NN)
    y = eye - n
    p = n
    for _ in range(int(math.log2(c // SUB)) - 1):
        p = _hdot(p, p, NN)
        y = y + _hdot(y, p, NN)
    return _hdot(y, x, NN)


@jax.custom_vjp
def _known_inverse(lower, inv):
    return inv


def _known_inverse_fwd(lower, inv):
    return inv, inv


def _known_inverse_bwd(inv, dinv):
    return -_hdot(_hdot(inv, dinv, TN), inv, NT), jnp.zeros_like(inv)


_known_inverse.defvjp(_known_inverse_fwd, _known_inverse_bwd)


def _gdn_chunk(cq, ck, cv, hz, bl, al, alog, dtb, nw, st, inv=None, with_inverse=False):
    c = CHUNK
    sq, sk, v = _silu(cq), _silu(ck), _silu(cv)
    q = sq * lax.rsqrt(jnp.sum(sq * sq, axis=-1, keepdims=True) + L2_EPS) * (HEAD_DIM ** -0.5)
    k = sk * lax.rsqrt(jnp.sum(sk * sk, axis=-1, keepdims=True) + L2_EPS)
    beta = jax.nn.sigmoid(bl)
    xs = al + dtb
    pos = xs > 0.0
    g = -jnp.exp(alog) * (jnp.where(pos, xs, 0.0) + jnp.log(1.0 + jnp.exp(jnp.where(pos, -xs, xs))))
    rr, cc = _iota((c, c), 0), _iota((c, c), 1)
    tri = (cc <= rr).astype(f32)
    gc = jnp.broadcast_to(g, (c, c))
    cum_c = _hdot(tri, gc, NN)
    cum_r = _hdot(gc, (rr <= cc).astype(f32), TN)
    g128 = jnp.broadcast_to(g, (c, HEAD_DIM))
    cum = _hdot(tri, g128, NN)
    total = jnp.sum(g128, axis=0, keepdims=True)
    diff = cum_c - cum_r
    ninf = -jnp.inf
    lower = beta * _bdot(k, k, NT) * jnp.exp(jnp.where(cc < rr, diff, ninf))
    rhs = jnp.concatenate([beta * v, (beta * jnp.exp(cum)) * k], axis=1)
    minv = _unit_lower_inverse(lower) if inv is None else _known_inverse(lower, inv)
    sol = _hdot(minv, rhs, NN)
    u0, w = sol[:, :HEAD_DIM], sol[:, HEAD_DIM:]
    qk = _bdot(q, k, NT) * jnp.exp(jnp.where(cc <= rr, diff, ninf))
    u = u0 - _bdot(w, st, NN)
    o = _bdot(q * jnp.exp(cum), st, NN) + _bdot(qk, u, NN)
    st_new = st * jnp.exp(total) + _bdot(k * jnp.exp(total - cum), u, TN)
    y = _gated_rmsnorm(o, hz, nw)
    return (y, st_new, minv) if with_inverse else (y, st_new)


def _rows(c):
    return pl.ds(pl.multiple_of(c * CHUNK, CHUNK), CHUNK)


HEADS_PER_STEP = 8
CHUNKS_PER_STEP = 4


def _scan_dims(s, d):
    h, n = d // HEAD_DIM, s // CHUNK
    ncb = min(n, CHUNKS_PER_STEP)
    hp = min(h, HEADS_PER_STEP)
    return h, n, ncb, n // ncb, ncb * CHUNK, hp


def _scan_specs(t, nblk, ncb, hp, reverse):
    blk = (lambda j: nblk - 1 - j) if reverse else (lambda j: j)
    w = hp * HEAD_DIM
    grp = lambda g: pl.BlockSpec((g, t, w), lambda h, j: (0, blk(j), h))
    one = lambda g: pl.BlockSpec((None, t, w), lambda h, j: (g, blk(j), h))
    col = lambda: pl.BlockSpec((t, w), lambda h, j: (blk(j), h))
    vec = lambda: pl.BlockSpec((1, w), lambda h, j: (0, h))
    shared = lambda: pl.BlockSpec((1, HEAD_DIM), lambda h, j: (0, 0))
    state = lambda: pl.BlockSpec((hp, ncb, HEAD_DIM, HEAD_DIM), lambda h, j: (h, blk(j), 0, 0))
    inverse = lambda: pl.BlockSpec((hp, ncb, CHUNK, CHUNK), lambda h, j: (h, blk(j), 0, 0))
    tok = lambda: pl.BlockSpec((hp, t, 1), lambda h, j: (h, blk(j), 0))
    par = lambda: pl.BlockSpec((hp, 1, 1), lambda h, j: (h, 0, 0))
    return dict(grp=grp, one=one, col=col, vec=vec, shared=shared, state=state, inverse=inverse, tok=tok, par=par)


def _scan_params():
    return pltpu.CompilerParams(dimension_semantics=("arbitrary", "arbitrary"), vmem_limit_bytes=VMEM_LIMIT)


def _state_scratch(hp):
    return [pltpu.VMEM((hp, HEAD_DIM, HEAD_DIM), f32)]


def _lanes(i):
    return pl.ds(i * HEAD_DIM, HEAD_DIM)


def _stack_heads(per_head):
    return tuple(jnp.stack(vals) for vals in zip(*per_head))


def _hgrn2_fwd(h4, lb, nw, carry=None):
    _, s, d = h4.shape
    h, n, ncb, nblk, t, hp = _scan_dims(s, d)
    sp = _scan_specs(t, nblk, ncb, hp, False)
    grid = (h // hp, nblk)
    ni, no = (len(carry.arrays), len(carry.out_shapes)) if carry else (0, 0)

    def body(*refs):
        h_ref, lb_ref, nw_ref = refs[:3]
        ins = refs[3:3 + ni]
        y_ref, st_ref = refs[3 + ni:5 + ni]
        outs = refs[5 + ni:5 + ni + no]
        st_scr = refs[5 + ni + no]
        sems = refs[6 + ni + no:]
        _carry_at(carry, grid, "first", ins, outs, sems)

        @pl.when(pl.program_id(1) == 0)
        def _():
            st_scr[...] = jnp.zeros_like(st_scr)

        def step(c, carry):
            r = _rows(c)
            args = [(h_ref[0, r, _lanes(i)], h_ref[1, r, _lanes(i)], h_ref[2, r, _lanes(i)], h_ref[3, r, _lanes(i)],
                     lb_ref[:, _lanes(i)], nw_ref[...], st_scr[i]) for i in range(hp)]
            y, st_new = jax.vmap(_hgrn2_chunk)(*_stack_heads(args))
            for i, a in enumerate(args):
                st_ref[i, c] = a[-1]
                y_ref[r, _lanes(i)] = y[i].astype(y_ref.dtype)
                st_scr[i] = st_new[i]
            return carry

        lax.fori_loop(0, ncb, step, 0)
        _carry_at(carry, grid, "last", ins, outs, sems)

    extra = _carry_layout(carry, 3, 2)
    res = pl.pallas_call(
        body, grid=grid, name="hgrn2_fwd",
        in_specs=[sp["grp"](4), sp["vec"](), sp["shared"]()] + extra.in_specs,
        out_specs=[sp["col"](), sp["state"]()] + extra.out_specs,
        out_shape=[jax.ShapeDtypeStruct((s, d), bf16), jax.ShapeDtypeStruct((h, n, HEAD_DIM, HEAD_DIM), f32)]
        + extra.out_shapes,
        input_output_aliases=extra.aliases,
        scratch_shapes=_state_scratch(hp) + extra.sems, compiler_params=_scan_params())(
            h4, lb, nw, *(carry.arrays if carry else []))
    return (*res[:2], res[2:]) if carry else res


def _hgrn2_bwd(h4, lb, nw, states, dy, carry=None):
    _, s, d = h4.shape
    h, n, ncb, nblk, t, hp = _scan_dims(s, d)
    sp = _scan_specs(t, nblk, ncb, hp, True)
    grid = (h // hp, nblk)
    ni, no = (len(carry.arrays), len(carry.out_shapes)) if carry else (0, 0)

    def body(*refs):
        h_ref, lb_ref, nw_ref, st_ref, dy_ref = refs[:5]
        ins = refs[5:5 + ni]
        dh_ref, dlb_ref, dnw_ref = refs[5 + ni:8 + ni]
        outs = refs[8 + ni:8 + ni + no]
        dst_scr = refs[8 + ni + no]
        sems = refs[9 + ni + no:]
        first_block = pl.program_id(1) == 0
        _carry_at(carry, grid, "first", ins, outs, sems)

        @pl.when(first_block)
        def _():
            dst_scr[...] = jnp.zeros_like(dst_scr)
            dlb_ref[...] = jnp.zeros_like(dlb_ref)

        @pl.when(first_block & (pl.program_id(0) == 0))
        def _():
            dnw_ref[...] = jnp.zeros_like(dnw_ref)

        def step(k, carry):
            c = ncb - 1 - k
            r = _rows(c)
            args = [(h_ref[0, r, _lanes(i)], h_ref[1, r, _lanes(i)], h_ref[2, r, _lanes(i)], h_ref[3, r, _lanes(i)],
                     lb_ref[:, _lanes(i)], nw_ref[...], st_ref[i, c]) for i in range(hp)]
            cots = [(dy_ref[r, _lanes(i)].astype(f32), dst_scr[i]) for i in range(hp)]
            dq, df, di, dz, dlb, dnw, dst = jax.vjp(jax.vmap(_hgrn2_chunk), *_stack_heads(args))[1](_stack_heads(cots))
            dnw_sum = dnw_ref[...]
            for i in range(hp):
                for g, val in enumerate((dq, df, di, dz)):
                    dh_ref[g, r, _lanes(i)] = val[i].astype(dh_ref.dtype)
                dlb_ref[:, _lanes(i)] += dlb[i]
                dnw_sum = dnw_sum + dnw[i]
                dst_scr[i] = dst[i]
            dnw_ref[...] = dnw_sum
            return carry

        lax.fori_loop(0, ncb, step, 0)
        _carry_at(carry, grid, "last", ins, outs, sems)

    extra = _carry_layout(carry, 5, 3)
    res = pl.pallas_call(
        body, grid=grid, name="hgrn2_bwd",
        in_specs=[sp["grp"](4), sp["vec"](), sp["shared"](), sp["state"](), sp["col"]()] + extra.in_specs,
        out_specs=[sp["grp"](4), sp["vec"](), sp["shared"]()] + extra.out_specs,
        out_shape=[jax.ShapeDtypeStruct((4, s, d), bf16), jax.ShapeDtypeStruct((1, d), f32),
                   jax.ShapeDtypeStruct((1, HEAD_DIM), f32)] + extra.out_shapes,
        input_output_aliases=extra.aliases,
        scratch_shapes=_state_scratch(hp) + extra.sems, compiler_params=_scan_params())(
            h4, lb, nw, states, dy, *(carry.arrays if carry else []))
    return (*res[:3], res[3:]) if carry else res


def _gdn_fwd(c3, h4, bl, al, alog, dtb, nw):
    _, s, d = c3.shape
    h, n, ncb, nblk, t, hp = _scan_dims(s, d)
    sp = _scan_specs(t, nblk, ncb, hp, False)

    def body(c_ref, hz_ref, bl_ref, al_ref, alog_ref, dtb_ref, nw_ref, y_ref, st_ref, inv_ref, st_scr):
        @pl.when(pl.program_id(1) == 0)
        def _():
            st_scr[...] = jnp.zeros_like(st_scr)

        def step(c, carry):
            r = _rows(c)
            args = [(c_ref[0, r, _lanes(i)], c_ref[1, r, _lanes(i)], c_ref[2, r, _lanes(i)], hz_ref[r, _lanes(i)],
                     bl_ref[i, r, :], al_ref[i, r, :], alog_ref[i], dtb_ref[i], nw_ref[...], st_scr[i]) for i in range(hp)]
            y, st_new, inv = jax.vmap(functools.partial(_gdn_chunk, with_inverse=True))(*_stack_heads(args))
            for i, a in enumerate(args):
                st_ref[i, c] = a[-1]
                inv_ref[i, c] = inv[i]
                y_ref[r, _lanes(i)] = y[i].astype(y_ref.dtype)
                st_scr[i] = st_new[i]
            return carry

        lax.fori_loop(0, ncb, step, 0)

    return pl.pallas_call(
        body, grid=(h // hp, nblk), name="gdn_fwd",
        in_specs=[sp["grp"](3), sp["one"](3), sp["tok"](), sp["tok"](), sp["par"](), sp["par"](), sp["shared"]()],
        out_specs=[sp["col"](), sp["state"](), sp["inverse"]()],
        out_shape=[jax.ShapeDtypeStruct((s, d), bf16), jax.ShapeDtypeStruct((h, n, HEAD_DIM, HEAD_DIM), f32),
                   jax.ShapeDtypeStruct((h, n, CHUNK, CHUNK), f32)],
        scratch_shapes=_state_scratch(hp), compiler_params=_scan_params())(c3, h4, bl, al, alog, dtb, nw)


def _gdn_bwd(c3, h4, bl, al, alog, dtb, nw, states, inverses, dy):
    _, s, d = c3.shape
    h, n, ncb, nblk, t, hp = _scan_dims(s, d)
    sp = _scan_specs(t, nblk, ncb, hp, True)

    def chunk(*a):
        return _gdn_chunk(*a[:-1], inv=a[-1])

    def body(c_ref, hz_ref, bl_ref, al_ref, alog_ref, dtb_ref, nw_ref, st_ref, inv_ref, dy_ref,
             dc_ref, dz_ref, dbl_ref, dal_ref, dalog_ref, ddtb_ref, dnw_ref, dst_scr):
        first_block = pl.program_id(1) == 0

        @pl.when(first_block)
        def _():
            dst_scr[...] = jnp.zeros_like(dst_scr)
            dalog_ref[...] = jnp.zeros_like(dalog_ref)
            ddtb_ref[...] = jnp.zeros_like(ddtb_ref)

        @pl.when(first_block & (pl.program_id(0) == 0))
        def _():
            dnw_ref[...] = jnp.zeros_like(dnw_ref)

        def step(k, carry):
            c = ncb - 1 - k
            r = _rows(c)
            args = [(c_ref[0, r, _lanes(i)], c_ref[1, r, _lanes(i)], c_ref[2, r, _lanes(i)], hz_ref[r, _lanes(i)],
                     bl_ref[i, r, :], al_ref[i, r, :], alog_ref[i], dtb_ref[i], nw_ref[...], st_ref[i, c], inv_ref[i, c])
                    for i in range(hp)]
            cots = [(dy_ref[r, _lanes(i)].astype(f32), dst_scr[i]) for i in range(hp)]
            dq, dk, dv, dz, dbl, dal, dalog, ddtb, dnw, dst, _ = jax.vjp(
                jax.vmap(chunk), *_stack_heads(args))[1](_stack_heads(cots))
            dnw_sum = dnw_ref[...]
            for i in range(hp):
                for g, val in enumerate((dq, dk, dv)):
                    dc_ref[g, r, _lanes(i)] = val[i]
                dz_ref[r, _lanes(i)] = dz[i].astype(dz_ref.dtype)
                dbl_ref[i, r, :] = dbl[i]
                dal_ref[i, r, :] = dal[i]
                dalog_ref[i] += dalog[i]
                ddtb_ref[i] += ddtb[i]
                dnw_sum = dnw_sum + dnw[i]
                dst_scr[i] = dst[i]
            dnw_ref[...] = dnw_sum
            return carry

        lax.fori_loop(0, ncb, step, 0)

    tokens = jax.ShapeDtypeStruct((h, s, 1), f32)
    heads = jax.ShapeDtypeStruct((h, 1, 1), f32)
    return pl.pallas_call(
        body, grid=(h // hp, nblk), name="gdn_bwd",
        in_specs=[sp["grp"](3), sp["one"](3), sp["tok"](), sp["tok"](), sp["par"](), sp["par"](), sp["shared"](),
                  sp["state"](), sp["inverse"](), sp["col"]()],
        out_specs=[sp["grp"](3), sp["one"](3), sp["tok"](), sp["tok"](), sp["par"](), sp["par"](), sp["shared"]()],
        out_shape=[jax.ShapeDtypeStruct((3, s, d), f32), jax.ShapeDtypeStruct((4, s, d), bf16), tokens, tokens, heads, heads,
                   jax.ShapeDtypeStruct((1, HEAD_DIM), f32)],
        scratch_shapes=_state_scratch(hp), compiler_params=_scan_params())(
            c3, h4, bl, al, alog, dtb, nw, states, inverses, dy)


def _tile(n, pref):
    return n if n <= pref else pref


def _mm(a, b, mode, *, name, out_dtype=f32, add=None, add_scale=1.0, b_g0=0, groups=None, carry=None, a_half=None,
        n_half=None, into=None):
    squeeze = a.ndim == 2 and b.ndim == 2
    a3 = a if a.ndim == 3 else a[None]
    b3 = b if b.ndim == 3 else b[None]
    if mode == "nt":
        g = groups or a3.shape[0]
        (_, m, r), (_, n, _) = a3.shape, b3.shape
    elif mode == "nn":
        g = groups or b3.shape[0]
        (_, m, r), (_, _, n) = a3.shape, b3.shape
        if n_half is not None:
            n = n // 2
    else:
        g = groups or b3.shape[0]
        (_, r, m), (_, _, n) = a3.shape, b3.shape
        if a_half is not None:
            m = m // 2
    tm, tn, tr = _tile(m, 1024), _tile(n, 1024), _tile(r, 2048)
    nr = r // tr
    a_grouped = a3.shape[0] > 1
    dn = {"nn": NN, "nt": NT, "tn": TN}[mode]

    if mode == "nt":
        grid = (m // tm, n // tn, g * nr)
        last = g * nr - 1
        kax = 2
        a_spec = pl.BlockSpec((None, tm, tr), lambda i, j, k: (k // nr, i, k % nr))
        b_spec = pl.BlockSpec((None, tn, tr), lambda i, j, k: (b_g0 + k // nr, j, k % nr))
        o_spec = pl.BlockSpec((tm, tn), lambda i, j, k: (i, j))
        add_spec = pl.BlockSpec((tm, tn), lambda i, j, k: (i, j))
        out_shape = jax.ShapeDtypeStruct((m, n), out_dtype)
        sem = ("parallel", "parallel", "arbitrary")
    else:
        grid = (g, m // tm, n // tn, nr)
        last = nr - 1
        kax = 3
        ag = (lambda q: q) if a_grouped else (lambda q: 0)
        if mode == "nn":
            a_spec = pl.BlockSpec((None, tm, tr), lambda q, i, j, k: (ag(q), i, k))
        else:
            def col0():
                if a_half is None:
                    return 0
                return (_my_core() if a_half == "mine" else 1 - _my_core()) * (m // tm)
            a_spec = pl.BlockSpec((None, tr, tm), lambda q, i, j, k: (ag(q), k, col0() + i))
        off = 0 if n_half is None else n_half * (n // tn)
        b_spec = pl.BlockSpec((None, tr, tn), lambda q, i, j, k: (b_g0 + q, k, off + j))
        o_spec = pl.BlockSpec((None, tm, tn), lambda q, i, j, k: (q, i, off + j))
        add_spec = ANY
        out_shape = jax.ShapeDtypeStruct((g, m, n if n_half is None else 2 * n), out_dtype)
        sem = ("parallel", "parallel", "parallel", "arbitrary")
        assert add is None
        add = into

    n_main = 2 if add is None else 3
    ni = len(carry.arrays) if carry else 0
    no = len(carry.out_shapes) if carry else 0

    def body(*refs):
        a_ref, b_ref = refs[:2]
        add_ref = None if add is None else refs[2]
        ins = refs[n_main:n_main + ni]
        o_ref = refs[n_main + ni]
        outs = refs[n_main + ni + 1:n_main + ni + 1 + no]
        acc = refs[n_main + ni + 1 + no]
        sems = refs[n_main + ni + 2 + no:]
        k = pl.program_id(kax)
        _carry_at(carry, grid, "first", ins, outs, sems)

        @pl.when(k == 0)
        def _():
            acc[...] = jnp.zeros_like(acc)

        acc[...] += _bdot(a_ref[...], b_ref[...], dn)

        @pl.when(k == last)
        def _():
            res = acc[...]
            if add is not None and mode == "nt":
                res = res + add_scale * add_ref[...]
            o_ref[...] = res.astype(o_ref.dtype)

        _carry_at(carry, grid, "last", ins, outs, sems)

    operands = [a3, b3] + ([] if add is None else [add])
    in_specs = [a_spec, b_spec] + ([] if add is None else [add_spec])
    extra = _carry_layout(carry, n_main, 1)
    if carry:
        sem = ("arbitrary",) * len(grid)
    out, *moved = pl.pallas_call(
        body, grid=grid, name=name, in_specs=in_specs + extra.in_specs, out_specs=[o_spec] + extra.out_specs,
        out_shape=[out_shape] + extra.out_shapes,
        input_output_aliases=extra.aliases | ({2: 0} if into is not None else {}),
        scratch_shapes=[pltpu.VMEM((tm, tn), f32)] + extra.sems,
        compiler_params=pltpu.CompilerParams(dimension_semantics=sem, vmem_limit_bytes=VMEM_LIMIT))(
            *operands, *(carry.arrays if carry else []))
    out = out[0] if (squeeze and mode != "nt") else out
    return (out, moved) if carry else out


def _row_block(rows, cols, target_bytes=1 << 20):
    if rows <= SUBLANES:
        return rows
    tr = SUBLANES
    while tr * 2 <= rows and tr * 2 * cols * 4 <= target_bytes and rows % (tr * 2) == 0:
        tr *= 2
    return tr


def _ew_params(n_axes=1):
    return pltpu.CompilerParams(dimension_semantics=("parallel",) * n_axes, vmem_limit_bytes=VMEM_LIMIT)


def _cast(x, dtype, name, into_chip_slot=False):
    r, c = x.shape
    tr = _row_block(r, c, 2 << 20)
    tr = max(tr, 16) if r >= 16 else tr
    spec = pl.BlockSpec((tr, c), lambda i: (i, 0))
    if into_chip_slot:
        out_spec = pl.BlockSpec((None, tr, c), lambda i: (_my_chip(), i, 0))
        out_shape = jax.ShapeDtypeStruct((N_CHIPS, r, c), dtype)
    else:
        out_spec, out_shape = spec, jax.ShapeDtypeStruct((r, c), dtype)

    def body(x_ref, o_ref):
        o_ref[...] = x_ref[...].astype(dtype)

    return pl.pallas_call(body, grid=(r // tr,), name=name, in_specs=[spec], out_specs=out_spec,
                          out_shape=out_shape, compiler_params=_ew_params())(x)


def _merge_tails(main, tail):
    _, d, _ = main.shape
    tr = max(_row_block(d, LANES), 16)
    head = pl.BlockSpec((None, tr, LANES), lambda g, i: (g + 1, i, 0))
    rest = pl.BlockSpec((None, tr, LANES), lambda g, i: (g, i, 0))

    def body(m_ref, t_ref, o_ref):
        o_ref[...] = m_ref[...] + t_ref[...]

    return pl.pallas_call(body, grid=(N_CHIPS - 1, d // tr), name="merge_shard_tails", in_specs=[head, rest],
                          out_specs=head, out_shape=jax.ShapeDtypeStruct(main.shape, main.dtype),
                          input_output_aliases={0: 0}, compiler_params=_ew_params(2))(main, tail)


def _ln_stats(r):
    mu = jnp.mean(r, axis=-1, keepdims=True)
    var = jnp.mean(jnp.square(r - mu), axis=-1, keepdims=True)
    return mu, lax.rsqrt(var + LN_EPS)


def _ln_fwd(x, yo, g, b):
    s, d = x.shape
    tr = _row_block(s, d, 2 << 20)
    tr = max(tr, 16)
    big = pl.BlockSpec((tr, d), lambda i: (i, 0))
    vec = pl.BlockSpec((1, d), lambda i: (0, 0))

    def body(x_ref, yo_ref, g_ref, b_ref, o_ref, ob_ref):
        r = DEEPNORM_ALPHA * x_ref[...] + yo_ref[...]
        mu, rstd = _ln_stats(r)
        y = (r - mu) * rstd * g_ref[...] + b_ref[...]
        o_ref[...] = y
        ob_ref[...] = y.astype(bf16)

    return pl.pallas_call(
        body, grid=(s // tr,), name="ln_fwd", in_specs=[big, big, vec, vec], out_specs=[big, big],
        out_shape=[jax.ShapeDtypeStruct((s, d), f32), jax.ShapeDtypeStruct((s, d), bf16)],
        compiler_params=_ew_params())(x, yo, g, b)


def _ln_bwd(x, yo, g, *, dy=None, b=None, target=None, name):
    s, d = x.shape
    with_loss = target is not None
    tr = max(_row_block(s, d, 2 << 20), 16)
    big = pl.BlockSpec((tr, d), lambda i: (i, 0))
    vec = pl.BlockSpec((1, d), lambda i: (0, 0))
    lane = pl.BlockSpec((1, LANES), lambda i: (0, 0))

    def body(*refs):
        if with_loss:
            x_ref, yo_ref, g_ref, b_ref, t_ref, dr_ref, drb_ref, dg_ref, db_ref, loss_ref = refs
        else:
            x_ref, yo_ref, g_ref, dy_ref, dr_ref, drb_ref, dg_ref, db_ref = refs

        @pl.when(pl.program_id(0) == 0)
        def _():
            dg_ref[...] = jnp.zeros_like(dg_ref)
            db_ref[...] = jnp.zeros_like(db_ref)
            if with_loss:
                loss_ref[...] = jnp.zeros_like(loss_ref)

        r = DEEPNORM_ALPHA * x_ref[...] + yo_ref[...]
        mu, rstd = _ln_stats(r)
        xhat = (r - mu) * rstd
        if with_loss:
            err = xhat * g_ref[...] + b_ref[...] - t_ref[...]
            loss_ref[...] += jnp.sum(err * err) * (0.5 / d)
            dyv = err * (1.0 / d)
        else:
            dyv = dy_ref[...]
        dxhat = dyv * g_ref[...]
        m1 = jnp.mean(dxhat, axis=-1, keepdims=True)
        m2 = jnp.mean(dxhat * xhat, axis=-1, keepdims=True)
        dr = rstd * (dxhat - m1 - xhat * m2)
        dr_ref[...] = dr
        drb_ref[...] = dr.astype(bf16)
        dg_ref[...] += jnp.sum(dyv * xhat, axis=0, keepdims=True)
        db_ref[...] += jnp.sum(dyv, axis=0, keepdims=True)

    operands = (x, yo, g, b, target) if with_loss else (x, yo, g, dy)
    in_specs = [big, big, vec, vec, big] if with_loss else [big, big, vec, big]
    out_specs = [big, big, vec, vec] + ([lane] if with_loss else [])
    out_shape = [jax.ShapeDtypeStruct((s, d), f32), jax.ShapeDtypeStruct((s, d), bf16),
                 jax.ShapeDtypeStruct((1, d), f32), jax.ShapeDtypeStruct((1, d), f32)]
    if with_loss:
        out_shape.append(jax.ShapeDtypeStruct((1, LANES), f32))
    return pl.pallas_call(
        body, grid=(s // tr,), name=name, in_specs=in_specs, out_specs=out_specs, out_shape=out_shape,
        compiler_params=pltpu.CompilerParams(dimension_semantics=("arbitrary",), vmem_limit_bytes=VMEM_LIMIT))(*operands)


def _lower_bound(a_lb):
    _, d = a_lb.shape

    def body(a_ref, o_ref):
        a = a_ref[...]
        ex = jnp.exp(a - jnp.max(a, axis=0, keepdims=True))
        p = ex / jnp.sum(ex, axis=0, keepdims=True)
        o_ref[...] = jnp.sum(jnp.where(_iota(p.shape, 0) == 0, p, 0.0), axis=0, keepdims=True)

    return pl.pallas_call(body, name="lower_bound", in_specs=[VMEM], out_specs=VMEM,
                          out_shape=jax.ShapeDtypeStruct((1, d), f32))(a_lb)


def _shift_rows(ext, k, rows):
    return pltpu.roll(ext, k, axis=0)[SUBLANES:SUBLANES + rows]


def _conv_tiles(s, d):
    return _tile(s, 512), _tile(d, 512)


def _conv_fwd(h4, w):
    _, s, d = h4.shape
    tr, tc = _conv_tiles(s, d)
    nj = d // tc
    hb = tr // SUBLANES
    cur = pl.BlockSpec((None, tr, tc), lambda g, j, i: (g, i, j))
    prev = pl.BlockSpec((None, SUBLANES, tc), lambda g, j, i: (g, jnp.maximum(i * hb - 1, 0), j))
    wsp = pl.BlockSpec((CONV_WIDTH, tc), lambda g, j, i: (0, g * nj + j))

    def body(x_ref, p_ref, w_ref, o_ref):
        x = x_ref[...]
        halo = jnp.where(pl.program_id(2) == 0, 0.0, p_ref[...])
        ext = jnp.concatenate([halo, x], axis=0)
        acc = w_ref[CONV_WIDTH - 1:CONV_WIDTH, :] * x
        for k in range(1, CONV_WIDTH):
            acc = acc + w_ref[CONV_WIDTH - 1 - k:CONV_WIDTH - k, :] * _shift_rows(ext, k, tr)
        o_ref[...] = acc

    return pl.pallas_call(
        body, grid=(3, nj, s // tr), name="conv_fwd", in_specs=[cur, prev, wsp], out_specs=cur,
        out_shape=jax.ShapeDtypeStruct((3, s, d), f32), compiler_params=_ew_params(3))(h4, h4, w)


def _conv_bwd(dy3, h4, w, dh4):
    _, s, d = dy3.shape
    tr, tc = _conv_tiles(s, d)
    nj = d // tc
    hb = tr // SUBLANES
    ni = s // tr
    cur = pl.BlockSpec((None, tr, tc), lambda g, j, i: (g, i, j))
    prev = pl.BlockSpec((None, SUBLANES, tc), lambda g, j, i: (g, jnp.maximum(i * hb - 1, 0), j))
    nxt = pl.BlockSpec((None, SUBLANES, tc), lambda g, j, i: (g, jnp.minimum((i + 1) * hb, s // SUBLANES - 1), j))
    wsp = pl.BlockSpec((CONV_WIDTH, tc), lambda g, j, i: (0, g * nj + j))

    def body(dy_ref, dn_ref, x_ref, p_ref, w_ref, _, dx_ref, dw_ref):
        i = pl.program_id(2)

        @pl.when(i == 0)
        def _():
            dw_ref[...] = jnp.zeros_like(dw_ref)

        dy = dy_ref[...]
        x = x_ref[...]
        tail = jnp.where(i == ni - 1, 0.0, dn_ref[...])
        dext = jnp.concatenate([dy, tail], axis=0)
        halo = jnp.where(i == 0, 0.0, p_ref[...])
        xext = jnp.concatenate([halo, x], axis=0)
        acc = w_ref[CONV_WIDTH - 1:CONV_WIDTH, :] * dy
        dw_ref[CONV_WIDTH - 1:CONV_WIDTH, :] += jnp.sum(dy * x, axis=0, keepdims=True)
        for k in range(1, CONV_WIDTH):
            ahead = pltpu.roll(dext, tr + SUBLANES - k, axis=0)[:tr]
            acc = acc + w_ref[CONV_WIDTH - 1 - k:CONV_WIDTH - k, :] * ahead
            dw_ref[CONV_WIDTH - 1 - k:CONV_WIDTH - k, :] += jnp.sum(dy * _shift_rows(xext, k, tr), axis=0, keepdims=True)
        dx_ref[...] = acc.astype(dx_ref.dtype)

    return pl.pallas_call(
        body, grid=(3, nj, ni), name="conv_bwd", in_specs=[cur, nxt, cur, prev, wsp, ANY], out_specs=[cur, wsp],
        out_shape=[jax.ShapeDtypeStruct((4, s, d), bf16), jax.ShapeDtypeStruct((CONV_WIDTH, 3 * d), f32)],
        input_output_aliases={5: 0},
        compiler_params=pltpu.CompilerParams(dimension_semantics=("parallel", "parallel", "arbitrary"),
                                             vmem_limit_bytes=VMEM_LIMIT))(dy3, dy3, h4, h4, w, dh4)


def _my_core():
    return lax.axis_index("c")


def _my_chip():
    return 2 * lax.axis_index("x") + lax.axis_index("y")


def _chip_sum(halves, other, name):
    _, r2, c = other.shape
    tr = max(_row_block(r2, c), 16)
    own = pl.BlockSpec((None, tr, c), lambda p, i: (2 * p + _my_core(), i, 0))
    blk = pl.BlockSpec((None, tr, c), lambda p, i: (p, i, 0))

    def body(a_ref, b_ref, o_ref):
        o_ref[...] = (a_ref[...] + b_ref[...]).astype(o_ref.dtype)

    return pl.pallas_call(
        body, name=name, grid=(N_CHIPS, r2 // tr), in_specs=[own, blk], out_specs=blk,
        out_shape=jax.ShapeDtypeStruct(other.shape, bf16), compiler_params=_ew_params(2))(halves, other)


def _final_sum(halves, other, remote, name):
    _, r2, c = remote.shape
    tr = max(_row_block(r2, c), 16)
    own = pl.BlockSpec((None, tr, c), lambda i: (2 * _my_chip() + _my_core(), i, 0))
    sib = pl.BlockSpec((None, tr, c), lambda i: (_my_chip(), i, 0))
    slot = lambda j: pl.BlockSpec((None, tr, c), lambda i: (j, i, 0))

    def body(a_ref, b_ref, r0_ref, r1_ref, r2_ref, o_ref):
        acc = a_ref[...] + b_ref[...]
        for ref in (r0_ref, r1_ref, r2_ref):
            acc = acc + ref[...].astype(f32)
        o_ref[...] = acc

    return pl.pallas_call(
        body, name=name, grid=(r2 // tr,), in_specs=[own, sib, slot(0), slot(1), slot(2)],
        out_specs=pl.BlockSpec((None, tr, c), lambda i: (_my_core(), i, 0)),
        out_shape=jax.ShapeDtypeStruct((2, r2, c), f32), compiler_params=_ew_params())(halves, other, remote, remote, remote)


def _adamw(w, g, m, v, name):
    r, c = w.shape
    tr = _row_block(r, c)
    spec = pl.BlockSpec((tr, c), lambda i: (i, 0))

    def body(w_ref, g_ref, m_ref, v_ref, go_ref, d_ref, nm_ref, nv_ref):
        gv = g_ref[...]
        go_ref[...] = gv
        nm = ADAM_B1 * m_ref[...] + (1.0 - ADAM_B1) * gv
        nv = ADAM_B2 * v_ref[...] + (1.0 - ADAM_B2) * jnp.square(gv)
        m_hat = nm / (1.0 - ADAM_B1 ** ADAM_STEP)
        v_hat = nv / (1.0 - ADAM_B2 ** ADAM_STEP)
        d_ref[...] = -ADAM_LR * (m_hat / (jnp.sqrt(v_hat) + ADAM_EPS) + ADAM_WD * w_ref[...])
        nm_ref[...] = nm
        nv_ref[...] = nv

    out = jax.ShapeDtypeStruct((r, c), f32)
    return pl.pallas_call(body, grid=(r // tr,), name=name, in_specs=[spec] * 4, out_specs=[spec] * 4,
                          out_shape=[out, out, out, out], compiler_params=_ew_params())(w, g, m, v)


class _Comm(NamedTuple):
    arrays: list
    aliases: dict
    out_shapes: list
    sems: list
    start: Callable
    finish: Callable


class _CarryLayout(NamedTuple):
    in_specs: list
    out_specs: list
    out_shapes: list
    aliases: dict
    sems: list


def _carry_layout(comm, n_in, n_out):
    if comm is None:
        return _CarryLayout([], [], [], {}, [])
    return _CarryLayout([ANY] * len(comm.arrays), [ANY] * len(comm.out_shapes), list(comm.out_shapes),
                        {n_in + i: n_out + o for i, o in comm.aliases.items()}, list(comm.sems))


def _carry_at(comm, grid, edge, ins, outs, sems):
    if comm is None:
        return
    hit = [pl.program_id(ax) == (0 if edge == "first" else extent - 1) for ax, extent in enumerate(grid)]

    @pl.when(functools.reduce(jnp.logical_and, hit))
    def _():
        (comm.start if edge == "first" else comm.finish)(ins, outs, sems)


def _compose(first, second):
    na, no, ns = len(first.arrays), len(first.out_shapes), len(first.sems)

    def both(which):
        def run(ins, outs, sems):
            getattr(first, which)(ins[:na], outs[:no], sems[:ns])
            getattr(second, which)(ins[na:], outs[no:], sems[ns:])
        return run

    aliases = dict(first.aliases) | {na + i: no + o for i, o in second.aliases.items()}
    return _Comm(first.arrays + second.arrays, aliases, first.out_shapes + second.out_shapes, first.sems + second.sems,
                 both("start"), both("finish"))


def _run_comm(comm, name):
    ni, no = len(comm.arrays), len(comm.out_shapes)

    def body(*refs):
        ins, outs, sems = refs[:ni], refs[ni:ni + no], refs[ni + no:]
        comm.start(ins, outs, sems)
        comm.finish(ins, outs, sems)

    return pl.pallas_call(body, name=name, in_specs=[ANY] * ni, out_specs=[ANY] * no, out_shape=comm.out_shapes,
                          input_output_aliases=comm.aliases, scratch_shapes=comm.sems)(*comm.arrays)


def _place():
    return lax.axis_index("x"), lax.axis_index("y"), lax.axis_index("c")


def _chip_peers(x, y):
    out = []
    for fx, fy in ((1, 0), (0, 1), (1, 1)):
        px, py = x ^ fx, y ^ fy
        out.append((px, py, 2 * px + py))
    return out


def _all_gather(arrays, split, cols=None):
    na = len(arrays)

    def part(ref, a, block, which):
        width = slice(None) if cols is None or cols[a] is None else pl.ds(*cols[a])
        if not split[a]:
            return ref.at[block, :, width]
        rows = arrays[a].shape[1] // 2
        return ref.at[block, pl.ds(which * rows, rows), width]

    def ici_sends(outs, sems):
        send_sems, recv_sems = sems
        x, y, c = _place()
        me = 2 * x + y
        out = []
        for a in range(na):
            for j, (px, py, _) in enumerate(_chip_peers(x, y)):
                mine = part(outs[a], a, me, c)
                out.append(pltpu.make_async_remote_copy(mine, mine, send_sems.at[a, j], recv_sems.at[a, j],
                                                        device_id=(px, py, c), device_id_type=MESH))
        return out

    def forwards(outs, sems, which_core):
        send_sems, recv_sems = sems
        x, y, c = _place()
        out = []
        for a in range(na):
            if split[a]:
                for j, (_, _, pid) in enumerate(_chip_peers(x, y)):
                    got = part(outs[a], a, pid, which_core)
                    out.append(pltpu.make_async_remote_copy(got, got, send_sems.at[a, 3 + j], recv_sems.at[a, 3 + j],
                                                            device_id=(x, y, 1 - c), device_id_type=MESH))
        return out

    def arrivals(outs, sems):
        send_sems, recv_sems = sems
        x, y, c = _place()
        out = []
        for a in range(na):
            for j, (px, py, pid) in enumerate(_chip_peers(x, y)):
                got = part(outs[a], a, pid, c)
                out.append(pltpu.make_async_remote_copy(got, got, send_sems.at[a, j], recv_sems.at[a, j],
                                                        device_id=(px, py, c), device_id_type=MESH))
        return out

    def finish(ins, outs, sems):
        c = lax.axis_index("c")
        for cp in arrivals(outs, sems):
            cp.wait_recv()
        mine = forwards(outs, sems, c)
        for cp in mine:
            cp.start()
        for cp in forwards(outs, sems, 1 - c):
            cp.wait_recv()
        for cp in ici_sends(outs, sems) + mine:
            cp.wait_send()

    return _Comm(
        arrays=list(arrays), aliases={a: a for a in range(na)},
        out_shapes=[jax.ShapeDtypeStruct(a.shape, a.dtype) for a in arrays],
        sems=[pltpu.SemaphoreType.DMA((na, 6)), pltpu.SemaphoreType.DMA((na, 6))],
        start=lambda ins, outs, sems: [cp.start() for cp in ici_sends(outs, sems)], finish=finish)


def _swap_halves(arrays):
    na = len(arrays)

    def copies(ins, outs, sems):
        send_sems, recv_sems = sems
        x, y, c = _place()
        out = []
        for a in range(na):
            rows = arrays[a].shape[1] // 2
            src = ins[a].at[:, pl.ds((1 - c) * rows, rows), :]
            out.append(pltpu.make_async_remote_copy(src, outs[a], send_sems.at[a], recv_sems.at[a],
                                                    device_id=(x, y, 1 - c), device_id_type=MESH))
        return out

    return _Comm(
        arrays=list(arrays), aliases={},
        out_shapes=[jax.ShapeDtypeStruct((a.shape[0], a.shape[1] // 2, a.shape[2]), a.dtype) for a in arrays],
        sems=[pltpu.SemaphoreType.DMA((na,)), pltpu.SemaphoreType.DMA((na,))],
        start=lambda ins, outs, sems: [cp.start() for cp in copies(ins, outs, sems)],
        finish=lambda ins, outs, sems: [cp.wait() for cp in copies(ins, outs, sems)])


def _send_to_sibling(arrays):
    na = len(arrays)

    def copies(ins, outs, sems):
        send_sems, recv_sems = sems
        x, y, c = _place()
        return [pltpu.make_async_remote_copy(ins[a], outs[a], send_sems.at[a], recv_sems.at[a],
                                             device_id=(x, y, 1 - c), device_id_type=MESH) for a in range(na)]

    return _Comm(
        arrays=list(arrays), aliases={}, out_shapes=[jax.ShapeDtypeStruct(a.shape, a.dtype) for a in arrays],
        sems=[pltpu.SemaphoreType.DMA((na,)), pltpu.SemaphoreType.DMA((na,))],
        start=lambda ins, outs, sems: [cp.start() for cp in copies(ins, outs, sems)],
        finish=lambda ins, outs, sems: [cp.wait() for cp in copies(ins, outs, sems)])


def _pair_sum(keep, other, name):
    _, r2, c = keep.shape
    tr = max(_row_block(r2, c), 16)
    blk = pl.BlockSpec((None, tr, c), lambda p, i: (p, i, 0))

    def body(a_ref, b_ref, o_ref):
        o_ref[...] = (a_ref[...] + b_ref[...]).astype(o_ref.dtype)

    return pl.pallas_call(body, name=name, grid=(N_CHIPS, r2 // tr), in_specs=[blk, blk], out_specs=blk,
                          out_shape=jax.ShapeDtypeStruct(keep.shape, bf16), compiler_params=_ew_params(2))(keep, other)


def _final_sum_pair(keep, other, remote, name):
    _, r2, c = remote.shape
    tr = max(_row_block(r2, c), 16)
    own = pl.BlockSpec((None, tr, c), lambda i: (_my_chip(), i, 0))
    slot = lambda j: pl.BlockSpec((None, tr, c), lambda i: (j, i, 0))

    def body(a_ref, b_ref, r0_ref, r1_ref, r2_ref, o_ref):
        acc = a_ref[...] + b_ref[...]
        for ref in (r0_ref, r1_ref, r2_ref):
            acc = acc + ref[...].astype(f32)
        o_ref[...] = acc

    return pl.pallas_call(
        body, name=name, grid=(r2 // tr,), in_specs=[own, own, slot(0), slot(1), slot(2)],
        out_specs=pl.BlockSpec((None, tr, c), lambda i: (_my_core(), i, 0)),
        out_shape=jax.ShapeDtypeStruct((2, r2, c), f32), compiler_params=_ew_params())(keep, other, remote, remote, remote)


def _scatter_blocks(arrays):
    na = len(arrays)

    def copies(ins, outs, sems):
        send_sems, recv_sems = sems
        x, y, c = _place()
        out = []
        for a in range(na):
            for j, (px, py, pid) in enumerate(_chip_peers(x, y)):
                out.append(pltpu.make_async_remote_copy(ins[a].at[pid], outs[a].at[j], send_sems.at[a, j],
                                                        recv_sems.at[a, j], device_id=(px, py, c), device_id_type=MESH))
        return out

    return _Comm(
        arrays=list(arrays), aliases={},
        out_shapes=[jax.ShapeDtypeStruct((3,) + a.shape[1:], a.dtype) for a in arrays],
        sems=[pltpu.SemaphoreType.DMA((na, 3)), pltpu.SemaphoreType.DMA((na, 3))],
        start=lambda ins, outs, sems: [cp.start() for cp in copies(ins, outs, sems)],
        finish=lambda ins, outs, sems: [cp.wait() for cp in copies(ins, outs, sems)])


def _join_halves(arrays):
    na = len(arrays)

    def copies(outs, sems, slot):
        send_sems, recv_sems = sems
        x, y, c = _place()
        which = c if slot == "mine" else 1 - c
        return [pltpu.make_async_remote_copy(outs[a].at[which], outs[a].at[which], send_sems.at[a], recv_sems.at[a],
                                             device_id=(x, y, 1 - c), device_id_type=MESH) for a in range(na)]

    def finish(ins, outs, sems):
        for cp in copies(outs, sems, "mine"):
            cp.wait_send()
        for cp in copies(outs, sems, "sibling's"):
            cp.wait_recv()

    return _Comm(
        arrays=list(arrays), aliases={a: a for a in range(na)},
        out_shapes=[jax.ShapeDtypeStruct(a.shape, a.dtype) for a in arrays],
        sems=[pltpu.SemaphoreType.DMA((na,)), pltpu.SemaphoreType.DMA((na,))],
        start=lambda ins, outs, sems: [cp.start() for cp in copies(outs, sems, "mine")], finish=finish)


SMALL_ROWS = 24


def _all_reduce_small(a_lb, dlb, pieces):
    d = a_lb.shape[1]
    npc = len(pieces)

    def body(*refs):
        alb_ref, dlb_ref = refs[:2]
        piece_refs = refs[2:2 + npc]
        out_ref, pack, slots, send_sems, recv_sems = refs[2 + npc:]
        x, y, c = _place()
        me = 4 * x + 2 * y + c
        pack[...] = jnp.zeros_like(pack)
        a = alb_ref[...]
        ex = jnp.exp(a - jnp.max(a, axis=0, keepdims=True))
        p = ex / jnp.sum(ex, axis=0, keepdims=True)
        first = (_iota(p.shape, 0) == 0).astype(f32)
        p0 = jnp.sum(p * first, axis=0, keepdims=True)
        pack[0:3, :] = p * dlb_ref[...] * (first - p0)
        for ref, (arr, r0) in zip(piece_refs, pieces):
            pack[r0:r0 + arr.shape[0], 0:arr.shape[1]] = ref[...]
        slots[me] = pack[...]
        copies = []
        for k in range(1, N_DEV):
            peer = (x ^ (k >> 2), y ^ ((k >> 1) & 1), c ^ (k & 1))
            cp = pltpu.make_async_remote_copy(pack, slots.at[me], send_sems.at[k - 1], recv_sems.at[k - 1],
                                              device_id=peer, device_id_type=MESH)
            cp.start()
            copies.append(cp)
        for k in range(1, N_DEV):
            src = (x ^ (k >> 2)) * 4 + (y ^ ((k >> 1) & 1)) * 2 + (c ^ (k & 1))
            pltpu.make_async_remote_copy(pack, slots.at[src], send_sems.at[k - 1], recv_sems.at[k - 1],
                                         device_id=(x, y, c), device_id_type=MESH).wait_recv()
        for cp in copies:
            cp.wait_send()
        acc = slots[0]
        for i in range(1, N_DEV):
            acc = acc + slots[i]
        out_ref[...] = acc

    return pl.pallas_call(
        body, name="all_reduce_small", in_specs=[VMEM] * (2 + npc), out_specs=VMEM,
        out_shape=jax.ShapeDtypeStruct((SMALL_ROWS, d), f32),
        scratch_shapes=[pltpu.VMEM((SMALL_ROWS, d), f32), pltpu.VMEM((N_DEV, SMALL_ROWS, d), f32),
                        pltpu.SemaphoreType.DMA((N_DEV - 1,)), pltpu.SemaphoreType.DMA((N_DEV - 1,))],
        compiler_params=pltpu.CompilerParams(vmem_limit_bytes=VMEM_LIMIT))(
            a_lb, dlb, *[p[0] for p in pieces])


def _as_halves(g):
    return g.reshape(2 * g.shape[0], g.shape[1] // 2, g.shape[2])


def _chip_sums(grads, from_sibling, tag):
    return [_chip_sum(_as_halves(g), t, f"grad_chip_sum_{tag}{a}") for a, (g, t) in enumerate(zip(grads, from_sibling))]


def _final_sums(grads, from_sibling, from_chips, tag):
    return [_final_sum(_as_halves(g), t, r, f"grad_final_sum_{tag}{a}")
            for a, (g, t, r) in enumerate(zip(grads, from_sibling, from_chips))]


def kernel(x, a_w_in, a_lower_bounds, a_norm_w, a_w_out, b_w_in, b_conv_w, b_a_log, b_dt_bias, b_norm_w, b_w_out, ln_g, ln_b, loss_target, m_a_w_in, m_a_lower_bounds, m_a_norm_w, m_a_w_out, m_b_w_in, m_b_conv_w, m_b_a_log, m_b_dt_bias, m_b_norm_w, m_b_w_out, m_ln_g, m_ln_b, v_a_w_in, v_a_lower_bounds, v_a_norm_w, v_a_w_out, v_b_w_in, v_b_conv_w, v_b_a_log, v_b_dt_bias, v_b_norm_w, v_b_w_out, v_ln_g, v_ln_b):
    _, s, d = x.shape
    h = d // HEAD_DIM
    wb = b_w_in.shape[2]
    x2, target = x[0], loss_target[0]
    chip = 2 * lax.axis_index("x") + lax.axis_index("y")

    e = wb - d
    shard_b = b_w_in[0]
    main_b = lax.dynamic_slice(jnp.pad(shard_b[:, :d], ((0, 0), (N_CHIPS * e, 0))), (0, e * (N_CHIPS - chip)), (d, d))
    tail_b = lax.dynamic_slice(jnp.pad(shard_b, ((0, 0), (0, LANES))), (0, d - e * chip), (d, LANES))
    tail_b = jnp.where(lax.broadcasted_iota(jnp.int32, (1, LANES), 1) < e * (chip + 1), tail_b, 0.0)
    shards = [_cast(a_w_in[0], bf16, "cast_a_w_in", True), _cast(a_w_out[0], bf16, "cast_a_w_out", True),
              _cast(main_b, bf16, "cast_b_w_in", True), _cast(tail_b, bf16, "cast_b_w_in_tail", True),
              _cast(b_w_out[0], bf16, "cast_b_w_out", True), _cast(b_conv_w[0], f32, "copy_b_conv_w", True)]
    sh_a_in, sh_a_out, sh_b_in, sh_b_tail, sh_b_out, sh_conv = shards
    left, right = (0, d // 2), (d // 2, d // 2)
    wa4, conv_g = _run_comm(_all_gather([sh_a_in, sh_conv], [True, False], [left, None]), "all_gather_first")
    conv_w = conv_g.transpose(1, 0, 2).reshape(CONV_WIDTH, 3 * d)

    x_bf = _cast(x2, bf16, "cast_x")
    lb = _lower_bound(a_lower_bounds)
    h4a, (wa4,) = _mm(x_bf, wa4, "nn", name="a_in_fwd_left", n_half=0, carry=_all_gather([wa4], [True], [right]))
    h4a, (wb_main, wb_tail) = _mm(x_bf, wa4, "nn", name="a_in_fwd_right", n_half=1, into=h4a,
                                  carry=_all_gather([sh_b_in, sh_b_tail], [True, True], [left, None]))
    ya, st_a, (wa_out_g, wb_main) = _hgrn2_fwd(h4a, lb, a_norm_w,
                                               carry=_all_gather([sh_a_out, wb_main], [True, True], [None, right]))
    wa_out = wa_out_g.reshape(d, d)
    wb4 = _merge_tails(wb_main, wb_tail)
    wba = wb_tail[N_CHIPS - 1]
    yo_a = _mm(ya, wa_out, "nn", name="a_out_fwd")
    x1, x1_bf = _ln_fwd(x2, yo_a, ln_g[0:1], ln_b[0:1])
    h4b, (wb_out_g,) = _mm(x1_bf, wb4, "nn", name="b_in_fwd", carry=_all_gather([sh_b_out], [True]))
    wb_out = wb_out_g.reshape(d, d)
    ba = _mm(x1_bf, wba, "nn", name="b_gate_fwd")
    c3 = _conv_fwd(h4b, conv_w)
    bl = ba[:, :h].T.reshape(h, s, 1)
    al = ba[:, h:2 * h].T.reshape(h, s, 1)
    alog, dtb = b_a_log.reshape(h, 1, 1), b_dt_bias.reshape(h, 1, 1)
    yb, st_b, inv_b = _gdn_fwd(c3, h4b, bl, al, alog, dtb, b_norm_w)
    yo_b = _mm(yb, wb_out, "nn", name="b_out_fwd")

    dr2, dr2_bf, dg2, db2, loss = _ln_bwd(x1, yo_b, ln_g[1:2], b=ln_b[1:2], target=target, name="ln2_loss_bwd")
    g_wb_out = _mm(yb, dr2_bf, "tn", name="b_out_dw")
    dyb = _mm(dr2_bf, wb_out, "nt", name="b_out_dx")
    dc3, dh4b, dbl, dal, dalog, ddtb, dnw_b = _gdn_bwd(c3, h4b, bl, al, alog, dtb, b_norm_w, st_b, inv_b, dyb)
    dh4b, dconv = _conv_bwd(dc3, h4b, conv_w, dh4b)
    dba = jnp.pad(jnp.concatenate([dbl.reshape(h, s).T, dal.reshape(h, s).T], axis=1), ((0, 0), (0, LANES - 2 * h)))
    g_wb4 = _mm(x1_bf, dh4b, "tn", name="b_in_dw")
    g_wba = _mm(x1_bf, dba, "tn", name="b_in_dw_gate")
    dx1 = _mm(dh4b, wb4, "nt", name="b_in_dx", add=dr2, add_scale=DEEPNORM_ALPHA)
    dx1 = _mm(dba, wba, "nt", name="b_in_dx_gate", add=dx1)
    dr1, dr1_bf, dg1, db1 = _ln_bwd(x2, yo_a, ln_g[0:1], dy=dx1, name="ln1_bwd")
    g_tail = jnp.concatenate([g_wb4[1:, :, :LANES], g_wba[None]], axis=0)
    grads_b = [g_wb4, g_tail, g_wb_out.reshape(N_CHIPS, d // N_CHIPS, d)]
    g_wa_out, sib_b = _mm(ya, dr1_bf, "tn", name="a_out_dw", carry=_swap_halves(grads_b))
    chip_b = _chip_sums(grads_b, sib_b, "b")
    dya = _mm(dr1_bf, wa_out, "nt", name="a_out_dx")
    dh4, dlb, dnw_a, far_b = _hgrn2_bwd(h4a, lb, a_norm_w, st_a, dya, carry=_scatter_blocks(chip_b))
    grads_a_out = [g_wa_out.reshape(N_CHIPS, d // N_CHIPS, d)]
    g_wa4_away = _mm(x_bf, dh4, "tn", name="a_in_dw_away", a_half="other")
    g_wa4_keep, (sib_a_in, sib_a_out) = _mm(x_bf, dh4, "tn", name="a_in_dw_keep", a_half="mine",
                                            carry=_compose(_send_to_sibling([g_wa4_away]), _swap_halves(grads_a_out)))
    chip_a = [_pair_sum(g_wa4_keep, sib_a_in, "grad_chip_sum_a_in")] + _chip_sums(grads_a_out, [sib_a_out], "a_out")
    grad_x, far_a = _mm(dh4, wa4, "nt", name="a_in_dx", add=dr1, add_scale=DEEPNORM_ALPHA, carry=_scatter_blocks(chip_a))

    finals = ([_final_sum_pair(g_wa4_keep, sib_a_in, far_a[0], "grad_final_sum_a_in")]
              + _final_sums(grads_a_out, [sib_a_out], far_a[1:], "a_out") + _final_sums(grads_b, sib_b, far_b, "b"))
    big = [j.reshape(2 * j.shape[1], j.shape[2]) for j in _run_comm(_join_halves(finals), "grad_join_halves")]
    small = _all_reduce_small(a_lower_bounds, dlb, [
        (dg1, 3), (dg2, 4), (db1, 5), (db2, 6), (dnw_a, 7), (dnw_b, 8), (dalog.reshape(1, h), 9), (ddtb.reshape(1, h), 10),
        (loss, 11), (dconv.reshape(3 * CONV_WIDTH, d), 12)])
    loss_out = small[11, 0]
    cw = 3 * d // N_CHIPS
    g_conv = lax.dynamic_slice_in_dim(small[12:24].reshape(CONV_WIDTH, 3 * d), chip * cw, cw, axis=1)
    g_b_in = lax.dynamic_slice(jnp.concatenate([big[2], big[3]], axis=1), (0, e * chip), (d, wb))
    grads = {
        "a_w_in": big[0], "a_lower_bounds": small[0:3], "a_norm_w": small[7:8, :HEAD_DIM], "a_w_out": big[1],
        "b_w_in": g_b_in, "b_conv_w": g_conv, "b_a_log": small[9:10, :h], "b_dt_bias": small[10:11, :h],
        "b_norm_w": small[8:9, :HEAD_DIM], "b_w_out": big[4],
        "ln_g": jnp.concatenate([small[3:4], small[4:5]], axis=0), "ln_b": jnp.concatenate([small[5:6], small[6:7]], axis=0),
    }

    weights = dict(a_w_in=a_w_in, a_lower_bounds=a_lower_bounds, a_norm_w=a_norm_w, a_w_out=a_w_out, b_w_in=b_w_in,
                   b_conv_w=b_conv_w, b_a_log=b_a_log, b_dt_bias=b_dt_bias, b_norm_w=b_norm_w, b_w_out=b_w_out,
                   ln_g=ln_g, ln_b=ln_b)
    m_in = dict(a_w_in=m_a_w_in, a_lower_bounds=m_a_lower_bounds, a_norm_w=m_a_norm_w, a_w_out=m_a_w_out, b_w_in=m_b_w_in,
                b_conv_w=m_b_conv_w, b_a_log=m_b_a_log, b_dt_bias=m_b_dt_bias, b_norm_w=m_b_norm_w, b_w_out=m_b_w_out,
                ln_g=m_ln_g, ln_b=m_ln_b)
    v_in = dict(a_w_in=v_a_w_in, a_lower_bounds=v_a_lower_bounds, a_norm_w=v_a_norm_w, a_w_out=v_a_w_out, b_w_in=v_b_w_in,
                b_conv_w=v_b_conv_w, b_a_log=v_b_a_log, b_dt_bias=v_b_dt_bias, b_norm_w=v_b_norm_w, b_w_out=v_b_w_out,
                ln_g=v_ln_g, ln_b=v_ln_b)
    out_g, out_d, out_m, out_v = [], [], [], []
    for name, w in weights.items():
        shape2 = w.shape[-2:]
        g, delta, nm, nv = _adamw(w.reshape(shape2), grads[name].reshape(shape2), m_in[name].reshape(shape2),
                                  v_in[name].reshape(shape2), "adamw_" + name)
        out_g.append(g.reshape(w.shape))
        out_d.append(delta.reshape(w.shape))
        out_m.append(nm.reshape(w.shape))
        out_v.append(nv.reshape(w.shape))
    return (loss_out, grad_x.reshape(x.shape), *out_g, *out_d, *out_m, *out_v)
```

```python
import functools
import math
from typing import Callable, NamedTuple

import jax
import jax.numpy as jnp
from jax import lax
from jax.experimental import pallas as pl
from jax.experimental.pallas import tpu as pltpu

f32 = jnp.float32
bf16 = jnp.bfloat16
HIGHEST = lax.Precision.HIGHEST

HEAD_DIM = 128
CHUNK = 64
SUB = 16
DEPTH = 2
DEEPNORM_ALPHA = (2.0 * DEPTH) ** 0.25
LN_EPS = 1e-5
RMS_EPS = 1e-6
L2_EPS = 1e-6
ADAM_LR, ADAM_B1, ADAM_B2, ADAM_EPS, ADAM_WD, ADAM_STEP = 0.001, 0.9, 0.999, 1e-08, 0.01, 10
CONV_WIDTH = 4
EXP_CLAMP = 40.0
VMEM_LIMIT = 56 << 20
MXU_DTYPE = bf16
SUBLANES = 8
LANES = 128
N_CHIPS = 4
N_DEV = 8

NN = (((1,), (0,)), ((), ()))
NT = (((1,), (1,)), ((), ()))
TN = (((0,), (0,)), ((), ()))
MESH = pl.DeviceIdType.MESH
ANY = pl.BlockSpec(memory_space=pl.ANY)
VMEM = pl.BlockSpec(memory_space=pltpu.VMEM)


def _bdot(a, b, dn):
    return lax.dot_general(a.astype(MXU_DTYPE), b.astype(MXU_DTYPE), dn, preferred_element_type=f32)


def _hdot(a, b, dn):
    return lax.dot_general(a, b, dn, precision=lax.Precision.HIGH, preferred_element_type=f32)


def _iota(shape, dim):
    return lax.broadcasted_iota(jnp.int32, shape, dim)


def _same_sub(a, b):
    shift = int(math.log2(SUB))
    return jnp.right_shift(a, shift) == jnp.right_shift(b, shift)


def _silu(x):
    return x * jax.nn.sigmoid(x)


def _gated_rmsnorm(o, hz, nw):
    return o * lax.rsqrt(jnp.mean(o * o, axis=-1, keepdims=True) + RMS_EPS) * nw * _silu(hz)


def _hgrn2_chunk(hq, hf, hi, hz, lb, nw, st):
    c = CHUNK
    q = _silu(hq)
    forget = lb + (1.0 - lb) * jax.nn.sigmoid(hf)
    k = 1.0 - forget
    g = jnp.log(forget)
    tri = (_iota((c, c), 1) <= _iota((c, c), 0)).astype(f32)
    cum = _hdot(tri, g, NN)
    total = jnp.sum(g, axis=0, keepdims=True)
    nb = c // SUB
    blk = [cum[SUB * j:SUB * (j + 1)] for j in range(nb)]
    r16 = _iota((SUB, HEAD_DIM), 0)

    def row(j, i):
        return jnp.sum(jnp.where(r16 == i, blk[j], 0.0), axis=0, keepdims=True)

    c15, c31, c47 = row(0, SUB - 1), row(1, SUB - 1), row(2, SUB - 1)
    z = jnp.zeros((SUB, HEAD_DIM), f32)
    e = jnp.exp
    q32 = jnp.concatenate([z, z, e(blk[2] - c31), e(blk[3] - c31)], axis=0)
    k32 = jnp.concatenate([e(c31 - blk[0]), e(c31 - blk[1]), z, z], axis=0)
    q16a = jnp.concatenate([z, e(blk[1] - c15), z, z], axis=0)
    k16a = jnp.concatenate([e(c15 - blk[0]), z, z, z], axis=0)
    q16b = jnp.concatenate([z, z, z, e(blk[3] - c47)], axis=0)
    k16b = jnp.concatenate([z, z, e(c47 - blk[2]), z], axis=0)
    dmid = jnp.concatenate([blk[j] - row(j, SUB // 2 - 1) for j in range(nb)], axis=0)
    qd = e(jnp.minimum(dmid, EXP_CLAMP))
    kd = e(jnp.minimum(-dmid, EXP_CLAMP))
    q2 = jnp.concatenate([q * q32, q * q16a, q * q16b], axis=1)
    k2 = jnp.concatenate([k * k32, k * k16a, k * k16b], axis=1)
    rr, cc = _iota((c, c), 0), _iota((c, c), 1)
    diag = _same_sub(rr, cc) & (cc <= rr)
    att = _bdot(q2, k2, NT) + jnp.where(diag, _bdot(q * qd, k * kd, NT), 0.0)
    o = _bdot(att, hi, NN) + _bdot(q * e(cum), st, NT)
    st_new = st * e(total) + _bdot(hi, k * e(total - cum), TN)
    return _gated_rmsnorm(o, hz, nw), st_new


def _unit_lower_inverse(lower):
    c = CHUNK
    rr, cc = _iota((c, c), 0), _iota((c, c), 1)
    eye = (rr == cc).astype(f32)
    same = _same_sub(rr, cc)
    ld = jnp.where(same, lower, 0.0)
    off = lower - ld
    x = eye - ld
    p = ld
    for _ in range(int(math.log2(SUB)) - 1):
        p = _hdot(p, p, NN)
        x = x + _hdot(x, p, NN)
    n = _hdot(x, off, NN)
    y = eye - n
    p = n
    for _ in range(int(math.log2(c // SUB)) - 1):
        p = _hdot(p, p, NN)
        y = y + _hdot(y, p, NN)
    return _hdot(y, x, NN)


@jax.custom_vjp
def _known_inverse(lower, inv):
    return inv


def _known_inverse_fwd(lower, inv):
    return inv, inv


def _known_inverse_bwd(inv, dinv):
    return -_hdot(_hdot(inv, dinv, TN), inv, NT), jnp.zeros_like(inv)


_known_inverse.defvjp(_known_inverse_fwd, _known_inverse_bwd)


def _gdn_chunk(cq, ck, cv, hz, bl, al, alog, dtb, nw, st, inv=None, with_inverse=False):
    c = CHUNK
    sq, sk, v = _silu(cq), _silu(ck), _silu(cv)
    q = sq * lax.rsqrt(jnp.sum(sq * sq, axis=-1, keepdims=True) + L2_EPS) * (HEAD_DIM ** -0.5)
    k = sk * lax.rsqrt(jnp.sum(sk * sk, axis=-1, keepdims=True) + L2_EPS)
    beta = jax.nn.sigmoid(bl)
    xs = al + dtb
    pos = xs > 0.0
    g = -jnp.exp(alog) * (jnp.where(pos, xs, 0.0) + jnp.log(1.0 + jnp.exp(jnp.where(pos, -xs, xs))))
    rr, cc = _iota((c, c), 0), _iota((c, c), 1)
    tri = (cc <= rr).astype(f32)
    gc = jnp.broadcast_to(g, (c, c))
    cum_c = _hdot(tri, gc, NN)
    cum_r = _hdot(gc, (rr <= cc).astype(f32), TN)
    g128 = jnp.broadcast_to(g, (c, HEAD_DIM))
    cum = _hdot(tri, g128, NN)
    total = jnp.sum(g128, axis=0, keepdims=True)
    diff = cum_c - cum_r
    ninf = -jnp.inf
    lower = beta * _bdot(k, k, NT) * jnp.exp(jnp.where(cc < rr, diff, ninf))
    rhs = jnp.concatenate([beta * v, (beta * jnp.exp(cum)) * k], axis=1)
    minv = _unit_lower_inverse(lower) if inv is None else _known_inverse(lower, inv)
    sol = _hdot(minv, rhs, NN)
    u0, w = sol[:, :HEAD_DIM], sol[:, HEAD_DIM:]
    qk = _bdot(q, k, NT) * jnp.exp(jnp.where(cc <= rr, diff, ninf))
    u = u0 - _bdot(w, st, NN)
    o = _bdot(q * jnp.exp(cum), st, NN) + _bdot(qk, u, NN)
    st_new = st * jnp.exp(total) + _bdot(k * jnp.exp(total - cum), u, TN)
    y = _gated_rmsnorm(o, hz, nw)
    return (y, st_new, minv) if with_inverse else (y, st_new)


def _rows(c):
    return pl.ds(pl.multiple_of(c * CHUNK, CHUNK), CHUNK)


HEADS_PER_STEP = 8
CHUNKS_PER_STEP = 4


def _scan_dims(s, d):
    h, n = d // HEAD_DIM, s // CHUNK
    ncb = min(n, CHUNKS_PER_STEP)
    hp = min(h, HEADS_PER_STEP)
    return h, n, ncb, n // ncb, ncb * CHUNK, hp


def _scan_specs(t, nblk, ncb, hp, reverse):
    blk = (lambda j: nblk - 1 - j) if reverse else (lambda j: j)
    w = hp * HEAD_DIM
    grp = lambda g: pl.BlockSpec((g, t, w), lambda h, j: (0, blk(j), h))
    one = lambda g: pl.BlockSpec((None, t, w), lambda h, j: (g, blk(j), h))
    col = lambda: pl.BlockSpec((t, w), lambda h, j: (blk(j), h))
    vec = lambda: pl.BlockSpec((1, w), lambda h, j: (0, h))
    shared = lambda: pl.BlockSpec((1, HEAD_DIM), lambda h, j: (0, 0))
    state = lambda: pl.BlockSpec((hp, ncb, HEAD_DIM, HEAD_DIM), lambda h, j: (h, blk(j), 0, 0))
    inverse = lambda: pl.BlockSpec((hp, ncb, CHUNK, CHUNK), lambda h, j: (h, blk(j), 0, 0))
    tok = lambda: pl.BlockSpec((hp, t, 1), lambda h, j: (h, blk(j), 0))
    par = lambda: pl.BlockSpec((hp, 1, 1), lambda h, j: (h, 0, 0))
    return dict(grp=grp, one=one, col=col, vec=vec, shared=shared, state=state, inverse=inverse, tok=tok, par=par)


def _scan_params():
    return pltpu.CompilerParams(dimension_semantics=("arbitrary", "arbitrary"), vmem_limit_bytes=VMEM_LIMIT)


def _state_scratch(hp):
    return [pltpu.VMEM((hp, HEAD_DIM, HEAD_DIM), f32)]


def _lanes(i):
    return pl.ds(i * HEAD_DIM, HEAD_DIM)


def _stack_heads(per_head):
    return tuple(jnp.stack(vals) for vals in zip(*per_head))


def _hgrn2_fwd(h4, lb, nw, carry=None):
    _, s, d = h4.shape
    h, n, ncb, nblk, t, hp = _scan_dims(s, d)
    sp = _scan_specs(t, nblk, ncb, hp, False)
    grid = (h // hp, nblk)
    ni, no = (len(carry.arrays), len(carry.out_shapes)) if carry else (0, 0)

    def body(*refs):
        h_ref, lb_ref, nw_ref = refs[:3]
        ins = refs[3:3 + ni]
        y_ref, st_ref = refs[3 + ni:5 + ni]
        outs = refs[5 + ni:5 + ni + no]
        st_scr = refs[5 + ni + no]
        sems = refs[6 + ni + no:]
        _carry_at(carry, grid, "first", ins, outs, sems)

        @pl.when(pl.program_id(1) == 0)
        def _():
            st_scr[...] = jnp.zeros_like(st_scr)

        def step(c, carry):
            r = _rows(c)
            args = [(h_ref[0, r, _lanes(i)], h_ref[1, r, _lanes(i)], h_ref[2, r, _lanes(i)], h_ref[3, r, _lanes(i)],
                     lb_ref[:, _lanes(i)], nw_ref[...], st_scr[i]) for i in range(hp)]
            y, st_new = jax.vmap(_hgrn2_chunk)(*_stack_heads(args))
            for i, a in enumerate(args):
                st_ref[i, c] = a[-1]
                y_ref[r, _lanes(i)] = y[i].astype(y_ref.dtype)
                st_scr[i] = st_new[i]
            return carry

        lax.fori_loop(0, ncb, step, 0)
        _carry_at(carry, grid, "last", ins, outs, sems)

    extra = _carry_layout(carry, 3, 2)
    res = pl.pallas_call(
        body, grid=grid, name="hgrn2_fwd",
        in_specs=[sp["grp"](4), sp["vec"](), sp["shared"]()] + extra.in_specs,
        out_specs=[sp["col"](), sp["state"]()] + extra.out_specs,
        out_shape=[jax.ShapeDtypeStruct((s, d), bf16), jax.ShapeDtypeStruct((h, n, HEAD_DIM, HEAD_DIM), f32)]
        + extra.out_shapes,
        input_output_aliases=extra.aliases,
        scratch_shapes=_state_scratch(hp) + extra.sems, compiler_params=_scan_params())(
            h4, lb, nw, *(carry.arrays if carry else []))
    return (*res[:2], res[2:]) if carry else res


def _hgrn2_bwd(h4, lb, nw, states, dy, carry=None):
    _, s, d = h4.shape
    h, n, ncb, nblk, t, hp = _scan_dims(s, d)
    sp = _scan_specs(t, nblk, ncb, hp, True)
    grid = (h // hp, nblk)
    ni, no = (len(carry.arrays), len(carry.out_shapes)) if carry else (0, 0)

    def body(*refs):
        h_ref, lb_ref, nw_ref, st_ref, dy_ref = refs[:5]
        ins = refs[5:5 + ni]
        dh_ref, dlb_ref, dnw_ref = refs[5 + ni:8 + ni]
        outs = refs[8 + ni:8 + ni + no]
        dst_scr = refs[8 + ni + no]
        sems = refs[9 + ni + no:]
        first_block = pl.program_id(1) == 0
        _carry_at(carry, grid, "first", ins, outs, sems)

        @pl.when(first_block)
        def _():
            dst_scr[...] = jnp.zeros_like(dst_scr)
            dlb_ref[...] = jnp.zeros_like(dlb_ref)

        @pl.when(first_block & (pl.program_id(0) == 0))
        def _():
            dnw_ref[...] = jnp.zeros_like(dnw_ref)

        def step(k, carry):
            c = ncb - 1 - k
            r = _rows(c)
            args = [(h_ref[0, r, _lanes(i)], h_ref[1, r, _lanes(i)], h_ref[2, r, _lanes(i)], h_ref[3, r, _lanes(i)],
                     lb_ref[:, _lanes(i)], nw_ref[...], st_ref[i, c]) for i in range(hp)]
            cots = [(dy_ref[r, _lanes(i)].astype(f32), dst_scr[i]) for i in range(hp)]
            dq, df, di, dz, dlb, dnw, dst = jax.vjp(jax.vmap(_hgrn2_chunk), *_stack_heads(args))[1](_stack_heads(cots))
            dnw_sum = dnw_ref[...]
            for i in range(hp):
                for g, val in enumerate((dq, df, di, dz)):
                    dh_ref[g, r, _lanes(i)] = val[i].astype(dh_ref.dtype)
                dlb_ref[:, _lanes(i)] += dlb[i]
                dnw_sum = dnw_sum + dnw[i]
                dst_scr[i] = dst[i]
            dnw_ref[...] = dnw_sum
            return carry

        lax.fori_loop(0, ncb, step, 0)
        _carry_at(carry, grid, "last", ins, outs, sems)

    extra = _carry_layout(carry, 5, 3)
    res = pl.pallas_call(
        body, grid=grid, name="hgrn2_bwd",
        in_specs=[sp["grp"](4), sp["vec"](), sp["shared"](), sp["state"](), sp["col"]()] + extra.in_specs,
        out_specs=[sp["grp"](4), sp["vec"](), sp["shared"]()] + extra.out_specs,
        out_shape=[jax.ShapeDtypeStruct((4, s, d), bf16), jax.ShapeDtypeStruct((1, d), f32),
                   jax.ShapeDtypeStruct((1, HEAD_DIM), f32)] + extra.out_shapes,
        input_output_aliases=extra.aliases,
        scratch_shapes=_state_scratch(hp) + extra.sems, compiler_params=_scan_params())(
            h4, lb, nw, states, dy, *(carry.arrays if carry else []))
    return (*res[:3], res[3:]) if carry else res


def _gdn_fwd(c3, h4, bl, al, alog, dtb, nw):
    _, s, d = c3.shape
    h, n, ncb, nblk, t, hp = _scan_dims(s, d)
    sp = _scan_specs(t, nblk, ncb, hp, False)

    def body(c_ref, hz_ref, bl_ref, al_ref, alog_ref, dtb_ref, nw_ref, y_ref, st_ref, inv_ref, st_scr):
        @pl.when(pl.program_id(1) == 0)
        def _():
            st_scr[...] = jnp.zeros_like(st_scr)

        def step(c, carry):
            r = _rows(c)
            args = [(c_ref[0, r, _lanes(i)], c_ref[1, r, _lanes(i)], c_ref[2, r, _lanes(i)], hz_ref[r, _lanes(i)],
                     bl_ref[i, r, :], al_ref[i, r, :], alog_ref[i], dtb_ref[i], nw_ref[...], st_scr[i]) for i in range(hp)]
            y, st_new, inv = jax.vmap(functools.partial(_gdn_chunk, with_inverse=True))(*_stack_heads(args))
            for i, a in enumerate(args):
                st_ref[i, c] = a[-1]
                inv_ref[i, c] = inv[i]
                y_ref[r, _lanes(i)] = y[i].astype(y_ref.dtype)
                st_scr[i] = st_new[i]
            return carry

        lax.fori_loop(0, ncb, step, 0)

    return pl.pallas_call(
        body, grid=(h // hp, nblk), name="gdn_fwd",
        in_specs=[sp["grp"](3), sp["one"](3), sp["tok"](), sp["tok"](), sp["par"](), sp["par"](), sp["shared"]()],
        out_specs=[sp["col"](), sp["state"](), sp["inverse"]()],
        out_shape=[jax.ShapeDtypeStruct((s, d), bf16), jax.ShapeDtypeStruct((h, n, HEAD_DIM, HEAD_DIM), f32),
                   jax.ShapeDtypeStruct((h, n, CHUNK, CHUNK), f32)],
        scratch_shapes=_state_scratch(hp), compiler_params=_scan_params())(c3, h4, bl, al, alog, dtb, nw)


def _gdn_bwd(c3, h4, bl, al, alog, dtb, nw, states, inverses, dy):
    _, s, d = c3.shape
    h, n, ncb, nblk, t, hp = _scan_dims(s, d)
    sp = _scan_specs(t, nblk, ncb, hp, True)

    def chunk(*a):
        return _gdn_chunk(*a[:-1], inv=a[-1])

    def body(c_ref, hz_ref, bl_ref, al_ref, alog_ref, dtb_ref, nw_ref, st_ref, inv_ref, dy_ref,
             dc_ref, dz_ref, dbl_ref, dal_ref, dalog_ref, ddtb_ref, dnw_ref, dst_scr):
        first_block = pl.program_id(1) == 0

        @pl.when(first_block)
        def _():
            dst_scr[...] = jnp.zeros_like(dst_scr)
            dalog_ref[...] = jnp.zeros_like(dalog_ref)
            ddtb_ref[...] = jnp.zeros_like(ddtb_ref)

        @pl.when(first_block & (pl.program_id(0) == 0))
        def _():
            dnw_ref[...] = jnp.zeros_like(dnw_ref)

        def step(k, carry):
            c = ncb - 1 - k
            r = _rows(c)
            args = [(c_ref[0, r, _lanes(i)], c_ref[1, r, _lanes(i)], c_ref[2, r, _lanes(i)], hz_ref[r, _lanes(i)],
                     bl_ref[i, r, :], al_ref[i, r, :], alog_ref[i], dtb_ref[i], nw_ref[...], st_ref[i, c], inv_ref[i, c])
                    for i in range(hp)]
            cots = [(dy_ref[r, _lanes(i)].astype(f32), dst_scr[i]) for i in range(hp)]
            dq, dk, dv, dz, dbl, dal, dalog, ddtb, dnw, dst, _ = jax.vjp(
                jax.vmap(chunk), *_stack_heads(args))[1](_stack_heads(cots))
            dnw_sum = dnw_ref[...]
            for i in range(hp):
                for g, val in enumerate((dq, dk, dv)):
                    dc_ref[g, r, _lanes(i)] = val[i]
                dz_ref[r, _lanes(i)] = dz[i].astype(dz_ref.dtype)
                dbl_ref[i, r, :] = dbl[i]
                dal_ref[i, r, :] = dal[i]
                dalog_ref[i] += dalog[i]
                ddtb_ref[i] += ddtb[i]
                dnw_sum = dnw_sum + dnw[i]
                dst_scr[i] = dst[i]
            dnw_ref[...] = dnw_sum
            return carry

        lax.fori_loop(0, ncb, step, 0)

    tokens = jax.ShapeDtypeStruct((h, s, 1), f32)
    heads = jax.ShapeDtypeStruct((h, 1, 1), f32)
    return pl.pallas_call(
        body, grid=(h // hp, nblk), name="gdn_bwd",
        in_specs=[sp["grp"](3), sp["one"](3), sp["tok"](), sp["tok"](), sp["par"](), sp["par"](), sp["shared"](),
                  sp["state"](), sp["inverse"](), sp["col"]()],
        out_specs=[sp["grp"](3), sp["one"](3), sp["tok"](), sp["tok"](), sp["par"](), sp["par"](), sp["shared"]()],
        out_shape=[jax.ShapeDtypeStruct((3, s, d), f32), jax.ShapeDtypeStruct((4, s, d), bf16), tokens, tokens, heads, heads,
                   jax.ShapeDtypeStruct((1, HEAD_DIM), f32)],
        scratch_shapes=_state_scratch(hp), compiler_params=_scan_params())(
            c3, h4, bl, al, alog, dtb, nw, states, inverses, dy)


def _tile(n, pref):
    return n if n <= pref else pref


def _mm(a, b, mode, *, name, out_dtype=f32, add=None, add_scale=1.0, b_g0=0, groups=None, carry=None, a_half=None,
        n_half=None, into=None):
    squeeze = a.ndim == 2 and b.ndim == 2
    a3 = a if a.ndim == 3 else a[None]
    b3 = b if b.ndim == 3 else b[None]
    if mode == "nt":
        g = groups or a3.shape[0]
        (_, m, r), (_, n, _) = a3.shape, b3.shape
    elif mode == "nn":
        g = groups or b3.shape[0]
        (_, m, r), (_, _, n) = a3.shape, b3.shape
        if n_half is not None:
            n = n // 2
    else:
        g = groups or b3.shape[0]
        (_, r, m), (_, _, n) = a3.shape, b3.shape
        if a_half is not None:
            m = m // 2
    tm, tn, tr = _tile(m, 1024), _tile(n, 1024), _tile(r, 2048)
    nr = r // tr
    a_grouped = a3.shape[0] > 1
    dn = {"nn": NN, "nt": NT, "tn": TN}[mode]

    if mode == "nt":
        grid = (m // tm, n // tn, g * nr)
        last = g * nr - 1
        kax = 2
        a_spec = pl.BlockSpec((None, tm, tr), lambda i, j, k: (k // nr, i, k % nr))
        b_spec = pl.BlockSpec((None, tn, tr), lambda i, j, k: (b_g0 + k // nr, j, k % nr))
        o_spec = pl.BlockSpec((tm, tn), lambda i, j, k: (i, j))
        add_spec = pl.BlockSpec((tm, tn), lambda i, j, k: (i, j))
        out_shape = jax.ShapeDtypeStruct((m, n), out_dtype)
        sem = ("parallel", "parallel", "arbitrary")
    else:
        grid = (g, m // tm, n // tn, nr)
        last = nr - 1
        kax = 3
        ag = (lambda q: q) if a_grouped else (lambda q: 0)
        if mode == "nn":
            a_spec = pl.BlockSpec((None, tm, tr), lambda q, i, j, k: (ag(q), i, k))
        else:
            def col0():
                if a_half is None:
                    return 0
                return (_my_core() if a_half == "mine" else 1 - _my_core()) * (m // tm)
            a_spec = pl.BlockSpec((None, tr, tm), lambda q, i, j, k: (ag(q), k, col0() + i))
        off = 0 if n_half is None else n_half * (n // tn)
        b_spec = pl.BlockSpec((None, tr, tn), lambda q, i, j, k: (b_g0 + q, k, off + j))
        o_spec = pl.BlockSpec((None, tm, tn), lambda q, i, j, k: (q, i, off + j))
        add_spec = ANY
        out_shape = jax.ShapeDtypeStruct((g, m, n if n_half is None else 2 * n), out_dtype)
        sem = ("parallel", "parallel", "parallel", "arbitrary")
        assert add is None
        add = into

    n_main = 2 if add is None else 3
    ni = len(carry.arrays) if carry else 0
    no = len(carry.out_shapes) if carry else 0

    def body(*refs):
        a_ref, b_ref = refs[:2]
        add_ref = None if add is None else refs[2]
        ins = refs[n_main:n_main + ni]
        o_ref = refs[n_main + ni]
        outs = refs[n_main + ni + 1:n_main + ni + 1 + no]
        acc = refs[n_main + ni + 1 + no]
        sems = refs[n_main + ni + 2 + no:]
        k = pl.program_id(kax)
        _carry_at(carry, grid, "first", ins, outs, sems)

        @pl.when(k == 0)
        def _():
            acc[...] = jnp.zeros_like(acc)

        acc[...] += _bdot(a_ref[...], b_ref[...], dn)

        @pl.when(k == last)
        def _():
            res = acc[...]
            if add is not None and mode == "nt":
                res = res + add_scale * add_ref[...]
            o_ref[...] = res.astype(o_ref.dtype)

        _carry_at(carry, grid, "last", ins, outs, sems)

    operands = [a3, b3] + ([] if add is None else [add])
    in_specs = [a_spec, b_spec] + ([] if add is None else [add_spec])
    extra = _carry_layout(carry, n_main, 1)
    if carry:
        sem = ("arbitrary",) * len(grid)
    out, *moved = pl.pallas_call(
        body, grid=grid, name=name, in_specs=in_specs + extra.in_specs, out_specs=[o_spec] + extra.out_specs,
        out_shape=[out_shape] + extra.out_shapes,
        input_output_aliases=extra.aliases | ({2: 0} if into is not None else {}),
        scratch_shapes=[pltpu.VMEM((tm, tn), f32)] + extra.sems,
        compiler_params=pltpu.CompilerParams(dimension_semantics=sem, vmem_limit_bytes=VMEM_LIMIT))(
            *operands, *(carry.arrays if carry else []))
    out = out[0] if (squeeze and mode != "nt") else out
    return (out, moved) if carry else out


def _row_block(rows, cols, target_bytes=1 << 20):
    if rows <= SUBLANES:
        return rows
    tr = SUBLANES
    while tr * 2 <= rows and tr * 2 * cols * 4 <= target_bytes and rows % (tr * 2) == 0:
        tr *= 2
    return tr


def _ew_params(n_axes=1):
    return pltpu.CompilerParams(dimension_semantics=("parallel",) * n_axes, vmem_limit_bytes=VMEM_LIMIT)


def _cast(x, dtype, name, into_chip_slot=False):
    r, c = x.shape
    tr = _row_block(r, c, 2 << 20)
    tr = max(tr, 16) if r >= 16 else tr
    spec = pl.BlockSpec((tr, c), lambda i: (i, 0))
    if into_chip_slot:
        out_spec = pl.BlockSpec((None, tr, c), lambda i: (_my_chip(), i, 0))
        out_shape = jax.ShapeDtypeStruct((N_CHIPS, r, c), dtype)
    else:
        out_spec, out_shape = spec, jax.ShapeDtypeStruct((r, c), dtype)

    def body(x_ref, o_ref):
        o_ref[...] = x_ref[...].astype(dtype)

    return pl.pallas_call(body, grid=(r // tr,), name=name, in_specs=[spec], out_specs=out_spec,
                          out_shape=out_shape, compiler_params=_ew_params())(x)


def _merge_tails(main, tail):
    _, d, _ = main.shape
    tr = max(_row_block(d, LANES), 16)
    head = pl.BlockSpec((None, tr, LANES), lambda g, i: (g + 1, i, 0))
    rest = pl.BlockSpec((None, tr, LANES), lambda g, i: (g, i, 0))

    def body(m_ref, t_ref, o_ref):
        o_ref[...] = m_ref[...] + t_ref[...]

    return pl.pallas_call(body, grid=(N_CHIPS - 1, d // tr), name="merge_shard_tails", in_specs=[head, rest],
                          out_specs=head, out_shape=jax.ShapeDtypeStruct(main.shape, main.dtype),
                          input_output_aliases={0: 0}, compiler_params=_ew_params(2))(main, tail)


def _ln_stats(r):
    mu = jnp.mean(r, axis=-1, keepdims=True)
    var = jnp.mean(jnp.square(r - mu), axis=-1, keepdims=True)
    return mu, lax.rsqrt(var + LN_EPS)


def _ln_fwd(x, yo, g, b):
    s, d = x.shape
    tr = _row_block(s, d, 2 << 20)
    tr = max(tr, 16)
    big = pl.BlockSpec((tr, d), lambda i: (i, 0))
    vec = pl.BlockSpec((1, d), lambda i: (0, 0))

    def body(x_ref, yo_ref, g_ref, b_ref, o_ref, ob_ref):
        r = DEEPNORM_ALPHA * x_ref[...] + yo_ref[...]
        mu, rstd = _ln_stats(r)
        y = (r - mu) * rstd * g_ref[...] + b_ref[...]
        o_ref[...] = y
        ob_ref[...] = y.astype(bf16)

    return pl.pallas_call(
        body, grid=(s // tr,), name="ln_fwd", in_specs=[big, big, vec, vec], out_specs=[big, big],
        out_shape=[jax.ShapeDtypeStruct((s, d), f32), jax.ShapeDtypeStruct((s, d), bf16)],
        compiler_params=_ew_params())(x, yo, g, b)


def _ln_bwd(x, yo, g, *, dy=None, b=None, target=None, name):
    s, d = x.shape
    with_loss = target is not None
    tr = max(_row_block(s, d, 2 << 20), 16)
    big = pl.BlockSpec((tr, d), lambda i: (i, 0))
    vec = pl.BlockSpec((1, d), lambda i: (0, 0))
    lane = pl.BlockSpec((1, LANES), lambda i: (0, 0))

    def body(*refs):
        if with_loss:
            x_ref, yo_ref, g_ref, b_ref, t_ref, dr_ref, drb_ref, dg_ref, db_ref, loss_ref = refs
        else:
            x_ref, yo_ref, g_ref, dy_ref, dr_ref, drb_ref, dg_ref, db_ref = refs

        @pl.when(pl.program_id(0) == 0)
        def _():
            dg_ref[...] = jnp.zeros_like(dg_ref)
            db_ref[...] = jnp.zeros_like(db_ref)
            if with_loss:
                loss_ref[...] = jnp.zeros_like(loss_ref)

        r = DEEPNORM_ALPHA * x_ref[...] + yo_ref[...]
        mu, rstd = _ln_stats(r)
        xhat = (r - mu) * rstd
        if with_loss:
            err = xhat * g_ref[...] + b_ref[...] - t_ref[...]
            loss_ref[...] += jnp.sum(err * err) * (0.5 / d)
            dyv = err * (1.0 / d)
        else:
            dyv = dy_ref[...]
        dxhat = dyv * g_ref[...]
        m1 = jnp.mean(dxhat, axis=-1, keepdims=True)
        m2 = jnp.mean(dxhat * xhat, axis=-1, keepdims=True)
        dr = rstd * (dxhat - m1 - xhat * m2)
        dr_ref[...] = dr
        drb_ref[...] = dr.astype(bf16)
        dg_ref[...] += jnp.sum(dyv * xhat, axis=0, keepdims=True)
        db_ref[...] += jnp.sum(dyv, axis=0, keepdims=True)

    operands = (x, yo, g, b, target) if with_loss else (x, yo, g, dy)
    in_specs = [big, big, vec, vec, big] if with_loss else [big, big, vec, big]
    out_specs = [big, big, vec, vec] + ([lane] if with_loss else [])
    out_shape = [jax.ShapeDtypeStruct((s, d), f32), jax.ShapeDtypeStruct((s, d), bf16),
                 jax.ShapeDtypeStruct((1, d), f32), jax.ShapeDtypeStruct((1, d), f32)]
    if with_loss:
        out_shape.append(jax.ShapeDtypeStruct((1, LANES), f32))
    return pl.pallas_call(
        body, grid=(s // tr,), name=name, in_specs=in_specs, out_specs=out_specs, out_shape=out_shape,
        compiler_params=pltpu.CompilerParams(dimension_semantics=("arbitrary",), vmem_limit_bytes=VMEM_LIMIT))(*operands)


def _lower_bound(a_lb):
    _, d = a_lb.shape

    def body(a_ref, o_ref):
        a = a_ref[...]
        ex = jnp.exp(a - jnp.max(a, axis=0, keepdims=True))
        p = ex / jnp.sum(ex, axis=0, keepdims=True)
        o_ref[...] = jnp.sum(jnp.where(_iota(p.shape, 0) == 0, p, 0.0), axis=0, keepdims=True)

    return pl.pallas_call(body, name="lower_bound", in_specs=[VMEM], out_specs=VMEM,
                          out_shape=jax.ShapeDtypeStruct((1, d), f32))(a_lb)


def _shift_rows(ext, k, rows):
    return pltpu.roll(ext, k, axis=0)[SUBLANES:SUBLANES + rows]


def _conv_tiles(s, d):
    return _tile(s, 512), _tile(d, 512)


def _conv_fwd(h4, w):
    _, s, d = h4.shape
    tr, tc = _conv_tiles(s, d)
    nj = d // tc
    hb = tr // SUBLANES
    cur = pl.BlockSpec((None, tr, tc), lambda g, j, i: (g, i, j))
    prev = pl.BlockSpec((None, SUBLANES, tc), lambda g, j, i: (g, jnp.maximum(i * hb - 1, 0), j))
    wsp = pl.BlockSpec((CONV_WIDTH, tc), lambda g, j, i: (0, g * nj + j))

    def body(x_ref, p_ref, w_ref, o_ref):
        x = x_ref[...]
        halo = jnp.where(pl.program_id(2) == 0, 0.0, p_ref[...])
        ext = jnp.concatenate([halo, x], axis=0)
        acc = w_ref[CONV_WIDTH - 1:CONV_WIDTH, :] * x
        for k in range(1, CONV_WIDTH):
            acc = acc + w_ref[CONV_WIDTH - 1 - k:CONV_WIDTH - k, :] * _shift_rows(ext, k, tr)
        o_ref[...] = acc

    return pl.pallas_call(
        body, grid=(3, nj, s // tr), name="conv_fwd", in_specs=[cur, prev, wsp], out_specs=cur,
        out_shape=jax.ShapeDtypeStruct((3, s, d), f32), compiler_params=_ew_params(3))(h4, h4, w)


def _conv_bwd(dy3, h4, w, dh4):
    _, s, d = dy3.shape
    tr, tc = _conv_tiles(s, d)
    nj = d // tc
    hb = tr // SUBLANES
    ni = s // tr
    cur = pl.BlockSpec((None, tr, tc), lambda g, j, i: (g, i, j))
    prev = pl.BlockSpec((None, SUBLANES, tc), lambda g, j, i: (g, jnp.maximum(i * hb - 1, 0), j))
    nxt = pl.BlockSpec((None, SUBLANES, tc), lambda g, j, i: (g, jnp.minimum((i + 1) * hb, s // SUBLANES - 1), j))
    wsp = pl.BlockSpec((CONV_WIDTH, tc), lambda g, j, i: (0, g * nj + j))

    def body(dy_ref, dn_ref, x_ref, p_ref, w_ref, _, dx_ref, dw_ref):
        i = pl.program_id(2)

        @pl.when(i == 0)
        def _():
            dw_ref[...] = jnp.zeros_like(dw_ref)

        dy = dy_ref[...]
        x = x_ref[...]
        tail = jnp.where(i == ni - 1, 0.0, dn_ref[...])
        dext = jnp.concatenate([dy, tail], axis=0)
        halo = jnp.where(i == 0, 0.0, p_ref[...])
        xext = jnp.concatenate([halo, x], axis=0)
        acc = w_ref[CONV_WIDTH - 1:CONV_WIDTH, :] * dy
        dw_ref[CONV_WIDTH - 1:CONV_WIDTH, :] += jnp.sum(dy * x, axis=0, keepdims=True)
        for k in range(1, CONV_WIDTH):
            ahead = pltpu.roll(dext, tr + SUBLANES - k, axis=0)[:tr]
            acc = acc + w_ref[CONV_WIDTH - 1 - k:CONV_WIDTH - k, :] * ahead
            dw_ref[CONV_WIDTH - 1 - k:CONV_WIDTH - k, :] += jnp.sum(dy * _shift_rows(xext, k, tr), axis=0, keepdims=True)
        dx_ref[...] = acc.astype(dx_ref.dtype)

    return pl.pallas_call(
        body, grid=(3, nj, ni), name="conv_bwd", in_specs=[cur, nxt, cur, prev, wsp, ANY], out_specs=[cur, wsp],
        out_shape=[jax.ShapeDtypeStruct((4, s, d), bf16), jax.ShapeDtypeStruct((CONV_WIDTH, 3 * d), f32)],
        input_output_aliases={5: 0},
        compiler_params=pltpu.CompilerParams(dimension_semantics=("parallel", "parallel", "arbitrary"),
                                             vmem_limit_bytes=VMEM_LIMIT))(dy3, dy3, h4, h4, w, dh4)


def _my_core():
    return lax.axis_index("c")


def _my_chip():
    return 2 * lax.axis_index("x") + lax.axis_index("y")


def _chip_sum(halves, other, name):
    _, r2, c = other.shape
    tr = max(_row_block(r2, c), 16)
    own = pl.BlockSpec((None, tr, c), lambda p, i: (2 * p + _my_core(), i, 0))
    blk = pl.BlockSpec((None, tr, c), lambda p, i: (p, i, 0))

    def body(a_ref, b_ref, o_ref):
        o_ref[...] = (a_ref[...] + b_ref[...]).astype(o_ref.dtype)

    return pl.pallas_call(
        body, name=name, grid=(N_CHIPS, r2 // tr), in_specs=[own, blk], out_specs=blk,
        out_shape=jax.ShapeDtypeStruct(other.shape, bf16), compiler_params=_ew_params(2))(halves, other)


def _final_sum(halves, other, remote, name):
    _, r2, c = remote.shape
    tr = max(_row_block(r2, c), 16)
    own = pl.BlockSpec((None, tr, c), lambda i: (2 * _my_chip() + _my_core(), i, 0))
    sib = pl.BlockSpec((None, tr, c), lambda i: (_my_chip(), i, 0))
    slot = lambda j: pl.BlockSpec((None, tr, c), lambda i: (j, i, 0))

    def body(a_ref, b_ref, r0_ref, r1_ref, r2_ref, o_ref):
        acc = a_ref[...] + b_ref[...]
        for ref in (r0_ref, r1_ref, r2_ref):
            acc = acc + ref[...].astype(f32)
        o_ref[...] = acc

    return pl.pallas_call(
        body, name=name, grid=(r2 // tr,), in_specs=[own, sib, slot(0), slot(1), slot(2)],
        out_specs=pl.BlockSpec((None, tr, c), lambda i: (_my_core(), i, 0)),
        out_shape=jax.ShapeDtypeStruct((2, r2, c), f32), compiler_params=_ew_params())(halves, other, remote, remote, remote)


def _adamw(w, g, m, v, name):
    r, c = w.shape
    tr = _row_block(r, c, 2 << 20)
    spec = pl.BlockSpec((tr, c), lambda i: (i, 0))

    def body(w_ref, g_ref, m_ref, v_ref, go_ref, d_ref, nm_ref, nv_ref):
        gv = g_ref[...]
        go_ref[...] = gv
        nm = ADAM_B1 * m_ref[...] + (1.0 - ADAM_B1) * gv
        nv = ADAM_B2 * v_ref[...] + (1.0 - ADAM_B2) * jnp.square(gv)
        m_hat = nm / (1.0 - ADAM_B1 ** ADAM_STEP)
        v_hat = nv / (1.0 - ADAM_B2 ** ADAM_STEP)
        d_ref[...] = -ADAM_LR * (m_hat / (jnp.sqrt(v_hat) + ADAM_EPS) + ADAM_WD * w_ref[...])
        nm_ref[...] = nm
        nv_ref[...] = nv

    out = jax.ShapeDtypeStruct((r, c), f32)
    return pl.pallas_call(body, grid=(r // tr,), name=name, in_specs=[spec] * 4, out_specs=[spec] * 4,
                          out_shape=[out, out, out, out], compiler_params=_ew_params())(w, g, m, v)


class _Comm(NamedTuple):
    arrays: list
    aliases: dict
    out_shapes: list
    sems: list
    start: Callable
    finish: Callable


class _CarryLayout(NamedTuple):
    in_specs: list
    out_specs: list
    out_shapes: list
    aliases: dict
    sems: list


def _carry_layout(comm, n_in, n_out):
    if comm is None:
        return _CarryLayout([], [], [], {}, [])
    return _CarryLayout([ANY] * len(comm.arrays), [ANY] * len(comm.out_shapes), list(comm.out_shapes),
                        {n_in + i: n_out + o for i, o in comm.aliases.items()}, list(comm.sems))


def _carry_at(comm, grid, edge, ins, outs, sems):
    if comm is None:
        return
    hit = [pl.program_id(ax) == (0 if edge == "first" else extent - 1) for ax, extent in enumerate(grid)]

    @pl.when(functools.reduce(jnp.logical_and, hit))
    def _():
        (comm.start if edge == "first" else comm.finish)(ins, outs, sems)


def _compose(first, second):
    na, no, ns = len(first.arrays), len(first.out_shapes), len(first.sems)

    def both(which):
        def run(ins, outs, sems):
            getattr(first, which)(ins[:na], outs[:no], sems[:ns])
            getattr(second, which)(ins[na:], outs[no:], sems[ns:])
        return run

    aliases = dict(first.aliases) | {na + i: no + o for i, o in second.aliases.items()}
    return _Comm(first.arrays + second.arrays, aliases, first.out_shapes + second.out_shapes, first.sems + second.sems,
                 both("start"), both("finish"))


def _run_comm(comm, name):
    ni, no = len(comm.arrays), len(comm.out_shapes)

    def body(*refs):
        ins, outs, sems = refs[:ni], refs[ni:ni + no], refs[ni + no:]
        comm.start(ins, outs, sems)
        comm.finish(ins, outs, sems)

    return pl.pallas_call(body, name=name, in_specs=[ANY] * ni, out_specs=[ANY] * no, out_shape=comm.out_shapes,
                          input_output_aliases=comm.aliases, scratch_shapes=comm.sems)(*comm.arrays)


def _place():
    return lax.axis_index("x"), lax.axis_index("y"), lax.axis_index("c")


def _chip_peers(x, y):
    out = []
    for fx, fy in ((1, 0), (0, 1), (1, 1)):
        px, py = x ^ fx, y ^ fy
        out.append((px, py, 2 * px + py))
    return out


def _all_gather(arrays, split, cols=None):
    na = len(arrays)

    def part(ref, a, block, which):
        width = slice(None) if cols is None or cols[a] is None else pl.ds(*cols[a])
        if not split[a]:
            return ref.at[block, :, width]
        rows = arrays[a].shape[1] // 2
        return ref.at[block, pl.ds(which * rows, rows), width]

    def ici_sends(outs, sems):
        send_sems, recv_sems = sems
        x, y, c = _place()
        me = 2 * x + y
        out = []
        for a in range(na):
            for j, (px, py, _) in enumerate(_chip_peers(x, y)):
                mine = part(outs[a], a, me, c)
                out.append(pltpu.make_async_remote_copy(mine, mine, send_sems.at[a, j], recv_sems.at[a, j],
                                                        device_id=(px, py, c), device_id_type=MESH))
        return out

    def forwards(outs, sems, which_core):
        send_sems, recv_sems = sems
        x, y, c = _place()
        out = []
        for a in range(na):
            if split[a]:
                for j, (_, _, pid) in enumerate(_chip_peers(x, y)):
                    got = part(outs[a], a, pid, which_core)
                    out.append(pltpu.make_async_remote_copy(got, got, send_sems.at[a, 3 + j], recv_sems.at[a, 3 + j],
                                                            device_id=(x, y, 1 - c), device_id_type=MESH))
        return out

    def arrivals(outs, sems):
        send_sems, recv_sems = sems
        x, y, c = _place()
        out = []
        for a in range(na):
            for j, (px, py, pid) in enumerate(_chip_peers(x, y)):
                got = part(outs[a], a, pid, c)
                out.append(pltpu.make_async_remote_copy(got, got, send_sems.at[a, j], recv_sems.at[a, j],
                                                        device_id=(px, py, c), device_id_type=MESH))
        return out

    def finish(ins, outs, sems):
        c = lax.axis_index("c")
        for cp in arrivals(outs, sems):
            cp.wait_recv()
        mine = forwards(outs, sems, c)
        for cp in mine:
            cp.start()
        for cp in forwards(outs, sems, 1 - c):
            cp.wait_recv()
        for cp in ici_sends(outs, sems) + mine:
            cp.wait_send()

    return _Comm(
        arrays=list(arrays), aliases={a: a for a in range(na)},
        out_shapes=[jax.ShapeDtypeStruct(a.shape, a.dtype) for a in arrays],
        sems=[pltpu.SemaphoreType.DMA((na, 6)), pltpu.SemaphoreType.DMA((na, 6))],
        start=lambda ins, outs, sems: [cp.start() for cp in ici_sends(outs, sems)], finish=finish)


def _swap_halves(arrays):
    na = len(arrays)

    def copies(ins, outs, sems):
        send_sems, recv_sems = sems
        x, y, c = _place()
        out = []
        for a in range(na):
            rows = arrays[a].shape[1] // 2
            src = ins[a].at[:, pl.ds((1 - c) * rows, rows), :]
            out.append(pltpu.make_async_remote_copy(src, outs[a], send_sems.at[a], recv_sems.at[a],
                                                    device_id=(x, y, 1 - c), device_id_type=MESH))
        return out

    return _Comm(
        arrays=list(arrays), aliases={},
        out_shapes=[jax.ShapeDtypeStruct((a.shape[0], a.shape[1] // 2, a.shape[2]), a.dtype) for a in arrays],
        sems=[pltpu.SemaphoreType.DMA((na,)), pltpu.SemaphoreType.DMA((na,))],
        start=lambda ins, outs, sems: [cp.start() for cp in copies(ins, outs, sems)],
        finish=lambda ins, outs, sems: [cp.wait() for cp in copies(ins, outs, sems)])


def _send_to_sibling(arrays):
    na = len(arrays)

    def copies(ins, outs, sems):
        send_sems, recv_sems = sems
        x, y, c = _place()
        return [pltpu.make_async_remote_copy(ins[a], outs[a], send_sems.at[a], recv_sems.at[a],
                                             device_id=(x, y, 1 - c), device_id_type=MESH) for a in range(na)]

    return _Comm(
        arrays=list(arrays), aliases={}, out_shapes=[jax.ShapeDtypeStruct(a.shape, a.dtype) for a in arrays],
        sems=[pltpu.SemaphoreType.DMA((na,)), pltpu.SemaphoreType.DMA((na,))],
        start=lambda ins, outs, sems: [cp.start() for cp in copies(ins, outs, sems)],
        finish=lambda ins, outs, sems: [cp.wait() for cp in copies(ins, outs, sems)])


def _pair_sum(keep, other, name):
    _, r2, c = keep.shape
    tr = max(_row_block(r2, c), 16)
    blk = pl.BlockSpec((None, tr, c), lambda p, i: (p, i, 0))

    def body(a_ref, b_ref, o_ref):
        o_ref[...] = (a_ref[...] + b_ref[...]).astype(o_ref.dtype)

    return pl.pallas_call(body, name=name, grid=(N_CHIPS, r2 // tr), in_specs=[blk, blk], out_specs=blk,
                          out_shape=jax.ShapeDtypeStruct(keep.shape, bf16), compiler_params=_ew_params(2))(keep, other)


def _final_sum_pair(keep, other, remote, name):
    _, r2, c = remote.shape
    tr = max(_row_block(r2, c), 16)
    own = pl.BlockSpec((None, tr, c), lambda i: (_my_chip(), i, 0))
    slot = lambda j: pl.BlockSpec((None, tr, c), lambda i: (j, i, 0))

    def body(a_ref, b_ref, r0_ref, r1_ref, r2_ref, o_ref):
        acc = a_ref[...] + b_ref[...]
        for ref in (r0_ref, r1_ref, r2_ref):
            acc = acc + ref[...].astype(f32)
        o_ref[...] = acc

    return pl.pallas_call(
        body, name=name, grid=(r2 // tr,), in_specs=[own, own, slot(0), slot(1), slot(2)],
        out_specs=pl.BlockSpec((None, tr, c), lambda i: (_my_core(), i, 0)),
        out_shape=jax.ShapeDtypeStruct((2, r2, c), f32), compiler_params=_ew_params())(keep, other, remote, remote, remote)


def _scatter_blocks(arrays):
    na = len(arrays)

    def copies(ins, outs, sems):
        send_sems, recv_sems = sems
        x, y, c = _place()
        out = []
        for a in range(na):
            for j, (px, py, pid) in enumerate(_chip_peers(x, y)):
                out.append(pltpu.make_async_remote_copy(ins[a].at[pid], outs[a].at[j], send_sems.at[a, j],
                                                        recv_sems.at[a, j], device_id=(px, py, c), device_id_type=MESH))
        return out

    return _Comm(
        arrays=list(arrays), aliases={},
        out_shapes=[jax.ShapeDtypeStruct((3,) + a.shape[1:], a.dtype) for a in arrays],
        sems=[pltpu.SemaphoreType.DMA((na, 3)), pltpu.SemaphoreType.DMA((na, 3))],
        start=lambda ins, outs, sems: [cp.start() for cp in copies(ins, outs, sems)],
        finish=lambda ins, outs, sems: [cp.wait() for cp in copies(ins, outs, sems)])


def _join_halves(arrays):
    na = len(arrays)

    def copies(outs, sems, slot):
        send_sems, recv_sems = sems
        x, y, c = _place()
        which = c if slot == "mine" else 1 - c
        return [pltpu.make_async_remote_copy(outs[a].at[which], outs[a].at[which], send_sems.at[a], recv_sems.at[a],
                                             device_id=(x, y, 1 - c), device_id_type=MESH) for a in range(na)]

    def finish(ins, outs, sems):
        for cp in copies(outs, sems, "mine"):
            cp.wait_send()
        for cp in copies(outs, sems, "sibling's"):
            cp.wait_recv()

    return _Comm(
        arrays=list(arrays), aliases={a: a for a in range(na)},
        out_shapes=[jax.ShapeDtypeStruct(a.shape, a.dtype) for a in arrays],
        sems=[pltpu.SemaphoreType.DMA((na,)), pltpu.SemaphoreType.DMA((na,))],
        start=lambda ins, outs, sems: [cp.start() for cp in copies(outs, sems, "mine")], finish=finish)


SMALL_ROWS = 24


def _all_reduce_small(a_lb, dlb, pieces):
    d = a_lb.shape[1]
    npc = len(pieces)

    def body(*refs):
        alb_ref, dlb_ref = refs[:2]
        piece_refs = refs[2:2 + npc]
        out_ref, pack, slots, send_sems, recv_sems = refs[2 + npc:]
        x, y, c = _place()
        me = 4 * x + 2 * y + c
        pack[...] = jnp.zeros_like(pack)
        a = alb_ref[...]
        ex = jnp.exp(a - jnp.max(a, axis=0, keepdims=True))
        p = ex / jnp.sum(ex, axis=0, keepdims=True)
        first = (_iota(p.shape, 0) == 0).astype(f32)
        p0 = jnp.sum(p * first, axis=0, keepdims=True)
        pack[0:3, :] = p * dlb_ref[...] * (first - p0)
        for ref, (arr, r0) in zip(piece_refs, pieces):
            pack[r0:r0 + arr.shape[0], 0:arr.shape[1]] = ref[...]
        slots[me] = pack[...]
        copies = []
        for k in range(1, N_DEV):
            peer = (x ^ (k >> 2), y ^ ((k >> 1) & 1), c ^ (k & 1))
            cp = pltpu.make_async_remote_copy(pack, slots.at[me], send_sems.at[k - 1], recv_sems.at[k - 1],
                                              device_id=peer, device_id_type=MESH)
            cp.start()
            copies.append(cp)
        for k in range(1, N_DEV):
            src = (x ^ (k >> 2)) * 4 + (y ^ ((k >> 1) & 1)) * 2 + (c ^ (k & 1))
            pltpu.make_async_remote_copy(pack, slots.at[src], send_sems.at[k - 1], recv_sems.at[k - 1],
                                         device_id=(x, y, c), device_id_type=MESH).wait_recv()
        for cp in copies:
            cp.wait_send()
        acc = slots[0]
        for i in range(1, N_DEV):
            acc = acc + slots[i]
        out_ref[...] = acc

    return pl.pallas_call(
        body, name="all_reduce_small", in_specs=[VMEM] * (2 + npc), out_specs=VMEM,
        out_shape=jax.ShapeDtypeStruct((SMALL_ROWS, d), f32),
        scratch_shapes=[pltpu.VMEM((SMALL_ROWS, d), f32), pltpu.VMEM((N_DEV, SMALL_ROWS, d), f32),
                        pltpu.SemaphoreType.DMA((N_DEV - 1,)), pltpu.SemaphoreType.DMA((N_DEV - 1,))],
        compiler_params=pltpu.CompilerParams(vmem_limit_bytes=VMEM_LIMIT))(
            a_lb, dlb, *[p[0] for p in pieces])


def _as_halves(g):
    return g.reshape(2 * g.shape[0], g.shape[1] // 2, g.shape[2])


def _chip_sums(grads, from_sibling, tag):
    return [_chip_sum(_as_halves(g), t, f"grad_chip_sum_{tag}{a}") for a, (g, t) in enumerate(zip(grads, from_sibling))]


def _final_sums(grads, from_sibling, from_chips, tag):
    return [_final_sum(_as_halves(g), t, r, f"grad_final_sum_{tag}{a}")
            for a, (g, t, r) in enumerate(zip(grads, from_sibling, from_chips))]


def kernel(x, a_w_in, a_lower_bounds, a_norm_w, a_w_out, b_w_in, b_conv_w, b_a_log, b_dt_bias, b_norm_w, b_w_out, ln_g, ln_b, loss_target, m_a_w_in, m_a_lower_bounds, m_a_norm_w, m_a_w_out, m_b_w_in, m_b_conv_w, m_b_a_log, m_b_dt_bias, m_b_norm_w, m_b_w_out, m_ln_g, m_ln_b, v_a_w_in, v_a_lower_bounds, v_a_norm_w, v_a_w_out, v_b_w_in, v_b_conv_w, v_b_a_log, v_b_dt_bias, v_b_norm_w, v_b_w_out, v_ln_g, v_ln_b):
    _, s, d = x.shape
    h = d // HEAD_DIM
    wb = b_w_in.shape[2]
    x2, target = x[0], loss_target[0]
    chip = 2 * lax.axis_index("x") + lax.axis_index("y")

    e = wb - d
    shard_b = b_w_in[0]
    main_b = lax.dynamic_slice(jnp.pad(shard_b[:, :d], ((0, 0), (N_CHIPS * e, 0))), (0, e * (N_CHIPS - chip)), (d, d))
    tail_b = lax.dynamic_slice(jnp.pad(shard_b, ((0, 0), (0, LANES))), (0, d - e * chip), (d, LANES))
    tail_b = jnp.where(lax.broadcasted_iota(jnp.int32, (1, LANES), 1) < e * (chip + 1), tail_b, 0.0)
    shards = [_cast(a_w_in[0], bf16, "cast_a_w_in", True), _cast(a_w_out[0], bf16, "cast_a_w_out", True),
              _cast(main_b, bf16, "cast_b_w_in", True), _cast(tail_b, bf16, "cast_b_w_in_tail", True),
              _cast(b_w_out[0], bf16, "cast_b_w_out", True), _cast(b_conv_w[0], f32, "copy_b_conv_w", True)]
    sh_a_in, sh_a_out, sh_b_in, sh_b_tail, sh_b_out, sh_conv = shards
    left, right = (0, d // 2), (d // 2, d // 2)
    wa4, conv_g = _run_comm(_all_gather([sh_a_in, sh_conv], [True, False], [left, None]), "all_gather_first")
    conv_w = conv_g.transpose(1, 0, 2).reshape(CONV_WIDTH, 3 * d)

    x_bf = _cast(x2, bf16, "cast_x")
    lb = _lower_bound(a_lower_bounds)
    h4a, (wa4,) = _mm(x_bf, wa4, "nn", name="a_in_fwd_left", n_half=0, carry=_all_gather([wa4], [True], [right]))
    h4a, (wb_main, wb_tail) = _mm(x_bf, wa4, "nn", name="a_in_fwd_right", n_half=1, into=h4a,
                                  carry=_all_gather([sh_b_in, sh_b_tail], [True, True], [left, None]))
    third, fourth = (d // 2, d // 4), (3 * d // 4, d // 4)
    ya, st_a, (wa_out_g, wb_main) = _hgrn2_fwd(h4a, lb, a_norm_w,
                                               carry=_all_gather([sh_a_out, wb_main], [True, True], [None, third]))
    wa_out = wa_out_g.reshape(d, d)
    yo_a, (wb_main,) = _mm(ya, wa_out, "nn", name="a_out_fwd", carry=_all_gather([wb_main], [True], [fourth]))
    wb4 = _merge_tails(wb_main, wb_tail)
    wba = wb_tail[N_CHIPS - 1]
    x1, x1_bf = _ln_fwd(x2, yo_a, ln_g[0:1], ln_b[0:1])
    h4b, (wb_out_g,) = _mm(x1_bf, wb4, "nn", name="b_in_fwd", carry=_all_gather([sh_b_out], [True]))
    wb_out = wb_out_g.reshape(d, d)
    ba = _mm(x1_bf, wba, "nn", name="b_gate_fwd")
    c3 = _conv_fwd(h4b, conv_w)
    bl = ba[:, :h].T.reshape(h, s, 1)
    al = ba[:, h:2 * h].T.reshape(h, s, 1)
    alog, dtb = b_a_log.reshape(h, 1, 1), b_dt_bias.reshape(h, 1, 1)
    yb, st_b, inv_b = _gdn_fwd(c3, h4b, bl, al, alog, dtb, b_norm_w)
    yo_b = _mm(yb, wb_out, "nn", name="b_out_fwd")

    dr2, dr2_bf, dg2, db2, loss = _ln_bwd(x1, yo_b, ln_g[1:2], b=ln_b[1:2], target=target, name="ln2_loss_bwd")
    g_wb_out = _mm(yb, dr2_bf, "tn", name="b_out_dw")
    dyb = _mm(dr2_bf, wb_out, "nt", name="b_out_dx")
    dc3, dh4b, dbl, dal, dalog, ddtb, dnw_b = _gdn_bwd(c3, h4b, bl, al, alog, dtb, b_norm_w, st_b, inv_b, dyb)
    dh4b, dconv = _conv_bwd(dc3, h4b, conv_w, dh4b)
    dba = jnp.pad(jnp.concatenate([dbl.reshape(h, s).T, dal.reshape(h, s).T], axis=1), ((0, 0), (0, LANES - 2 * h)))
    g_wb4 = _mm(x1_bf, dh4b, "tn", name="b_in_dw")
    g_wba = _mm(x1_bf, dba, "tn", name="b_in_dw_gate")
    dx1 = _mm(dh4b, wb4, "nt", name="b_in_dx", add=dr2, add_scale=DEEPNORM_ALPHA)
    dx1 = _mm(dba, wba, "nt", name="b_in_dx_gate", add=dx1)
    dr1, dr1_bf, dg1, db1 = _ln_bwd(x2, yo_a, ln_g[0:1], dy=dx1, name="ln1_bwd")
    g_tail = jnp.concatenate([g_wb4[1:, :, :LANES], g_wba[None]], axis=0)
    grads_b = [g_wb4, g_tail, g_wb_out.reshape(N_CHIPS, d // N_CHIPS, d)]
    g_wa_out, sib_b = _mm(ya, dr1_bf, "tn", name="a_out_dw", carry=_swap_halves(grads_b))
    chip_b = _chip_sums(grads_b, sib_b, "b")
    dya = _mm(dr1_bf, wa_out, "nt", name="a_out_dx")
    dh4, dlb, dnw_a, far_b = _hgrn2_bwd(h4a, lb, a_norm_w, st_a, dya, carry=_scatter_blocks(chip_b))
    grads_a_out = [g_wa_out.reshape(N_CHIPS, d // N_CHIPS, d)]
    g_wa4_away = _mm(x_bf, dh4, "tn", name="a_in_dw_away", a_half="other")
    g_wa4_keep, (sib_a_in, sib_a_out) = _mm(x_bf, dh4, "tn", name="a_in_dw_keep", a_half="mine",
                                            carry=_compose(_send_to_sibling([g_wa4_away]), _swap_halves(grads_a_out)))
    chip_a = [_pair_sum(g_wa4_keep, sib_a_in, "grad_chip_sum_a_in")] + _chip_sums(grads_a_out, [sib_a_out], "a_out")
    grad_x, far_a = _mm(dh4, wa4, "nt", name="a_in_dx", add=dr1, add_scale=DEEPNORM_ALPHA, carry=_scatter_blocks(chip_a))

    finals = ([_final_sum_pair(g_wa4_keep, sib_a_in, far_a[0], "grad_final_sum_a_in")]
              + _final_sums(grads_a_out, [sib_a_out], far_a[1:], "a_out") + _final_sums(grads_b, sib_b, far_b, "b"))
    big = [j.reshape(2 * j.shape[1], j.shape[2]) for j in _run_comm(_join_halves(finals), "grad_join_halves")]
    small = _all_reduce_small(a_lower_bounds, dlb, [
        (dg1, 3), (dg2, 4), (db1, 5), (db2, 6), (dnw_a, 7), (dnw_b, 8), (dalog.reshape(1, h), 9), (ddtb.reshape(1, h), 10),
        (loss, 11), (dconv.reshape(3 * CONV_WIDTH, d), 12)])
    loss_out = small[11, 0]
    cw = 3 * d // N_CHIPS
    g_conv = lax.dynamic_slice_in_dim(small[12:24].reshape(CONV_WIDTH, 3 * d), chip * cw, cw, axis=1)
    g_b_in = lax.dynamic_slice(jnp.concatenate([big[2], big[3]], axis=1), (0, e * chip), (d, wb))
    grads = {
        "a_w_in": big[0], "a_lower_bounds": small[0:3], "a_norm_w": small[7:8, :HEAD_DIM], "a_w_out": big[1],
        "b_w_in": g_b_in, "b_conv_w": g_conv, "b_a_log": small[9:10, :h], "b_dt_bias": small[10:11, :h],
        "b_norm_w": small[8:9, :HEAD_DIM], "b_w_out": big[4],
        "ln_g": jnp.concatenate([small[3:4], small[4:5]], axis=0), "ln_b": jnp.concatenate([small[5:6], small[6:7]], axis=0),
    }

    weights = dict(a_w_in=a_w_in, a_lower_bounds=a_lower_bounds, a_norm_w=a_norm_w, a_w_out=a_w_out, b_w_in=b_w_in,
                   b_conv_w=b_conv_w, b_a_log=b_a_log, b_dt_bias=b_dt_bias, b_norm_w=b_norm_w, b_w_out=b_w_out,
                   ln_g=ln_g, ln_b=ln_b)
    m_in = dict(a_w_in=m_a_w_in, a_lower_bounds=m_a_lower_bounds, a_norm_w=m_a_norm_w, a_w_out=m_a_w_out, b_w_in=m_b_w_in,
                b_conv_w=m_b_conv_w, b_a_log=m_b_a_log, b_dt_bias=m_b_dt_bias, b_norm_w=m_b_norm_w, b_w_out=m_b_w_out,
                ln_g=m_ln_g, ln_b=m_ln_b)
    v_in = dict(a_w_in=v_a_w_in, a_lower_bounds=v_a_lower_bounds, a_norm_w=v_a_norm_w, a_w_out=v_a_w_out, b_w_in=v_b_w_in,
                b_conv_w=v_b_conv_w, b_a_log=v_b_a_log, b_dt_bias=v_b_dt_bias, b_norm_w=v_b_norm_w, b_w_out=v_b_w_out,
                ln_g=v_ln_g, ln_b=v_ln_b)
    out_g, out_d, out_m, out_v = [], [], [], []
    for name, w in weights.items():
        shape2 = w.shape[-2:]
        g, delta, nm, nv = _adamw(w.reshape(shape2), grads[name].reshape(shape2), m_in[name].reshape(shape2),
                                  v_in[name].reshape(shape2), "adamw_" + name)
        out_g.append(g.reshape(w.shape))
        out_d.append(delta.reshape(w.shape))
        out_m.append(nm.reshape(w.shape))
        out_v.append(nv.reshape(w.shape))
    return (loss_out, grad_x.reshape(x.shape), *out_g, *out_d, *out_m, *out_v)
```

```python
import functools
import math
from typing import Callable, NamedTuple

import jax
import jax.numpy as jnp
from jax import lax
from jax.experimental import pallas as pl
from jax.experimental.pallas import tpu as pltpu

f32 = jnp.float32
bf16 = jnp.bfloat16
HIGHEST = lax.Precision.HIGHEST

HEAD_DIM = 128
CHUNK = 64
SUB = 16
DEPTH = 2
DEEPNORM_ALPHA = (2.0 * DEPTH) ** 0.25
LN_EPS = 1e-5
RMS_EPS = 1e-6
L2_EPS = 1e-6
ADAM_LR, ADAM_B1, ADAM_B2, ADAM_EPS, ADAM_WD, ADAM_STEP = 0.001, 0.9, 0.999, 1e-08, 0.01, 10
CONV_WIDTH = 4
EXP_CLAMP = 40.0
VMEM_LIMIT = 56 << 20
MXU_DTYPE = bf16
SUBLANES = 8
LANES = 128
N_CHIPS = 4
N_DEV = 8

NN = (((1,), (0,)), ((), ()))
NT = (((1,), (1,)), ((), ()))
TN = (((0,), (0,)), ((), ()))
MESH = pl.DeviceIdType.MESH
ANY = pl.BlockSpec(memory_space=pl.ANY)
VMEM = pl.BlockSpec(memory_space=pltpu.VMEM)


def _bdot(a, b, dn):
    return lax.dot_general(a.astype(MXU_DTYPE), b.astype(MXU_DTYPE), dn, preferred_element_type=f32)


def _hdot(a, b, dn):
    return lax.dot_general(a, b, dn, precision=lax.Precision.HIGH, preferred_element_type=f32)


def _iota(shape, dim):
    return lax.broadcasted_iota(jnp.int32, shape, dim)


def _same_sub(a, b):
    shift = int(math.log2(SUB))
    return jnp.right_shift(a, shift) == jnp.right_shift(b, shift)


def _silu(x):
    return x * jax.nn.sigmoid(x)


def _gated_rmsnorm(o, hz, nw):
    return o * lax.rsqrt(jnp.mean(o * o, axis=-1, keepdims=True) + RMS_EPS) * nw * _silu(hz)


def _hgrn2_chunk(hq, hf, hi, hz, lb, nw, st):
    c = CHUNK
    q = _silu(hq)
    forget = lb + (1.0 - lb) * jax.nn.sigmoid(hf)
    k = 1.0 - forget
    g = jnp.log(forget)
    tri = (_iota((c, c), 1) <= _iota((c, c), 0)).astype(f32)
    cum = _hdot(tri, g, NN)
    total = jnp.sum(g, axis=0, keepdims=True)
    nb = c // SUB
    blk = [cum[SUB * j:SUB * (j + 1)] for j in range(nb)]
    r16 = _iota((SUB, HEAD_DIM), 0)

    def row(j, i):
        return jnp.sum(jnp.where(r16 == i, blk[j], 0.0), axis=0, keepdims=True)

    c15, c31, c47 = row(0, SUB - 1), row(1, SUB - 1), row(2, SUB - 1)
    z = jnp.zeros((SUB, HEAD_DIM), f32)
    e = jnp.exp
    q32 = jnp.concatenate([z, z, e(blk[2] - c31), e(blk[3] - c31)], axis=0)
    k32 = jnp.concatenate([e(c31 - blk[0]), e(c31 - blk[1]), z, z], axis=0)
    q16a = jnp.concatenate([z, e(blk[1] - c15), z, z], axis=0)
    k16a = jnp.concatenate([e(c15 - blk[0]), z, z, z], axis=0)
    q16b = jnp.concatenate([z, z, z, e(blk[3] - c47)], axis=0)
    k16b = jnp.concatenate([z, z, e(c47 - blk[2]), z], axis=0)
    dmid = jnp.concatenate([blk[j] - row(j, SUB // 2 - 1) for j in range(nb)], axis=0)
    qd = e(jnp.minimum(dmid, EXP_CLAMP))
    kd = e(jnp.minimum(-dmid, EXP_CLAMP))
    q2 = jnp.concatenate([q * q32, q * q16a, q * q16b], axis=1)
    k2 = jnp.concatenate([k * k32, k * k16a, k * k16b], axis=1)
    rr, cc = _iota((c, c), 0), _iota((c, c), 1)
    diag = _same_sub(rr, cc) & (cc <= rr)
    att = _bdot(q2, k2, NT) + jnp.where(diag, _bdot(q * qd, k * kd, NT), 0.0)
    o = _bdot(att, hi, NN) + _bdot(q * e(cum), st, NT)
    st_new = st * e(total) + _bdot(hi, k * e(total - cum), TN)
    return _gated_rmsnorm(o, hz, nw), st_new


def _unit_lower_inverse(lower):
    c = CHUNK
    rr, cc = _iota((c, c), 0), _iota((c, c), 1)
    eye = (rr == cc).astype(f32)
    same = _same_sub(rr, cc)
    ld = jnp.where(same, lower, 0.0)
    off = lower - ld
    x = eye - ld
    p = ld
    for _ in range(int(math.log2(SUB)) - 1):
        p = _hdot(p, p, NN)
        x = x + _hdot(x, p, NN)
    n = _hdot(x, off, NN)
    y = eye - n
    p = n
    for _ in range(int(math.log2(c // SUB)) - 1):
        p = _hdot(p, p, NN)
        y = y + _hdot(y, p, NN)
    return _hdot(y, x, NN)


@jax.custom_vjp
def _known_inverse(lower, inv):
    return inv


def _known_inverse_fwd(lower, inv):
    return inv, inv


def _known_inverse_bwd(inv, dinv):
    return -_hdot(_hdot(inv, dinv, TN), inv, NT), jnp.zeros_like(inv)


_known_inverse.defvjp(_known_inverse_fwd, _known_inverse_bwd)


def _gdn_chunk(cq, ck, cv, hz, bl, al, alog, dtb, nw, st, inv=None, with_inverse=False):
    c = CHUNK
    sq, sk, v = _silu(cq), _silu(ck), _silu(cv)
    q = sq * lax.rsqrt(jnp.sum(sq * sq, axis=-1, keepdims=True) + L2_EPS) * (HEAD_DIM ** -0.5)
    k = sk * lax.rsqrt(jnp.sum(sk * sk, axis=-1, keepdims=True) + L2_EPS)
    beta = jax.nn.sigmoid(bl)
    xs = al + dtb
    pos = xs > 0.0
    g = -jnp.exp(alog) * (jnp.where(pos, xs, 0.0) + jnp.log(1.0 + jnp.exp(jnp.where(pos, -xs, xs))))
    rr, cc = _iota((c, c), 0), _iota((c, c), 1)
    tri = (cc <= rr).astype(f32)
    gc = jnp.broadcast_to(g, (c, c))
    cum_c = _hdot(tri, gc, NN)
    cum_r = _hdot(gc, (rr <= cc).astype(f32), TN)
    g128 = jnp.broadcast_to(g, (c, HEAD_DIM))
    cum = _hdot(tri, g128, NN)
    total = jnp.sum(g128, axis=0, keepdims=True)
    diff = cum_c - cum_r
    ninf = -jnp.inf
    lower = beta * _bdot(k, k, NT) * jnp.exp(jnp.where(cc < rr, diff, ninf))
    rhs = jnp.concatenate([beta * v, (beta * jnp.exp(cum)) * k], axis=1)
    minv = _unit_lower_inverse(lower) if inv is None else _known_inverse(lower, inv)
    sol = _hdot(minv, rhs, NN)
    u0, w = sol[:, :HEAD_DIM], sol[:, HEAD_DIM:]
    qk = _bdot(q, k, NT) * jnp.exp(jnp.where(cc <= rr, diff, ninf))
    u = u0 - _bdot(w, st, NN)
    o = _bdot(q * jnp.exp(cum), st, NN) + _bdot(qk, u, NN)
    st_new = st * jnp.exp(total) + _bdot(k * jnp.exp(total - cum), u, TN)
    y = _gated_rmsnorm(o, hz, nw)
    return (y, st_new, minv) if with_inverse else (y, st_new)


def _rows(c):
    return pl.ds(pl.multiple_of(c * CHUNK, CHUNK), CHUNK)


HEADS_PER_STEP = 8
CHUNKS_PER_STEP = 4


def _scan_dims(s, d):
    h, n = d // HEAD_DIM, s // CHUNK
    ncb = min(n, CHUNKS_PER_STEP)
    hp = min(h, HEADS_PER_STEP)
    return h, n, ncb, n // ncb, ncb * CHUNK, hp


def _scan_specs(t, nblk, ncb, hp, reverse):
    blk = (lambda j: nblk - 1 - j) if reverse else (lambda j: j)
    w = hp * HEAD_DIM
    grp = lambda g: pl.BlockSpec((g, t, w), lambda h, j: (0, blk(j), h))
    one = lambda g: pl.BlockSpec((None, t, w), lambda h, j: (g, blk(j), h))
    col = lambda: pl.BlockSpec((t, w), lambda h, j: (blk(j), h))
    vec = lambda: pl.BlockSpec((1, w), lambda h, j: (0, h))
    shared = lambda: pl.BlockSpec((1, HEAD_DIM), lambda h, j: (0, 0))
    state = lambda: pl.BlockSpec((hp, ncb, HEAD_DIM, HEAD_DIM), lambda h, j: (h, blk(j), 0, 0))
    inverse = lambda: pl.BlockSpec((hp, ncb, CHUNK, CHUNK), lambda h, j: (h, blk(j), 0, 0))
    tok = lambda: pl.BlockSpec((hp, t, 1), lambda h, j: (h, blk(j), 0))
    par = lambda: pl.BlockSpec((hp, 1, 1), lambda h, j: (h, 0, 0))
    return dict(grp=grp, one=one, col=col, vec=vec, shared=shared, state=state, inverse=inverse, tok=tok, par=par)


def _scan_params():
    return pltpu.CompilerParams(dimension_semantics=("arbitrary", "arbitrary"), vmem_limit_bytes=VMEM_LIMIT)


def _state_scratch(hp):
    return [pltpu.VMEM((hp, HEAD_DIM, HEAD_DIM), f32)]


def _lanes(i):
    return pl.ds(i * HEAD_DIM, HEAD_DIM)


def _stack_heads(per_head):
    return tuple(jnp.stack(vals) for vals in zip(*per_head))


def _hgrn2_fwd(h4, lb, nw, carry=None):
    _, s, d = h4.shape
    h, n, ncb, nblk, t, hp = _scan_dims(s, d)
    sp = _scan_specs(t, nblk, ncb, hp, False)
    grid = (h // hp, nblk)
    ni, no = (len(carry.arrays), len(carry.out_shapes)) if carry else (0, 0)

    def body(*refs):
        h_ref, lb_ref, nw_ref = refs[:3]
        ins = refs[3:3 + ni]
        y_ref, st_ref = refs[3 + ni:5 + ni]
        outs = refs[5 + ni:5 + ni + no]
        st_scr = refs[5 + ni + no]
        sems = refs[6 + ni + no:]
        _carry_at(carry, grid, "first", ins, outs, sems)

        @pl.when(pl.program_id(1) == 0)
        def _():
            st_scr[...] = jnp.zeros_like(st_scr)

        def step(c, carry):
            r = _rows(c)
            args = [(h_ref[0, r, _lanes(i)], h_ref[1, r, _lanes(i)], h_ref[2, r, _lanes(i)], h_ref[3, r, _lanes(i)],
                     lb_ref[:, _lanes(i)], nw_ref[...], st_scr[i]) for i in range(hp)]
            y, st_new = jax.vmap(_hgrn2_chunk)(*_stack_heads(args))
            for i, a in enumerate(args):
                st_ref[i, c] = a[-1]
                y_ref[r, _lanes(i)] = y[i].astype(y_ref.dtype)
                st_scr[i] = st_new[i]
            return carry

        lax.fori_loop(0, ncb, step, 0)
        _carry_at(carry, grid, "last", ins, outs, sems)

    extra = _carry_layout(carry, 3, 2)
    res = pl.pallas_call(
        body, grid=grid, name="hgrn2_fwd",
        in_specs=[sp["grp"](4), sp["vec"](), sp["shared"]()] + extra.in_specs,
        out_specs=[sp["col"](), sp["state"]()] + extra.out_specs,
        out_shape=[jax.ShapeDtypeStruct((s, d), bf16), jax.ShapeDtypeStruct((h, n, HEAD_DIM, HEAD_DIM), f32)]
        + extra.out_shapes,
        input_output_aliases=extra.aliases,
        scratch_shapes=_state_scratch(hp) + extra.sems, compiler_params=_scan_params())(
            h4, lb, nw, *(carry.arrays if carry else []))
    return (*res[:2], res[2:]) if carry else res


def _hgrn2_bwd(h4, lb, nw, states, dy, carry=None):
    _, s, d = h4.shape
    h, n, ncb, nblk, t, hp = _scan_dims(s, d)
    sp = _scan_specs(t, nblk, ncb, hp, True)
    grid = (h // hp, nblk)
    ni, no = (len(carry.arrays), len(carry.out_shapes)) if carry else (0, 0)

    def body(*refs):
        h_ref, lb_ref, nw_ref, st_ref, dy_ref = refs[:5]
        ins = refs[5:5 + ni]
        dh_ref, dlb_ref, dnw_ref = refs[5 + ni:8 + ni]
        outs = refs[8 + ni:8 + ni + no]
        dst_scr = refs[8 + ni + no]
        sems = refs[9 + ni + no:]
        first_block = pl.program_id(1) == 0
        _carry_at(carry, grid, "first", ins, outs, sems)

        @pl.when(first_block)
        def _():
            dst_scr[...] = jnp.zeros_like(dst_scr)
            dlb_ref[...] = jnp.zeros_like(dlb_ref)

        @pl.when(first_block & (pl.program_id(0) == 0))
        def _():
            dnw_ref[...] = jnp.zeros_like(dnw_ref)

        def step(k, carry):
            c = ncb - 1 - k
            r = _rows(c)
            args = [(h_ref[0, r, _lanes(i)], h_ref[1, r, _lanes(i)], h_ref[2, r, _lanes(i)], h_ref[3, r, _lanes(i)],
                     lb_ref[:, _lanes(i)], nw_ref[...], st_ref[i, c]) for i in range(hp)]
            cots = [(dy_ref[r, _lanes(i)].astype(f32), dst_scr[i]) for i in range(hp)]
            dq, df, di, dz, dlb, dnw, dst = jax.vjp(jax.vmap(_hgrn2_chunk), *_stack_heads(args))[1](_stack_heads(cots))
            dnw_sum = dnw_ref[...]
            for i in range(hp):
                for g, val in enumerate((dq, df, di, dz)):
                    dh_ref[g, r, _lanes(i)] = val[i].astype(dh_ref.dtype)
                dlb_ref[:, _lanes(i)] += dlb[i]
                dnw_sum = dnw_sum + dnw[i]
                dst_scr[i] = dst[i]
            dnw_ref[...] = dnw_sum
            return carry

        lax.fori_loop(0, ncb, step, 0)
        _carry_at(carry, grid, "last", ins, outs, sems)

    extra = _carry_layout(carry, 5, 3)
    res = pl.pallas_call(
        body, grid=grid, name="hgrn2_bwd",
        in_specs=[sp["grp"](4), sp["vec"](), sp["shared"](), sp["state"](), sp["col"]()] + extra.in_specs,
        out_specs=[sp["grp"](4), sp["vec"](), sp["shared"]()] + extra.out_specs,
        out_shape=[jax.ShapeDtypeStruct((4, s, d), bf16), jax.ShapeDtypeStruct((1, d), f32),
                   jax.ShapeDtypeStruct((1, HEAD_DIM), f32)] + extra.out_shapes,
        input_output_aliases=extra.aliases,
        scratch_shapes=_state_scratch(hp) + extra.sems, compiler_params=_scan_params())(
            h4, lb, nw, states, dy, *(carry.arrays if carry else []))
    return (*res[:3], res[3:]) if carry else res


def _gdn_fwd(c3, h4, bl, al, alog, dtb, nw):
    _, s, d = c3.shape
    h, n, ncb, nblk, t, hp = _scan_dims(s, d)
    sp = _scan_specs(t, nblk, ncb, hp, False)

    def body(c_ref, hz_ref, bl_ref, al_ref, alog_ref, dtb_ref, nw_ref, y_ref, st_ref, inv_ref, st_scr):
        @pl.when(pl.program_id(1) == 0)
        def _():
            st_scr[...] = jnp.zeros_like(st_scr)

        def step(c, carry):
            r = _rows(c)
            args = [(c_ref[0, r, _lanes(i)], c_ref[1, r, _lanes(i)], c_ref[2, r, _lanes(i)], hz_ref[r, _lanes(i)],
                     bl_ref[i, r, :], al_ref[i, r, :], alog_ref[i], dtb_ref[i], nw_ref[...], st_scr[i]) for i in range(hp)]
            y, st_new, inv = jax.vmap(functools.partial(_gdn_chunk, with_inverse=True))(*_stack_heads(args))
            for i, a in enumerate(args):
                st_ref[i, c] = a[-1]
                inv_ref[i, c] = inv[i]
                y_ref[r, _lanes(i)] = y[i].astype(y_ref.dtype)
                st_scr[i] = st_new[i]
            return carry

        lax.fori_loop(0, ncb, step, 0)

    return pl.pallas_call(
        body, grid=(h // hp, nblk), name="gdn_fwd",
        in_specs=[sp["grp"](3), sp["one"](3), sp["tok"](), sp["tok"](), sp["par"](), sp["par"](), sp["shared"]()],
        out_specs=[sp["col"](), sp["state"](), sp["inverse"]()],
        out_shape=[jax.ShapeDtypeStruct((s, d), bf16), jax.ShapeDtypeStruct((h, n, HEAD_DIM, HEAD_DIM), f32),
                   jax.ShapeDtypeStruct((h, n, CHUNK, CHUNK), f32)],
        scratch_shapes=_state_scratch(hp), compiler_params=_scan_params())(c3, h4, bl, al, alog, dtb, nw)


def _gdn_bwd(c3, h4, bl, al, alog, dtb, nw, states, inverses, dy):
    _, s, d = c3.shape
    h, n, ncb, nblk, t, hp = _scan_dims(s, d)
    sp = _scan_specs(t, nblk, ncb, hp, True)

    def chunk(*a):
        return _gdn_chunk(*a[:-1], inv=a[-1])

    def body(c_ref, hz_ref, bl_ref, al_ref, alog_ref, dtb_ref, nw_ref, st_ref, inv_ref, dy_ref,
             dc_ref, dz_ref, dbl_ref, dal_ref, dalog_ref, ddtb_ref, dnw_ref, dst_scr):
        first_block = pl.program_id(1) == 0

        @pl.when(first_block)
        def _():
            dst_scr[...] = jnp.zeros_like(dst_scr)
            dalog_ref[...] = jnp.zeros_like(dalog_ref)
            ddtb_ref[...] = jnp.zeros_like(ddtb_ref)

        @pl.when(first_block & (pl.program_id(0) == 0))
        def _():
            dnw_ref[...] = jnp.zeros_like(dnw_ref)

        def step(k, carry):
            c = ncb - 1 - k
            r = _rows(c)
            args = [(c_ref[0, r, _lanes(i)], c_ref[1, r, _lanes(i)], c_ref[2, r, _lanes(i)], hz_ref[r, _lanes(i)],
                     bl_ref[i, r, :], al_ref[i, r, :], alog_ref[i], dtb_ref[i], nw_ref[...], st_ref[i, c], inv_ref[i, c])
                    for i in range(hp)]
            cots = [(dy_ref[r, _lanes(i)].astype(f32), dst_scr[i]) for i in range(hp)]
            dq, dk, dv, dz, dbl, dal, dalog, ddtb, dnw, dst, _ = jax.vjp(
                jax.vmap(chunk), *_stack_heads(args))[1](_stack_heads(cots))
            dnw_sum = dnw_ref[...]
            for i in range(hp):
                for g, val in enumerate((dq, dk, dv)):
                    dc_ref[g, r, _lanes(i)] = val[i]
                dz_ref[r, _lanes(i)] = dz[i].astype(dz_ref.dtype)
                dbl_ref[i, r, :] = dbl[i]
                dal_ref[i, r, :] = dal[i]
                dalog_ref[i] += dalog[i]
                ddtb_ref[i] += ddtb[i]
                dnw_sum = dnw_sum + dnw[i]
                dst_scr[i] = dst[i]
            dnw_ref[...] = dnw_sum
            return carry

        lax.fori_loop(0, ncb, step, 0)

    tokens = jax.ShapeDtypeStruct((h, s, 1), f32)
    heads = jax.ShapeDtypeStruct((h, 1, 1), f32)
    return pl.pallas_call(
        body, grid=(h // hp, nblk), name="gdn_bwd",
        in_specs=[sp["grp"](3), sp["one"](3), sp["tok"](), sp["tok"](), sp["par"](), sp["par"](), sp["shared"](),
                  sp["state"](), sp["inverse"](), sp["col"]()],
        out_specs=[sp["grp"](3), sp["one"](3), sp["tok"](), sp["tok"](), sp["par"](), sp["par"](), sp["shared"]()],
        out_shape=[jax.ShapeDtypeStruct((3, s, d), f32), jax.ShapeDtypeStruct((4, s, d), bf16), tokens, tokens, heads, heads,
                   jax.ShapeDtypeStruct((1, HEAD_DIM), f32)],
        scratch_shapes=_state_scratch(hp), compiler_params=_scan_params())(
            c3, h4, bl, al, alog, dtb, nw, states, inverses, dy)


def _tile(n, pref):
    return n if n <= pref else pref


def _mm(a, b, mode, *, name, out_dtype=f32, add=None, add_scale=1.0, b_g0=0, groups=None, carry=None, a_half=None,
        n_half=None, into=None):
    squeeze = a.ndim == 2 and b.ndim == 2
    a3 = a if a.ndim == 3 else a[None]
    b3 = b if b.ndim == 3 else b[None]
    if mode == "nt":
        g = groups or a3.shape[0]
        (_, m, r), (_, n, _) = a3.shape, b3.shape
    elif mode == "nn":
        g = groups or b3.shape[0]
        (_, m, r), (_, _, n) = a3.shape, b3.shape
        if n_half is not None:
            n = n // 2
    else:
        g = groups or b3.shape[0]
        (_, r, m), (_, _, n) = a3.shape, b3.shape
        if a_half is not None:
            m = m // 2
    tm, tn, tr = _tile(m, 1024), _tile(n, 1024), _tile(r, 2048)
    nr = r // tr
    a_grouped = a3.shape[0] > 1
    dn = {"nn": NN, "nt": NT, "tn": TN}[mode]

    if mode == "nt":
        grid = (m // tm, n // tn, g * nr)
        last = g * nr - 1
        kax = 2
        a_spec = pl.BlockSpec((None, tm, tr), lambda i, j, k: (k // nr, i, k % nr))
        b_spec = pl.BlockSpec((None, tn, tr), lambda i, j, k: (b_g0 + k // nr, j, k % nr))
        o_spec = pl.BlockSpec((tm, tn), lambda i, j, k: (i, j))
        add_spec = pl.BlockSpec((tm, tn), lambda i, j, k: (i, j))
        out_shape = jax.ShapeDtypeStruct((m, n), out_dtype)
        sem = ("parallel", "parallel", "arbitrary")
    else:
        grid = (g, m // tm, n // tn, nr)
        last = nr - 1
        kax = 3
        ag = (lambda q: q) if a_grouped else (lambda q: 0)
        if mode == "nn":
            a_spec = pl.BlockSpec((None, tm, tr), lambda q, i, j, k: (ag(q), i, k))
        else:
            def col0():
                if a_half is None:
                    return 0
                return (_my_core() if a_half == "mine" else 1 - _my_core()) * (m // tm)
            a_spec = pl.BlockSpec((None, tr, tm), lambda q, i, j, k: (ag(q), k, col0() + i))
        off = 0 if n_half is None else n_half * (n // tn)
        b_spec = pl.BlockSpec((None, tr, tn), lambda q, i, j, k: (b_g0 + q, k, off + j))
        o_spec = pl.BlockSpec((None, tm, tn), lambda q, i, j, k: (q, i, off + j))
        add_spec = ANY
        out_shape = jax.ShapeDtypeStruct((g, m, n if n_half is None else 2 * n), out_dtype)
        sem = ("parallel", "parallel", "parallel", "arbitrary")
        assert add is None
        add = into

    n_main = 2 if add is None else 3
    ni = len(carry.arrays) if carry else 0
    no = len(carry.out_shapes) if carry else 0

    def body(*refs):
        a_ref, b_ref = refs[:2]
        add_ref = None if add is None else refs[2]
        ins = refs[n_main:n_main + ni]
        o_ref = refs[n_main + ni]
        outs = refs[n_main + ni + 1:n_main + ni + 1 + no]
        acc = refs[n_main + ni + 1 + no]
        sems = refs[n_main + ni + 2 + no:]
        k = pl.program_id(kax)
        _carry_at(carry, grid, "first", ins, outs, sems)

        @pl.when(k == 0)
        def _():
            acc[...] = jnp.zeros_like(acc)

        acc[...] += _bdot(a_ref[...], b_ref[...], dn)

        @pl.when(k == last)
        def _():
            res = acc[...]
            if add is not None and mode == "nt":
                res = res + add_scale * add_ref[...]
            o_ref[...] = res.astype(o_ref.dtype)

        _carry_at(carry, grid, "last", ins, outs, sems)

    operands = [a3, b3] + ([] if add is None else [add])
    in_specs = [a_spec, b_spec] + ([] if add is None else [add_spec])
    extra = _carry_layout(carry, n_main, 1)
    if carry:
        sem = ("arbitrary",) * len(grid)
    out, *moved = pl.pallas_call(
        body, grid=grid, name=name, in_specs=in_specs + extra.in_specs, out_specs=[o_spec] + extra.out_specs,
        out_shape=[out_shape] + extra.out_shapes,
        input_output_aliases=extra.aliases | ({2: 0} if into is not None else {}),
        scratch_shapes=[pltpu.VMEM((tm, tn), f32)] + extra.sems,
        compiler_params=pltpu.CompilerParams(dimension_semantics=sem, vmem_limit_bytes=VMEM_LIMIT))(
            *operands, *(carry.arrays if carry else []))
    out = out[0] if (squeeze and mode != "nt") else out
    return (out, moved) if carry else out


def _row_block(rows, cols, target_bytes=1 << 20):
    if rows <= SUBLANES:
        return rows
    tr = SUBLANES
    while tr * 2 <= rows and tr * 2 * cols * 4 <= target_bytes and rows % (tr * 2) == 0:
        tr *= 2
    return tr


def _ew_params(n_axes=1):
    return pltpu.CompilerParams(dimension_semantics=("parallel",) * n_axes, vmem_limit_bytes=VMEM_LIMIT)


def _cast(x, dtype, name, into_chip_slot=False):
    r, c = x.shape
    tr = _row_block(r, c, 2 << 20)
    tr = max(tr, 16) if r >= 16 else tr
    spec = pl.BlockSpec((tr, c), lambda i: (i, 0))
    if into_chip_slot:
        out_spec = pl.BlockSpec((None, tr, c), lambda i: (_my_chip(), i, 0))
        out_shape = jax.ShapeDtypeStruct((N_CHIPS, r, c), dtype)
    else:
        out_spec, out_shape = spec, jax.ShapeDtypeStruct((r, c), dtype)

    def body(x_ref, o_ref):
        o_ref[...] = x_ref[...].astype(dtype)

    return pl.pallas_call(body, grid=(r // tr,), name=name, in_specs=[spec], out_specs=out_spec,
                          out_shape=out_shape, compiler_params=_ew_params())(x)


def _merge_tails(main, tail):
    _, d, _ = main.shape
    tr = max(_row_block(d, LANES), 16)
    head = pl.BlockSpec((None, tr, LANES), lambda g, i: (g + 1, i, 0))
    rest = pl.BlockSpec((None, tr, LANES), lambda g, i: (g, i, 0))

    def body(m_ref, t_ref, o_ref):
        o_ref[...] = m_ref[...] + t_ref[...]

    return pl.pallas_call(body, grid=(N_CHIPS - 1, d // tr), name="merge_shard_tails", in_specs=[head, rest],
                          out_specs=head, out_shape=jax.ShapeDtypeStruct(main.shape, main.dtype),
                          input_output_aliases={0: 0}, compiler_params=_ew_params(2))(main, tail)


def _ln_stats(r):
    mu = jnp.mean(r, axis=-1, keepdims=True)
    var = jnp.mean(jnp.square(r - mu), axis=-1, keepdims=True)
    return mu, lax.rsqrt(var + LN_EPS)


def _ln_fwd(x, yo, g, b):
    s, d = x.shape
    tr = _row_block(s, d, 2 << 20)
    tr = max(tr, 16)
    big = pl.BlockSpec((tr, d), lambda i: (i, 0))
    vec = pl.BlockSpec((1, d), lambda i: (0, 0))

    def body(x_ref, yo_ref, g_ref, b_ref, o_ref, ob_ref):
        r = DEEPNORM_ALPHA * x_ref[...] + yo_ref[...]
        mu, rstd = _ln_stats(r)
        y = (r - mu) * rstd * g_ref[...] + b_ref[...]
        o_ref[...] = y
        ob_ref[...] = y.astype(bf16)

    return pl.pallas_call(
        body, grid=(s // tr,), name="ln_fwd", in_specs=[big, big, vec, vec], out_specs=[big, big],
        out_shape=[jax.ShapeDtypeStruct((s, d), f32), jax.ShapeDtypeStruct((s, d), bf16)],
        compiler_params=_ew_params())(x, yo, g, b)


def _ln_bwd(x, yo, g, *, dy=None, b=None, target=None, name):
    s, d = x.shape
    with_loss = target is not None
    tr = max(_row_block(s, d, 2 << 20), 16)
    big = pl.BlockSpec((tr, d), lambda i: (i, 0))
    vec = pl.BlockSpec((1, d), lambda i: (0, 0))
    lane = pl.BlockSpec((1, LANES), lambda i: (0, 0))

    def body(*refs):
        if with_loss:
            x_ref, yo_ref, g_ref, b_ref, t_ref, dr_ref, drb_ref, dg_ref, db_ref, loss_ref = refs
        else:
            x_ref, yo_ref, g_ref, dy_ref, dr_ref, drb_ref, dg_ref, db_ref = refs

        @pl.when(pl.program_id(0) == 0)
        def _():
            dg_ref[...] = jnp.zeros_like(dg_ref)
            db_ref[...] = jnp.zeros_like(db_ref)
            if with_loss:
                loss_ref[...] = jnp.zeros_like(loss_ref)

        r = DEEPNORM_ALPHA * x_ref[...] + yo_ref[...]
        mu, rstd = _ln_stats(r)
        xhat = (r - mu) * rstd
        if with_loss:
            err = xhat * g_ref[...] + b_ref[...] - t_ref[...]
            loss_ref[...] += jnp.sum(err * err) * (0.5 / d)
            dyv = err * (1.0 / d)
        else:
            dyv = dy_ref[...]
        dxhat = dyv * g_ref[...]
        m1 = jnp.mean(dxhat, axis=-1, keepdims=True)
        m2 = jnp.mean(dxhat * xhat, axis=-1, keepdims=True)
        dr = rstd * (dxhat - m1 - xhat * m2)
        dr_ref[...] = dr
        drb_ref[...] = dr.astype(bf16)
        dg_ref[...] += jnp.sum(dyv * xhat, axis=0, keepdims=True)
        db_ref[...] += jnp.sum(dyv, axis=0, keepdims=True)

    operands = (x, yo, g, b, target) if with_loss else (x, yo, g, dy)
    in_specs = [big, big, vec, vec, big] if with_loss else [big, big, vec, big]
    out_specs = [big, big, vec, vec] + ([lane] if with_loss else [])
    out_shape = [jax.ShapeDtypeStruct((s, d), f32), jax.ShapeDtypeStruct((s, d), bf16),
                 jax.ShapeDtypeStruct((1, d), f32), jax.ShapeDtypeStruct((1, d), f32)]
    if with_loss:
        out_shape.append(jax.ShapeDtypeStruct((1, LANES), f32))
    return pl.pallas_call(
        body, grid=(s // tr,), name=name, in_specs=in_specs, out_specs=out_specs, out_shape=out_shape,
        compiler_params=pltpu.CompilerParams(dimension_semantics=("arbitrary",), vmem_limit_bytes=VMEM_LIMIT))(*operands)


def _lower_bound(a_lb):
    _, d = a_lb.shape

    def body(a_ref, o_ref):
        a = a_ref[...]
        ex = jnp.exp(a - jnp.max(a, axis=0, keepdims=True))
        p = ex / jnp.sum(ex, axis=0, keepdims=True)
        o_ref[...] = jnp.sum(jnp.where(_iota(p.shape, 0) == 0, p, 0.0), axis=0, keepdims=True)

    return pl.pallas_call(body, name="lower_bound", in_specs=[VMEM], out_specs=VMEM,
                          out_shape=jax.ShapeDtypeStruct((1, d), f32))(a_lb)


def _shift_rows(ext, k, rows):
    return pltpu.roll(ext, k, axis=0)[SUBLANES:SUBLANES + rows]


def _conv_tiles(s, d):
    return _tile(s, 512), _tile(d, 512)


def _conv_fwd(h4, w):
    _, s, d = h4.shape
    tr, tc = _conv_tiles(s, d)
    nj = d // tc
    hb = tr // SUBLANES
    cur = pl.BlockSpec((None, tr, tc), lambda g, j, i: (g, i, j))
    prev = pl.BlockSpec((None, SUBLANES, tc), lambda g, j, i: (g, jnp.maximum(i * hb - 1, 0), j))
    wsp = pl.BlockSpec((CONV_WIDTH, tc), lambda g, j, i: (0, g * nj + j))

    def body(x_ref, p_ref, w_ref, o_ref):
        x = x_ref[...]
        halo = jnp.where(pl.program_id(2) == 0, 0.0, p_ref[...])
        ext = jnp.concatenate([halo, x], axis=0)
        acc = w_ref[CONV_WIDTH - 1:CONV_WIDTH, :] * x
        for k in range(1, CONV_WIDTH):
            acc = acc + w_ref[CONV_WIDTH - 1 - k:CONV_WIDTH - k, :] * _shift_rows(ext, k, tr)
        o_ref[...] = acc

    return pl.pallas_call(
        body, grid=(3, nj, s // tr), name="conv_fwd", in_specs=[cur, prev, wsp], out_specs=cur,
        out_shape=jax.ShapeDtypeStruct((3, s, d), f32), compiler_params=_ew_params(3))(h4, h4, w)


def _conv_bwd(dy3, h4, w, dh4):
    _, s, d = dy3.shape
    tr, tc = _conv_tiles(s, d)
    nj = d // tc
    hb = tr // SUBLANES
    ni = s // tr
    cur = pl.BlockSpec((None, tr, tc), lambda g, j, i: (g, i, j))
    prev = pl.BlockSpec((None, SUBLANES, tc), lambda g, j, i: (g, jnp.maximum(i * hb - 1, 0), j))
    nxt = pl.BlockSpec((None, SUBLANES, tc), lambda g, j, i: (g, jnp.minimum((i + 1) * hb, s // SUBLANES - 1), j))
    wsp = pl.BlockSpec((CONV_WIDTH, tc), lambda g, j, i: (0, g * nj + j))

    def body(dy_ref, dn_ref, x_ref, p_ref, w_ref, _, dx_ref, dw_ref):
        i = pl.program_id(2)

        @pl.when(i == 0)
        def _():
            dw_ref[...] = jnp.zeros_like(dw_ref)

        dy = dy_ref[...]
        x = x_ref[...]
        tail = jnp.where(i == ni - 1, 0.0, dn_ref[...])
        dext = jnp.concatenate([dy, tail], axis=0)
        halo = jnp.where(i == 0, 0.0, p_ref[...])
        xext = jnp.concatenate([halo, x], axis=0)
        acc = w_ref[CONV_WIDTH - 1:CONV_WIDTH, :] * dy
        dw_ref[CONV_WIDTH - 1:CONV_WIDTH, :] += jnp.sum(dy * x, axis=0, keepdims=True)
        for k in range(1, CONV_WIDTH):
            ahead = pltpu.roll(dext, tr + SUBLANES - k, axis=0)[:tr]
            acc = acc + w_ref[CONV_WIDTH - 1 - k:CONV_WIDTH - k, :] * ahead
            dw_ref[CONV_WIDTH - 1 - k:CONV_WIDTH - k, :] += jnp.sum(dy * _shift_rows(xext, k, tr), axis=0, keepdims=True)
        dx_ref[...] = acc.astype(dx_ref.dtype)

    return pl.pallas_call(
        body, grid=(3, nj, ni), name="conv_bwd", in_specs=[cur, nxt, cur, prev, wsp, ANY], out_specs=[cur, wsp],
        out_shape=[jax.ShapeDtypeStruct((4, s, d), bf16), jax.ShapeDtypeStruct((CONV_WIDTH, 3 * d), f32)],
        input_output_aliases={5: 0},
        compiler_params=pltpu.CompilerParams(dimension_semantics=("parallel", "parallel", "arbitrary"),
                                             vmem_limit_bytes=VMEM_LIMIT))(dy3, dy3, h4, h4, w, dh4)


def _my_core():
    return lax.axis_index("c")


def _my_chip():
    return 2 * lax.axis_index("x") + lax.axis_index("y")


def _chip_sum(halves, other, name):
    _, r2, c = other.shape
    tr = max(_row_block(r2, c), 16)
    own = pl.BlockSpec((None, tr, c), lambda p, i: (2 * p + _my_core(), i, 0))
    blk = pl.BlockSpec((None, tr, c), lambda p, i: (p, i, 0))

    def body(a_ref, b_ref, o_ref):
        o_ref[...] = (a_ref[...] + b_ref[...]).astype(o_ref.dtype)

    return pl.pallas_call(
        body, name=name, grid=(N_CHIPS, r2 // tr), in_specs=[own, blk], out_specs=blk,
        out_shape=jax.ShapeDtypeStruct(other.shape, bf16), compiler_params=_ew_params(2))(halves, other)


def _final_sum(halves, other, remote, name):
    _, r2, c = remote.shape
    tr = max(_row_block(r2, c), 16)
    own = pl.BlockSpec((None, tr, c), lambda i: (2 * _my_chip() + _my_core(), i, 0))
    sib = pl.BlockSpec((None, tr, c), lambda i: (_my_chip(), i, 0))
    slot = lambda j: pl.BlockSpec((None, tr, c), lambda i: (j, i, 0))

    def body(a_ref, b_ref, r0_ref, r1_ref, r2_ref, o_ref):
        acc = a_ref[...] + b_ref[...]
        for ref in (r0_ref, r1_ref, r2_ref):
            acc = acc + ref[...].astype(f32)
        o_ref[...] = acc

    return pl.pallas_call(
        body, name=name, grid=(r2 // tr,), in_specs=[own, sib, slot(0), slot(1), slot(2)],
        out_specs=pl.BlockSpec((None, tr, c), lambda i: (_my_core(), i, 0)),
        out_shape=jax.ShapeDtypeStruct((2, r2, c), f32), compiler_params=_ew_params())(halves, other, remote, remote, remote)


def _adamw(w, g, m, v, name):
    r, c = w.shape
    tr = _row_block(r, c, 2 << 20)
    spec = pl.BlockSpec((tr, c), lambda i: (i, 0))

    def body(w_ref, g_ref, m_ref, v_ref, go_ref, d_ref, nm_ref, nv_ref):
        gv = g_ref[...]
        go_ref[...] = gv
        nm = ADAM_B1 * m_ref[...] + (1.0 - ADAM_B1) * gv
        nv = ADAM_B2 * v_ref[...] + (1.0 - ADAM_B2) * jnp.square(gv)
        m_hat = nm / (1.0 - ADAM_B1 ** ADAM_STEP)
        v_hat = nv / (1.0 - ADAM_B2 ** ADAM_STEP)
        d_ref[...] = -ADAM_LR * (m_hat / (jnp.sqrt(v_hat) + ADAM_EPS) + ADAM_WD * w_ref[...])
        nm_ref[...] = nm
        nv_ref[...] = nv

    out = jax.ShapeDtypeStruct((r, c), f32)
    return pl.pallas_call(body, grid=(r // tr,), name=name, in_specs=[spec] * 4, out_specs=[spec] * 4,
                          out_shape=[out, out, out, out], compiler_params=_ew_params())(w, g, m, v)


class _Comm(NamedTuple):
    arrays: list
    aliases: dict
    out_shapes: list
    sems: list
    start: Callable
    finish: Callable


class _CarryLayout(NamedTuple):
    in_specs: list
    out_specs: list
    out_shapes: list
    aliases: dict
    sems: list


def _carry_layout(comm, n_in, n_out):
    if comm is None:
        return _CarryLayout([], [], [], {}, [])
    return _CarryLayout([ANY] * len(comm.arrays), [ANY] * len(comm.out_shapes), list(comm.out_shapes),
                        {n_in + i: n_out + o for i, o in comm.aliases.items()}, list(comm.sems))


def _carry_at(comm, grid, edge, ins, outs, sems):
    if comm is None:
        return
    hit = [pl.program_id(ax) == (0 if edge == "first" else extent - 1) for ax, extent in enumerate(grid)]

    @pl.when(functools.reduce(jnp.logical_and, hit))
    def _():
        (comm.start if edge == "first" else comm.finish)(ins, outs, sems)


def _compose(first, second):
    na, no, ns = len(first.arrays), len(first.out_shapes), len(first.sems)

    def both(which):
        def run(ins, outs, sems):
            getattr(first, which)(ins[:na], outs[:no], sems[:ns])
            getattr(second, which)(ins[na:], outs[no:], sems[ns:])
        return run

    aliases = dict(first.aliases) | {na + i: no + o for i, o in second.aliases.items()}
    return _Comm(first.arrays + second.arrays, aliases, first.out_shapes + second.out_shapes, first.sems + second.sems,
                 both("start"), both("finish"))


def _run_comm(comm, name):
    ni, no = len(comm.arrays), len(comm.out_shapes)

    def body(*refs):
        ins, outs, sems = refs[:ni], refs[ni:ni + no], refs[ni + no:]
        comm.start(ins, outs, sems)
        comm.finish(ins, outs, sems)

    return pl.pallas_call(body, name=name, in_specs=[ANY] * ni, out_specs=[ANY] * no, out_shape=comm.out_shapes,
                          input_output_aliases=comm.aliases, scratch_shapes=comm.sems)(*comm.arrays)


def _place():
    return lax.axis_index("x"), lax.axis_index("y"), lax.axis_index("c")


def _chip_peers(x, y):
    out = []
    for fx, fy in ((1, 0), (0, 1), (1, 1)):
        px, py = x ^ fx, y ^ fy
        out.append((px, py, 2 * px + py))
    return out


def _all_gather(arrays, split, cols=None):
    na = len(arrays)

    def part(ref, a, block, which):
        width = slice(None) if cols is None or cols[a] is None else pl.ds(*cols[a])
        if not split[a]:
            return ref.at[block, :, width]
        rows = arrays[a].shape[1] // 2
        return ref.at[block, pl.ds(which * rows, rows), width]

    def ici_sends(outs, sems):
        send_sems, recv_sems = sems
        x, y, c = _place()
        me = 2 * x + y
        out = []
        for a in range(na):
            for j, (px, py, _) in enumerate(_chip_peers(x, y)):
                mine = part(outs[a], a, me, c)
                out.append(pltpu.make_async_remote_copy(mine, mine, send_sems.at[a, j], recv_sems.at[a, j],
                                                        device_id=(px, py, c), device_id_type=MESH))
        return out

    def forwards(outs, sems, which_core):
        send_sems, recv_sems = sems
        x, y, c = _place()
        out = []
        for a in range(na):
            if split[a]:
                for j, (_, _, pid) in enumerate(_chip_peers(x, y)):
                    got = part(outs[a], a, pid, which_core)
                    out.append(pltpu.make_async_remote_copy(got, got, send_sems.at[a, 3 + j], recv_sems.at[a, 3 + j],
                                                            device_id=(x, y, 1 - c), device_id_type=MESH))
        return out

    def arrivals(outs, sems):
        send_sems, recv_sems = sems
        x, y, c = _place()
        out = []
        for a in range(na):
            for j, (px, py, pid) in enumerate(_chip_peers(x, y)):
                got = part(outs[a], a, pid, c)
                out.append(pltpu.make_async_remote_copy(got, got, send_sems.at[a, j], recv_sems.at[a, j],
                                                        device_id=(px, py, c), device_id_type=MESH))
        return out

    def finish(ins, outs, sems):
        c = lax.axis_index("c")
        for cp in arrivals(outs, sems):
            cp.wait_recv()
        mine = forwards(outs, sems, c)
        for cp in mine:
            cp.start()
        for cp in forwards(outs, sems, 1 - c):
            cp.wait_recv()
        for cp in ici_sends(outs, sems) + mine:
            cp.wait_send()

    return _Comm(
        arrays=list(arrays), aliases={a: a for a in range(na)},
        out_shapes=[jax.ShapeDtypeStruct(a.shape, a.dtype) for a in arrays],
        sems=[pltpu.SemaphoreType.DMA((na, 6)), pltpu.SemaphoreType.DMA((na, 6))],
        start=lambda ins, outs, sems: [cp.start() for cp in ici_sends(outs, sems)], finish=finish)


def _swap_halves(arrays):
    na = len(arrays)

    def copies(ins, outs, sems):
        send_sems, recv_sems = sems
        x, y, c = _place()
        out = []
        for a in range(na):
            rows = arrays[a].shape[1] // 2
            src = ins[a].at[:, pl.ds((1 - c) * rows, rows), :]
            out.append(pltpu.make_async_remote_copy(src, outs[a], send_sems.at[a], recv_sems.at[a],
                                                    device_id=(x, y, 1 - c), device_id_type=MESH))
        return out

    return _Comm(
        arrays=list(arrays), aliases={},
        out_shapes=[jax.ShapeDtypeStruct((a.shape[0], a.shape[1] // 2, a.shape[2]), a.dtype) for a in arrays],
        sems=[pltpu.SemaphoreType.DMA((na,)), pltpu.SemaphoreType.DMA((na,))],
        start=lambda ins, outs, sems: [cp.start() for cp in copies(ins, outs, sems)],
        finish=lambda ins, outs, sems: [cp.wait() for cp in copies(ins, outs, sems)])


def _send_to_sibling(arrays):
    na = len(arrays)

    def copies(ins, outs, sems):
        send_sems, recv_sems = sems
        x, y, c = _place()
        return [pltpu.make_async_remote_copy(ins[a], outs[a], send_sems.at[a], recv_sems.at[a],
                                             device_id=(x, y, 1 - c), device_id_type=MESH) for a in range(na)]

    return _Comm(
        arrays=list(arrays), aliases={}, out_shapes=[jax.ShapeDtypeStruct(a.shape, a.dtype) for a in arrays],
        sems=[pltpu.SemaphoreType.DMA((na,)), pltpu.SemaphoreType.DMA((na,))],
        start=lambda ins, outs, sems: [cp.start() for cp in copies(ins, outs, sems)],
        finish=lambda ins, outs, sems: [cp.wait() for cp in copies(ins, outs, sems)])


def _pair_sum(keep, other, name):
    _, r2, c = keep.shape
    tr = max(_row_block(r2, c), 16)
    blk = pl.BlockSpec((None, tr, c), lambda p, i: (p, i, 0))

    def body(a_ref, b_ref, o_ref):
        o_ref[...] = (a_ref[...] + b_ref[...]).astype(o_ref.dtype)

    return pl.pallas_call(body, name=name, grid=(N_CHIPS, r2 // tr), in_specs=[blk, blk], out_specs=blk,
                          out_shape=jax.ShapeDtypeStruct(keep.shape, bf16), compiler_params=_ew_params(2))(keep, other)


def _final_sum_pair(keep, other, remote, name):
    _, r2, c = remote.shape
    tr = max(_row_block(r2, c), 16)
    own = pl.BlockSpec((None, tr, c), lambda i: (_my_chip(), i, 0))
    slot = lambda j: pl.BlockSpec((None, tr, c), lambda i: (j, i, 0))

    def body(a_ref, b_ref, r0_ref, r1_ref, r2_ref, o_ref):
        acc = a_ref[...] + b_ref[...]
        for ref in (r0_ref, r1_ref, r2_ref):
            acc = acc + ref[...].astype(f32)
        o_ref[...] = acc

    return pl.pallas_call(
        body, name=name, grid=(r2 // tr,), in_specs=[own, own, slot(0), slot(1), slot(2)],
        out_specs=pl.BlockSpec((None, tr, c), lambda i: (_my_core(), i, 0)),
        out_shape=jax.ShapeDtypeStruct((2, r2, c), f32), compiler_params=_ew_params())(keep, other, remote, remote, remote)


def _scatter_blocks(arrays):
    na = len(arrays)

    def copies(ins, outs, sems):
        send_sems, recv_sems = sems
        x, y, c = _place()
        out = []
        for a in range(na):
            for j, (px, py, pid) in enumerate(_chip_peers(x, y)):
                out.append(pltpu.make_async_remote_copy(ins[a].at[pid], outs[a].at[j], send_sems.at[a, j],
                                                        recv_sems.at[a, j], device_id=(px, py, c), device_id_type=MESH))
        return out

    return _Comm(
        arrays=list(arrays), aliases={},
        out_shapes=[jax.ShapeDtypeStruct((3,) + a.shape[1:], a.dtype) for a in arrays],
        sems=[pltpu.SemaphoreType.DMA((na, 3)), pltpu.SemaphoreType.DMA((na, 3))],
        start=lambda ins, outs, sems: [cp.start() for cp in copies(ins, outs, sems)],
        finish=lambda ins, outs, sems: [cp.wait() for cp in copies(ins, outs, sems)])


def _join_halves(arrays):
    na = len(arrays)

    def copies(outs, sems, slot):
        send_sems, recv_sems = sems
        x, y, c = _place()
        which = c if slot == "mine" else 1 - c
        return [pltpu.make_async_remote_copy(outs[a].at[which], outs[a].at[which], send_sems.at[a], recv_sems.at[a],
                                             device_id=(x, y, 1 - c), device_id_type=MESH) for a in range(na)]

    def finish(ins, outs, sems):
        for cp in copies(outs, sems, "mine"):
            cp.wait_send()
        for cp in copies(outs, sems, "sibling's"):
            cp.wait_recv()

    return _Comm(
        arrays=list(arrays), aliases={a: a for a in range(na)},
        out_shapes=[jax.ShapeDtypeStruct(a.shape, a.dtype) for a in arrays],
        sems=[pltpu.SemaphoreType.DMA((na,)), pltpu.SemaphoreType.DMA((na,))],
        start=lambda ins, outs, sems: [cp.start() for cp in copies(outs, sems, "mine")], finish=finish)


SMALL_ROWS = 24


def _all_reduce_small(a_lb, dlb, pieces):
    d = a_lb.shape[1]
    npc = len(pieces)

    def body(*refs):
        alb_ref, dlb_ref = refs[:2]
        piece_refs = refs[2:2 + npc]
        out_ref, pack, slots, send_sems, recv_sems = refs[2 + npc:]
        x, y, c = _place()
        me = 4 * x + 2 * y + c
        pack[...] = jnp.zeros_like(pack)
        a = alb_ref[...]
        ex = jnp.exp(a - jnp.max(a, axis=0, keepdims=True))
        p = ex / jnp.sum(ex, axis=0, keepdims=True)
        first = (_iota(p.shape, 0) == 0).astype(f32)
        p0 = jnp.sum(p * first, axis=0, keepdims=True)
        pack[0:3, :] = p * dlb_ref[...] * (first - p0)
        for ref, (arr, r0) in zip(piece_refs, pieces):
            pack[r0:r0 + arr.shape[0], 0:arr.shape[1]] = ref[...]
        slots[me] = pack[...]
        copies = []
        for k in range(1, N_DEV):
            peer = (x ^ (k >> 2), y ^ ((k >> 1) & 1), c ^ (k & 1))
            cp = pltpu.make_async_remote_copy(pack, slots.at[me], send_sems.at[k - 1], recv_sems.at[k - 1],
                                              device_id=peer, device_id_type=MESH)
            cp.start()
            copies.append(cp)
        for k in range(1, N_DEV):
            src = (x ^ (k >> 2)) * 4 + (y ^ ((k >> 1) & 1)) * 2 + (c ^ (k & 1))
            pltpu.make_async_remote_copy(pack, slots.at[src], send_sems.at[k - 1], recv_sems.at[k - 1],
                                         device_id=(x, y, c), device_id_type=MESH).wait_recv()
        for cp in copies:
            cp.wait_send()
        acc = slots[0]
        for i in range(1, N_DEV):
            acc = acc + slots[i]
        out_ref[...] = acc

    return pl.pallas_call(
        body, name="all_reduce_small", in_specs=[VMEM] * (2 + npc), out_specs=VMEM,
        out_shape=jax.ShapeDtypeStruct((SMALL_ROWS, d), f32),
        scratch_shapes=[pltpu.VMEM((SMALL_ROWS, d), f32), pltpu.VMEM((N_DEV, SMALL_ROWS, d), f32),
                        pltpu.SemaphoreType.DMA((N_DEV - 1,)), pltpu.SemaphoreType.DMA((N_DEV - 1,))],
        compiler_params=pltpu.CompilerParams(vmem_limit_bytes=VMEM_LIMIT))(
            a_lb, dlb, *[p[0] for p in pieces])


def _as_halves(g):
    return g.reshape(2 * g.shape[0], g.shape[1] // 2, g.shape[2])


def _chip_sums(grads, from_sibling, tag):
    return [_chip_sum(_as_halves(g), t, f"grad_chip_sum_{tag}{a}") for a, (g, t) in enumerate(zip(grads, from_sibling))]


def _final_sums(grads, from_sibling, from_chips, tag):
    return [_final_sum(_as_halves(g), t, r, f"grad_final_sum_{tag}{a}")
            for a, (g, t, r) in enumerate(zip(grads, from_sibling, from_chips))]


def kernel(x, a_w_in, a_lower_bounds, a_norm_w, a_w_out, b_w_in, b_conv_w, b_a_log, b_dt_bias, b_norm_w, b_w_out, ln_g, ln_b, loss_target, m_a_w_in, m_a_lower_bounds, m_a_norm_w, m_a_w_out, m_b_w_in, m_b_conv_w, m_b_a_log, m_b_dt_bias, m_b_norm_w, m_b_w_out, m_ln_g, m_ln_b, v_a_w_in, v_a_lower_bounds, v_a_norm_w, v_a_w_out, v_b_w_in, v_b_conv_w, v_b_a_log, v_b_dt_bias, v_b_norm_w, v_b_w_out, v_ln_g, v_ln_b):
    _, s, d = x.shape
    h = d // HEAD_DIM
    wb = b_w_in.shape[2]
    x2, target = x[0], loss_target[0]
    chip = 2 * lax.axis_index("x") + lax.axis_index("y")

    e = wb - d
    shard_b = b_w_in[0]
    main_b = lax.dynamic_slice(jnp.pad(shard_b[:, :d], ((0, 0), (N_CHIPS * e, 0))), (0, e * (N_CHIPS - chip)), (d, d))
    tail_b = lax.dynamic_slice(jnp.pad(shard_b, ((0, 0), (0, LANES))), (0, d - e * chip), (d, LANES))
    tail_b = jnp.where(lax.broadcasted_iota(jnp.int32, (1, LANES), 1) < e * (chip + 1), tail_b, 0.0)
    shards = [_cast(a_w_in[0], bf16, "cast_a_w_in", True), _cast(a_w_out[0], bf16, "cast_a_w_out", True),
              _cast(main_b, bf16, "cast_b_w_in", True), _cast(tail_b, bf16, "cast_b_w_in_tail", True),
              _cast(b_w_out[0], bf16, "cast_b_w_out", True), _cast(b_conv_w[0], f32, "copy_b_conv_w", True)]
    sh_a_in, sh_a_out, sh_b_in, sh_b_tail, sh_b_out, sh_conv = shards
    left, right = (0, d // 2), (d // 2, d // 2)
    wa4, conv_g = _run_comm(_all_gather([sh_a_in, sh_conv], [True, False], [left, None]), "all_gather_first")
    conv_w = conv_g.transpose(1, 0, 2).reshape(CONV_WIDTH, 3 * d)

    x_bf = _cast(x2, bf16, "cast_x")
    lb = _lower_bound(a_lower_bounds)
    h4a, (wa4,) = _mm(x_bf, wa4, "nn", name="a_in_fwd_left", n_half=0, carry=_all_gather([wa4], [True], [right]))
    h4a, (wb_main, wb_tail) = _mm(x_bf, wa4, "nn", name="a_in_fwd_right", n_half=1, into=h4a,
                                  carry=_all_gather([sh_b_in, sh_b_tail], [True, True], [left, None]))
    third, fourth = (d // 2, d // 4), (3 * d // 4, d // 4)
    ya, st_a, (wa_out_g, wb_main) = _hgrn2_fwd(h4a, lb, a_norm_w,
                                               carry=_all_gather([sh_a_out, wb_main], [True, True], [None, third]))
    wa_out = wa_out_g.reshape(d, d)
    yo_a, (wb_main,) = _mm(ya, wa_out, "nn", name="a_out_fwd", carry=_all_gather([wb_main], [True], [fourth]))
    wb4 = _merge_tails(wb_main, wb_tail)
    wba = wb_tail[N_CHIPS - 1]
    x1, x1_bf = _ln_fwd(x2, yo_a, ln_g[0:1], ln_b[0:1])
    h4b, (wb_out_g,) = _mm(x1_bf, wb4, "nn", name="b_in_fwd", carry=_all_gather([sh_b_out], [True]))
    wb_out = wb_out_g.reshape(d, d)
    ba = _mm(x1_bf, wba, "nn", name="b_gate_fwd")
    c3 = _conv_fwd(h4b, conv_w)
    bl = ba[:, :h].T.reshape(h, s, 1)
    al = ba[:, h:2 * h].T.reshape(h, s, 1)
    alog, dtb = b_a_log.reshape(h, 1, 1), b_dt_bias.reshape(h, 1, 1)
    yb, st_b, inv_b = _gdn_fwd(c3, h4b, bl, al, alog, dtb, b_norm_w)
    yo_b = _mm(yb, wb_out, "nn", name="b_out_fwd")

    dr2, dr2_bf, dg2, db2, loss = _ln_bwd(x1, yo_b, ln_g[1:2], b=ln_b[1:2], target=target, name="ln2_loss_bwd")
    g_wb_out = _mm(yb, dr2_bf, "tn", name="b_out_dw")
    dyb = _mm(dr2_bf, wb_out, "nt", name="b_out_dx")
    dc3, dh4b, dbl, dal, dalog, ddtb, dnw_b = _gdn_bwd(c3, h4b, bl, al, alog, dtb, b_norm_w, st_b, inv_b, dyb)
    dh4b, dconv = _conv_bwd(dc3, h4b, conv_w, dh4b)
    dba = jnp.pad(jnp.concatenate([dbl.reshape(h, s).T, dal.reshape(h, s).T], axis=1), ((0, 0), (0, LANES - 2 * h)))
    g_wb4 = _mm(x1_bf, dh4b, "tn", name="b_in_dw")
    g_wba = _mm(x1_bf, dba, "tn", name="b_in_dw_gate")
    g_tail = jnp.concatenate([g_wb4[1:, :, :LANES], g_wba[None]], axis=0)
    grads_b = [g_wb4, g_tail, g_wb_out.reshape(N_CHIPS, d // N_CHIPS, d)]
    dx1, sib_b = _mm(dh4b, wb4, "nt", name="b_in_dx", add=dr2, add_scale=DEEPNORM_ALPHA, carry=_swap_halves(grads_b))
    dx1 = _mm(dba, wba, "nt", name="b_in_dx_gate", add=dx1)
    dr1, dr1_bf, dg1, db1 = _ln_bwd(x2, yo_a, ln_g[0:1], dy=dx1, name="ln1_bwd")
    chip_b = _chip_sums(grads_b, sib_b, "b")
    g_wa_out = _mm(ya, dr1_bf, "tn", name="a_out_dw")
    grads_a_out = [g_wa_out.reshape(N_CHIPS, d // N_CHIPS, d)]
    dya, sib_a_out = _mm(dr1_bf, wa_out, "nt", name="a_out_dx", carry=_swap_halves(grads_a_out))
    chip_a_out = _chip_sums(grads_a_out, sib_a_out, "a_out")
    dh4, dlb, dnw_a, far_b = _hgrn2_bwd(h4a, lb, a_norm_w, st_a, dya, carry=_scatter_blocks(chip_b))
    g_wa4_away, far_a_out = _mm(x_bf, dh4, "tn", name="a_in_dw_away", a_half="other", carry=_scatter_blocks(chip_a_out))
    g_wa4_keep, (sib_a_in,) = _mm(x_bf, dh4, "tn", name="a_in_dw_keep", a_half="mine",
                                  carry=_send_to_sibling([g_wa4_away]))
    chip_a_in = _pair_sum(g_wa4_keep, sib_a_in, "grad_chip_sum_a_in")
    grad_x, far_a_in = _mm(dh4, wa4, "nt", name="a_in_dx", add=dr1, add_scale=DEEPNORM_ALPHA,
                           carry=_scatter_blocks([chip_a_in]))

    finals = ([_final_sum_pair(g_wa4_keep, sib_a_in, far_a_in[0], "grad_final_sum_a_in")]
              + _final_sums(grads_a_out, sib_a_out, far_a_out, "a_out") + _final_sums(grads_b, sib_b, far_b, "b"))
    big = [j.reshape(2 * j.shape[1], j.shape[2]) for j in _run_comm(_join_halves(finals), "grad_join_halves")]
    small = _all_reduce_small(a_lower_bounds, dlb, [
        (dg1, 3), (dg2, 4), (db1, 5), (db2, 6), (dnw_a, 7), (dnw_b, 8), (dalog.reshape(1, h), 9), (ddtb.reshape(1, h), 10),
        (loss, 11), (dconv.reshape(3 * CONV_WIDTH, d), 12)])
    loss_out = small[11, 0]
    cw = 3 * d // N_CHIPS
    g_conv = lax.dynamic_slice_in_dim(small[12:24].reshape(CONV_WIDTH, 3 * d), chip * cw, cw, axis=1)
    g_b_in = lax.dynamic_slice(jnp.concatenate([big[2], big[3]], axis=1), (0, e * chip), (d, wb))
    grads = {
        "a_w_in": big[0], "a_lower_bounds": small[0:3], "a_norm_w": small[7:8, :HEAD_DIM], "a_w_out": big[1],
        "b_w_in": g_b_in, "b_conv_w": g_conv, "b_a_log": small[9:10, :h], "b_dt_bias": small[10:11, :h],
        "b_norm_w": small[8:9, :HEAD_DIM], "b_w_out": big[4],
        "ln_g": jnp.concatenate([small[3:4], small[4:5]], axis=0), "ln_b": jnp.concatenate([small[5:6], small[6:7]], axis=0),
    }

    weights = dict(a_w_in=a_w_in, a_lower_bounds=a_lower_bounds, a_norm_w=a_norm_w, a_w_out=a_w_out, b_w_in=b_w_in,
                   b_conv_w=b_conv_w, b_a_log=b_a_log, b_dt_bias=b_dt_bias, b_norm_w=b_norm_w, b_w_out=b_w_out,
                   ln_g=ln_g, ln_b=ln_b)
    m_in = dict(a_w_in=m_a_w_in, a_lower_bounds=m_a_lower_bounds, a_norm_w=m_a_norm_w, a_w_out=m_a_w_out, b_w_in=m_b_w_in,
                b_conv_w=m_b_conv_w, b_a_log=m_b_a_log, b_dt_bias=m_b_dt_bias, b_norm_w=m_b_norm_w, b_w_out=m_b_w_out,
                ln_g=m_ln_g, ln_b=m_ln_b)
    v_in = dict(a_w_in=v_a_w_in, a_lower_bounds=v_a_lower_bounds, a_norm_w=v_a_norm_w, a_w_out=v_a_w_out, b_w_in=v_b_w_in,
                b_conv_w=v_b_conv_w, b_a_log=v_b_a_log, b_dt_bias=v_b_dt_bias, b_norm_w=v_b_norm_w, b_w_out=v_b_w_out,
                ln_g=v_ln_g, ln_b=v_ln_b)
    out_g, out_d, out_m, out_v = [], [], [], []
    for name, w in weights.items():
        shape2 = w.shape[-2:]
        g, delta, nm, nv = _adamw(w.reshape(shape2), grads[name].reshape(shape2), m_in[name].reshape(shape2),
                                  v_in[name].reshape(shape2), "adamw_" + name)
        out_g.append(g.reshape(w.shape))
        out_d.append(delta.reshape(w.shape))
        out_m.append(nm.reshape(w.shape))
        out_v.append(nv.reshape(w.shape))
    return (loss_out, grad_x.reshape(x.shape), *out_g, *out_d, *out_m, *out_v)
```
